```python
import math
import numpy as np
import jax, jax.numpy as jnp
from jax import lax

D_MODEL = 2048
BATCH = 4
SEQ = 2048
DEPTH = 4
DEC_BATCH = 128
DEC_SEQ = 4
PAST_LEN = 16384
PAGE_SIZE = 128

N_PAIRS = DEPTH // 2
CHUNK = 64
CONV_W = 4
MEM_LEN = 256
MEM_HEADS = 4
MEM_HEAD_DIM = D_MODEL // 8
MEM_WIDTH = MEM_HEADS * MEM_HEAD_DIM
GLA_HEADS = 4
GLA_WIDTH = D_MODEL // 2
GLA_DV = GLA_WIDTH // GLA_HEADS
GLA_DK = GLA_DV // 2
GLA_QK = GLA_HEADS * GLA_DK
GLA_RANK = 16
GLA_TAU = 16.0
SSD_WIDTH = D_MODEL
SSD_HEAD_DIM = 64
SSD_HEADS = SSD_WIDTH // SSD_HEAD_DIM
SSD_GROUPS = 4
SSD_REP = SSD_HEADS // SSD_GROUPS
SSD_STATE = 128
SSD_CONV_DIM = SSD_WIDTH + 2 * SSD_GROUPS * SSD_STATE
GDN_WIDTH = D_MODEL
GDN_HEAD_DIM = 128
GDN_HEADS = GDN_WIDTH // GDN_HEAD_DIM
RWKV_WIDTH = D_MODEL // 2
RWKV_HEAD_DIM = 64
RWKV_HEADS = RWKV_WIDTH // RWKV_HEAD_DIM
RWKV_W_RANK = 64
RWKV_A_RANK = 64
RWKV_SHIFT_DIM = 3 * RWKV_WIDTH + RWKV_W_RANK + RWKV_A_RANK
RWKV_GN_EPS = 64e-5
EVEN_SIZES = (GLA_QK, GLA_QK, GLA_WIDTH, GLA_RANK, GLA_WIDTH,
              SSD_WIDTH, SSD_CONV_DIM, SSD_HEADS,
              MEM_WIDTH, MEM_WIDTH)
ODD_SIZES = (3 * GDN_WIDTH, GDN_WIDTH, GDN_HEADS, GDN_HEADS,
             RWKV_SHIFT_DIM, RWKV_WIDTH,
             MEM_WIDTH, MEM_WIDTH)
EVEN_MIX = GLA_WIDTH + SSD_WIDTH + MEM_WIDTH
ODD_MIX = GDN_WIDTH + RWKV_WIDTH + MEM_WIDTH
DEEPNORM_ALPHA = (2 * DEPTH) ** 0.25
DEEPNORM_BETA = (8 * DEPTH) ** -0.25
F32 = jnp.float32

kernel_name = 'hybrid_gla_ssd_gdn_rwkv7_mem_decode_step'


def _split(h, sizes):
    return jnp.split(h, np.cumsum(sizes)[:-1].tolist(), axis=-1)


def _layer_norm(x, g, b, eps=1e-5):
    xf = x.astype(F32)
    xc = xf - jnp.mean(xf, -1, keepdims=True)
    var = jnp.mean(xc * xc, -1, keepdims=True)
    return (xc * lax.rsqrt(var + eps) * g.astype(F32) + b.astype(F32)).astype(x.dtype)


def _rms_norm(x, g, eps=1e-6):
    xf = x.astype(F32)
    return (xf * lax.rsqrt(jnp.mean(xf * xf, -1, keepdims=True) + eps) * g.astype(F32)).astype(x.dtype)


def _l2_normalize(x, eps=1e-6):
    xf = x.astype(F32)
    return xf * lax.rsqrt(jnp.sum(xf * xf, -1, keepdims=True) + eps)


def _group_norm(x, g, b, eps):
    xf = x.astype(F32)
    xc = xf - jnp.mean(xf, -1, keepdims=True)
    n = xc * lax.rsqrt(jnp.mean(xc * xc, -1, keepdims=True) + eps)
    return n.reshape(x.shape[:2] + (-1,)) * g.astype(F32) + b.astype(F32)


def _causal_conv(u, buf, w):
    t = u.shape[1]
    full = jnp.concatenate([buf.astype(u.dtype), u], axis=1)
    out = full[:, 0:t] * w[0]
    for i in range(1, CONV_W):
        out = out + full[:, i:i + t] * w[i]
    return out, full[:, t:]


def _chunk(a, length):
    bsz, t = a.shape[:2]
    n = -(-t // length)
    a = jnp.pad(a, [(0, 0), (0, n * length - t)] + [(0, 0)] * (a.ndim - 2))
    return jnp.moveaxis(a.reshape((bsz, n, length) + a.shape[2:]), 1, 0)


def _unchunk(a, t):
    n, bsz, length = a.shape[:3]
    return jnp.moveaxis(a, 0, 1).reshape((bsz, n * length) + a.shape[3:])[:, :t]


def _gla_chunked(q, k, v, log_a, s0):
    t = q.shape[1]
    length = min(CHUNK, t)
    causal = jnp.tril(jnp.ones((length, length), bool))[None, :, :, None, None]

    def step(s, inp):
        qi, ki, vi, gi = inp
        b = jnp.cumsum(gi, axis=1)
        b_last = b[:, -1]
        decay = jnp.exp(jnp.where(causal, b[:, :, None] - b[:, None, :], -jnp.inf))
        att = jnp.einsum('btshc,bshc->bhts', qi[:, :, None] * decay, ki)
        o = (jnp.einsum('bthc,bhcv->bthv', qi * jnp.exp(b), s)
             + jnp.einsum('bhts,bshv->bthv', att, vi))
        s = s * jnp.exp(b_last)[..., None] + jnp.einsum(
            'bshc,bshv->bhcv', ki * jnp.exp(b_last[:, None] - b), vi)
        return s, o

    xs = tuple(_chunk(a.astype(F32), length) for a in (q, k, v, log_a))
    s, o = lax.scan(step, s0.astype(F32), xs)
    return _unchunk(o, t), s


def _ssd_chunked(xh, dt, bm, cm, a, s0):
    t = xh.shape[1]
    length = min(CHUNK, t)
    causal = jnp.tril(jnp.ones((length, length), bool))[None, :, :, None, None]

    def step(s, inp):
        xi, dti, bi, ci = inp
        c = jnp.cumsum(dti * a, axis=1)
        c_last = c[:, -1]
        seg = jnp.exp(jnp.where(causal, c[:, :, None] - c[:, None, :], -jnp.inf))
        xdt = xi * dti[..., None]
        cb = jnp.einsum('btgn,bsgn->btsg', ci, bi)
        y = (jnp.einsum('btsgr,bsgrp->btgrp', cb[..., None] * seg, xdt)
             + jnp.einsum('btgn,bgrpn->btgrp', ci, s) * jnp.exp(c)[..., None])
        s = s * jnp.exp(c_last)[..., None, None] + jnp.einsum(
            'bsgn,bsgrp->bgrpn', bi, xdt * jnp.exp(c_last[:, None] - c)[..., None])
        return s, y

    xs = tuple(_chunk(z.astype(F32), length) for z in (xh, dt, bm, cm))
    s, y = lax.scan(step, s0.astype(F32), xs)
    return _unchunk(y, t), s


def _gdn_chunked(q, k, v, g, beta, s0):
    t = q.shape[1]
    length = min(CHUNK, t)
    causal = jnp.tril(jnp.ones((length, length), bool))
    strict = jnp.tril(jnp.ones((length, length), bool), -1)
    eye = jnp.eye(length, dtype=F32)

    def step(s, inp):
        qi, ki, vi, gi, bi = inp
        c = jnp.cumsum(gi, axis=1).transpose(0, 2, 1)
        c_last = c[..., -1]
        decay = jnp.exp(jnp.where(causal, c[..., :, None] - c[..., None, :], -jnp.inf))
        bh = bi.transpose(0, 2, 1)
        kk = jnp.einsum('bthd,bshd->bhts', ki, ki)
        lower = jnp.where(strict, kk * decay * bh[..., None], 0.0) + eye
        rhs = bh[..., None] * (vi.transpose(0, 2, 1, 3)
                               - jnp.exp(c)[..., None] * jnp.einsum('bthd,bhdv->bhtv', ki, s))
        u = lax.linalg.triangular_solve(lower, rhs, left_side=True, lower=True, unit_diagonal=True)
        qk = jnp.einsum('bthd,bshd->bhts', qi, ki) * decay
        o = (jnp.exp(c)[..., None] * jnp.einsum('bthd,bhdv->bhtv', qi, s)
             + jnp.einsum('bhts,bhsv->bhtv', qk, u))
        kdec = ki * jnp.exp(c_last[..., None] - c).transpose(0, 2, 1)[..., None]
        s = s * jnp.exp(c_last)[..., None, None] + jnp.einsum('bshd,bhsv->bhdv', kdec, u)
        return s, o.transpose(0, 2, 1, 3)

    xs = tuple(_chunk(z.astype(F32), length) for z in (q, k, v, g, beta))
    s, o = lax.scan(step, s0.astype(F32), xs)
    return _unchunk(o, t), s


def _rwkv7_scan(r, w, k, v, kk, a, s0):
    def step(s, inp):
        rt, wt, kt, vt, kkt, at = inp
        sk = jnp.einsum('bhvk,bhk->bhv', s, kkt)
        s = (s * wt[:, :, None, :] - sk[..., None] * (kkt * at)[:, :, None, :]
             + vt[..., None] * kt[:, :, None, :])
        return s, jnp.einsum('bhvk,bhk->bhv', s, rt)

    xs = tuple(jnp.moveaxis(z.astype(F32), 1, 0) for z in (r, w, k, v, kk, a))
    s, o = lax.scan(step, s0.astype(F32), xs)
    return jnp.moveaxis(o, 0, 1), s


def _mem_attention(mq, mem_k, mem_v):
    bsz, t, _ = mq.shape
    q = mq.reshape(bsz, t, MEM_HEADS, MEM_HEAD_DIM)
    s = jnp.einsum('bthd,bmhd->bhtm', q, mem_k.astype(q.dtype)).astype(F32) * MEM_HEAD_DIM ** -0.5
    p = jax.nn.softmax(s, axis=-1).astype(q.dtype)
    return jnp.einsum('bhtm,bmhd->bthd', p, mem_v.astype(q.dtype)).reshape(bsz, t, MEM_WIDTH)


def _even_layer(x, mem_k, mem_v, s_gla, s_ssd, s_conv, P):
    bsz, t, _ = x.shape
    h = jnp.einsum('btd,dc->btc', x, P['w_in'])
    gq, gk, gv, glr, ggate, sz, sxbc, sdt, mq, mgate = _split(h, EVEN_SIZES)
    hk = (bsz, t, GLA_HEADS, GLA_DK)
    log_a = jax.nn.log_sigmoid((glr @ P['gla_w2'] + P['gla_b']).astype(F32)) / GLA_TAU
    o_gla, s_gla = _gla_chunked(gq.reshape(hk) * GLA_DK ** -0.5, gk.reshape(hk),
                                gv.reshape(bsz, t, GLA_HEADS, GLA_DV), log_a.reshape(hk), s_gla)
    o_gla = _rms_norm(o_gla.astype(x.dtype), P['gla_norm']).reshape(bsz, t, GLA_WIDTH) * jax.nn.silu(ggate)
    xbc, s_conv = _causal_conv(sxbc, s_conv, P['ssd_conv_w'])
    xbc = jax.nn.silu(xbc + P['ssd_conv_b'])
    sx, sb, sc = _split(xbc, (SSD_WIDTH, SSD_GROUPS * SSD_STATE, SSD_GROUPS * SSD_STATE))
    xh = sx.reshape(bsz, t, SSD_GROUPS, SSD_REP, SSD_HEAD_DIM)
    dt = jax.nn.softplus((sdt + P['ssd_dt_bias']).astype(F32)).reshape(bsz, t, SSD_GROUPS, SSD_REP)
    a = -jnp.exp(P['ssd_a_log'].astype(F32)).reshape(SSD_GROUPS, SSD_REP)
    gs = (bsz, t, SSD_GROUPS, SSD_STATE)
    y, s_ssd = _ssd_chunked(xh, dt, sb.reshape(gs), sc.reshape(gs), a,
                            s_ssd.reshape(bsz, SSD_GROUPS, SSD_REP, SSD_HEAD_DIM, SSD_STATE))
    y = y.astype(x.dtype) + xh * P['ssd_d'].reshape(SSD_GROUPS, SSD_REP, 1)
    y = (y.reshape(bsz, t, SSD_WIDTH) * jax.nn.silu(sz)).reshape(bsz, t, SSD_GROUPS, SSD_WIDTH // SSD_GROUPS)
    y = _rms_norm(y, P['ssd_norm'].reshape(SSD_GROUPS, SSD_WIDTH // SSD_GROUPS)).reshape(bsz, t, SSD_WIDTH)
    o_mem = _mem_attention(mq, mem_k, mem_v) * jax.nn.silu(mgate)
    mix = jnp.concatenate([o_gla.astype(x.dtype), y.astype(x.dtype), o_mem.astype(x.dtype)], axis=-1)
    out = jnp.einsum('btc,cd->btd', mix, P['w_out'])
    x = _layer_norm(DEEPNORM_ALPHA * x + out, P['ln_g'], P['ln_b'])
    return x, s_gla, s_ssd.reshape(bsz, SSD_HEADS, SSD_HEAD_DIM, SSD_STATE), s_conv


def _odd_layer(x, mem_k, mem_v, s_gdn, s_conv, s_rwkv, s_shift, P):
    bsz, t, _ = x.shape
    h = jnp.einsum('btd,dc->btc', x, P['w_in'])
    cqkv, cz, cb, ca, rsh, rgate, mq, mgate = _split(h, ODD_SIZES)
    qkv, s_conv = _causal_conv(cqkv, s_conv, P['gdn_conv_w'])
    q, k, v = jnp.split(jax.nn.silu(qkv), 3, axis=-1)
    hd = (bsz, t, GDN_HEADS, GDN_HEAD_DIM)
    q = _l2_normalize(q.reshape(hd)) * GDN_HEAD_DIM ** -0.5
    k = _l2_normalize(k.reshape(hd))
    beta = jax.nn.sigmoid(cb.astype(F32))
    g = -jnp.exp(P['gdn_a_log'].astype(F32)) * jax.nn.softplus((ca + P['gdn_dt_bias']).astype(F32))
    o_gdn, s_gdn = _gdn_chunked(q, k, v.reshape(hd), g, beta, s_gdn)
    o_gdn = _rms_norm(o_gdn.astype(x.dtype), P['gdn_norm']).reshape(bsz, t, GDN_WIDTH) * jax.nn.silu(cz)
    prev = jnp.concatenate([s_shift[:, None].astype(rsh.dtype), rsh[:, :-1]], axis=1)
    s_shift = rsh[:, -1]
    rs = rsh + (prev - rsh) * P['rwkv_mu']
    r, k7, v7, xw, xa = _split(rs, (RWKV_WIDTH, RWKV_WIDTH, RWKV_WIDTH, RWKV_W_RANK, RWKV_A_RANK))
    w_log = -jax.nn.softplus(-(P['rwkv_w0'] + jnp.tanh(xw) @ P['rwkv_w2']).astype(F32)) - 0.5
    w = jnp.exp(-jnp.exp(w_log))
    a7 = jax.nn.sigmoid((P['rwkv_a0'] + xa @ P['rwkv_a2']).astype(F32))
    hd7 = (bsz, t, RWKV_HEADS, RWKV_HEAD_DIM)
    kk = _l2_normalize((k7 * P['rwkv_kk']).reshape(hd7))
    k7 = (k7.astype(F32) * (1.0 + (a7 - 1.0) * P['rwkv_ka'].astype(F32))).reshape(hd7)
    r = r.astype(F32).reshape(hd7)
    v7 = v7.astype(F32).reshape(hd7)
    o7, s_rwkv = _rwkv7_scan(r, w.reshape(hd7), k7, v7, kk, a7.reshape(hd7), s_rwkv)
    bonus = jnp.sum(r * k7 * P['rwkv_rk'].astype(F32), -1, keepdims=True) * v7
    o7 = (_group_norm(o7, P['rwkv_ln_g'], P['rwkv_ln_b'], RWKV_GN_EPS) + bonus.reshape(bsz, t, RWKV_WIDTH))
    o7 = o7.astype(x.dtype) * jax.nn.silu(rgate)
    o_mem = _mem_attention(mq, mem_k, mem_v) * jax.nn.silu(mgate)
    mix = jnp.concatenate([o_gdn.astype(x.dtype), o7.astype(x.dtype), o_mem.astype(x.dtype)], axis=-1)
    out = jnp.einsum('btc,cd->btd', mix, P['w_out'])
    x = _layer_norm(DEEPNORM_ALPHA * x + out, P['ln_g'], P['ln_b'])
    return x, s_gdn, s_conv, s_rwkv, s_shift


def _trunk(x, mem_k, mem_v, st, ev, od):
    new = {name: [] for name in st}
    for layer in range(DEPTH):
        p = layer // 2
        if layer % 2 == 0:
            P = {n: w[p] for n, w in ev.items()}
            x, s1, s2, s3 = _even_layer(x, mem_k[layer], mem_v[layer],
                                        st['gla'][p], st['ssd'][p], st['ssd_conv'][p], P)
            new['gla'].append(s1)
            new['ssd'].append(s2)
            new['ssd_conv'].append(s3)
        else:
            P = {n: w[p] for n, w in od.items()}
            x, s1, s2, s3, s4 = _odd_layer(x, mem_k[layer], mem_v[layer], st['gdn'][p], st['gdn_conv'][p],
                                           st['rwkv'][p], st['rwkv_shift'][p], P)
            new['gdn'].append(s1)
            new['gdn_conv'].append(s2)
            new['rwkv'].append(s3)
            new['rwkv_shift'].append(s4)
    return x, {n: jnp.stack(v) for n, v in new.items()}


def setup_inputs(seed: int = 0) -> dict:
    key = jax.random.key(seed)
    ks = iter(jax.random.split(key, 64))

    def nrm(shape, scale=1.0):
        return jax.random.normal(next(ks), shape, F32) * scale

    def uni(shape, lo, hi):
        return jax.random.uniform(next(ks), shape, F32, lo, hi)

    def dt_bias(shape):
        dt = jnp.exp(uni(shape, math.log(1e-3), math.log(1e-1)))
        return dt + jnp.log(-jnp.expm1(-dt))

    P = N_PAIRS
    return {
        'x_prompt': nrm((BATCH, SEQ, D_MODEL)),
        'x_sample': nrm((DEC_BATCH, DEC_SEQ, D_MODEL)),
        'mem_prompt': nrm((BATCH, MEM_LEN, D_MODEL)),
        'cache_mem_k': nrm((DEPTH, DEC_BATCH, MEM_LEN, MEM_HEADS, MEM_HEAD_DIM)),
        'cache_mem_v': nrm((DEPTH, DEC_BATCH, MEM_LEN, MEM_HEADS, MEM_HEAD_DIM)),
        'state_gla': nrm((P, DEC_BATCH, GLA_HEADS, GLA_DK, GLA_DV), 0.1),
        'state_ssd': nrm((P, DEC_BATCH, SSD_HEADS, SSD_HEAD_DIM, SSD_STATE), 0.1),
        'state_ssd_conv': nrm((P, DEC_BATCH, CONV_W - 1, SSD_CONV_DIM)),
        'state_gdn': nrm((P, DEC_BATCH, GDN_HEADS, GDN_HEAD_DIM, GDN_HEAD_DIM), 0.1),
        'state_gdn_conv': nrm((P, DEC_BATCH, CONV_W - 1, 3 * GDN_WIDTH)),
        'state_rwkv': nrm((P, DEC_BATCH, RWKV_HEADS, RWKV_HEAD_DIM, RWKV_HEAD_DIM), 0.1),
        'state_rwkv_shift': nrm((P, DEC_BATCH, RWKV_SHIFT_DIM)),
        'mem_w_kv': nrm((DEPTH, D_MODEL, 2 * MEM_WIDTH), D_MODEL ** -0.5),
        'ev_w_in': nrm((P, D_MODEL, sum(EVEN_SIZES)), D_MODEL ** -0.5),
        'ev_gla_w2': nrm((P, GLA_RANK, GLA_QK), GLA_RANK ** -0.5),
        'ev_gla_b': nrm((P, GLA_QK), 0.1),
        'ev_gla_norm': 1.0 + nrm((P, GLA_DV), 0.02),
        'ev_ssd_conv_w': nrm((P, CONV_W, SSD_CONV_DIM), CONV_W ** -0.5),
        'ev_ssd_conv_b': nrm((P, SSD_CONV_DIM), 0.01),
        'ev_ssd_dt_bias': dt_bias((P, SSD_HEADS)),
        'ev_ssd_a_log': jnp.log(uni((P, SSD_HEADS), 1.0, 16.0)),
        'ev_ssd_d': 1.0 + nrm((P, SSD_HEADS), 0.01),
        'ev_ssd_norm': 1.0 + nrm((P, SSD_WIDTH), 0.02),
        'ev_w_out': nrm((P, EVEN_MIX, D_MODEL), EVEN_MIX ** -0.5 * DEEPNORM_BETA),
        'ev_ln_g': 1.0 + nrm((P, D_MODEL), 0.02),
        'ev_ln_b': nrm((P, D_MODEL), 0.01),
        'od_w_in': nrm((P, D_MODEL, sum(ODD_SIZES)), D_MODEL ** -0.5),
        'od_gdn_conv_w': nrm((P, CONV_W, 3 * GDN_WIDTH), CONV_W ** -0.5),
        'od_gdn_dt_bias': dt_bias((P, GDN_HEADS)),
        'od_gdn_a_log': jnp.log(uni((P, GDN_HEADS), 1.0, 16.0)),
        'od_gdn_norm': 1.0 + nrm((P, GDN_HEAD_DIM), 0.02),
        'od_rwkv_mu': uni((P, RWKV_SHIFT_DIM), 0.0, 1.0),
        'od_rwkv_w0': uni((P, RWKV_WIDTH), -4.0, -1.0),
        'od_rwkv_w2': nrm((P, RWKV_W_RANK, RWKV_WIDTH), 0.1 * RWKV_W_RANK ** -0.5),
        'od_rwkv_a0': nrm((P, RWKV_WIDTH), 0.1),
        'od_rwkv_a2': nrm((P, RWKV_A_RANK, RWKV_WIDTH), 0.1 * RWKV_A_RANK ** -0.5),
        'od_rwkv_kk': 0.85 + nrm((P, RWKV_WIDTH), 0.02),
        'od_rwkv_ka': 1.0 + nrm((P, RWKV_WIDTH), 0.02),
        'od_rwkv_rk': nrm((P, RWKV_HEADS, RWKV_HEAD_DIM), 0.1),
        'od_rwkv_ln_g': 1.0 + nrm((P, RWKV_WIDTH), 0.02),
        'od_rwkv_ln_b': nrm((P, RWKV_WIDTH), 0.01),
        'od_w_out': nrm((P, ODD_MIX, D_MODEL), ODD_MIX ** -0.5 * DEEPNORM_BETA),
        'od_ln_g': 1.0 + nrm((P, D_MODEL), 0.02),
        'od_ln_b': nrm((P, D_MODEL), 0.01),
    }


def reference(x_prompt, x_sample, mem_prompt, cache_mem_k, cache_mem_v,
              state_gla, state_ssd, state_ssd_conv, state_gdn, state_gdn_conv, state_rwkv, state_rwkv_shift,
              mem_w_kv,
              ev_w_in, ev_gla_w2, ev_gla_b, ev_gla_norm, ev_ssd_conv_w, ev_ssd_conv_b, ev_ssd_dt_bias,
              ev_ssd_a_log, ev_ssd_d, ev_ssd_norm, ev_w_out, ev_ln_g, ev_ln_b,
              od_w_in, od_gdn_conv_w, od_gdn_dt_bias, od_gdn_a_log, od_gdn_norm, od_rwkv_mu, od_rwkv_w0,
              od_rwkv_w2, od_rwkv_a0, od_rwkv_a2, od_rwkv_kk, od_rwkv_ka, od_rwkv_rk, od_rwkv_ln_g, od_rwkv_ln_b,
              od_w_out, od_ln_g, od_ln_b):
    ev = dict(w_in=ev_w_in, gla_w2=ev_gla_w2, gla_b=ev_gla_b, gla_norm=ev_gla_norm,
              ssd_conv_w=ev_ssd_conv_w, ssd_conv_b=ev_ssd_conv_b, ssd_dt_bias=ev_ssd_dt_bias,
              ssd_a_log=ev_ssd_a_log, ssd_d=ev_ssd_d, ssd_norm=ev_ssd_norm,
              w_out=ev_w_out, ln_g=ev_ln_g, ln_b=ev_ln_b)
    od = dict(w_in=od_w_in, gdn_conv_w=od_gdn_conv_w, gdn_dt_bias=od_gdn_dt_bias, gdn_a_log=od_gdn_a_log,
              gdn_norm=od_gdn_norm, rwkv_mu=od_rwkv_mu, rwkv_w0=od_rwkv_w0, rwkv_w2=od_rwkv_w2,
              rwkv_a0=od_rwkv_a0, rwkv_a2=od_rwkv_a2, rwkv_kk=od_rwkv_kk, rwkv_ka=od_rwkv_ka,
              rwkv_rk=od_rwkv_rk, rwkv_ln_g=od_rwkv_ln_g, rwkv_ln_b=od_rwkv_ln_b,
              w_out=od_w_out, ln_g=od_ln_g, ln_b=od_ln_b)
    bp = x_prompt.shape[0]
    kv = jnp.einsum('bmd,ldc->lbmc', mem_prompt, mem_w_kv)
    mem_k_p, mem_v_p = jnp.split(kv, 2, axis=-1)
    mshape = (DEPTH, bp, mem_prompt.shape[1], MEM_HEADS, MEM_HEAD_DIM)
    mem_k_p = mem_k_p.reshape(mshape)
    mem_v_p = mem_v_p.reshape(mshape)
    dtp = x_prompt.dtype
    init_p = dict(
        gla=jnp.zeros((N_PAIRS, bp, GLA_HEADS, GLA_DK, GLA_DV), F32),
        ssd=jnp.zeros((N_PAIRS, bp, SSD_HEADS, SSD_HEAD_DIM, SSD_STATE), F32),
        ssd_conv=jnp.zeros((N_PAIRS, bp, CONV_W - 1, SSD_CONV_DIM), dtp),
        gdn=jnp.zeros((N_PAIRS, bp, GDN_HEADS, GDN_HEAD_DIM, GDN_HEAD_DIM), F32),
        gdn_conv=jnp.zeros((N_PAIRS, bp, CONV_W - 1, 3 * GDN_WIDTH), dtp),
        rwkv=jnp.zeros((N_PAIRS, bp, RWKV_HEADS, RWKV_HEAD_DIM, RWKV_HEAD_DIM), F32),
        rwkv_shift=jnp.zeros((N_PAIRS, bp, RWKV_SHIFT_DIM), dtp))
    init_s = dict(gla=state_gla, ssd=state_ssd, ssd_conv=state_ssd_conv, gdn=state_gdn,
                  gdn_conv=state_gdn_conv, rwkv=state_rwkv, rwkv_shift=state_rwkv_shift)
    y_prompt, new_p = _trunk(x_prompt, mem_k_p, mem_v_p, init_p, ev, od)
    y_sample, new_s = _trunk(x_sample, cache_mem_k, cache_mem_v, init_s, ev, od)
    return (y_prompt, y_sample, mem_k_p, mem_v_p,
            new_p['gla'], new_s['gla'], new_p['ssd'], new_s['ssd'],
            new_p['ssd_conv'], new_s['ssd_conv'], new_p['gdn'], new_s['gdn'],
            new_p['gdn_conv'], new_s['gdn_conv'], new_p['rwkv'], new_s['rwkv'],
            new_p['rwkv_shift'], new_s['rwkv_shift'])
```

```python
import functools
import math

import numpy as np
import jax
import jax.numpy as jnp
from jax import lax
from jax.experimental import pallas as pl
from jax.experimental.pallas import tpu as pltpu

F32 = jnp.float32
BF16 = jnp.bfloat16
HI = lax.Precision.HIGHEST

D_MODEL = 2048
DEPTH = 4
CONV_W = 4
MEM_LEN = 256
MEM_HEADS = 4
MEM_HEAD_DIM = 256
MEM_WIDTH = 1024
GLA_HEADS = 4
GLA_DK = 128
GLA_DV = 256
GLA_QK = 512
GLA_WIDTH = 1024
GLA_RANK = 16
GLA_TAU = 16.0
SSD_WIDTH = 2048
SSD_HEAD_DIM = 64
SSD_HEADS = 32
SSD_GROUPS = 4
SSD_REP = 8
SSD_STATE = 128
SSD_CONV_DIM = SSD_WIDTH + 2 * SSD_GROUPS * SSD_STATE
GDN_WIDTH = 2048
GDN_HEAD_DIM = 128
GDN_HEADS = 16
RWKV_WIDTH = 1024
RWKV_HEAD_DIM = 64
RWKV_HEADS = 16
RWKV_W_RANK = 64
RWKV_A_RANK = 64
RWKV_SHIFT_DIM = 3 * RWKV_WIDTH + RWKV_W_RANK + RWKV_A_RANK
RWKV_GN_EPS = 64e-5
EVEN_SIZES = (GLA_QK, GLA_QK, GLA_WIDTH, GLA_RANK, GLA_WIDTH, SSD_WIDTH, SSD_CONV_DIM, SSD_HEADS,
              MEM_WIDTH, MEM_WIDTH)
ODD_SIZES = (3 * GDN_WIDTH, GDN_WIDTH, GDN_HEADS, GDN_HEADS, RWKV_SHIFT_DIM, RWKV_WIDTH, MEM_WIDTH, MEM_WIDTH)
MIX_WIDTH = 4096
DEEPNORM_ALPHA = (2 * DEPTH) ** 0.25

EVEN_ORDER = (0, 1, 2, 4, 5, 6, 8, 9, 3, 7)
ODD_ORDER = (0, 1, 4, 5, 6, 7, 2, 3)
PROJ_TN = 768
VMEM_LIMIT = 48 * 1024 * 1024

GLA_CHUNK = 16
SSD_CHUNK = 64
GDN_CHUNK = 64
SMALL_T = 8


def _packed_layout(sizes, order):
    offs, o = {}, 0
    for i in order:
        offs[i] = o
        o += sizes[i]
    total = -(-o // PROJ_TN) * PROJ_TN
    return offs, total


EVEN_OFF, EVEN_N = _packed_layout(EVEN_SIZES, EVEN_ORDER)
ODD_OFF, ODD_N = _packed_layout(ODD_SIZES, ODD_ORDER)


def _pack_w_in(w, sizes, order, total):
    segs = jnp.split(w, np.cumsum(sizes)[:-1].tolist(), axis=-1)
    parts = [segs[i] for i in order]
    used = sum(sizes)
    if total > used:
        parts.append(jnp.zeros((w.shape[0], total - used), w.dtype))
    return jnp.concatenate(parts, axis=-1).astype(BF16)


def _seg(h, offs, sizes, i):
    return h[..., offs[i]:offs[i] + sizes[i]]


def _cparams(sem):
    return pltpu.CompilerParams(dimension_semantics=sem, vmem_limit_bytes=VMEM_LIMIT)


def _mm_kernel(x_ref, w_ref, o_ref, *, precision):
    o_ref[...] = jnp.dot(x_ref[...], w_ref[...], preferred_element_type=F32, precision=precision)


def _matmul(x, w, tm, tn, precision=None):
    m, k = x.shape
    n = w.shape[1]
    assert m % tm == 0 and n % tn == 0
    return pl.pallas_call(
        functools.partial(_mm_kernel, precision=precision),
        grid=(n // tn, m // tm),
        in_specs=[pl.BlockSpec((tm, k), lambda j, i: (i, 0)),
                  pl.BlockSpec((k, tn), lambda j, i: (0, j))],
        out_specs=pl.BlockSpec((tm, tn), lambda j, i: (i, j)),
        out_shape=jax.ShapeDtypeStruct((m, n), F32),
        compiler_params=_cparams(("parallel", "parallel")),
        name="matmul",
    )(x, w)


def _out_ln_kernel(mix_ref, w_ref, x_ref, g_ref, b_ref, y_ref, ybf_ref, acc, *, nk):
    kk = pl.program_id(1)

    @pl.when(kk == 0)
    def _():
        acc[...] = jnp.zeros_like(acc)

    acc[...] += jnp.dot(mix_ref[...], w_ref[...], preferred_element_type=F32)

    @pl.when(kk == nk - 1)
    def _():
        z = DEEPNORM_ALPHA * x_ref[...] + acc[...]
        zc = z - jnp.mean(z, axis=-1, keepdims=True)
        var = jnp.mean(zc * zc, axis=-1, keepdims=True)
        y = zc * lax.rsqrt(var + 1e-5) * g_ref[...] + b_ref[...]
        y_ref[...] = y
        ybf_ref[...] = y.astype(BF16)


def _out_ln(mix, w, x, g, b, tm=512, tk=1024):
    m, k = mix.shape
    d = w.shape[1]
    nk = k // tk
    return pl.pallas_call(
        functools.partial(_out_ln_kernel, nk=nk),
        grid=(m // tm, nk),
        in_specs=[pl.BlockSpec((tm, tk), lambda i, j: (i, j)),
                  pl.BlockSpec((tk, d), lambda i, j: (j, 0)),
                  pl.BlockSpec((tm, d), lambda i, j: (i, 0)),
                  pl.BlockSpec((1, d), lambda i, j: (0, 0)),
                  pl.BlockSpec((1, d), lambda i, j: (0, 0))],
        out_specs=[pl.BlockSpec((tm, d), lambda i, j: (i, 0)),
                   pl.BlockSpec((tm, d), lambda i, j: (i, 0))],
        out_shape=[jax.ShapeDtypeStruct((m, d), F32), jax.ShapeDtypeStruct((m, d), BF16)],
        scratch_shapes=[pltpu.VMEM((tm, d), F32)],
        compiler_params=_cparams(("parallel", "arbitrary")),
        name="out_ln",
    )(mix, w, x, g.reshape(1, d), b.reshape(1, d))


def _mem_kernel(q_ref, k_ref, v_ref, o_ref):
    for h in range(MEM_HEADS):
        sl = slice(h * MEM_HEAD_DIM, (h + 1) * MEM_HEAD_DIM)
        q = q_ref[:, sl].astype(BF16)
        k = k_ref[:, sl].astype(BF16)
        v = v_ref[:, sl].astype(BF16)
        s = lax.dot_general(q, k, (((1,), (1,)), ((), ())), preferred_element_type=F32)
        s = s * MEM_HEAD_DIM ** -0.5
        p = jnp.exp(s - jnp.max(s, axis=-1, keepdims=True))
        p = p / jnp.sum(p, axis=-1, keepdims=True)
        o_ref[:, sl] = jnp.dot(p.astype(BF16), v, preferred_element_type=F32)


def _mem_attention(mq, mem_k, mem_v, tq):
    bsz, t, _ = mq.shape
    return pl.pallas_call(
        _mem_kernel,
        grid=(bsz, t // tq),
        in_specs=[pl.BlockSpec((None, tq, MEM_WIDTH), lambda b, i: (b, i, 0)),
                  pl.BlockSpec((None, MEM_LEN, MEM_WIDTH), lambda b, i: (b, 0, 0)),
                  pl.BlockSpec((None, MEM_LEN, MEM_WIDTH), lambda b, i: (b, 0, 0))],
        out_specs=pl.BlockSpec((None, tq, MEM_WIDTH), lambda b, i: (b, i, 0)),
        out_shape=jax.ShapeDtypeStruct((bsz, t, MEM_WIDTH), F32),
        compiler_params=_cparams(("parallel", "parallel")),
        name="mem_attention",
    )(mq, mem_k, mem_v)


def _gla_kernel(q_ref, k_ref, v_ref, g_ref, s0_ref, o_ref, s_ref, S, *, L, nchunk, nblk):
    tb = pl.program_id(1)

    @pl.when(tb == 0)
    def _():
        S[...] = s0_ref[...]

    row = lax.broadcasted_iota(jnp.int32, (L, L), 0)
    col = lax.broadcasted_iota(jnp.int32, (L, L), 1)
    tril = (col <= row).astype(F32)
    t_iota = lax.broadcasted_iota(jnp.int32, (L, GLA_DK), 0)
    r128 = lax.broadcasted_iota(jnp.int32, (GLA_DK, GLA_DK), 0)
    c128 = lax.broadcasted_iota(jnp.int32, (GLA_DK, GLA_DK), 1)

    def chunk(ci, carry):
        r0 = pl.multiple_of(ci * L, L)
        for h in range(GLA_HEADS):
            ks = slice(h * GLA_DK, (h + 1) * GLA_DK)
            vs = slice(h * GLA_DV, (h + 1) * GLA_DV)
            q = q_ref[pl.ds(r0, L), ks]
            k = k_ref[pl.ds(r0, L), ks]
            g = g_ref[pl.ds(r0, L), ks]
            v = v_ref[pl.ds(r0, L), vs]
            b = jnp.dot(tril, g, precision=HI, preferred_element_type=F32)
            b_last = b[L - 1:L, :]
            att = jnp.zeros((L, L), F32)
            for s in range(L):
                d = jnp.where(t_iota >= s, b - b[s:s + 1, :], -jnp.inf)
                p = jnp.exp(d) * q * k[s:s + 1, :]
                att = jnp.where(col == s, jnp.sum(p, axis=-1, keepdims=True), att)
            s_old = S[h]
            o = (jnp.dot(q * jnp.exp(b), s_old, precision=HI, preferred_element_type=F32)
                 + jnp.dot(att, v, precision=HI, preferred_element_type=F32))
            o_ref[pl.ds(r0, L), vs] = o
            kd = k * jnp.exp(b_last - b)
            e_col = jnp.sum(jnp.where(r128 == c128, jnp.exp(b_last), 0.0), axis=-1, keepdims=True)
            S[h] = s_old * e_col + lax.dot_general(kd, v, (((0,), (0,)), ((), ())),
                                                   precision=HI, preferred_element_type=F32)
        return carry

    lax.fori_loop(0, nchunk, chunk, 0)

    @pl.when(tb == nblk - 1)
    def _():
        s_ref[...] = S[...]


def _gla(q, k, v, g, s0, L, tb):
    bsz, t, _ = q.shape
    nblk = t // tb
    qk_spec = pl.BlockSpec((None, tb, GLA_QK), lambda b, i: (b, i, 0))
    v_spec = pl.BlockSpec((None, tb, GLA_WIDTH), lambda b, i: (b, i, 0))
    s_spec = pl.BlockSpec((None, GLA_HEADS, GLA_DK, GLA_DV), lambda b, i: (b, 0, 0, 0))
    return pl.pallas_call(
        functools.partial(_gla_kernel, L=L, nchunk=tb // L, nblk=nblk),
        grid=(bsz, nblk),
        in_specs=[qk_spec, qk_spec, v_spec, qk_spec, s_spec],
        out_specs=[v_spec, s_spec],
        out_shape=[jax.ShapeDtypeStruct((bsz, t, GLA_WIDTH), F32),
                   jax.ShapeDtypeStruct(s0.shape, F32)],
        scratch_shapes=[pltpu.VMEM((GLA_HEADS, GLA_DK, GLA_DV), F32)],
        compiler_params=_cparams(("parallel", "arbitrary")),
        name="gla",
    )(q, k, v, g, s0)


def _ssd_kernel(x_ref, b_ref, c_ref, da_ref, s0_ref, y_ref, s_ref, S, *, L, nchunk):
    grp = pl.program_id(1)
    ci = pl.program_id(2)

    @pl.when(ci == 0)
    def _():
        S[...] = s0_ref[...]

    row = lax.broadcasted_iota(jnp.int32, (L, L), 0)
    col = lax.broadcasted_iota(jnp.int32, (L, L), 1)
    causal = col <= row
    tril = causal.astype(F32)
    lane = lax.broadcasted_iota(jnp.int32, (L, SSD_HEADS), 1)
    bm = b_ref[...]
    cm = c_ref[...]
    cb = lax.dot_general(cm, bm, (((1,), (1,)), ((), ())), precision=HI, preferred_element_type=F32)
    da_all = da_ref[...]
    c_all = jnp.dot(tril, da_all, precision=HI, preferred_element_type=F32)
    for r in range(SSD_REP):
        sel = lane == grp * SSD_REP + r
        c_col = jnp.sum(jnp.where(sel, c_all, 0.0), axis=-1, keepdims=True)
        d_col = jnp.sum(jnp.where(sel, da_all, 0.0), axis=-1, keepdims=True)
        c_row = jnp.sum(jnp.where(row <= col, d_col, 0.0), axis=0, keepdims=True)
        seg = jnp.exp(jnp.where(causal, c_col - c_row, -jnp.inf))
        hs = slice(r * SSD_HEAD_DIM, (r + 1) * SSD_HEAD_DIM)
        xr = x_ref[:, hs]
        s_old = S[r]
        y = (jnp.dot(cb * seg, xr, precision=HI, preferred_element_type=F32)
             + lax.dot_general(cm, s_old, (((1,), (1,)), ((), ())), precision=HI,
                               preferred_element_type=F32) * jnp.exp(c_col))
        y_ref[:, hs] = y
        c_last = c_col[L - 1:L, :]
        S[r] = s_old * jnp.exp(c_last) + lax.dot_general(
            xr * jnp.exp(c_last - c_col), bm, (((0,), (0,)), ((), ())), precision=HI,
            preferred_element_type=F32)

    @pl.when(ci == nchunk - 1)
    def _():
        s_ref[...] = S[...]


def _ssd(xdt, bm, cm, da, s0, L):
    bsz, t, _ = xdt.shape
    nchunk = t // L
    gw = SSD_REP * SSD_HEAD_DIM
    x_spec = pl.BlockSpec((None, L, gw), lambda b, g, c: (b, c, g))
    bc_spec = pl.BlockSpec((None, L, SSD_STATE), lambda b, g, c: (b, c, g))
    s_spec = pl.BlockSpec((None, SSD_REP, SSD_HEAD_DIM, SSD_STATE), lambda b, g, c: (b, g, 0, 0))
    return pl.pallas_call(
        functools.partial(_ssd_kernel, L=L, nchunk=nchunk),
        grid=(bsz, SSD_GROUPS, nchunk),
        in_specs=[x_spec, bc_spec, bc_spec,
                  pl.BlockSpec((None, L, SSD_HEADS), lambda b, g, c: (b, c, 0)), s_spec],
        out_specs=[x_spec, s_spec],
        out_shape=[jax.ShapeDtypeStruct((bsz, t, SSD_WIDTH), F32),
                   jax.ShapeDtypeStruct(s0.shape, F32)],
        scratch_shapes=[pltpu.VMEM((SSD_REP, SSD_HEAD_DIM, SSD_STATE), F32)],
        compiler_params=_cparams(("parallel", "parallel", "arbitrary")),
        name="ssd",
    )(xdt, bm, cm, da, s0)


GDN_HB = 4


def _gdn_kernel(q_ref, k_ref, v_ref, g_ref, beta_ref, s0_ref, o_ref, s_ref, S, *, L, nchunk, rounds):
    hb = pl.program_id(1)
    ci = pl.program_id(2)

    @pl.when(ci == 0)
    def _():
        S[...] = s0_ref[...]

    row = lax.broadcasted_iota(jnp.int32, (L, L), 0)
    col = lax.broadcasted_iota(jnp.int32, (L, L), 1)
    causal = col <= row
    strict = col < row
    tril = causal.astype(F32)
    lane = lax.broadcasted_iota(jnp.int32, (L, GDN_HEADS), 1)
    g_all = g_ref[...]
    beta_all = beta_ref[...]
    c_all = jnp.dot(tril, g_all, precision=HI, preferred_element_type=F32)
    nt = (((1,), (1,)), ((), ()))
    for j in range(GDN_HB):
        sel = lane == hb * GDN_HB + j
        c_col = jnp.sum(jnp.where(sel, c_all, 0.0), axis=-1, keepdims=True)
        g_col = jnp.sum(jnp.where(sel, g_all, 0.0), axis=-1, keepdims=True)
        beta_col = jnp.sum(jnp.where(sel, beta_all, 0.0), axis=-1, keepdims=True)
        c_row = jnp.sum(jnp.where(row <= col, g_col, 0.0), axis=0, keepdims=True)
        decay = jnp.exp(jnp.where(causal, c_col - c_row, -jnp.inf))
        hs = slice(j * GDN_HEAD_DIM, (j + 1) * GDN_HEAD_DIM)
        qh = q_ref[:, hs]
        kh = k_ref[:, hs]
        vh = v_ref[:, hs]
        s_old = S[j]
        kkm = lax.dot_general(kh, kh, nt, precision=HI, preferred_element_type=F32)
        a_neg = -jnp.where(strict, kkm * decay * beta_col, 0.0)
        ec = jnp.exp(c_col)
        u = beta_col * (vh - ec * jnp.dot(kh, s_old, precision=HI, preferred_element_type=F32))
        for r in range(rounds):
            u = u + jnp.dot(a_neg, u, precision=HI, preferred_element_type=F32)
            if r < rounds - 1:
                a_neg = jnp.dot(a_neg, a_neg, precision=HI, preferred_element_type=F32)
        qk = lax.dot_general(qh, kh, nt, precision=HI, preferred_element_type=F32) * decay
        o = (ec * jnp.dot(qh, s_old, precision=HI, preferred_element_type=F32)
             + jnp.dot(qk, u, precision=HI, preferred_element_type=F32))
        o_ref[:, hs] = o
        c_last = c_col[L - 1:L, :]
        S[j] = s_old * jnp.exp(c_last) + lax.dot_general(
            kh * jnp.exp(c_last - c_col), u, (((0,), (0,)), ((), ())), precision=HI,
            preferred_element_type=F32)

    @pl.when(ci == nchunk - 1)
    def _():
        s_ref[...] = S[...]


def _gdn(q, k, v, g, beta, s0, L):
    bsz, t, _ = q.shape
    nchunk = t // L
    hw = GDN_HB * GDN_HEAD_DIM
    x_spec = pl.BlockSpec((None, L, hw), lambda b, h, c: (b, c, h))
    gb_spec = pl.BlockSpec((None, L, GDN_HEADS), lambda b, h, c: (b, c, 0))
    s_spec = pl.BlockSpec((None, GDN_HB, GDN_HEAD_DIM, GDN_HEAD_DIM), lambda b, h, c: (b, h, 0, 0))
    return pl.pallas_call(
        functools.partial(_gdn_kernel, L=L, nchunk=nchunk, rounds=int(math.log2(L))),
        grid=(bsz, GDN_HEADS // GDN_HB, nchunk),
        in_specs=[x_spec, x_spec, x_spec, gb_spec, gb_spec, s_spec],
        out_specs=[x_spec, s_spec],
        out_shape=[jax.ShapeDtypeStruct((bsz, t, GDN_WIDTH), F32),
                   jax.ShapeDtypeStruct(s0.shape, F32)],
        scratch_shapes=[pltpu.VMEM((GDN_HB, GDN_HEAD_DIM, GDN_HEAD_DIM), F32)],
        compiler_params=_cparams(("parallel", "parallel", "arbitrary")),
        name="gdn",
    )(q, k, v, g, beta, s0)


def _rwkv_kernel(r_ref, w_ref, k_ref, kk_ref, kka_ref, vt_ref, s0_ref, ot_ref, s_ref, S, *, tb, nblk):
    ib = pl.program_id(1)

    @pl.when(ib == 0)
    def _():
        S[...] = s0_ref[...]

    n = RWKV_HEAD_DIM
    lane_t = lax.broadcasted_iota(jnp.int32, (n, tb), 1)

    def step(t, carry):
        rt = r_ref[t]
        wt = w_ref[t]
        kt = k_ref[t]
        kkt = kk_ref[t]
        kkat = kka_ref[t]
        here = lane_t == t
        for h in range(RWKV_HEADS):
            rows = slice(h * n, (h + 1) * n)
            sh = S[rows, :]
            vcol = jnp.sum(jnp.where(here, vt_ref[rows, :], 0.0), axis=-1, keepdims=True)
            sk = jnp.sum(sh * kkt[h:h + 1, :], axis=-1, keepdims=True)
            sn = sh * wt[h:h + 1, :] - sk * kkat[h:h + 1, :] + vcol * kt[h:h + 1, :]
            S[rows, :] = sn
            ocol = jnp.sum(sn * rt[h:h + 1, :], axis=-1, keepdims=True)
            ot_ref[rows, :] = jnp.where(here, ocol, ot_ref[rows, :])
        return carry

    lax.fori_loop(0, tb, step, 0)

    @pl.when(ib == nblk - 1)
    def _():
        s_ref[...] = S[...]


def _rwkv(r, w, k, kk, kka, vt, s0, tb):
    bsz, t = r.shape[:2]
    nblk = t // tb
    x_spec = pl.BlockSpec((None, tb, RWKV_HEADS, RWKV_HEAD_DIM), lambda b, i: (b, i, 0, 0))
    t_spec = pl.BlockSpec((None, RWKV_WIDTH, tb), lambda b, i: (b, 0, i))
    s_spec = pl.BlockSpec((None, RWKV_WIDTH, RWKV_HEAD_DIM), lambda b, i: (b, 0, 0))
    return pl.pallas_call(
        functools.partial(_rwkv_kernel, tb=tb, nblk=nblk),
        grid=(bsz, nblk),
        in_specs=[x_spec] * 5 + [t_spec, s_spec],
        out_specs=[t_spec, s_spec],
        out_shape=[jax.ShapeDtypeStruct((bsz, RWKV_WIDTH, t), F32),
                   jax.ShapeDtypeStruct(s0.shape, F32)],
        scratch_shapes=[pltpu.VMEM((RWKV_WIDTH, RWKV_HEAD_DIM), F32)],
        compiler_params=_cparams(("parallel", "arbitrary")),
        name="rwkv7",
    )(r, w, k, kk, kka, vt, s0)


def _pad_t(a, t_to):
    t = a.shape[1]
    if t == t_to:
        return a
    return jnp.pad(a, [(0, 0), (0, t_to - t)] + [(0, 0)] * (a.ndim - 2))


def _causal_conv(u, buf, w):
    t = u.shape[1]
    full = jnp.concatenate([buf, u], axis=1)
    out = full[:, 0:t] * w[0]
    for i in range(1, CONV_W):
        out = out + full[:, i:i + t] * w[i]
    return out, full[:, t:]


def _rms(x, g, eps=1e-6):
    return x * lax.rsqrt(jnp.mean(x * x, -1, keepdims=True) + eps) * g


def _l2n(x, eps=1e-6):
    return x * lax.rsqrt(jnp.sum(x * x, -1, keepdims=True) + eps)


def _even_mix(h, z, mem_k, mem_v, s_gla, s_ssd, s_conv, P, long_seq):
    bsz, t, _ = h.shape
    seg = lambda i: _seg(h, EVEN_OFF, EVEN_SIZES, i)
    gq, gk, gv, ggate, sz, sxbc, sdt, mq, mgate = (seg(i) for i in (0, 1, 2, 4, 5, 6, 7, 8, 9))
    tp = t if long_seq else SMALL_T
    log_a = jax.nn.log_sigmoid(z) / GLA_TAU
    L = GLA_CHUNK if long_seq else SMALL_T
    o_gla, s_gla = _gla(_pad_t(gq * GLA_DK ** -0.5, tp), _pad_t(gk, tp), _pad_t(gv, tp), _pad_t(log_a, tp),
                        s_gla, L, 256 if long_seq else SMALL_T)
    o_gla = _rms(o_gla[:, :t].reshape(bsz, t, GLA_HEADS, GLA_DV), P['gla_norm']).reshape(bsz, t, GLA_WIDTH)
    o_gla = o_gla * jax.nn.silu(ggate)
    xbc, s_conv = _causal_conv(sxbc, s_conv, P['ssd_conv_w'])
    xbc = jax.nn.silu(xbc + P['ssd_conv_b'])
    sx = xbc[..., :SSD_WIDTH]
    sb = xbc[..., SSD_WIDTH:SSD_WIDTH + SSD_GROUPS * SSD_STATE]
    sc = xbc[..., SSD_WIDTH + SSD_GROUPS * SSD_STATE:]
    dt = jax.nn.softplus(sdt + P['ssd_dt_bias'])
    a = -jnp.exp(P['ssd_a_log'])
    xh = sx.reshape(bsz, t, SSD_HEADS, SSD_HEAD_DIM)
    xdt = (xh * dt[..., None]).reshape(bsz, t, SSD_WIDTH)
    L = SSD_CHUNK if long_seq else SMALL_T
    y, s_ssd = _ssd(_pad_t(xdt, tp), _pad_t(sb, tp), _pad_t(sc, tp), _pad_t(dt * a, tp), s_ssd, L)
    y = y[:, :t].reshape(bsz, t, SSD_HEADS, SSD_HEAD_DIM) + xh * P['ssd_d'][:, None]
    y = (y.reshape(bsz, t, SSD_WIDTH) * jax.nn.silu(sz)).reshape(bsz, t, SSD_GROUPS, SSD_WIDTH // SSD_GROUPS)
    y = _rms(y, P['ssd_norm'].reshape(SSD_GROUPS, SSD_WIDTH // SSD_GROUPS)).reshape(bsz, t, SSD_WIDTH)
    o_mem = _mem_attention(mq, mem_k, mem_v, 512 if long_seq else t) * jax.nn.silu(mgate)
    mix = jnp.concatenate([o_gla, y, o_mem], axis=-1).astype(BF16)
    return mix, s_gla, s_ssd, s_conv


def _odd_mix(h, wa, mem_k, mem_v, s_gdn, s_conv, s_rwkv, s_shift, P, long_seq):
    bsz, t, _ = h.shape
    seg = lambda i: _seg(h, ODD_OFF, ODD_SIZES, i)
    cqkv, cz, cb, ca, rsh, rgate, mq, mgate = (seg(i) for i in range(8))
    tp = t if long_seq else SMALL_T
    qkv, s_conv = _causal_conv(cqkv, s_conv, P['gdn_conv_w'])
    q, k, v = jnp.split(jax.nn.silu(qkv), 3, axis=-1)
    hd = (bsz, t, GDN_HEADS, GDN_HEAD_DIM)
    q = (_l2n(q.reshape(hd)) * GDN_HEAD_DIM ** -0.5).reshape(bsz, t, GDN_WIDTH)
    k = _l2n(k.reshape(hd)).reshape(bsz, t, GDN_WIDTH)
    beta = jax.nn.sigmoid(cb)
    g = -jnp.exp(P['gdn_a_log']) * jax.nn.softplus(ca + P['gdn_dt_bias'])
    L = GDN_CHUNK if long_seq else SMALL_T
    o_gdn, s_gdn = _gdn(_pad_t(q, tp), _pad_t(k, tp), _pad_t(v, tp), _pad_t(g, tp), _pad_t(beta, tp), s_gdn, L)
    o_gdn = _rms(o_gdn[:, :t].reshape(hd), P['gdn_norm']).reshape(bsz, t, GDN_WIDTH) * jax.nn.silu(cz)
    prev = jnp.concatenate([s_shift[:, None], rsh[:, :-1]], axis=1)
    s_shift = rsh[:, -1]
    rs = rsh + (prev - rsh) * P['rwkv_mu']
    r, k7, v7, xw, xa = jnp.split(rs, np.cumsum((RWKV_WIDTH,) * 3 + (RWKV_W_RANK,)).tolist(), axis=-1)
    lowrank = wa(jnp.concatenate([jnp.tanh(xw), xa], axis=-1))
    w_log = -jax.nn.softplus(-(P['rwkv_w0'] + lowrank[..., :RWKV_WIDTH])) - 0.5
    w = jnp.exp(-jnp.exp(w_log))
    a7 = jax.nn.sigmoid(P['rwkv_a0'] + lowrank[..., RWKV_WIDTH:])
    hd7 = (bsz, t, RWKV_HEADS, RWKV_HEAD_DIM)
    kk = _l2n((k7 * P['rwkv_kk']).reshape(hd7))
    k7 = (k7 * (1.0 + (a7 - 1.0) * P['rwkv_ka'])).reshape(hd7)
    r = r.reshape(hd7)
    v7 = v7.reshape(hd7)
    vt = jnp.swapaxes(v7.reshape(bsz, t, RWKV_WIDTH), 1, 2)
    ot, s_rwkv = _rwkv(r, w.reshape(hd7), k7, kk, kk * a7.reshape(hd7), vt,
                       s_rwkv.reshape(bsz, RWKV_WIDTH, RWKV_HEAD_DIM), 128 if long_seq else t)
    o7 = jnp.swapaxes(ot, 1, 2).reshape(hd7)
    s_rwkv = s_rwkv.reshape(bsz, RWKV_HEADS, RWKV_HEAD_DIM, RWKV_HEAD_DIM)
    bonus = jnp.sum(r * k7 * P['rwkv_rk'], -1, keepdims=True) * v7
    oc = o7 - jnp.mean(o7, -1, keepdims=True)
    gn = oc * lax.rsqrt(jnp.mean(oc * oc, -1, keepdims=True) + RWKV_GN_EPS)
    o7 = gn.reshape(bsz, t, RWKV_WIDTH) * P['rwkv_ln_g'] + P['rwkv_ln_b'] + bonus.reshape(bsz, t, RWKV_WIDTH)
    o7 = o7 * jax.nn.silu(rgate)
    o_mem = _mem_attention(mq, mem_k, mem_v, 512 if long_seq else t) * jax.nn.silu(mgate)
    mix = jnp.concatenate([o_gdn, o7, o_mem], axis=-1).astype(BF16)
    return mix, s_gdn, s_conv, s_rwkv, s_shift


def kernel(x_prompt, x_sample, mem_prompt, cache_mem_k, cache_mem_v, state_gla, state_ssd, state_ssd_conv, state_gdn, state_gdn_conv, state_rwkv, state_rwkv_shift, mem_w_kv, ev_w_in, ev_gla_w2, ev_gla_b, ev_gla_norm, ev_ssd_conv_w, ev_ssd_conv_b, ev_ssd_dt_bias, ev_ssd_a_log, ev_ssd_d, ev_ssd_norm, ev_w_out, ev_ln_g, ev_ln_b, od_w_in, od_gdn_conv_w, od_gdn_dt_bias, od_gdn_a_log, od_gdn_norm, od_rwkv_mu, od_rwkv_w0, od_rwkv_w2, od_rwkv_a0, od_rwkv_a2, od_rwkv_kk, od_rwkv_ka, od_rwkv_rk, od_rwkv_ln_g, od_rwkv_ln_b, od_w_out, od_ln_g, od_ln_b):
    ev = dict(w_in=ev_w_in, gla_w2=ev_gla_w2, gla_b=ev_gla_b, gla_norm=ev_gla_norm,
              ssd_conv_w=ev_ssd_conv_w, ssd_conv_b=ev_ssd_conv_b, ssd_dt_bias=ev_ssd_dt_bias,
              ssd_a_log=ev_ssd_a_log, ssd_d=ev_ssd_d, ssd_norm=ev_ssd_norm,
              w_out=ev_w_out, ln_g=ev_ln_g, ln_b=ev_ln_b)
    od = dict(w_in=od_w_in, gdn_conv_w=od_gdn_conv_w, gdn_dt_bias=od_gdn_dt_bias, gdn_a_log=od_gdn_a_log,
              gdn_norm=od_gdn_norm, rwkv_mu=od_rwkv_mu, rwkv_w0=od_rwkv_w0, rwkv_w2=od_rwkv_w2,
              rwkv_a0=od_rwkv_a0, rwkv_a2=od_rwkv_a2, rwkv_kk=od_rwkv_kk, rwkv_ka=od_rwkv_ka,
              rwkv_rk=od_rwkv_rk, rwkv_ln_g=od_rwkv_ln_g, rwkv_ln_b=od_rwkv_ln_b,
              w_out=od_w_out, ln_g=od_ln_g, ln_b=od_ln_b)
    bp, tp, _ = x_prompt.shape
    bs, ts, _ = x_sample.shape
    mp, ms = bp * tp, bs * ts
    n_pairs = DEPTH // 2

    w_kv = jnp.moveaxis(mem_w_kv, 0, 1).reshape(D_MODEL, DEPTH * 2 * MEM_WIDTH).astype(BF16)
    kv = _matmul(mem_prompt.reshape(bp * MEM_LEN, D_MODEL).astype(BF16), w_kv, 512, 1024)
    kv = kv.reshape(bp, MEM_LEN, DEPTH, 2, MEM_HEADS, MEM_HEAD_DIM)
    mem_k_p = jnp.moveaxis(kv[:, :, :, 0], 2, 0)
    mem_v_p = jnp.moveaxis(kv[:, :, :, 1], 2, 0)
    flat = lambda a: a.reshape(a.shape[:-2] + (MEM_WIDTH,))

    x = jnp.concatenate([x_prompt.reshape(mp, D_MODEL), x_sample.reshape(ms, D_MODEL)], axis=0)
    x_bf = x.astype(BF16)
    zp = lambda shape: jnp.zeros(shape, F32)
    new_p = {n: [] for n in ('gla', 'ssd', 'ssd_conv', 'gdn', 'gdn_conv', 'rwkv', 'rwkv_shift')}
    new_s = {n: [] for n in new_p}
    tm = 512
    for layer in range(DEPTH):
        p = layer // 2
        if layer % 2 == 0:
            P = {n: w[p] for n, w in ev.items()}
            h = _matmul(x_bf, _pack_w_in(P['w_in'], EVEN_SIZES, EVEN_ORDER, EVEN_N), tm, PROJ_TN)
            glr = _seg(h, EVEN_OFF, EVEN_SIZES, 3)
            z = _matmul(glr, P['gla_w2'], tm, GLA_QK, precision=HI) + P['gla_b']
            mix_p, s1, s2, s3 = _even_mix(
                h[:mp].reshape(bp, tp, EVEN_N), z[:mp].reshape(bp, tp, GLA_QK),
                flat(mem_k_p[layer]), flat(mem_v_p[layer]),
                zp((bp, GLA_HEADS, GLA_DK, GLA_DV)), zp((bp, SSD_HEADS, SSD_HEAD_DIM, SSD_STATE)),
                zp((bp, CONV_W - 1, SSD_CONV_DIM)), P, True)
            mix_s, t1, t2, t3 = _even_mix(
                h[mp:].reshape(bs, ts, EVEN_N), z[mp:].reshape(bs, ts, GLA_QK),
                flat(cache_mem_k[layer]), flat(cache_mem_v[layer]),
                state_gla[p], state_ssd[p], state_ssd_conv[p], P, False)
            for name, a, b in (('gla', s1, t1), ('ssd', s2, t2), ('ssd_conv', s3, t3)):
                new_p[name].append(a)
                new_s[name].append(b)
        else:
            P = {n: w[p] for n, w in od.items()}
            h = _matmul(x_bf, _pack_w_in(P['w_in'], ODD_SIZES, ODD_ORDER, ODD_N), tm, PROJ_TN)
            w_lr = jnp.zeros((RWKV_W_RANK + RWKV_A_RANK, 2 * RWKV_WIDTH), F32)
            w_lr = w_lr.at[:RWKV_W_RANK, :RWKV_WIDTH].set(P['rwkv_w2'])
            w_lr = w_lr.at[RWKV_W_RANK:, RWKV_WIDTH:].set(P['rwkv_a2'])

            def wa(u, w_lr=w_lr):
                b_, t_, c_ = u.shape
                return _matmul(u.reshape(b_ * t_, c_), w_lr, 512, 1024, precision=HI).reshape(b_, t_, -1)

            mix_p, s1, s2, s3, s4 = _odd_mix(
                h[:mp].reshape(bp, tp, ODD_N), wa, flat(mem_k_p[layer]), flat(mem_v_p[layer]),
                zp((bp, GDN_HEADS, GDN_HEAD_DIM, GDN_HEAD_DIM)), zp((bp, CONV_W - 1, 3 * GDN_WIDTH)),
                zp((bp, RWKV_HEADS, RWKV_HEAD_DIM, RWKV_HEAD_DIM)), zp((bp, RWKV_SHIFT_DIM)), P, True)
            mix_s, t1, t2, t3, t4 = _odd_mix(
                h[mp:].reshape(bs, ts, ODD_N), wa, flat(cache_mem_k[layer]), flat(cache_mem_v[layer]),
                state_gdn[p], state_gdn_conv[p], state_rwkv[p], state_rwkv_shift[p], P, False)
            for name, a, b in (('gdn', s1, t1), ('gdn_conv', s2, t2), ('rwkv', s3, t3), ('rwkv_shift', s4, t4)):
                new_p[name].append(a)
                new_s[name].append(b)
        mix = jnp.concatenate([mix_p.reshape(mp, MIX_WIDTH), mix_s.reshape(ms, MIX_WIDTH)], axis=0)
        x, x_bf = _out_ln(mix, P['w_out'].astype(BF16), x, P['ln_g'], P['ln_b'])

    y_prompt = x[:mp].reshape(bp, tp, D_MODEL)
    y_sample = x[mp:].reshape(bs, ts, D_MODEL)
    st = lambda d, n: jnp.stack(d[n])
    return (y_prompt, y_sample, mem_k_p, mem_v_p,
            st(new_p, 'gla'), st(new_s, 'gla'), st(new_p, 'ssd'), st(new_s, 'ssd'),
            st(new_p, 'ssd_conv'), st(new_s, 'ssd_conv'), st(new_p, 'gdn'), st(new_s, 'gdn'),
            st(new_p, 'gdn_conv'), st(new_s, 'gdn_conv'), st(new_p, 'rwkv'), st(new_s, 'rwkv'),
            st(new_p, 'rwkv_shift'), st(new_s, 'rwkv_shift'))
```

```python
import functools
import math

import numpy as np
import jax
import jax.numpy as jnp
from jax import lax
from jax.experimental import pallas as pl
from jax.experimental.pallas import tpu as pltpu

F32 = jnp.float32
BF16 = jnp.bfloat16
HI = lax.Precision.HIGHEST

D_MODEL = 2048
DEPTH = 4
N_PAIRS = DEPTH // 2
CONV_W = 4
MEM_LEN = 256
MEM_HEADS = 4
MEM_HEAD_DIM = 256
MEM_WIDTH = 1024
GLA_HEADS = 4
GLA_DK = 128
GLA_DV = 256
GLA_QK = 512
GLA_WIDTH = 1024
GLA_RANK = 16
GLA_TAU = 16.0
SSD_WIDTH = 2048
SSD_HEAD_DIM = 64
SSD_HEADS = 32
SSD_GROUPS = 4
SSD_REP = 8
SSD_STATE = 128
SSD_CONV_DIM = SSD_WIDTH + 2 * SSD_GROUPS * SSD_STATE
GDN_WIDTH = 2048
GDN_HEAD_DIM = 128
GDN_HEADS = 16
RWKV_WIDTH = 1024
RWKV_HEAD_DIM = 64
RWKV_HEADS = 16
RWKV_W_RANK = 64
RWKV_A_RANK = 64
RWKV_SHIFT_DIM = 3 * RWKV_WIDTH + RWKV_W_RANK + RWKV_A_RANK
RWKV_GN_EPS = 64e-5
EVEN_SIZES = (GLA_QK, GLA_QK, GLA_WIDTH, GLA_RANK, GLA_WIDTH, SSD_WIDTH, SSD_CONV_DIM, SSD_HEADS,
              MEM_WIDTH, MEM_WIDTH)
ODD_SIZES = (3 * GDN_WIDTH, GDN_WIDTH, GDN_HEADS, GDN_HEADS, RWKV_SHIFT_DIM, RWKV_WIDTH, MEM_WIDTH, MEM_WIDTH)
MIX_WIDTH = 4096
DEEPNORM_ALPHA = (2 * DEPTH) ** 0.25

EVEN_ORDER = (0, 1, 2, 4, 5, 6, 8, 9, 3, 7)
ODD_ORDER = (0, 1, 4, 5, 6, 7, 2, 3)
PROJ_TN = 768
VMEM_LIMIT = 48 * 1024 * 1024

GLA_CHUNK = 16
SSD_CHUNK = 64
GDN_CHUNK = 64
RWKV_CHUNK = 64
SMALL_T = 8

_NN = ((1,), (0,))
_NT = ((1,), (1,))
_TN = ((0,), (0,))


def _packed_layout(sizes, order):
    offs, o = {}, 0
    for i in order:
        offs[i] = o
        o += sizes[i]
    total = -(-o // PROJ_TN) * PROJ_TN
    return offs, total


EVEN_OFF, EVEN_N = _packed_layout(EVEN_SIZES, EVEN_ORDER)
ODD_OFF, ODD_N = _packed_layout(ODD_SIZES, ODD_ORDER)


def _pack_w_in(w, sizes, order, total):
    segs = jnp.split(w, np.cumsum(sizes)[:-1].tolist(), axis=-1)
    parts = [segs[i] for i in order]
    used = sum(sizes)
    if total > used:
        parts.append(jnp.zeros((w.shape[0], total - used), w.dtype))
    return jnp.concatenate(parts, axis=-1).astype(BF16)


def _seg(h, offs, sizes, i):
    return h[..., offs[i]:offs[i] + sizes[i]]


def _cparams(sem):
    return pltpu.CompilerParams(dimension_semantics=sem, vmem_limit_bytes=VMEM_LIMIT)


def _mxu(a, b, dims):
    return lax.dot_general(a.astype(BF16), b.astype(BF16), (dims, ((), ())), preferred_element_type=F32)


def _mxu_f32(a, b, dims):
    return lax.dot_general(a, b, (dims, ((), ())), precision=HI, preferred_element_type=F32)


def _neumann_inverse(nn, tt, n, rounds):
    idx = range(len(nn))
    if rounds >= 2:
        for j in idx:
            nn[j] = _mxu(nn[j], nn[j], _NN)
        for _ in range(rounds - 2):
            for j in idx:
                both = _mxu(nn[j], jnp.concatenate([tt[j], nn[j]], axis=1), _NN)
                tt[j] = tt[j] + both[:, :n]
                nn[j] = both[:, n:]
        for j in idx:
            tt[j] = tt[j] + _mxu(nn[j], tt[j], _NN)
    return tt


def _state_io(tail, p, s_in, s_prev, bsz):
    zeros = (0,) * len(tail)
    spec = pl.BlockSpec((None, None) + tail, lambda *g: (p, g[0]) + zeros)
    ins, specs = [], []
    if s_in is not None:
        ins.append(s_in)
        specs.append(spec)
    if s_prev is not None:
        ins.append(s_prev)
        specs.append(pl.BlockSpec(memory_space=pl.ANY))
    shape = jax.ShapeDtypeStruct((N_PAIRS, bsz) + tail, F32)
    return ins, specs, spec, shape


def _mm_kernel(x_ref, w_ref, o_ref, *, precision):
    o_ref[...] = jnp.dot(x_ref[...], w_ref[...], preferred_element_type=F32, precision=precision)


def _matmul(x, w, tm, tn, precision=None):
    m, k = x.shape
    n = w.shape[1]
    assert m % tm == 0 and n % tn == 0
    return pl.pallas_call(
        functools.partial(_mm_kernel, precision=precision),
        grid=(n // tn, m // tm),
        in_specs=[pl.BlockSpec((tm, k), lambda j, i: (i, 0)),
                  pl.BlockSpec((k, tn), lambda j, i: (0, j))],
        out_specs=pl.BlockSpec((tm, tn), lambda j, i: (i, j)),
        out_shape=jax.ShapeDtypeStruct((m, n), F32),
        compiler_params=_cparams(("parallel", "parallel")),
        name="matmul",
    )(x, w)


def _out_ln_kernel(mix_ref, w_ref, x_ref, g_ref, b_ref, y_ref, ybf_ref, acc, *, nk):
    kk = pl.program_id(1)

    @pl.when(kk == 0)
    def _():
        acc[...] = jnp.zeros_like(acc)

    acc[...] += jnp.dot(mix_ref[...], w_ref[...], preferred_element_type=F32)

    @pl.when(kk == nk - 1)
    def _():
        z = DEEPNORM_ALPHA * x_ref[...] + acc[...]
        zc = z - jnp.mean(z, axis=-1, keepdims=True)
        var = jnp.mean(zc * zc, axis=-1, keepdims=True)
        y = zc * lax.rsqrt(var + 1e-5) * g_ref[...] + b_ref[...]
        y_ref[...] = y
        ybf_ref[...] = y.astype(BF16)


def _out_ln(mix, w, x, g, b, tm=512, tk=1024):
    m, k = mix.shape
    d = w.shape[1]
    nk = k // tk
    return pl.pallas_call(
        functools.partial(_out_ln_kernel, nk=nk),
        grid=(m // tm, nk),
        in_specs=[pl.BlockSpec((tm, tk), lambda i, j: (i, j)),
                  pl.BlockSpec((tk, d), lambda i, j: (j, 0)),
                  pl.BlockSpec((tm, d), lambda i, j: (i, 0)),
                  pl.BlockSpec((1, d), lambda i, j: (0, 0)),
                  pl.BlockSpec((1, d), lambda i, j: (0, 0))],
        out_specs=[pl.BlockSpec((tm, d), lambda i, j: (i, 0)),
                   pl.BlockSpec((tm, d), lambda i, j: (i, 0))],
        out_shape=[jax.ShapeDtypeStruct((m, d), F32), jax.ShapeDtypeStruct((m, d), BF16)],
        scratch_shapes=[pltpu.VMEM((tm, d), F32)],
        compiler_params=_cparams(("parallel", "arbitrary")),
        name="out_ln",
    )(mix, w, x, g.reshape(1, d), b.reshape(1, d))


def _mem_kernel(q_ref, k_ref, v_ref, o_ref):
    for h in range(MEM_HEADS):
        sl = slice(h * MEM_HEAD_DIM, (h + 1) * MEM_HEAD_DIM)
        s = _mxu(q_ref[:, sl], k_ref[:, sl], _NT) * MEM_HEAD_DIM ** -0.5
        p = jnp.exp(s - jnp.max(s, axis=-1, keepdims=True))
        p = p / jnp.sum(p, axis=-1, keepdims=True)
        o_ref[:, sl] = _mxu(p, v_ref[:, sl], _NN)


def _mem_attention(mq, mem_k, mem_v, layer, tq):
    bsz, t, _ = mq.shape
    kv_spec = pl.BlockSpec((None, None, MEM_LEN, MEM_WIDTH), lambda b, i: (layer, b, 0, 0))
    return pl.pallas_call(
        _mem_kernel,
        grid=(bsz, t // tq),
        in_specs=[pl.BlockSpec((None, tq, MEM_WIDTH), lambda b, i: (b, i, 0)), kv_spec, kv_spec],
        out_specs=pl.BlockSpec((None, tq, MEM_WIDTH), lambda b, i: (b, i, 0)),
        out_shape=jax.ShapeDtypeStruct((bsz, t, MEM_WIDTH), F32),
        compiler_params=_cparams(("parallel", "parallel")),
        name="mem_attention",
    )(mq, mem_k, mem_v)


def _gla_kernel(*refs, L, nchunk, nblk, has_s0):
    q_ref, k_ref, v_ref, g_ref = refs[:4]
    o_ref, s_ref, S = refs[-3:]
    tb = pl.program_id(1)

    @pl.when(tb == 0)
    def _():
        S[...] = refs[4][...] if has_s0 else jnp.zeros_like(S)

    row = lax.broadcasted_iota(jnp.int32, (L, L), 0)
    col = lax.broadcasted_iota(jnp.int32, (L, L), 1)
    tril = (col <= row).astype(F32)
    t_iota = lax.broadcasted_iota(jnp.int32, (L, GLA_DK), 0)
    r128 = lax.broadcasted_iota(jnp.int32, (GLA_DK, GLA_DK), 0)
    c128 = lax.broadcasted_iota(jnp.int32, (GLA_DK, GLA_DK), 1)

    def chunk(ci, carry):
        r0 = pl.multiple_of(ci * L, L)
        for h in range(GLA_HEADS):
            ks = slice(h * GLA_DK, (h + 1) * GLA_DK)
            vs = slice(h * GLA_DV, (h + 1) * GLA_DV)
            q = q_ref[pl.ds(r0, L), ks]
            k = k_ref[pl.ds(r0, L), ks]
            g = g_ref[pl.ds(r0, L), ks]
            v = v_ref[pl.ds(r0, L), vs]
            b = _mxu_f32(tril, g, _NN)
            b_last = b[L - 1:L, :]
            att = jnp.zeros((L, L), F32)
            for s in range(L):
                d = jnp.where(t_iota >= s, b - b[s:s + 1, :], -jnp.inf)
                p = jnp.exp(d) * q * k[s:s + 1, :]
                att = jnp.where(col == s, jnp.sum(p, axis=-1, keepdims=True), att)
            s_old = S[h]
            o = _mxu_f32(q * jnp.exp(b), s_old, _NN) + _mxu_f32(att, v, _NN)
            o_ref[pl.ds(r0, L), vs] = o
            kd = k * jnp.exp(b_last - b)
            e_col = jnp.sum(jnp.where(r128 == c128, jnp.exp(b_last), 0.0), axis=-1, keepdims=True)
            S[h] = s_old * e_col + _mxu_f32(kd, v, _TN)
        return carry

    lax.fori_loop(0, nchunk, chunk, 0)

    @pl.when(tb == nblk - 1)
    def _():
        s_ref[...] = S[...]


def _gla(q, k, v, g, s_in, s_prev, p, L, tb):
    bsz, t, _ = q.shape
    nblk = t // tb
    qk_spec = pl.BlockSpec((None, tb, GLA_QK), lambda b, i: (b, i, 0))
    v_spec = pl.BlockSpec((None, tb, GLA_WIDTH), lambda b, i: (b, i, 0))
    tail = (GLA_HEADS, GLA_DK, GLA_DV)
    s_ins, s_specs, s_out, s_shape = _state_io(tail, p, s_in, s_prev, bsz)
    alias = {} if s_prev is None else {4 + len(s_ins) - 1: 1}
    return pl.pallas_call(
        functools.partial(_gla_kernel, L=L, nchunk=tb // L, nblk=nblk, has_s0=s_in is not None),
        grid=(bsz, nblk),
        in_specs=[qk_spec, qk_spec, v_spec, qk_spec] + s_specs,
        out_specs=[v_spec, s_out],
        out_shape=[jax.ShapeDtypeStruct((bsz, t, GLA_WIDTH), F32), s_shape],
        scratch_shapes=[pltpu.VMEM(tail, F32)],
        input_output_aliases=alias,
        compiler_params=_cparams(("parallel", "arbitrary")),
        name="gla",
    )(q, k, v, g, *s_ins)


def _ssd_kernel(*refs, L, nchunk, has_s0):
    x_ref, b_ref, c_ref, da_ref = refs[:4]
    y_ref, s_ref, S = refs[-3:]
    grp = pl.program_id(1)
    ci = pl.program_id(2)

    @pl.when(ci == 0)
    def _():
        S[...] = refs[4][...] if has_s0 else jnp.zeros_like(S)

    row = lax.broadcasted_iota(jnp.int32, (L, L), 0)
    col = lax.broadcasted_iota(jnp.int32, (L, L), 1)
    causal = col <= row
    tril = causal.astype(F32)
    lane = lax.broadcasted_iota(jnp.int32, (L, SSD_HEADS), 1)
    bm = b_ref[...]
    cm = c_ref[...]
    cb = _mxu_f32(cm, bm, _NT)
    da_all = da_ref[...]
    c_all = _mxu_f32(tril, da_all, _NN)
    for r in range(SSD_REP):
        sel = lane == grp * SSD_REP + r
        c_col = jnp.sum(jnp.where(sel, c_all, 0.0), axis=-1, keepdims=True)
        d_col = jnp.sum(jnp.where(sel, da_all, 0.0), axis=-1, keepdims=True)
        c_row = jnp.sum(jnp.where(row <= col, d_col, 0.0), axis=0, keepdims=True)
        seg = jnp.exp(jnp.where(causal, c_col - c_row, -jnp.inf))
        hs = slice(r * SSD_HEAD_DIM, (r + 1) * SSD_HEAD_DIM)
        xr = x_ref[:, hs]
        s_old = S[r]
        y = _mxu_f32(cb * seg, xr, _NN) + _mxu_f32(cm, s_old, _NT) * jnp.exp(c_col)
        y_ref[:, hs] = y
        c_last = c_col[L - 1:L, :]
        S[r] = s_old * jnp.exp(c_last) + _mxu_f32(xr * jnp.exp(c_last - c_col), bm, _TN)

    @pl.when(ci == nchunk - 1)
    def _():
        s_ref[...] = S[...]


def _ssd(xdt, bm, cm, da, s_in, s_prev, p, L):
    bsz, t, _ = xdt.shape
    nchunk = t // L
    gw = SSD_REP * SSD_HEAD_DIM
    x_spec = pl.BlockSpec((None, L, gw), lambda b, g, c: (b, c, g))
    bc_spec = pl.BlockSpec((None, L, SSD_STATE), lambda b, g, c: (b, c, g))
    blk = (None, None, SSD_REP, SSD_HEAD_DIM, SSD_STATE)
    s_spec = pl.BlockSpec(blk, lambda b, g, c: (p, b, g, 0, 0))
    s_ins, s_specs = [], []
    if s_in is not None:
        s_ins.append(s_in)
        s_specs.append(s_spec)
    if s_prev is not None:
        s_ins.append(s_prev)
        s_specs.append(pl.BlockSpec(memory_space=pl.ANY))
    alias = {} if s_prev is None else {4 + len(s_ins) - 1: 1}
    return pl.pallas_call(
        functools.partial(_ssd_kernel, L=L, nchunk=nchunk, has_s0=s_in is not None),
        grid=(bsz, SSD_GROUPS, nchunk),
        in_specs=[x_spec, bc_spec, bc_spec,
                  pl.BlockSpec((None, L, SSD_HEADS), lambda b, g, c: (b, c, 0))] + s_specs,
        out_specs=[x_spec, s_spec],
        out_shape=[jax.ShapeDtypeStruct((bsz, t, SSD_WIDTH), F32),
                   jax.ShapeDtypeStruct((N_PAIRS, bsz, SSD_HEADS, SSD_HEAD_DIM, SSD_STATE), F32)],
        scratch_shapes=[pltpu.VMEM((SSD_REP, SSD_HEAD_DIM, SSD_STATE), F32)],
        input_output_aliases=alias,
        compiler_params=_cparams(("parallel", "parallel", "arbitrary")),
        name="ssd",
    )(xdt, bm, cm, da, *s_ins)


def _gdn_kernel(*refs, L, nchunk, rounds, has_s0):
    q_ref, k_ref, v_ref, g_ref, beta_ref = refs[:5]
    o_ref, s_ref, S = refs[-3:]
    ci = pl.program_id(1)

    @pl.when(ci == 0)
    def _():
        S[...] = refs[5][...] if has_s0 else jnp.zeros_like(S)

    n2 = 2 * L
    d = GDN_HEAD_DIM
    cat = jnp.concatenate
    row = lax.broadcasted_iota(jnp.int32, (L, L), 0)
    col = lax.broadcasted_iota(jnp.int32, (L, L), 1)
    tril = (col <= row).astype(F32)
    r2 = lax.broadcasted_iota(jnp.int32, (n2, n2), 0)
    c2 = lax.broadcasted_iota(jnp.int32, (n2, n2), 1)
    same = (r2 >= L) == (c2 >= L)
    strict = same & (c2 < r2)
    incl = same & (c2 <= r2)
    upper = same & (r2 <= c2)
    eye = (r2 == c2).astype(F32)
    zl = jnp.zeros((L, d), F32)
    g_all = g_ref[...]
    beta_all = beta_ref[...]
    c_all = _mxu_f32(tril, g_all, _NN)
    pairs = range(GDN_HEADS // 2)
    nn, tt, qk, kq, kdec, ec, bcol, elast = [], [], [], [], [], [], [], []
    for j in pairs:
        h0, h1 = 2 * j, 2 * j + 1
        stack_col = lambda a: cat([a[:, h0:h0 + 1], a[:, h1:h1 + 1]], axis=0)
        c_col = stack_col(c_all)
        beta_col = stack_col(beta_all)
        c_row = jnp.sum(jnp.where(upper, stack_col(g_all), 0.0), axis=0, keepdims=True)
        decay = jnp.exp(jnp.where(incl, c_col - c_row, -jnp.inf))
        last = lambda rows: cat([jnp.broadcast_to(c_all[L - 1:L, h0:h0 + 1], (rows, 1)),
                                 jnp.broadcast_to(c_all[L - 1:L, h1:h1 + 1], (rows, 1))], axis=0)
        s0, s1 = slice(h0 * d, h1 * d), slice(h1 * d, (h1 + 1) * d)
        k_st = cat([cat([k_ref[:, s0], zl], axis=1), cat([zl, k_ref[:, s1]], axis=1)], axis=0)
        q_st = cat([cat([q_ref[:, s0], zl], axis=1), cat([zl, q_ref[:, s1]], axis=1)], axis=0)
        both = cat([k_st, q_st], axis=0)
        full = _mxu(both, k_st, _NT)
        a = jnp.where(strict, full[:n2] * decay * beta_col, 0.0)
        nn.append(-a)
        tt.append(eye - a)
        qk.append(full[n2:] * decay)
        kq.append(both)
        kdec.append(k_st * jnp.exp(last(L) - c_col))
        ec.append(jnp.exp(c_col))
        bcol.append(beta_col)
        elast.append(jnp.exp(last(d)))
    tt = _neumann_inverse(nn, tt, n2, rounds)
    s_old, ksqs, u = [], [], []
    for j in pairs:
        s_old.append(cat([S[2 * j], S[2 * j + 1]], axis=0))
        ksqs.append(_mxu(kq[j], s_old[j], _NN))
    for j in pairs:
        s0, s1 = slice(2 * j * d, (2 * j + 1) * d), slice((2 * j + 1) * d, (2 * j + 2) * d)
        v_st = cat([v_ref[:, s0], v_ref[:, s1]], axis=0)
        u.append(_mxu(tt[j], bcol[j] * (v_st - ec[j] * ksqs[j][:n2]), _NN))
    for j in pairs:
        s0, s1 = slice(2 * j * d, (2 * j + 1) * d), slice((2 * j + 1) * d, (2 * j + 2) * d)
        o = ec[j] * ksqs[j][n2:] + _mxu(qk[j], u[j], _NN)
        o_ref[:, s0] = o[:L]
        o_ref[:, s1] = o[L:]
        new = s_old[j] * elast[j] + _mxu(kdec[j], u[j], _TN)
        S[2 * j] = new[:d]
        S[2 * j + 1] = new[d:]

    @pl.when(ci == nchunk - 1)
    def _():
        s_ref[...] = S[...]


def _gdn(q, k, v, g, beta, s_in, s_prev, p, L):
    bsz, t, _ = q.shape
    nchunk = t // L
    x_spec = pl.BlockSpec((None, L, GDN_WIDTH), lambda b, c: (b, c, 0))
    gb_spec = pl.BlockSpec((None, L, GDN_HEADS), lambda b, c: (b, c, 0))
    tail = (GDN_HEADS, GDN_HEAD_DIM, GDN_HEAD_DIM)
    s_ins, s_specs, s_out, s_shape = _state_io(tail, p, s_in, s_prev, bsz)
    alias = {} if s_prev is None else {5 + len(s_ins) - 1: 1}
    return pl.pallas_call(
        functools.partial(_gdn_kernel, L=L, nchunk=nchunk, rounds=int(math.log2(L)), has_s0=s_in is not None),
        grid=(bsz, nchunk),
        in_specs=[x_spec, x_spec, x_spec, gb_spec, gb_spec] + s_specs,
        out_specs=[x_spec, s_out],
        out_shape=[jax.ShapeDtypeStruct((bsz, t, GDN_WIDTH), F32), s_shape],
        scratch_shapes=[pltpu.VMEM(tail, F32)],
        input_output_aliases=alias,
        compiler_params=_cparams(("parallel", "arbitrary")),
        name="gdn",
    )(q, k, v, g, beta, *s_ins)


RWKV_PAIRS = RWKV_HEADS // 2
RWKV_PW = 2 * RWKV_HEAD_DIM


def _rwkv_kernel(*refs, L, nchunk, rounds, has_s0):
    r_ref, lw_ref, k_ref, v_ref, kk_ref, b_ref = refs[:6]
    o_ref, s_ref, S = refs[-3:]
    ci = pl.program_id(1)
    n = RWKV_HEAD_DIM
    cat = jnp.concatenate
    pairs = range(RWKV_PAIRS)

    @pl.when(ci == 0)
    def _():
        if has_s0:
            zn = jnp.zeros((n, n), F32)
            for j in pairs:
                S[j] = cat([cat([refs[6][2 * j], zn], axis=1), cat([zn, refs[6][2 * j + 1]], axis=1)], axis=0)
        else:
            S[...] = jnp.zeros_like(S)

    n2 = 2 * L
    row = lax.broadcasted_iota(jnp.int32, (L, L), 0)
    col = lax.broadcasted_iota(jnp.int32, (L, L), 1)
    tril = (col <= row).astype(F32)
    r2 = lax.broadcasted_iota(jnp.int32, (n2, n2), 0)
    c2 = lax.broadcasted_iota(jnp.int32, (n2, n2), 1)
    same = (r2 >= L) == (c2 >= L)
    strict = same & (c2 < r2)
    incl = same & (c2 <= r2)
    eye = (r2 == c2).astype(F32)
    lane = lax.broadcasted_iota(jnp.int32, (L, RWKV_PW), 1)
    lo = lane < n

    def stack(x):
        return cat([jnp.where(lo, x, 0.0), jnp.where(lo, 0.0, x)], axis=0)

    lw_all = lw_ref[...]
    g_all = _mxu_f32(tril, lw_all, _NN)
    a_ak, a_rk, a_rb, nn, tt, sread, kdbd, egl = [], [], [], [], [], [], [], []
    for j in pairs:
        sl = slice(j * RWKV_PW, (j + 1) * RWKV_PW)
        g = g_all[:, sl]
        gp = g - lw_all[:, sl]
        gm = g[L // 2 - 1:L // 2, :]
        gl = g[L - 1:L, :]
        r = r_ref[:, sl]
        k = k_ref[:, sl]
        kk = kk_ref[:, sl]
        b = b_ref[:, sl]
        e_neg = jnp.exp(gm - g)
        lhs = cat([stack(kk * jnp.exp(gp - gm)), stack(r * jnp.exp(g - gm))], axis=0)
        rhs = cat([stack(b * e_neg), stack(k * e_neg)], axis=0)
        full = _mxu(lhs, rhs, _NT)
        a_ab = jnp.where(strict, full[:n2, :n2], 0.0)
        a_ak.append(jnp.where(strict, full[:n2, n2:], 0.0))
        a_rb.append(jnp.where(incl, full[n2:, :n2], 0.0))
        a_rk.append(jnp.where(incl, full[n2:, n2:], 0.0))
        nn.append(-a_ab)
        tt.append(eye - a_ab)
        sread.append(cat([stack(kk * jnp.exp(gp)), stack(r * jnp.exp(g))], axis=0))
        dec = jnp.exp(gl - g)
        kdbd.append(cat([stack(k * dec), stack(-b * dec)], axis=0))
        egl.append(jnp.exp(gl))
    tt = _neumann_inverse(nn, tt, n2, rounds)
    s_old, vs, sr, av, u = [], [], [], [], []
    for j in pairs:
        sl = slice(j * RWKV_PW, (j + 1) * RWKV_PW)
        s_old.append(S[j])
        vs.append(stack(v_ref[:, sl]))
        sr.append(_mxu(sread[j], s_old[j], _NT))
        av.append(_mxu(cat([a_ak[j], a_rk[j]], axis=0), vs[j], _NN))
    for j in pairs:
        u.append(_mxu(tt[j], sr[j][:n2] + av[j][:n2], _NN))
    for j in pairs:
        sl = slice(j * RWKV_PW, (j + 1) * RWKV_PW)
        o = sr[j][n2:] + av[j][n2:] - _mxu(a_rb[j], u[j], _NN)
        o_ref[:, sl] = o[:L] + o[L:]
        S[j] = s_old[j] * egl[j] + _mxu(cat([vs[j], u[j]], axis=0), kdbd[j], _TN)

    @pl.when(ci == nchunk - 1)
    def _():
        for j in pairs:
            s_ref[2 * j] = S[j][:n, :n]
            s_ref[2 * j + 1] = S[j][n:, n:]


def _rwkv(r, lw, k, v, kk, b, s_in, s_prev, p, L):
    bsz, t, _ = r.shape
    nchunk = t // L
    x_spec = pl.BlockSpec((None, L, RWKV_WIDTH), lambda i, c: (i, c, 0))
    tail = (RWKV_HEADS, RWKV_HEAD_DIM, RWKV_HEAD_DIM)
    s_ins, s_specs, s_out, s_shape = _state_io(tail, p, s_in, s_prev, bsz)
    alias = {} if s_prev is None else {6 + len(s_ins) - 1: 1}
    return pl.pallas_call(
        functools.partial(_rwkv_kernel, L=L, nchunk=nchunk, rounds=int(math.log2(L)), has_s0=s_in is not None),
        grid=(bsz, nchunk),
        in_specs=[x_spec] * 6 + s_specs,
        out_specs=[x_spec, s_out],
        out_shape=[jax.ShapeDtypeStruct((bsz, t, RWKV_WIDTH), F32), s_shape],
        scratch_shapes=[pltpu.VMEM((RWKV_PAIRS, RWKV_PW, RWKV_PW), F32)],
        input_output_aliases=alias,
        compiler_params=_cparams(("parallel", "arbitrary")),
        name="rwkv7",
    )(r, lw, k, v, kk, b, *s_ins)


def _pad_t(a, t_to):
    t = a.shape[1]
    if t == t_to:
        return a
    return jnp.pad(a, [(0, 0), (0, t_to - t)] + [(0, 0)] * (a.ndim - 2))


def _causal_conv(u, buf, w):
    t = u.shape[1]
    full = jnp.concatenate([buf, u], axis=1)
    out = full[:, 0:t] * w[0]
    for i in range(1, CONV_W):
        out = out + full[:, i:i + t] * w[i]
    return out, full[:, t:]


def _rms(x, g, eps=1e-6):
    return x * lax.rsqrt(jnp.mean(x * x, -1, keepdims=True) + eps) * g


def _l2n(x, eps=1e-6):
    return x * lax.rsqrt(jnp.sum(x * x, -1, keepdims=True) + eps)


def _even_mix(h, z, mem_k, mem_v, layer, st_in, st_prev, s_conv, P, long_seq):
    bsz, t, _ = h.shape
    p = layer // 2
    seg = lambda i: _seg(h, EVEN_OFF, EVEN_SIZES, i)
    gq, gk, gv, ggate, sz, sxbc, sdt, mq, mgate = (seg(i) for i in (0, 1, 2, 4, 5, 6, 7, 8, 9))
    tp = t if long_seq else SMALL_T
    log_a = jax.nn.log_sigmoid(z) / GLA_TAU
    L = GLA_CHUNK if long_seq else SMALL_T
    o_gla, s_gla = _gla(_pad_t(gq * GLA_DK ** -0.5, tp), _pad_t(gk, tp), _pad_t(gv, tp), _pad_t(log_a, tp),
                        st_in['gla'], st_prev['gla'], p, L, 256 if long_seq else SMALL_T)
    o_gla = _rms(o_gla[:, :t].reshape(bsz, t, GLA_HEADS, GLA_DV), P['gla_norm']).reshape(bsz, t, GLA_WIDTH)
    o_gla = o_gla * jax.nn.silu(ggate)
    xbc, s_conv = _causal_conv(sxbc, s_conv, P['ssd_conv_w'])
    xbc = jax.nn.silu(xbc + P['ssd_conv_b'])
    sx = xbc[..., :SSD_WIDTH]
    sb = xbc[..., SSD_WIDTH:SSD_WIDTH + SSD_GROUPS * SSD_STATE]
    sc = xbc[..., SSD_WIDTH + SSD_GROUPS * SSD_STATE:]
    dt = jax.nn.softplus(sdt + P['ssd_dt_bias'])
    a = -jnp.exp(P['ssd_a_log'])
    xh = sx.reshape(bsz, t, SSD_HEADS, SSD_HEAD_DIM)
    xdt = (xh * dt[..., None]).reshape(bsz, t, SSD_WIDTH)
    L = SSD_CHUNK if long_seq else SMALL_T
    y, s_ssd = _ssd(_pad_t(xdt, tp), _pad_t(sb, tp), _pad_t(sc, tp), _pad_t(dt * a, tp),
                    st_in['ssd'], st_prev['ssd'], p, L)
    y = y[:, :t].reshape(bsz, t, SSD_HEADS, SSD_HEAD_DIM) + xh * P['ssd_d'][:, None]
    y = (y.reshape(bsz, t, SSD_WIDTH) * jax.nn.silu(sz)).reshape(bsz, t, SSD_GROUPS, SSD_WIDTH // SSD_GROUPS)
    y = _rms(y, P['ssd_norm'].reshape(SSD_GROUPS, SSD_WIDTH // SSD_GROUPS)).reshape(bsz, t, SSD_WIDTH)
    o_mem = _mem_attention(mq, mem_k, mem_v, layer, 512 if long_seq else t) * jax.nn.silu(mgate)
    mix = jnp.concatenate([o_gla, y, o_mem], axis=-1).astype(BF16)
    return mix, dict(gla=s_gla, ssd=s_ssd), s_conv


def _odd_mix(h, wa, mem_k, mem_v, layer, st_in, st_prev, s_conv, s_shift, P, long_seq):
    bsz, t, _ = h.shape
    p = layer // 2
    seg = lambda i: _seg(h, ODD_OFF, ODD_SIZES, i)
    cqkv, cz, cb, ca, rsh, rgate, mq, mgate = (seg(i) for i in range(8))
    tp = t if long_seq else SMALL_T
    qkv, s_conv = _causal_conv(cqkv, s_conv, P['gdn_conv_w'])
    q, k, v = jnp.split(jax.nn.silu(qkv), 3, axis=-1)
    hd = (bsz, t, GDN_HEADS, GDN_HEAD_DIM)
    q = (_l2n(q.reshape(hd)) * GDN_HEAD_DIM ** -0.5).reshape(bsz, t, GDN_WIDTH)
    k = _l2n(k.reshape(hd)).reshape(bsz, t, GDN_WIDTH)
    beta = jax.nn.sigmoid(cb)
    g = -jnp.exp(P['gdn_a_log']) * jax.nn.softplus(ca + P['gdn_dt_bias'])
    L = GDN_CHUNK if long_seq else SMALL_T
    o_gdn, s_gdn = _gdn(_pad_t(q, tp), _pad_t(k, tp), _pad_t(v, tp), _pad_t(g, tp), _pad_t(beta, tp),
                        st_in['gdn'], st_prev['gdn'], p, L)
    o_gdn = _rms(o_gdn[:, :t].reshape(hd), P['gdn_norm']).reshape(bsz, t, GDN_WIDTH) * jax.nn.silu(cz)
    prev = jnp.concatenate([s_shift[:, None], rsh[:, :-1]], axis=1)
    s_shift = rsh[:, -1]
    rs = rsh + (prev - rsh) * P['rwkv_mu']
    r, k7, v7, xw, xa = jnp.split(rs, np.cumsum((RWKV_WIDTH,) * 3 + (RWKV_W_RANK,)).tolist(), axis=-1)
    lowrank = wa(jnp.concatenate([jnp.tanh(xw), xa], axis=-1))
    w_log = -jax.nn.softplus(-(P['rwkv_w0'] + lowrank[..., :RWKV_WIDTH])) - 0.5
    lw = -jnp.exp(w_log)
    a7 = jax.nn.sigmoid(P['rwkv_a0'] + lowrank[..., RWKV_WIDTH:])
    hd7 = (bsz, t, RWKV_HEADS, RWKV_HEAD_DIM)
    flat7 = lambda a: _pad_t(a.reshape(bsz, t, RWKV_WIDTH), tp)
    kk = _l2n((k7 * P['rwkv_kk']).reshape(hd7))
    k7 = (k7 * (1.0 + (a7 - 1.0) * P['rwkv_ka'])).reshape(hd7)
    r = r.reshape(hd7)
    v7 = v7.reshape(hd7)
    o7, s_rwkv = _rwkv(flat7(r), flat7(lw), flat7(k7), flat7(v7), flat7(kk), flat7(kk * a7.reshape(hd7)),
                       st_in['rwkv'], st_prev['rwkv'], p, RWKV_CHUNK if long_seq else SMALL_T)
    o7 = o7[:, :t].reshape(hd7)
    bonus = jnp.sum(r * k7 * P['rwkv_rk'], -1, keepdims=True) * v7
    oc = o7 - jnp.mean(o7, -1, keepdims=True)
    gn = oc * lax.rsqrt(jnp.mean(oc * oc, -1, keepdims=True) + RWKV_GN_EPS)
    o7 = gn.reshape(bsz, t, RWKV_WIDTH) * P['rwkv_ln_g'] + P['rwkv_ln_b'] + bonus.reshape(bsz, t, RWKV_WIDTH)
    o7 = o7 * jax.nn.silu(rgate)
    o_mem = _mem_attention(mq, mem_k, mem_v, layer, 512 if long_seq else t) * jax.nn.silu(mgate)
    mix = jnp.concatenate([o_gdn, o7, o_mem], axis=-1).astype(BF16)
    return mix, dict(gdn=s_gdn, rwkv=s_rwkv), s_conv, s_shift


def kernel(x_prompt, x_sample, mem_prompt, cache_mem_k, cache_mem_v, state_gla, state_ssd, state_ssd_conv, state_gdn, state_gdn_conv, state_rwkv, state_rwkv_shift, mem_w_kv, ev_w_in, ev_gla_w2, ev_gla_b, ev_gla_norm, ev_ssd_conv_w, ev_ssd_conv_b, ev_ssd_dt_bias, ev_ssd_a_log, ev_ssd_d, ev_ssd_norm, ev_w_out, ev_ln_g, ev_ln_b, od_w_in, od_gdn_conv_w, od_gdn_dt_bias, od_gdn_a_log, od_gdn_norm, od_rwkv_mu, od_rwkv_w0, od_rwkv_w2, od_rwkv_a0, od_rwkv_a2, od_rwkv_kk, od_rwkv_ka, od_rwkv_rk, od_rwkv_ln_g, od_rwkv_ln_b, od_w_out, od_ln_g, od_ln_b):
    ev = dict(w_in=ev_w_in, gla_w2=ev_gla_w2, gla_b=ev_gla_b, gla_norm=ev_gla_norm,
              ssd_conv_w=ev_ssd_conv_w, ssd_conv_b=ev_ssd_conv_b, ssd_dt_bias=ev_ssd_dt_bias,
              ssd_a_log=ev_ssd_a_log, ssd_d=ev_ssd_d, ssd_norm=ev_ssd_norm,
              w_out=ev_w_out, ln_g=ev_ln_g, ln_b=ev_ln_b)
    od = dict(w_in=od_w_in, gdn_conv_w=od_gdn_conv_w, gdn_dt_bias=od_gdn_dt_bias, gdn_a_log=od_gdn_a_log,
              gdn_norm=od_gdn_norm, rwkv_mu=od_rwkv_mu, rwkv_w0=od_rwkv_w0, rwkv_w2=od_rwkv_w2,
              rwkv_a0=od_rwkv_a0, rwkv_a2=od_rwkv_a2, rwkv_kk=od_rwkv_kk, rwkv_ka=od_rwkv_ka,
              rwkv_rk=od_rwkv_rk, rwkv_ln_g=od_rwkv_ln_g, rwkv_ln_b=od_rwkv_ln_b,
              w_out=od_w_out, ln_g=od_ln_g, ln_b=od_ln_b)
    bp, tp, _ = x_prompt.shape
    bs, ts, _ = x_sample.shape
    mp, ms = bp * tp, bs * ts

    w_kv = jnp.moveaxis(mem_w_kv, 0, 1).reshape(D_MODEL, DEPTH * 2 * MEM_WIDTH).astype(BF16)
    kv = _matmul(mem_prompt.reshape(bp * MEM_LEN, D_MODEL).astype(BF16), w_kv, 512, 1024)
    kv = kv.reshape(bp, MEM_LEN, DEPTH, 2, MEM_HEADS, MEM_HEAD_DIM)
    mem_k_p = jnp.moveaxis(kv[:, :, :, 0], 2, 0)
    mem_v_p = jnp.moveaxis(kv[:, :, :, 1], 2, 0)
    flat = lambda a: a.reshape(a.shape[:-2] + (MEM_WIDTH,))
    mk_p, mv_p, mk_s, mv_s = flat(mem_k_p), flat(mem_v_p), flat(cache_mem_k), flat(cache_mem_v)

    x = jnp.concatenate([x_prompt.reshape(mp, D_MODEL), x_sample.reshape(ms, D_MODEL)], axis=0)
    x_bf = x.astype(BF16)
    zp = lambda shape: jnp.zeros(shape, F32)
    names = ('gla', 'ssd', 'gdn', 'rwkv')
    none = {n: None for n in names}
    in_s = dict(gla=state_gla, ssd=state_ssd, gdn=state_gdn, rwkv=state_rwkv)
    out_p, out_s = dict(none), dict(none)
    small_p = {n: [] for n in ('ssd_conv', 'gdn_conv', 'rwkv_shift')}
    small_s = {n: [] for n in small_p}
    tm = 512
    for layer in range(DEPTH):
        p = layer // 2
        if layer % 2 == 0:
            P = {n: w[p] for n, w in ev.items()}
            h = _matmul(x_bf, _pack_w_in(P['w_in'], EVEN_SIZES, EVEN_ORDER, EVEN_N), tm, PROJ_TN)
            glr = _seg(h, EVEN_OFF, EVEN_SIZES, 3)
            z = _matmul(glr, P['gla_w2'], tm, GLA_QK, precision=HI) + P['gla_b']
            mix_p, new, c1 = _even_mix(h[:mp].reshape(bp, tp, EVEN_N), z[:mp].reshape(bp, tp, GLA_QK),
                                       mk_p, mv_p, layer, none, out_p, zp((bp, CONV_W - 1, SSD_CONV_DIM)), P, True)
            out_p.update(new)
            mix_s, new, c2 = _even_mix(h[mp:].reshape(bs, ts, EVEN_N), z[mp:].reshape(bs, ts, GLA_QK),
                                       mk_s, mv_s, layer, in_s, out_s, state_ssd_conv[p], P, False)
            out_s.update(new)
            small_p['ssd_conv'].append(c1)
            small_s['ssd_conv'].append(c2)
        else:
            P = {n: w[p] for n, w in od.items()}
            h = _matmul(x_bf, _pack_w_in(P['w_in'], ODD_SIZES, ODD_ORDER, ODD_N), tm, PROJ_TN)
            w_lr = jnp.zeros((RWKV_W_RANK + RWKV_A_RANK, 2 * RWKV_WIDTH), F32)
            w_lr = w_lr.at[:RWKV_W_RANK, :RWKV_WIDTH].set(P['rwkv_w2'])
            w_lr = w_lr.at[RWKV_W_RANK:, RWKV_WIDTH:].set(P['rwkv_a2'])

            def wa(u, w_lr=w_lr):
                b_, t_, c_ = u.shape
                return _matmul(u.reshape(b_ * t_, c_), w_lr, 512, 1024, precision=HI).reshape(b_, t_, -1)

            mix_p, new, c1, h1 = _odd_mix(h[:mp].reshape(bp, tp, ODD_N), wa, mk_p, mv_p, layer, none, out_p,
                                          zp((bp, CONV_W - 1, 3 * GDN_WIDTH)), zp((bp, RWKV_SHIFT_DIM)), P, True)
            out_p.update(new)
            mix_s, new, c2, h2 = _odd_mix(h[mp:].reshape(bs, ts, ODD_N), wa, mk_s, mv_s, layer, in_s, out_s,
                                          state_gdn_conv[p], state_rwkv_shift[p], P, False)
            out_s.update(new)
            small_p['gdn_conv'].append(c1)
            small_s['gdn_conv'].append(c2)
            small_p['rwkv_shift'].append(h1)
            small_s['rwkv_shift'].append(h2)
        mix = jnp.concatenate([mix_p.reshape(mp, MIX_WIDTH), mix_s.reshape(ms, MIX_WIDTH)], axis=0)
        x, x_bf = _out_ln(mix, P['w_out'].astype(BF16), x, P['ln_g'], P['ln_b'])

    y_prompt = x[:mp].reshape(bp, tp, D_MODEL)
    y_sample = x[mp:].reshape(bs, ts, D_MODEL)
    st = lambda d, n: jnp.stack(d[n])
    return (y_prompt, y_sample, mem_k_p, mem_v_p,
            out_p['gla'], out_s['gla'], out_p['ssd'], out_s['ssd'],
            st(small_p, 'ssd_conv'), st(small_s, 'ssd_conv'), out_p['gdn'], out_s['gdn'],
            st(small_p, 'gdn_conv'), st(small_s, 'gdn_conv'), out_p['rwkv'], out_s['rwkv'],
            st(small_p, 'rwkv_shift'), st(small_s, 'rwkv_shift'))
```

```python
import functools
import math

import numpy as np
import jax
import jax.numpy as jnp
from jax import lax
from jax.experimental import pallas as pl
from jax.experimental.pallas import tpu as pltpu

F32 = jnp.float32
BF16 = jnp.bfloat16
HI = lax.Precision.HIGHEST

D_MODEL = 2048
DEPTH = 4
N_PAIRS = DEPTH // 2
CONV_W = 4
MEM_LEN = 256
MEM_HEADS = 4
MEM_HEAD_DIM = 256
MEM_WIDTH = 1024
GLA_HEADS = 4
GLA_DK = 128
GLA_DV = 256
GLA_QK = 512
GLA_WIDTH = 1024
GLA_RANK = 16
GLA_TAU = 16.0
SSD_WIDTH = 2048
SSD_HEAD_DIM = 64
SSD_HEADS = 32
SSD_GROUPS = 4
SSD_REP = 8
SSD_STATE = 128
SSD_CONV_DIM = SSD_WIDTH + 2 * SSD_GROUPS * SSD_STATE
GDN_WIDTH = 2048
GDN_HEAD_DIM = 128
GDN_HEADS = 16
RWKV_WIDTH = 1024
RWKV_HEAD_DIM = 64
RWKV_HEADS = 16
RWKV_W_RANK = 64
RWKV_A_RANK = 64
RWKV_SHIFT_DIM = 3 * RWKV_WIDTH + RWKV_W_RANK + RWKV_A_RANK
RWKV_GN_EPS = 64e-5
EVEN_SIZES = (GLA_QK, GLA_QK, GLA_WIDTH, GLA_RANK, GLA_WIDTH, SSD_WIDTH, SSD_CONV_DIM, SSD_HEADS,
              MEM_WIDTH, MEM_WIDTH)
ODD_SIZES = (3 * GDN_WIDTH, GDN_WIDTH, GDN_HEADS, GDN_HEADS, RWKV_SHIFT_DIM, RWKV_WIDTH, MEM_WIDTH, MEM_WIDTH)
MIX_WIDTH = 4096
DEEPNORM_ALPHA = (2 * DEPTH) ** 0.25

EVEN_ORDER = (0, 1, 2, 4, 5, 6, 8, 9, 3, 7)
ODD_ORDER = (0, 1, 5, 6, 7, 4, 2, 3)
PROJ_TN = 768
VMEM_LIMIT = 48 * 1024 * 1024

GLA_CHUNK = 16
SSD_CHUNK = 64
GDN_CHUNK = 64
RWKV_CHUNK = 64
SMALL_T = 8

_NN = ((1,), (0,))
_NT = ((1,), (1,))
_TN = ((0,), (0,))


def _packed_layout(sizes, order):
    offs, o = {}, 0
    for i in order:
        offs[i] = o
        o += sizes[i]
    total = -(-o // PROJ_TN) * PROJ_TN
    return offs, total


EVEN_OFF, EVEN_N = _packed_layout(EVEN_SIZES, EVEN_ORDER)
ODD_OFF, ODD_N = _packed_layout(ODD_SIZES, ODD_ORDER)


def _pack_w_in(w, sizes, order, total):
    segs = jnp.split(w, np.cumsum(sizes)[:-1].tolist(), axis=-1)
    parts = [segs[i] for i in order]
    used = sum(sizes)
    if total > used:
        parts.append(jnp.zeros((w.shape[0], total - used), w.dtype))
    return jnp.concatenate(parts, axis=-1).astype(BF16)


def _seg(h, offs, sizes, i):
    return h[..., offs[i]:offs[i] + sizes[i]]


def _cparams(sem):
    return pltpu.CompilerParams(dimension_semantics=sem, vmem_limit_bytes=VMEM_LIMIT)


def _mxu(a, b, dims):
    return lax.dot_general(a.astype(BF16), b.astype(BF16), (dims, ((), ())), preferred_element_type=F32)


def _mxu_f32(a, b, dims):
    return lax.dot_general(a, b, (dims, ((), ())), precision=HI, preferred_element_type=F32)


def _sigmoid(x):
    return 1.0 / (1.0 + jnp.exp(-x))


def _silu(x):
    return x * _sigmoid(x)


def _softplus(x):
    return jnp.maximum(x, 0.0) + jnp.log(1.0 + jnp.exp(-jnp.abs(x)))


def _shifted(u, prev8, j, row8):
    ru = pltpu.roll(u, j, 0)
    top = jnp.where(row8 < j, pltpu.roll(prev8, j, 0), ru[:8])
    return top if u.shape[0] == 8 else jnp.concatenate([top, ru[8:]], axis=0)


class _Group:
    def __init__(self, bsz, t_rows, t_real, row0, chunk):
        self.bsz, self.t_rows, self.t_real, self.row0, self.chunk = bsz, t_rows, t_real, row0, chunk

    def spec(self, rows, width, off):
        assert off % width == 0 and self.row0 % rows == 0 and self.t_rows % rows == 0
        base, per, cb = self.row0 // rows, self.t_rows // rows, off // width
        return pl.BlockSpec((rows, width), lambda b, i: (base + b * per + i, cb))


def _alias_last(n_inputs, has_prev, out_index=0):
    return {n_inputs - 1: out_index} if has_prev else {}


def _neumann_inverse(nn, tt, n, rounds):
    idx = range(len(nn))
    if rounds >= 2:
        for j in idx:
            nn[j] = _mxu(nn[j], nn[j], _NN)
        for _ in range(rounds - 2):
            for j in idx:
                both = _mxu(nn[j], jnp.concatenate([tt[j], nn[j]], axis=1), _NN)
                tt[j] = tt[j] + both[:, :n]
                nn[j] = both[:, n:]
        for j in idx:
            tt[j] = tt[j] + _mxu(nn[j], tt[j], _NN)
    return tt


def _state_io(tail, p, s_in, s_prev, bsz):
    zeros = (0,) * len(tail)
    spec = pl.BlockSpec((None, None) + tail, lambda *g: (p, g[0]) + zeros)
    ins, specs = [], []
    if s_in is not None:
        ins.append(s_in)
        specs.append(spec)
    if s_prev is not None:
        ins.append(s_prev)
        specs.append(pl.BlockSpec(memory_space=pl.ANY))
    shape = jax.ShapeDtypeStruct((N_PAIRS, bsz) + tail, F32)
    return ins, specs, spec, shape


def _mm_kernel(x_ref, w_ref, o_ref, *, precision):
    o_ref[...] = jnp.dot(x_ref[...], w_ref[...], preferred_element_type=F32, precision=precision)


def _matmul(x, w, tm, tn, precision=None):
    m, k = x.shape
    n = w.shape[1]
    assert m % tm == 0 and n % tn == 0
    return pl.pallas_call(
        functools.partial(_mm_kernel, precision=precision),
        grid=(n // tn, m // tm),
        in_specs=[pl.BlockSpec((tm, k), lambda j, i: (i, 0)),
                  pl.BlockSpec((k, tn), lambda j, i: (0, j))],
        out_specs=pl.BlockSpec((tm, tn), lambda j, i: (i, j)),
        out_shape=jax.ShapeDtypeStruct((m, n), F32),
        compiler_params=_cparams(("parallel", "parallel")),
        name="matmul",
    )(x, w)


def _out_ln_kernel(mix_ref, w_ref, x_ref, g_ref, b_ref, y_ref, ybf_ref, acc, *, nk):
    kk = pl.program_id(1)

    @pl.when(kk == 0)
    def _():
        acc[...] = jnp.zeros_like(acc)

    acc[...] += jnp.dot(mix_ref[...].astype(BF16), w_ref[...], preferred_element_type=F32)

    @pl.when(kk == nk - 1)
    def _():
        z = DEEPNORM_ALPHA * x_ref[...] + acc[...]
        zc = z - jnp.mean(z, axis=-1, keepdims=True)
        var = jnp.mean(zc * zc, axis=-1, keepdims=True)
        y = zc * lax.rsqrt(var + 1e-5) * g_ref[...] + b_ref[...]
        y_ref[...] = y
        ybf_ref[...] = y.astype(BF16)


def _out_ln(mix, w, x, g, b, tm=512, tk=1024):
    m, k = mix.shape
    d = w.shape[1]
    nk = k // tk
    return pl.pallas_call(
        functools.partial(_out_ln_kernel, nk=nk),
        grid=(m // tm, nk),
        in_specs=[pl.BlockSpec((tm, tk), lambda i, j: (i, j)),
                  pl.BlockSpec((tk, d), lambda i, j: (j, 0)),
                  pl.BlockSpec((tm, d), lambda i, j: (i, 0)),
                  pl.BlockSpec((1, d), lambda i, j: (0, 0)),
                  pl.BlockSpec((1, d), lambda i, j: (0, 0))],
        out_specs=[pl.BlockSpec((tm, d), lambda i, j: (i, 0)),
                   pl.BlockSpec((tm, d), lambda i, j: (i, 0))],
        out_shape=[jax.ShapeDtypeStruct((m, d), F32), jax.ShapeDtypeStruct((m, d), BF16)],
        scratch_shapes=[pltpu.VMEM((tm, d), F32)],
        compiler_params=_cparams(("parallel", "arbitrary")),
        name="out_ln",
    )(mix, w, x, g.reshape(1, d), b.reshape(1, d))


def _mem_kernel(q_ref, gate_ref, k_ref, v_ref, *rest, heads_on_rows):
    o_ref = rest[-1]
    for h in range(MEM_HEADS):
        sl = slice(h * MEM_HEAD_DIM, (h + 1) * MEM_HEAD_DIM)
        k = k_ref[:, h, :] if heads_on_rows else k_ref[:, sl]
        v = v_ref[:, h, :] if heads_on_rows else v_ref[:, sl]
        s = _mxu(q_ref[:, sl], k, _NT) * MEM_HEAD_DIM ** -0.5
        p = jnp.exp(s - jnp.max(s, axis=-1, keepdims=True))
        p = p / jnp.sum(p, axis=-1, keepdims=True)
        o_ref[:, sl] = _mxu(p, v, _NN) * _silu(gate_ref[:, sl])


def _mem_attention(h, mix_prev, grp, q_off, gate_off, mem_k, mem_v, layer, tq):
    cached = mem_k.ndim == 5
    if cached:
        kv_specs = [pl.BlockSpec((None, None, MEM_LEN, MEM_HEADS, MEM_HEAD_DIM), lambda b, i: (layer, b, 0, 0, 0))] * 2
    else:
        kv_specs = [pl.BlockSpec((MEM_LEN, MEM_WIDTH), lambda b, i: (b, 2 * layer)),
                    pl.BlockSpec((MEM_LEN, MEM_WIDTH), lambda b, i: (b, 2 * layer + 1))]
    ins = [h, h, mem_k, mem_v] + ([] if mix_prev is None else [mix_prev])
    specs = [grp.spec(tq, MEM_WIDTH, q_off), grp.spec(tq, MEM_WIDTH, gate_off)] + kv_specs
    if mix_prev is not None:
        specs.append(pl.BlockSpec(memory_space=pl.ANY))
    return pl.pallas_call(
        functools.partial(_mem_kernel, heads_on_rows=cached),
        grid=(grp.bsz, grp.t_rows // tq),
        in_specs=specs,
        out_specs=grp.spec(tq, MEM_WIDTH, MIX_WIDTH - MEM_WIDTH),
        out_shape=jax.ShapeDtypeStruct((h.shape[0], MIX_WIDTH), F32),
        input_output_aliases=_alias_last(len(ins), mix_prev is not None),
        compiler_params=_cparams(("parallel", "parallel")),
        name="mem_attention",
    )(*ins)


def _gla_kernel(*refs, L, nchunk, nblk, has_s0):
    q_ref, k_ref, v_ref, g_ref = refs[:4]
    o_ref, s_ref, S = refs[-3:]
    tb = pl.program_id(1)

    @pl.when(tb == 0)
    def _():
        S[...] = refs[4][...] if has_s0 else jnp.zeros_like(S)

    row = lax.broadcasted_iota(jnp.int32, (L, L), 0)
    col = lax.broadcasted_iota(jnp.int32, (L, L), 1)
    tril = (col <= row).astype(F32)
    t_iota = lax.broadcasted_iota(jnp.int32, (L, GLA_DK), 0)
    r128 = lax.broadcasted_iota(jnp.int32, (GLA_DK, GLA_DK), 0)
    c128 = lax.broadcasted_iota(jnp.int32, (GLA_DK, GLA_DK), 1)

    def chunk(ci, carry):
        r0 = pl.multiple_of(ci * L, L)
        for h in range(GLA_HEADS):
            ks = slice(h * GLA_DK, (h + 1) * GLA_DK)
            vs = slice(h * GLA_DV, (h + 1) * GLA_DV)
            q = q_ref[pl.ds(r0, L), ks]
            k = k_ref[pl.ds(r0, L), ks]
            g = g_ref[pl.ds(r0, L), ks]
            v = v_ref[pl.ds(r0, L), vs]
            b = _mxu_f32(tril, g, _NN)
            b_last = b[L - 1:L, :]
            att = jnp.zeros((L, L), F32)
            for s in range(L):
                d = jnp.where(t_iota >= s, b - b[s:s + 1, :], -jnp.inf)
                p = jnp.exp(d) * q * k[s:s + 1, :]
                att = jnp.where(col == s, jnp.sum(p, axis=-1, keepdims=True), att)
            s_old = S[h]
            o = _mxu_f32(q * jnp.exp(b), s_old, _NN) + _mxu_f32(att, v, _NN)
            o_ref[pl.ds(r0, L), vs] = o
            kd = k * jnp.exp(b_last - b)
            e_col = jnp.sum(jnp.where(r128 == c128, jnp.exp(b_last), 0.0), axis=-1, keepdims=True)
            S[h] = s_old * e_col + _mxu_f32(kd, v, _TN)
        return carry

    lax.fori_loop(0, nchunk, chunk, 0)

    @pl.when(tb == nblk - 1)
    def _():
        s_ref[...] = S[...]


def _gla(q, k, v, g, s_in, s_prev, p, L, tb):
    bsz, t, _ = q.shape
    nblk = t // tb
    qk_spec = pl.BlockSpec((None, tb, GLA_QK), lambda b, i: (b, i, 0))
    v_spec = pl.BlockSpec((None, tb, GLA_WIDTH), lambda b, i: (b, i, 0))
    tail = (GLA_HEADS, GLA_DK, GLA_DV)
    s_ins, s_specs, s_out, s_shape = _state_io(tail, p, s_in, s_prev, bsz)
    alias = {} if s_prev is None else {4 + len(s_ins) - 1: 1}
    return pl.pallas_call(
        functools.partial(_gla_kernel, L=L, nchunk=tb // L, nblk=nblk, has_s0=s_in is not None),
        grid=(bsz, nblk),
        in_specs=[qk_spec, qk_spec, v_spec, qk_spec] + s_specs,
        out_specs=[v_spec, s_out],
        out_shape=[jax.ShapeDtypeStruct((bsz, t, GLA_WIDTH), F32), s_shape],
        scratch_shapes=[pltpu.VMEM(tail, F32)],
        input_output_aliases=alias,
        compiler_params=_cparams(("parallel", "arbitrary")),
        name="gla",
    )(q, k, v, g, *s_ins)


def _ssd_kernel(*refs, L, nchunk, has_s0):
    x_ref, b_ref, c_ref, da_ref = refs[:4]
    y_ref, s_ref, S = refs[-3:]
    grp = pl.program_id(1)
    ci = pl.program_id(2)

    @pl.when(ci == 0)
    def _():
        S[...] = refs[4][...] if has_s0 else jnp.zeros_like(S)

    row = lax.broadcasted_iota(jnp.int32, (L, L), 0)
    col = lax.broadcasted_iota(jnp.int32, (L, L), 1)
    causal = col <= row
    tril = causal.astype(F32)
    lane = lax.broadcasted_iota(jnp.int32, (L, SSD_HEADS), 1)
    bm = b_ref[...]
    cm = c_ref[...]
    cb = _mxu_f32(cm, bm, _NT)
    da_all = da_ref[...]
    c_all = _mxu_f32(tril, da_all, _NN)
    for r in range(SSD_REP):
        sel = lane == grp * SSD_REP + r
        c_col = jnp.sum(jnp.where(sel, c_all, 0.0), axis=-1, keepdims=True)
        d_col = jnp.sum(jnp.where(sel, da_all, 0.0), axis=-1, keepdims=True)
        c_row = jnp.sum(jnp.where(row <= col, d_col, 0.0), axis=0, keepdims=True)
        seg = jnp.exp(jnp.where(causal, c_col - c_row, -jnp.inf))
        hs = slice(r * SSD_HEAD_DIM, (r + 1) * SSD_HEAD_DIM)
        xr = x_ref[:, hs]
        s_old = S[r]
        y = _mxu_f32(cb * seg, xr, _NN) + _mxu_f32(cm, s_old, _NT) * jnp.exp(c_col)
        y_ref[:, hs] = y
        c_last = c_col[L - 1:L, :]
        S[r] = s_old * jnp.exp(c_last) + _mxu_f32(xr * jnp.exp(c_last - c_col), bm, _TN)

    @pl.when(ci == nchunk - 1)
    def _():
        s_ref[...] = S[...]


def _ssd(xdt, bm, cm, da, s_in, s_prev, p, L):
    bsz, t, _ = xdt.shape
    nchunk = t // L
    gw = SSD_REP * SSD_HEAD_DIM
    x_spec = pl.BlockSpec((None, L, gw), lambda b, g, c: (b, c, g))
    bc_spec = pl.BlockSpec((None, L, SSD_STATE), lambda b, g, c: (b, c, g))
    blk = (None, None, SSD_REP, SSD_HEAD_DIM, SSD_STATE)
    s_spec = pl.BlockSpec(blk, lambda b, g, c: (p, b, g, 0, 0))
    s_ins, s_specs = [], []
    if s_in is not None:
        s_ins.append(s_in)
        s_specs.append(s_spec)
    if s_prev is not None:
        s_ins.append(s_prev)
        s_specs.append(pl.BlockSpec(memory_space=pl.ANY))
    alias = {} if s_prev is None else {4 + len(s_ins) - 1: 1}
    return pl.pallas_call(
        functools.partial(_ssd_kernel, L=L, nchunk=nchunk, has_s0=s_in is not None),
        grid=(bsz, SSD_GROUPS, nchunk),
        in_specs=[x_spec, bc_spec, bc_spec,
                  pl.BlockSpec((None, L, SSD_HEADS), lambda b, g, c: (b, c, 0))] + s_specs,
        out_specs=[x_spec, s_spec],
        out_shape=[jax.ShapeDtypeStruct((bsz, t, SSD_WIDTH), F32),
                   jax.ShapeDtypeStruct((N_PAIRS, bsz, SSD_HEADS, SSD_HEAD_DIM, SSD_STATE), F32)],
        scratch_shapes=[pltpu.VMEM((SSD_REP, SSD_HEAD_DIM, SSD_STATE), F32)],
        input_output_aliases=alias,
        compiler_params=_cparams(("parallel", "parallel", "arbitrary")),
        name="ssd",
    )(xdt, bm, cm, da, *s_ins)


def _gdn_kernel(qkv_ref, cz_ref, sm_ref, cw_ref, alog_ref, dtb_ref, nw_ref, *rest, L, nchunk, rounds, t_real,
                has_state, has_prev):
    n_opt = 2 * has_state + has_prev
    mix_ref, s_ref, conv_out_ref, S, tail = rest[n_opt:]
    ci = pl.program_id(1)

    @pl.when(ci == 0)
    def _():
        tail[...] = jnp.zeros_like(tail)
        if has_state:
            tail[8 - (CONV_W - 1):8, :] = rest[0][...]
            S[...] = rest[1][...]
        else:
            S[...] = jnp.zeros_like(S)

    n2 = 2 * L
    d = GDN_HEAD_DIM
    cat = jnp.concatenate
    row8 = lax.broadcasted_iota(jnp.int32, (8, d), 0)

    def conv_tile(c0):
        u = qkv_ref[:, c0:c0 + d]
        p8 = tail[:, c0:c0 + d]
        w = cw_ref[:, c0:c0 + d]
        acc = u * w[CONV_W - 1:CONV_W]
        for j in range(1, CONV_W):
            acc = acc + _shifted(u, p8, j, row8) * w[CONV_W - 1 - j:CONV_W - j]
        return _silu(acc)

    def l2n(x):
        return x * lax.rsqrt(jnp.sum(x * x, axis=-1, keepdims=True) + 1e-6)

    row = lax.broadcasted_iota(jnp.int32, (L, L), 0)
    col = lax.broadcasted_iota(jnp.int32, (L, L), 1)
    tril = (col <= row).astype(F32)
    r2 = lax.broadcasted_iota(jnp.int32, (n2, n2), 0)
    c2 = lax.broadcasted_iota(jnp.int32, (n2, n2), 1)
    same = (r2 >= L) == (c2 >= L)
    strict = same & (c2 < r2)
    incl = same & (c2 <= r2)
    upper = same & (r2 <= c2)
    eye = (r2 == c2).astype(F32)
    zl = jnp.zeros((L, d), F32)
    sm = sm_ref[...]
    beta_all = _sigmoid(sm[:, :GDN_HEADS])
    g_all = -jnp.exp(alog_ref[...]) * _softplus(sm[:, GDN_HEADS:2 * GDN_HEADS] + dtb_ref[...])
    if t_real < L:
        valid = lax.broadcasted_iota(jnp.int32, (L, GDN_HEADS), 0) < t_real
        beta_all = jnp.where(valid, beta_all, 0.0)
        g_all = jnp.where(valid, g_all, 0.0)
    c_all = _mxu_f32(tril, g_all, _NN)
    pairs = range(GDN_HEADS // 2)
    nn, tt, qk, kq, kdec, ec, bcol, elast, vst = [], [], [], [], [], [], [], [], []
    for j in pairs:
        h0, h1 = 2 * j, 2 * j + 1
        stack_col = lambda a: cat([a[:, h0:h0 + 1], a[:, h1:h1 + 1]], axis=0)
        c_col = stack_col(c_all)
        beta_col = stack_col(beta_all)
        c_row = jnp.sum(jnp.where(upper, stack_col(g_all), 0.0), axis=0, keepdims=True)
        decay = jnp.exp(jnp.where(incl, c_col - c_row, -jnp.inf))
        last = lambda rows: cat([jnp.broadcast_to(c_all[L - 1:L, h0:h0 + 1], (rows, 1)),
                                 jnp.broadcast_to(c_all[L - 1:L, h1:h1 + 1], (rows, 1))], axis=0)
        q0, q1 = (l2n(conv_tile(h * d)) * d ** -0.5 for h in (h0, h1))
        k0, k1 = (l2n(conv_tile(GDN_WIDTH + h * d)) for h in (h0, h1))
        vst.append(cat([conv_tile(2 * GDN_WIDTH + h0 * d), conv_tile(2 * GDN_WIDTH + h1 * d)], axis=0))
        k_st = cat([cat([k0, zl], axis=1), cat([zl, k1], axis=1)], axis=0)
        q_st = cat([cat([q0, zl], axis=1), cat([zl, q1], axis=1)], axis=0)
        both = cat([k_st, q_st], axis=0)
        full = _mxu(both, k_st, _NT)
        a = jnp.where(strict, full[:n2] * decay * beta_col, 0.0)
        nn.append(-a)
        tt.append(eye - a)
        qk.append(full[n2:] * decay)
        kq.append(both)
        kdec.append(k_st * jnp.exp(last(L) - c_col))
        ec.append(jnp.exp(c_col))
        bcol.append(beta_col)
        elast.append(jnp.exp(last(d)))
    tt = _neumann_inverse(nn, tt, n2, rounds)
    s_old, ksqs, u = [], [], []
    for j in pairs:
        s_old.append(cat([S[2 * j], S[2 * j + 1]], axis=0))
        ksqs.append(_mxu(kq[j], s_old[j], _NN))
    for j in pairs:
        u.append(_mxu(tt[j], bcol[j] * (vst[j] - ec[j] * ksqs[j][:n2]), _NN))
    nw = nw_ref[...]
    for j in pairs:
        o = ec[j] * ksqs[j][n2:] + _mxu(qk[j], u[j], _NN)
        for hh, oh in ((2 * j, o[:L]), (2 * j + 1, o[L:])):
            cols = slice(hh * d, (hh + 1) * d)
            y = oh * lax.rsqrt(jnp.mean(oh * oh, axis=-1, keepdims=True) + 1e-6) * nw
            mix_ref[:, cols] = y * _silu(cz_ref[:, cols])
        new = s_old[j] * elast[j] + _mxu(kdec[j], u[j], _TN)
        S[2 * j] = new[:d]
        S[2 * j + 1] = new[d:]
    if nchunk > 1:
        tail[...] = qkv_ref[L - 8:L, :]

    @pl.when(ci == nchunk - 1)
    def _():
        s_ref[...] = S[...]
        conv_out_ref[...] = qkv_ref[t_real - (CONV_W - 1):t_real, :]


def _gdn(h, mix_prev, grp, P, st_in, conv_in, s_prev, p):
    L = grp.chunk
    nchunk = grp.t_rows // L
    tail = (GDN_HEADS, GDN_HEAD_DIM, GDN_HEAD_DIM)
    s_ins, s_specs, s_out, s_shape = _state_io(tail, p, st_in, None, grp.bsz)
    cw = 3 * GDN_WIDTH
    full = lambda shape: pl.BlockSpec(shape, lambda b, c: (0,) * len(shape))
    ins = [h, h, h, P['gdn_conv_w'], P['gdn_a_log'].reshape(1, GDN_HEADS), P['gdn_dt_bias'].reshape(1, GDN_HEADS),
           P['gdn_norm'].reshape(1, GDN_HEAD_DIM)]
    specs = [grp.spec(L, cw, ODD_OFF[0]), grp.spec(L, GDN_WIDTH, ODD_OFF[1]), grp.spec(L, 128, ODD_OFF[2]),
             full((CONV_W, cw)), full((1, GDN_HEADS)), full((1, GDN_HEADS)), full((1, GDN_HEAD_DIM))]
    if st_in is not None:
        ins += [conv_in] + s_ins
        specs += [pl.BlockSpec((None, None, CONV_W - 1, cw), lambda b, c: (p, b, 0, 0))] + s_specs
    any_spec = pl.BlockSpec(memory_space=pl.ANY)
    alias = {}
    for prev, out_idx in ((mix_prev, 0), (s_prev, 1)):
        if prev is not None:
            alias[len(ins)] = out_idx
            ins.append(prev)
            specs.append(any_spec)
    n_prev = (mix_prev is not None) + (s_prev is not None)
    return pl.pallas_call(
        functools.partial(_gdn_kernel, L=L, nchunk=nchunk, rounds=int(math.log2(L)), t_real=grp.t_real or L,
                          has_state=st_in is not None, has_prev=n_prev),
        grid=(grp.bsz, nchunk),
        in_specs=specs,
        out_specs=[grp.spec(L, GDN_WIDTH, 0), s_out,
                   pl.BlockSpec((None, CONV_W - 1, cw), lambda b, c: (b, 0, 0))],
        out_shape=[jax.ShapeDtypeStruct((h.shape[0], MIX_WIDTH), F32), s_shape,
                   jax.ShapeDtypeStruct((grp.bsz, CONV_W - 1, cw), F32)],
        scratch_shapes=[pltpu.VMEM(tail, F32), pltpu.VMEM((8, cw), F32)],
        input_output_aliases=alias,
        compiler_params=_cparams(("parallel", "arbitrary")),
        name="gdn",
    )(*ins)


RWKV_PAIRS = RWKV_HEADS // 2
RWKV_PW = 2 * RWKV_HEAD_DIM


def _rwkv_kernel(r_ref, k_ref, v_ref, xwa_ref, gate_ref, mu_ref, w0_ref, w2_ref, a0_ref, a2_ref, kkp_ref, ka_ref,
                 rk_ref, lng_ref, lnb_ref, *rest, L, nchunk, rounds, t_real, has_state, has_prev):
    n_opt = 2 * has_state + has_prev
    mix_ref, s_ref, shift_out_ref, S, last = rest[n_opt:]
    ci = pl.program_id(1)
    n = RWKV_HEAD_DIM
    w3 = 3 * RWKV_WIDTH
    cat = jnp.concatenate
    pairs = range(RWKV_PAIRS)

    @pl.when(ci == 0)
    def _():
        last[...] = jnp.zeros_like(last)
        if has_state:
            last[7:8, :] = rest[0][...]
            zn = jnp.zeros((n, n), F32)
            for j in pairs:
                S[j] = cat([cat([rest[1][2 * j], zn], axis=1), cat([zn, rest[1][2 * j + 1]], axis=1)], axis=0)
        else:
            S[...] = jnp.zeros_like(S)

    n2 = 2 * L
    row = lax.broadcasted_iota(jnp.int32, (L, L), 0)
    col = lax.broadcasted_iota(jnp.int32, (L, L), 1)
    tril = (col <= row).astype(F32)
    r2 = lax.broadcasted_iota(jnp.int32, (n2, n2), 0)
    c2 = lax.broadcasted_iota(jnp.int32, (n2, n2), 1)
    same = (r2 >= L) == (c2 >= L)
    strict = same & (c2 < r2)
    incl = same & (c2 <= r2)
    eye = (r2 == c2).astype(F32)
    lane = lax.broadcasted_iota(jnp.int32, (L, RWKV_PW), 1)
    lo = lane < n

    def stack(x):
        return cat([jnp.where(lo, x, 0.0), jnp.where(lo, 0.0, x)], axis=0)

    row8 = lax.broadcasted_iota(jnp.int32, (8, RWKV_PW), 0)
    valid = lax.broadcasted_iota(jnp.int32, (L, RWKV_PW), 0) < t_real

    def seg_sum(x):
        s_lo = jnp.sum(jnp.where(lo, x, 0.0), axis=-1, keepdims=True)
        s_hi = jnp.sum(jnp.where(lo, 0.0, x), axis=-1, keepdims=True)
        return jnp.where(lo, s_lo, s_hi)

    def shift_mix(ref, c_src, c_all):
        x = ref[:, c_src:c_src + RWKV_PW]
        prev = _shifted(x, last[:, c_all:c_all + RWKV_PW], 1, row8)
        return x + (prev - x) * mu_ref[:, c_all:c_all + RWKV_PW]

    xwa = shift_mix(xwa_ref, 0, w3)
    lr_w = _mxu_f32(jnp.tanh(xwa), w2_ref[...], _NN)
    lr_a = _mxu_f32(xwa, a2_ref[...], _NN)
    a_ak, a_rk, a_rb, nn, tt, sread, kdbd, egl, vs, bonus = [], [], [], [], [], [], [], [], [], []
    for j in pairs:
        sl = slice(j * RWKV_PW, (j + 1) * RWKV_PW)
        r = shift_mix(r_ref, j * RWKV_PW, j * RWKV_PW)
        k = shift_mix(k_ref, j * RWKV_PW, RWKV_WIDTH + j * RWKV_PW)
        v = shift_mix(v_ref, j * RWKV_PW, 2 * RWKV_WIDTH + j * RWKV_PW)
        w_log = -_softplus(-(w0_ref[:, sl] + lr_w[:, sl])) - 0.5
        lw = -jnp.exp(w_log)
        a7 = _sigmoid(a0_ref[:, sl] + lr_a[:, sl])
        kx = k * kkp_ref[:, sl]
        kk = kx * lax.rsqrt(seg_sum(kx * kx) + 1e-6)
        k = k * (1.0 + (a7 - 1.0) * ka_ref[:, sl])
        if t_real < L:
            lw, kk, k, v = (jnp.where(valid, a, 0.0) for a in (lw, kk, k, v))
        b = kk * a7
        bonus.append(seg_sum(r * k * rk_ref[:, sl]) * v)
        vs.append(stack(v))
        g = _mxu_f32(tril, lw, _NN)
        gp = g - lw
        gm = g[L // 2 - 1:L // 2, :]
        gl = g[L - 1:L, :]
        e_neg = jnp.exp(gm - g)
        lhs = cat([stack(kk * jnp.exp(gp - gm)), stack(r * jnp.exp(g - gm))], axis=0)
        rhs = cat([stack(b * e_neg), stack(k * e_neg)], axis=0)
        full = _mxu(lhs, rhs, _NT)
        a_ab = jnp.where(strict, full[:n2, :n2], 0.0)
        a_ak.append(jnp.where(strict, full[:n2, n2:], 0.0))
        a_rb.append(jnp.where(incl, full[n2:, :n2], 0.0))
        a_rk.append(jnp.where(incl, full[n2:, n2:], 0.0))
        nn.append(-a_ab)
        tt.append(eye - a_ab)
        sread.append(cat([stack(kk * jnp.exp(gp)), stack(r * jnp.exp(g))], axis=0))
        dec = jnp.exp(gl - g)
        kdbd.append(cat([stack(k * dec), stack(-b * dec)], axis=0))
        egl.append(jnp.exp(gl))
    tt = _neumann_inverse(nn, tt, n2, rounds)
    s_old, sr, av, u = [], [], [], []
    for j in pairs:
        s_old.append(S[j])
        sr.append(_mxu(sread[j], s_old[j], _NT))
        av.append(_mxu(cat([a_ak[j], a_rk[j]], axis=0), vs[j], _NN))
    for j in pairs:
        u.append(_mxu(tt[j], sr[j][:n2] + av[j][:n2], _NN))
    for j in pairs:
        sl = slice(j * RWKV_PW, (j + 1) * RWKV_PW)
        o = sr[j][n2:] + av[j][n2:] - _mxu(a_rb[j], u[j], _NN)
        o = o[:L] + o[L:]
        oc = o - seg_sum(o) * (1.0 / n)
        gn = oc * lax.rsqrt(seg_sum(oc * oc) * (1.0 / n) + RWKV_GN_EPS)
        y = gn * lng_ref[:, sl] + lnb_ref[:, sl] + bonus[j]
        mix_ref[:, sl] = y * _silu(gate_ref[:, sl])
        S[j] = s_old[j] * egl[j] + _mxu(cat([vs[j], u[j]], axis=0), kdbd[j], _TN)
    pieces = ((r_ref, 0, RWKV_WIDTH), (k_ref, RWKV_WIDTH, RWKV_WIDTH), (v_ref, 2 * RWKV_WIDTH, RWKV_WIDTH),
              (xwa_ref, w3, RWKV_PW))
    if nchunk > 1:
        for ref, c0, wd in pieces:
            last[:, c0:c0 + wd] = ref[L - 8:L, :]

    @pl.when(ci == nchunk - 1)
    def _():
        for ref, c0, wd in pieces:
            shift_out_ref[:, c0:c0 + wd] = ref[t_real - 1:t_real, :]
        for j in pairs:
            s_ref[2 * j] = S[j][:n, :n]
            s_ref[2 * j + 1] = S[j][n:, n:]


def _rwkv(h, mix_prev, grp, P, st_in, shift_in, s_prev, p):
    L = grp.chunk
    nchunk = grp.t_rows // L
    tail = (RWKV_HEADS, RWKV_HEAD_DIM, RWKV_HEAD_DIM)
    s_ins, s_specs, s_out, s_shape = _state_io(tail, p, st_in, None, grp.bsz)
    full = lambda shape: pl.BlockSpec(shape, lambda b, c: (0,) * len(shape))
    row = lambda a: a.reshape(1, -1)
    zr = jnp.zeros((RWKV_HEAD_DIM, RWKV_WIDTH), F32)
    w2 = jnp.concatenate([P['rwkv_w2'], zr], axis=0)
    a2 = jnp.concatenate([zr, P['rwkv_a2']], axis=0)
    off = ODD_OFF[4]
    ins = [h, h, h, h, h, row(P['rwkv_mu']), row(P['rwkv_w0']), w2, row(P['rwkv_a0']), a2, row(P['rwkv_kk']),
           row(P['rwkv_ka']), row(P['rwkv_rk']), row(P['rwkv_ln_g']), row(P['rwkv_ln_b'])]
    vec = full((1, RWKV_WIDTH))
    specs = [grp.spec(L, RWKV_WIDTH, off), grp.spec(L, RWKV_WIDTH, off + RWKV_WIDTH),
             grp.spec(L, RWKV_WIDTH, off + 2 * RWKV_WIDTH), grp.spec(L, RWKV_PW, off + 3 * RWKV_WIDTH),
             grp.spec(L, RWKV_WIDTH, ODD_OFF[5]), full((1, RWKV_SHIFT_DIM)), vec, full((RWKV_PW, RWKV_WIDTH)), vec,
             full((RWKV_PW, RWKV_WIDTH)), vec, vec, vec, vec, vec]
    if st_in is not None:
        ins += [shift_in] + s_ins
        specs += [pl.BlockSpec((None, None, 1, RWKV_SHIFT_DIM), lambda b, c: (p, b, 0, 0))] + s_specs
    any_spec = pl.BlockSpec(memory_space=pl.ANY)
    alias = {}
    for prev, out_idx in ((mix_prev, 0), (s_prev, 1)):
        if prev is not None:
            alias[len(ins)] = out_idx
            ins.append(prev)
            specs.append(any_spec)
    n_prev = (mix_prev is not None) + (s_prev is not None)
    return pl.pallas_call(
        functools.partial(_rwkv_kernel, L=L, nchunk=nchunk, rounds=int(math.log2(L)), t_real=grp.t_real or L,
                          has_state=st_in is not None, has_prev=n_prev),
        grid=(grp.bsz, nchunk),
        in_specs=specs,
        out_specs=[grp.spec(L, RWKV_WIDTH, GDN_WIDTH), s_out,
                   pl.BlockSpec((None, 1, RWKV_SHIFT_DIM), lambda b, c: (b, 0, 0))],
        out_shape=[jax.ShapeDtypeStruct((h.shape[0], MIX_WIDTH), F32), s_shape,
                   jax.ShapeDtypeStruct((grp.bsz, 1, RWKV_SHIFT_DIM), F32)],
        scratch_shapes=[pltpu.VMEM((RWKV_PAIRS, RWKV_PW, RWKV_PW), F32), pltpu.VMEM((8, RWKV_SHIFT_DIM), F32)],
        input_output_aliases=alias,
        compiler_params=_cparams(("parallel", "arbitrary")),
        name="rwkv7",
    )(*ins)


def _pad_t(a, t_to):
    t = a.shape[1]
    if t == t_to:
        return a
    return jnp.pad(a, [(0, 0), (0, t_to - t)] + [(0, 0)] * (a.ndim - 2))


def _causal_conv(u, buf, w):
    t = u.shape[1]
    full = jnp.concatenate([buf, u], axis=1)
    out = full[:, 0:t] * w[0]
    for i in range(1, CONV_W):
        out = out + full[:, i:i + t] * w[i]
    return out, full[:, t:]


def _rms(x, g, eps=1e-6):
    return x * lax.rsqrt(jnp.mean(x * x, -1, keepdims=True) + eps) * g


def _l2n(x, eps=1e-6):
    return x * lax.rsqrt(jnp.sum(x * x, -1, keepdims=True) + eps)


def _even_mix(h, z, mem_k, mem_v, layer, st_in, st_prev, s_conv, P, long_seq):
    bsz, t, _ = h.shape
    p = layer // 2
    seg = lambda i: _seg(h, EVEN_OFF, EVEN_SIZES, i)
    gq, gk, gv, ggate, sz, sxbc, sdt, mq, mgate = (seg(i) for i in (0, 1, 2, 4, 5, 6, 7, 8, 9))
    tp = t if long_seq else SMALL_T
    log_a = jax.nn.log_sigmoid(z) / GLA_TAU
    L = GLA_CHUNK if long_seq else SMALL_T
    o_gla, s_gla = _gla(_pad_t(gq * GLA_DK ** -0.5, tp), _pad_t(gk, tp), _pad_t(gv, tp), _pad_t(log_a, tp),
                        st_in['gla'], st_prev['gla'], p, L, 256 if long_seq else SMALL_T)
    o_gla = _rms(o_gla[:, :t].reshape(bsz, t, GLA_HEADS, GLA_DV), P['gla_norm']).reshape(bsz, t, GLA_WIDTH)
    o_gla = o_gla * jax.nn.silu(ggate)
    xbc, s_conv = _causal_conv(sxbc, s_conv, P['ssd_conv_w'])
    xbc = jax.nn.silu(xbc + P['ssd_conv_b'])
    sx = xbc[..., :SSD_WIDTH]
    sb = xbc[..., SSD_WIDTH:SSD_WIDTH + SSD_GROUPS * SSD_STATE]
    sc = xbc[..., SSD_WIDTH + SSD_GROUPS * SSD_STATE:]
    dt = jax.nn.softplus(sdt + P['ssd_dt_bias'])
    a = -jnp.exp(P['ssd_a_log'])
    xh = sx.reshape(bsz, t, SSD_HEADS, SSD_HEAD_DIM)
    xdt = (xh * dt[..., None]).reshape(bsz, t, SSD_WIDTH)
    L = SSD_CHUNK if long_seq else SMALL_T
    y, s_ssd = _ssd(_pad_t(xdt, tp), _pad_t(sb, tp), _pad_t(sc, tp), _pad_t(dt * a, tp),
                    st_in['ssd'], st_prev['ssd'], p, L)
    y = y[:, :t].reshape(bsz, t, SSD_HEADS, SSD_HEAD_DIM) + xh * P['ssd_d'][:, None]
    y = (y.reshape(bsz, t, SSD_WIDTH) * jax.nn.silu(sz)).reshape(bsz, t, SSD_GROUPS, SSD_WIDTH // SSD_GROUPS)
    y = _rms(y, P['ssd_norm'].reshape(SSD_GROUPS, SSD_WIDTH // SSD_GROUPS)).reshape(bsz, t, SSD_WIDTH)
    return jnp.concatenate([o_gla, y], axis=-1), dict(gla=s_gla, ssd=s_ssd), s_conv


def _odd_mix(h, mix, grp, mem_k, mem_v, layer, st_in, st_prev, conv_in, shift_in, P):
    p = layer // 2
    mix, s_gdn, s_conv = _gdn(h, mix, grp, P, st_in['gdn'], conv_in, st_prev['gdn'], p)
    mix, s_rwkv, s_shift = _rwkv(h, mix, grp, P, st_in['rwkv'], shift_in, st_prev['rwkv'], p)
    mix = _mem_attention(h, mix, grp, ODD_OFF[6], ODD_OFF[7], mem_k, mem_v, layer, min(512, grp.t_rows))
    return mix, dict(gdn=s_gdn, rwkv=s_rwkv), s_conv, s_shift.reshape(grp.bsz, RWKV_SHIFT_DIM)


def kernel(x_prompt, x_sample, mem_prompt, cache_mem_k, cache_mem_v, state_gla, state_ssd, state_ssd_conv, state_gdn, state_gdn_conv, state_rwkv, state_rwkv_shift, mem_w_kv, ev_w_in, ev_gla_w2, ev_gla_b, ev_gla_norm, ev_ssd_conv_w, ev_ssd_conv_b, ev_ssd_dt_bias, ev_ssd_a_log, ev_ssd_d, ev_ssd_norm, ev_w_out, ev_ln_g, ev_ln_b, od_w_in, od_gdn_conv_w, od_gdn_dt_bias, od_gdn_a_log, od_gdn_norm, od_rwkv_mu, od_rwkv_w0, od_rwkv_w2, od_rwkv_a0, od_rwkv_a2, od_rwkv_kk, od_rwkv_ka, od_rwkv_rk, od_rwkv_ln_g, od_rwkv_ln_b, od_w_out, od_ln_g, od_ln_b):
    ev = dict(w_in=ev_w_in, gla_w2=ev_gla_w2, gla_b=ev_gla_b, gla_norm=ev_gla_norm,
              ssd_conv_w=ev_ssd_conv_w, ssd_conv_b=ev_ssd_conv_b, ssd_dt_bias=ev_ssd_dt_bias,
              ssd_a_log=ev_ssd_a_log, ssd_d=ev_ssd_d, ssd_norm=ev_ssd_norm,
              w_out=ev_w_out, ln_g=ev_ln_g, ln_b=ev_ln_b)
    od = dict(w_in=od_w_in, gdn_conv_w=od_gdn_conv_w, gdn_dt_bias=od_gdn_dt_bias, gdn_a_log=od_gdn_a_log,
              gdn_norm=od_gdn_norm, rwkv_mu=od_rwkv_mu, rwkv_w0=od_rwkv_w0, rwkv_w2=od_rwkv_w2,
              rwkv_a0=od_rwkv_a0, rwkv_a2=od_rwkv_a2, rwkv_kk=od_rwkv_kk, rwkv_ka=od_rwkv_ka,
              rwkv_rk=od_rwkv_rk, rwkv_ln_g=od_rwkv_ln_g, rwkv_ln_b=od_rwkv_ln_b,
              w_out=od_w_out, ln_g=od_ln_g, ln_b=od_ln_b)
    bp, tp, _ = x_prompt.shape
    bs, ts, _ = x_sample.shape
    mp, ms = bp * tp, bs * SMALL_T
    grp_p = _Group(bp, tp, None, 0, GDN_CHUNK)
    grp_s = _Group(bs, SMALL_T, ts, mp, SMALL_T)

    w_kv = jnp.moveaxis(mem_w_kv, 0, 1).reshape(D_MODEL, DEPTH * 2 * MEM_WIDTH).astype(BF16)
    kv = _matmul(mem_prompt.reshape(bp * MEM_LEN, D_MODEL).astype(BF16), w_kv, 512, 1024)
    kv6 = kv.reshape(bp, MEM_LEN, DEPTH, 2, MEM_HEADS, MEM_HEAD_DIM)
    mem_k_p = jnp.moveaxis(kv6[:, :, :, 0], 2, 0)
    mem_v_p = jnp.moveaxis(kv6[:, :, :, 1], 2, 0)
    flat = lambda a: a.reshape(a.shape[:-2] + (MEM_WIDTH,))

    x = jnp.concatenate([x_prompt.reshape(mp, D_MODEL),
                         _pad_t(x_sample, SMALL_T).reshape(ms, D_MODEL)], axis=0)
    x_bf = x.astype(BF16)
    zp = lambda shape: jnp.zeros(shape, F32)
    names = ('gla', 'ssd', 'gdn', 'rwkv')
    none = {n: None for n in names}
    in_s = dict(gla=state_gla, ssd=state_ssd, gdn=state_gdn, rwkv=state_rwkv)
    shift_s = state_rwkv_shift.reshape(N_PAIRS, bs, 1, RWKV_SHIFT_DIM)
    out_p, out_s = dict(none), dict(none)
    small_p = {n: [] for n in ('ssd_conv', 'gdn_conv', 'rwkv_shift')}
    small_s = {n: [] for n in small_p}
    tm = 512
    for layer in range(DEPTH):
        p = layer // 2
        if layer % 2 == 0:
            P = {n: w[p] for n, w in ev.items()}
            h = _matmul(x_bf, _pack_w_in(P['w_in'], EVEN_SIZES, EVEN_ORDER, EVEN_N), tm, PROJ_TN)
            glr = _seg(h, EVEN_OFF, EVEN_SIZES, 3)
            z = _matmul(glr, P['gla_w2'], tm, GLA_QK, precision=HI) + P['gla_b']
            hs = h[mp:].reshape(bs, SMALL_T, EVEN_N)[:, :ts]
            zs = z[mp:].reshape(bs, SMALL_T, GLA_QK)[:, :ts]
            mix_p, new, c1 = _even_mix(h[:mp].reshape(bp, tp, EVEN_N), z[:mp].reshape(bp, tp, GLA_QK),
                                       None, None, layer, none, out_p, zp((bp, CONV_W - 1, SSD_CONV_DIM)), P, True)
            out_p.update(new)
            mix_s, new, c2 = _even_mix(hs, zs, None, None, layer, in_s, out_s, state_ssd_conv[p], P, False)
            out_s.update(new)
            small_p['ssd_conv'].append(c1)
            small_s['ssd_conv'].append(c2)
            mw = MIX_WIDTH - MEM_WIDTH
            mix = jnp.concatenate([mix_p.reshape(mp, mw), _pad_t(mix_s, SMALL_T).reshape(ms, mw)], axis=0)
            mix = jnp.pad(mix, [(0, 0), (0, MEM_WIDTH)])
            mix = _mem_attention(h, mix, grp_p, EVEN_OFF[8], EVEN_OFF[9], kv, kv, layer, 512)
            mix = _mem_attention(h, mix, grp_s, EVEN_OFF[8], EVEN_OFF[9], cache_mem_k, cache_mem_v, layer, SMALL_T)
        else:
            P = {n: w[p] for n, w in od.items()}
            h = _matmul(x_bf, _pack_w_in(P['w_in'], ODD_SIZES, ODD_ORDER, ODD_N), tm, PROJ_TN)
            mix, new, c1, h1 = _odd_mix(h, None, grp_p, kv, kv, layer, none, out_p, None, None, P)
            out_p.update(new)
            mix, new, c2, h2 = _odd_mix(h, mix, grp_s, cache_mem_k, cache_mem_v, layer, in_s, out_s,
                                        state_gdn_conv, shift_s, P)
            out_s.update(new)
            small_p['gdn_conv'].append(c1)
            small_s['gdn_conv'].append(c2)
            small_p['rwkv_shift'].append(h1)
            small_s['rwkv_shift'].append(h2)
        x, x_bf = _out_ln(mix, P['w_out'].astype(BF16), x, P['ln_g'], P['ln_b'])

    y_prompt = x[:mp].reshape(bp, tp, D_MODEL)
    y_sample = x[mp:].reshape(bs, SMALL_T, D_MODEL)[:, :ts]
    st = lambda d, n: jnp.stack(d[n])
    return (y_prompt, y_sample, mem_k_p, mem_v_p,
            out_p['gla'], out_s['gla'], out_p['ssd'], out_s['ssd'],
            st(small_p, 'ssd_conv'), st(small_s, 'ssd_conv'), out_p['gdn'], out_s['gdn'],
            st(small_p, 'gdn_conv'), st(small_s, 'gdn_conv'), out_p['rwkv'], out_s['rwkv'],
            st(small_p, 'rwkv_shift'), st(small_s, 'rwkv_shift'))
```

```python
import functools
import math

import numpy as np
import jax
import jax.numpy as jnp
from jax import lax
from jax.experimental import pallas as pl
from jax.experimental.pallas import tpu as pltpu

F32 = jnp.float32
BF16 = jnp.bfloat16
HI = lax.Precision.HIGHEST

D_MODEL = 2048
DEPTH = 4
N_PAIRS = DEPTH // 2
CONV_W = 4
MEM_LEN = 256
MEM_HEADS = 4
MEM_HEAD_DIM = 256
MEM_WIDTH = 1024
GLA_HEADS = 4
GLA_DK = 128
GLA_DV = 256
GLA_QK = 512
GLA_WIDTH = 1024
GLA_RANK = 16
GLA_TAU = 16.0
SSD_WIDTH = 2048
SSD_HEAD_DIM = 64
SSD_HEADS = 32
SSD_GROUPS = 4
SSD_REP = 8
SSD_STATE = 128
SSD_CONV_DIM = SSD_WIDTH + 2 * SSD_GROUPS * SSD_STATE
GDN_WIDTH = 2048
GDN_HEAD_DIM = 128
GDN_HEADS = 16
RWKV_WIDTH = 1024
RWKV_HEAD_DIM = 64
RWKV_HEADS = 16
RWKV_W_RANK = 64
RWKV_A_RANK = 64
RWKV_SHIFT_DIM = 3 * RWKV_WIDTH + RWKV_W_RANK + RWKV_A_RANK
RWKV_GN_EPS = 64e-5
EVEN_SIZES = (GLA_QK, GLA_QK, GLA_WIDTH, GLA_RANK, GLA_WIDTH, SSD_WIDTH, SSD_CONV_DIM, SSD_HEADS,
              MEM_WIDTH, MEM_WIDTH)
ODD_SIZES = (3 * GDN_WIDTH, GDN_WIDTH, GDN_HEADS, GDN_HEADS, RWKV_SHIFT_DIM, RWKV_WIDTH, MEM_WIDTH, MEM_WIDTH)
MIX_WIDTH = 4096
DEEPNORM_ALPHA = (2 * DEPTH) ** 0.25

EVEN_ORDER = (5, 2, 4, 8, 9, 6, 0, 1, 3, 7)
ODD_ORDER = (0, 1, 5, 6, 7, 4, 2, 3)
PROJ_TN = 768
VMEM_LIMIT = 48 * 1024 * 1024

GLA_CHUNK = 16
SSD_CHUNK = 64
GDN_CHUNK = 64
RWKV_CHUNK = 64
SMALL_T = 8

_NN = ((1,), (0,))
_NT = ((1,), (1,))
_TN = ((0,), (0,))


def _packed_layout(sizes, order):
    offs, o = {}, 0
    for i in order:
        offs[i] = o
        o += sizes[i]
    total = -(-o // PROJ_TN) * PROJ_TN
    return offs, total


EVEN_OFF, EVEN_N = _packed_layout(EVEN_SIZES, EVEN_ORDER)
ODD_OFF, ODD_N = _packed_layout(ODD_SIZES, ODD_ORDER)


def _pack_w_in(w, sizes, order, total):
    segs = jnp.split(w, np.cumsum(sizes)[:-1].tolist(), axis=-1)
    parts = [segs[i] for i in order]
    used = sum(sizes)
    if total > used:
        parts.append(jnp.zeros((w.shape[0], total - used), w.dtype))
    return jnp.concatenate(parts, axis=-1).astype(BF16)


def _seg(h, offs, sizes, i):
    return h[..., offs[i]:offs[i] + sizes[i]]


def _cparams(sem):
    return pltpu.CompilerParams(dimension_semantics=sem, vmem_limit_bytes=VMEM_LIMIT)


def _mxu(a, b, dims):
    return lax.dot_general(a.astype(BF16), b.astype(BF16), (dims, ((), ())), preferred_element_type=F32)


def _mxu_f32(a, b, dims):
    return lax.dot_general(a, b, (dims, ((), ())), precision=HI, preferred_element_type=F32)


def _sigmoid(x):
    return 1.0 / (1.0 + jnp.exp(-x))


def _silu(x):
    return x * _sigmoid(x)


def _softplus(x):
    return jnp.maximum(x, 0.0) + jnp.log(1.0 + jnp.exp(-jnp.abs(x)))


def _shifted(u, prev8, j, row8):
    ru = pltpu.roll(u, j, 0)
    top = jnp.where(row8 < j, pltpu.roll(prev8, j, 0), ru[:8])
    return top if u.shape[0] == 8 else jnp.concatenate([top, ru[8:]], axis=0)


class _Group:
    def __init__(self, bsz, t_rows, t_real, row0, chunk):
        self.bsz, self.t_rows, self.t_real, self.row0, self.chunk = bsz, t_rows, t_real, row0, chunk

    def spec(self, rows, width, off):
        assert off % width == 0 and self.row0 % rows == 0 and self.t_rows % rows == 0
        base, per, cb = self.row0 // rows, self.t_rows // rows, off // width
        return pl.BlockSpec((rows, width), lambda b, i: (base + b * per + i, cb))


def _alias_last(n_inputs, has_prev, out_index=0):
    return {n_inputs - 1: out_index} if has_prev else {}


def _neumann_inverse(nn, tt, n, rounds):
    idx = range(len(nn))
    if rounds >= 2:
        for j in idx:
            nn[j] = _mxu(nn[j], nn[j], _NN)
        for _ in range(rounds - 2):
            for j in idx:
                both = _mxu(nn[j], jnp.concatenate([tt[j], nn[j]], axis=1), _NN)
                tt[j] = tt[j] + both[:, :n]
                nn[j] = both[:, n:]
        for j in idx:
            tt[j] = tt[j] + _mxu(nn[j], tt[j], _NN)
    return tt


def _state_io(tail, p, s_in, s_prev, bsz):
    zeros = (0,) * len(tail)
    spec = pl.BlockSpec((None, None) + tail, lambda *g: (p, g[0]) + zeros)
    ins, specs = [], []
    if s_in is not None:
        ins.append(s_in)
        specs.append(spec)
    if s_prev is not None:
        ins.append(s_prev)
        specs.append(pl.BlockSpec(memory_space=pl.ANY))
    shape = jax.ShapeDtypeStruct((N_PAIRS, bsz) + tail, F32)
    return ins, specs, spec, shape


def _mm_kernel(x_ref, w_ref, o_ref, *, precision):
    o_ref[...] = jnp.dot(x_ref[...], w_ref[...], preferred_element_type=F32, precision=precision)


def _matmul(x, w, tm, tn, precision=None):
    m, k = x.shape
    n = w.shape[1]
    assert m % tm == 0 and n % tn == 0
    return pl.pallas_call(
        functools.partial(_mm_kernel, precision=precision),
        grid=(n // tn, m // tm),
        in_specs=[pl.BlockSpec((tm, k), lambda j, i: (i, 0)),
                  pl.BlockSpec((k, tn), lambda j, i: (0, j))],
        out_specs=pl.BlockSpec((tm, tn), lambda j, i: (i, j)),
        out_shape=jax.ShapeDtypeStruct((m, n), F32),
        compiler_params=_cparams(("parallel", "parallel")),
        name="matmul",
    )(x, w)


def _out_ln_kernel(mix_ref, w_ref, x_ref, g_ref, b_ref, y_ref, ybf_ref, acc, *, nk):
    kk = pl.program_id(1)

    @pl.when(kk == 0)
    def _():
        acc[...] = jnp.zeros_like(acc)

    acc[...] += jnp.dot(mix_ref[...].astype(BF16), w_ref[...], preferred_element_type=F32)

    @pl.when(kk == nk - 1)
    def _():
        z = DEEPNORM_ALPHA * x_ref[...] + acc[...]
        zc = z - jnp.mean(z, axis=-1, keepdims=True)
        var = jnp.mean(zc * zc, axis=-1, keepdims=True)
        y = zc * lax.rsqrt(var + 1e-5) * g_ref[...] + b_ref[...]
        y_ref[...] = y
        ybf_ref[...] = y.astype(BF16)


def _out_ln(mix, w, x, g, b, tm=512, tk=1024):
    m, k = mix.shape
    d = w.shape[1]
    nk = k // tk
    return pl.pallas_call(
        functools.partial(_out_ln_kernel, nk=nk),
        grid=(m // tm, nk),
        in_specs=[pl.BlockSpec((tm, tk), lambda i, j: (i, j)),
                  pl.BlockSpec((tk, d), lambda i, j: (j, 0)),
                  pl.BlockSpec((tm, d), lambda i, j: (i, 0)),
                  pl.BlockSpec((1, d), lambda i, j: (0, 0)),
                  pl.BlockSpec((1, d), lambda i, j: (0, 0))],
        out_specs=[pl.BlockSpec((tm, d), lambda i, j: (i, 0)),
                   pl.BlockSpec((tm, d), lambda i, j: (i, 0))],
        out_shape=[jax.ShapeDtypeStruct((m, d), F32), jax.ShapeDtypeStruct((m, d), BF16)],
        scratch_shapes=[pltpu.VMEM((tm, d), F32)],
        compiler_params=_cparams(("parallel", "arbitrary")),
        name="out_ln",
    )(mix, w, x, g.reshape(1, d), b.reshape(1, d))


def _mem_kernel(q_ref, gate_ref, k_ref, v_ref, *rest):
    o_ref = rest[-1]
    for h in range(MEM_HEADS):
        sl = slice(h * MEM_HEAD_DIM, (h + 1) * MEM_HEAD_DIM)
        k = k_ref[:, sl]
        v = v_ref[:, sl]
        s = _mxu(q_ref[:, sl], k, _NT) * MEM_HEAD_DIM ** -0.5
        p = jnp.exp(s - jnp.max(s, axis=-1, keepdims=True))
        p = p / jnp.sum(p, axis=-1, keepdims=True)
        o_ref[:, sl] = _mxu(p, v, _NN) * _silu(gate_ref[:, sl])


def _mem_attention(h, mix_prev, grp, q_off, gate_off, mem_k, mem_v, layer, tq):
    if mem_k.ndim == 4:
        kv_specs = [pl.BlockSpec((None, None, MEM_LEN, MEM_WIDTH), lambda b, i: (layer, b, 0, 0))] * 2
    else:
        kv_specs = [pl.BlockSpec((MEM_LEN, MEM_WIDTH), lambda b, i: (b, 2 * layer)),
                    pl.BlockSpec((MEM_LEN, MEM_WIDTH), lambda b, i: (b, 2 * layer + 1))]
    ins = [h, h, mem_k, mem_v] + ([] if mix_prev is None else [mix_prev])
    specs = [grp.spec(tq, MEM_WIDTH, q_off), grp.spec(tq, MEM_WIDTH, gate_off)] + kv_specs
    if mix_prev is not None:
        specs.append(pl.BlockSpec(memory_space=pl.ANY))
    return pl.pallas_call(
        _mem_kernel,
        grid=(grp.bsz, grp.t_rows // tq),
        in_specs=specs,
        out_specs=grp.spec(tq, MEM_WIDTH, MIX_WIDTH - MEM_WIDTH),
        out_shape=jax.ShapeDtypeStruct((h.shape[0], MIX_WIDTH), F32),
        input_output_aliases=_alias_last(len(ins), mix_prev is not None),
        compiler_params=_cparams(("parallel", "parallel")),
        name="mem_attention",
    )(*ins)


def _gla_kernel(q_ref, k_ref, v_ref, gate_ref, sm_ref, w2_ref, gb_ref, nw_ref, *rest, L, nchunk, nblk, t_real,
                has_state, has_prev):
    n_opt = has_state + has_prev
    mix_ref, s_ref, S, G = rest[n_opt:]
    tb = pl.program_id(1)

    @pl.when(tb == 0)
    def _():
        S[...] = rest[0][...] if has_state else jnp.zeros_like(S)

    z = _mxu_f32(sm_ref[...], w2_ref[...], _NN) + gb_ref[...]
    G[...] = -_softplus(-z) * (1.0 / GLA_TAU)
    row = lax.broadcasted_iota(jnp.int32, (L, L), 0)
    col = lax.broadcasted_iota(jnp.int32, (L, L), 1)
    tril = (col <= row).astype(F32)
    t_iota = lax.broadcasted_iota(jnp.int32, (L, GLA_DK), 0)
    r128 = lax.broadcasted_iota(jnp.int32, (GLA_DK, GLA_DK), 0)
    c128 = lax.broadcasted_iota(jnp.int32, (GLA_DK, GLA_DK), 1)
    nw = nw_ref[...]

    def chunk(ci, carry):
        r0 = pl.multiple_of(ci * L, L)
        for h in range(GLA_HEADS):
            ks = slice(h * GLA_DK, (h + 1) * GLA_DK)
            vs = slice(h * GLA_DV, (h + 1) * GLA_DV)
            q = q_ref[pl.ds(r0, L), ks] * GLA_DK ** -0.5
            k = k_ref[pl.ds(r0, L), ks]
            g = G[pl.ds(r0, L), ks]
            v = v_ref[pl.ds(r0, L), vs]
            if t_real < L:
                g = jnp.where(t_iota < t_real, g, 0.0)
                k = jnp.where(t_iota < t_real, k, 0.0)
            b = _mxu_f32(tril, g, _NN)
            b_last = b[L - 1:L, :]
            att = jnp.zeros((L, L), F32)
            for s in range(L):
                d = jnp.where(t_iota >= s, b - b[s:s + 1, :], -jnp.inf)
                p = jnp.exp(d) * q * k[s:s + 1, :]
                att = jnp.where(col == s, jnp.sum(p, axis=-1, keepdims=True), att)
            s_old = S[h]
            o = _mxu_f32(q * jnp.exp(b), s_old, _NN) + _mxu_f32(att, v, _NN)
            y = o * lax.rsqrt(jnp.mean(o * o, axis=-1, keepdims=True) + 1e-6) * nw
            mix_ref[pl.ds(r0, L), vs] = y * _silu(gate_ref[pl.ds(r0, L), vs])
            kd = k * jnp.exp(b_last - b)
            e_col = jnp.sum(jnp.where(r128 == c128, jnp.exp(b_last), 0.0), axis=-1, keepdims=True)
            S[h] = s_old * e_col + _mxu_f32(kd, v, _TN)
        return carry

    lax.fori_loop(0, nchunk, chunk, 0)

    @pl.when(tb == nblk - 1)
    def _():
        s_ref[...] = S[...]


def _gla(h, mix_prev, grp, P, st_in, s_prev, p, L, tb):
    nblk = grp.t_rows // tb
    tail = (GLA_HEADS, GLA_DK, GLA_DV)
    s_ins, s_specs, s_out, s_shape = _state_io(tail, p, st_in, None, grp.bsz)
    full = lambda shape: pl.BlockSpec(shape, lambda b, c: (0,) * len(shape))
    w2 = jnp.concatenate([P['gla_w2'], jnp.zeros((128 - GLA_RANK, GLA_QK), F32)], axis=0)
    ins = [h, h, h, h, h, w2, P['gla_b'].reshape(1, GLA_QK), P['gla_norm'].reshape(1, GLA_DV)]
    specs = [grp.spec(tb, GLA_QK, EVEN_OFF[0]), grp.spec(tb, GLA_QK, EVEN_OFF[1]), grp.spec(tb, GLA_WIDTH, EVEN_OFF[2]),
             grp.spec(tb, GLA_WIDTH, EVEN_OFF[4]), grp.spec(tb, 128, EVEN_OFF[3]),
             full((128, GLA_QK)), full((1, GLA_QK)), full((1, GLA_DV))]
    ins += s_ins
    specs += s_specs
    any_spec = pl.BlockSpec(memory_space=pl.ANY)
    alias = {}
    for prev, out_idx in ((mix_prev, 0), (s_prev, 1)):
        if prev is not None:
            alias[len(ins)] = out_idx
            ins.append(prev)
            specs.append(any_spec)
    n_prev = (mix_prev is not None) + (s_prev is not None)
    return pl.pallas_call(
        functools.partial(_gla_kernel, L=L, nchunk=tb // L, nblk=nblk, t_real=grp.t_real or L,
                          has_state=st_in is not None, has_prev=n_prev),
        grid=(grp.bsz, nblk),
        in_specs=specs,
        out_specs=[grp.spec(tb, GLA_WIDTH, SSD_WIDTH), s_out],
        out_shape=[jax.ShapeDtypeStruct((h.shape[0], MIX_WIDTH), F32), s_shape],
        scratch_shapes=[pltpu.VMEM(tail, F32), pltpu.VMEM((tb, GLA_QK), F32)],
        input_output_aliases=alias,
        compiler_params=_cparams(("parallel", "arbitrary")),
        name="gla",
    )(*ins)


SSD_GW = SSD_REP * SSD_HEAD_DIM


def _ssd_lanes(L):
    return max(SSD_REP * L, 128)


def _ssd_kernel(sz_ref, xbc_ref, sm_ref, cw_ref, cbias_ref, dtb_ref, alog_ref, dvec_ref, nw_ref, ep_ref, es_ref, *rest,
                L, nchunk, t_real, has_state, has_prev):
    n_opt = 2 * has_state + has_prev
    mix_ref, s_ref, conv_out_ref, ST, tail = rest[n_opt:]
    ci = pl.program_id(1)
    gs = _ssd_lanes(L)
    cat = jnp.concatenate

    @pl.when(ci == 0)
    def _():
        tail[...] = jnp.zeros_like(tail)
        if has_state:
            tail[8 - (CONV_W - 1):8, :] = rest[0][...]
            for g in range(SSD_GROUPS):
                ST[g] = rest[1][g * SSD_REP:(g + 1) * SSD_REP].reshape(SSD_GW, SSD_STATE).T
        else:
            ST[...] = jnp.zeros_like(ST)

    d = 128
    row8 = lax.broadcasted_iota(jnp.int32, (8, d), 0)

    def conv_tile(c0):
        u = xbc_ref[:, c0:c0 + d]
        p8 = tail[:, c0:c0 + d]
        w = cw_ref[:, c0:c0 + d]
        acc = u * w[CONV_W - 1:CONV_W] + cbias_ref[:, c0:c0 + d]
        for j in range(1, CONV_W):
            acc = acc + _shifted(u, p8, j, row8) * w[CONV_W - 1 - j:CONV_W - j]
        return _silu(acc)

    row = lax.broadcasted_iota(jnp.int32, (L, L), 0)
    col = lax.broadcasted_iota(jnp.int32, (L, L), 1)
    tril = (col <= row).astype(F32)
    dt = _softplus(sm_ref[:, GLA_RANK:GLA_RANK + SSD_HEADS] + dtb_ref[...])
    if t_real < L:
        dt = jnp.where(lax.broadcasted_iota(jnp.int32, (L, SSD_HEADS), 0) < t_real, dt, 0.0)
    c = _mxu_f32(tril, dt * -jnp.exp(alog_ref[...]), _NN)
    ep = ep_ref[...]
    dt_x = _mxu_f32(dt, ep, _NN)
    c_x = _mxu_f32(c, ep, _NN)
    c_s = _mxu_f32(c, es_ref[...], _NN)
    t_i = lax.broadcasted_iota(jnp.int32, (L, SSD_GROUPS * gs), 0)
    s_i = lax.broadcasted_iota(jnp.int32, (L, SSD_GROUPS * gs), 1) & (L - 1)
    c_src = jnp.sum(jnp.where(t_i == s_i, c_s, 0.0), axis=0, keepdims=True)
    seg = jnp.exp(jnp.where(s_i <= t_i, c_s - c_src, -jnp.inf))
    blk_r = lax.broadcasted_iota(jnp.int32, (gs, SSD_GW), 0) // L
    blk_c = lax.broadcasted_iota(jnp.int32, (gs, SSD_GW), 1) // SSD_HEAD_DIM
    diag = blk_r == blk_c
    reps = SSD_REP * L
    for g in range(SSD_GROUPS):
        gl = slice(g * SSD_GW, (g + 1) * SSD_GW)
        sx = cat([conv_tile(g * SSD_GW + i * d) for i in range(SSD_GW // d)], axis=1)
        bm = conv_tile(SSD_WIDTH + g * SSD_STATE)
        cm = conv_tile(SSD_WIDTH + SSD_GROUPS * SSD_STATE + g * SSD_STATE)
        xdt = sx * dt_x[:, gl]
        pad_rows = [] if reps == gs else [jnp.zeros((gs - reps, SSD_STATE), F32)]
        cb = _mxu(cm, cat([bm] * SSD_REP + pad_rows, axis=0), _NT)
        pad_rows = [] if reps == gs else [jnp.zeros((gs - reps, SSD_GW), F32)]
        xbd = jnp.where(diag, cat([xdt] * SSD_REP + pad_rows, axis=0), 0.0)
        st = ST[g]
        y = _mxu(cb * seg[:, g * gs:(g + 1) * gs], xbd, _NN) + _mxu(cm, st, _NN) * jnp.exp(c_x[:, gl])
        y = (y + sx * dvec_ref[:, gl]) * _silu(sz_ref[:, gl])
        y = y * lax.rsqrt(jnp.mean(y * y, axis=-1, keepdims=True) + 1e-6) * nw_ref[:, gl]
        mix_ref[:, gl] = y
        c_end = c_x[L - 1:L, gl]
        ST[g] = st * jnp.exp(c_end) + _mxu(bm, xdt * jnp.exp(c_end - c_x[:, gl]), _TN)
    if nchunk > 1:
        tail[...] = xbc_ref[L - 8:L, :]

    @pl.when(ci == nchunk - 1)
    def _():
        conv_out_ref[...] = xbc_ref[t_real - (CONV_W - 1):t_real, :]
        for g in range(SSD_GROUPS):
            s_ref[g * SSD_REP:(g + 1) * SSD_REP] = ST[g].T.reshape(SSD_REP, SSD_HEAD_DIM, SSD_STATE)


def _ssd(h, mix_prev, grp, P, st_in, conv_in, s_prev, p):
    L = grp.chunk
    nchunk = grp.t_rows // L
    gs = _ssd_lanes(L)
    tail = (SSD_HEADS, SSD_HEAD_DIM, SSD_STATE)
    s_ins, s_specs, s_out, s_shape = _state_io(tail, p, st_in, None, grp.bsz)
    full = lambda shape: pl.BlockSpec(shape, lambda b, c: (0,) * len(shape))
    heads = jnp.arange(SSD_HEADS)[:, None]
    lane_p = jnp.arange(SSD_WIDTH)[None, :]
    ep = (lane_p // SSD_HEAD_DIM == heads).astype(F32)
    lane_s = jnp.arange(SSD_GROUPS * gs)[None, :]
    in_grp = lane_s % gs
    es = ((in_grp < SSD_REP * L) & ((lane_s // gs) * SSD_REP + in_grp // L == heads)).astype(F32)
    row = lambda a: a.reshape(1, -1)
    ins = [h, h, h, P['ssd_conv_w'], row(P['ssd_conv_b']), row(P['ssd_dt_bias']), row(P['ssd_a_log']),
           row(jnp.repeat(P['ssd_d'], SSD_HEAD_DIM)), row(P['ssd_norm']), ep, es]
    specs = [grp.spec(L, SSD_WIDTH, EVEN_OFF[5]), grp.spec(L, SSD_CONV_DIM, EVEN_OFF[6]), grp.spec(L, 128, EVEN_OFF[3]),
             full((CONV_W, SSD_CONV_DIM)), full((1, SSD_CONV_DIM)), full((1, SSD_HEADS)), full((1, SSD_HEADS)),
             full((1, SSD_WIDTH)), full((1, SSD_WIDTH)), full(ep.shape), full(es.shape)]
    if st_in is not None:
        ins += [conv_in] + s_ins
        specs += [pl.BlockSpec((None, None, CONV_W - 1, SSD_CONV_DIM), lambda b, c: (p, b, 0, 0))] + s_specs
    any_spec = pl.BlockSpec(memory_space=pl.ANY)
    alias = {}
    for prev, out_idx in ((mix_prev, 0), (s_prev, 1)):
        if prev is not None:
            alias[len(ins)] = out_idx
            ins.append(prev)
            specs.append(any_spec)
    n_prev = (mix_prev is not None) + (s_prev is not None)
    return pl.pallas_call(
        functools.partial(_ssd_kernel, L=L, nchunk=nchunk, t_real=grp.t_real or L,
                          has_state=st_in is not None, has_prev=n_prev),
        grid=(grp.bsz, nchunk),
        in_specs=specs,
        out_specs=[grp.spec(L, SSD_WIDTH, 0), s_out,
                   pl.BlockSpec((None, CONV_W - 1, SSD_CONV_DIM), lambda b, c: (b, 0, 0))],
        out_shape=[jax.ShapeDtypeStruct((h.shape[0], MIX_WIDTH), F32), s_shape,
                   jax.ShapeDtypeStruct((grp.bsz, CONV_W - 1, SSD_CONV_DIM), F32)],
        scratch_shapes=[pltpu.VMEM((SSD_GROUPS, SSD_STATE, SSD_GW), F32), pltpu.VMEM((8, SSD_CONV_DIM), F32)],
        input_output_aliases=alias,
        compiler_params=_cparams(("parallel", "arbitrary")),
        name="ssd",
    )(*ins)


def _gdn_kernel(qkv_ref, cz_ref, sm_ref, cw_ref, alog_ref, dtb_ref, nw_ref, *rest, L, nchunk, rounds, t_real,
                has_state, has_prev):
    n_opt = 2 * has_state + has_prev
    mix_ref, s_ref, conv_out_ref, S, tail = rest[n_opt:]
    ci = pl.program_id(1)

    @pl.when(ci == 0)
    def _():
        tail[...] = jnp.zeros_like(tail)
        if has_state:
            tail[8 - (CONV_W - 1):8, :] = rest[0][...]
            S[...] = rest[1][...]
        else:
            S[...] = jnp.zeros_like(S)

    n2 = 2 * L
    d = GDN_HEAD_DIM
    cat = jnp.concatenate
    row8 = lax.broadcasted_iota(jnp.int32, (8, d), 0)

    def conv_tile(c0):
        u = qkv_ref[:, c0:c0 + d]
        p8 = tail[:, c0:c0 + d]
        w = cw_ref[:, c0:c0 + d]
        acc = u * w[CONV_W - 1:CONV_W]
        for j in range(1, CONV_W):
            acc = acc + _shifted(u, p8, j, row8) * w[CONV_W - 1 - j:CONV_W - j]
        return _silu(acc)

    def l2n(x):
        return x * lax.rsqrt(jnp.sum(x * x, axis=-1, keepdims=True) + 1e-6)

    row = lax.broadcasted_iota(jnp.int32, (L, L), 0)
    col = lax.broadcasted_iota(jnp.int32, (L, L), 1)
    tril = (col <= row).astype(F32)
    r2 = lax.broadcasted_iota(jnp.int32, (n2, n2), 0)
    c2 = lax.broadcasted_iota(jnp.int32, (n2, n2), 1)
    same = (r2 >= L) == (c2 >= L)
    strict = same & (c2 < r2)
    incl = same & (c2 <= r2)
    upper = same & (r2 <= c2)
    eye = (r2 == c2).astype(F32)
    zl = jnp.zeros((L, d), F32)
    sm = sm_ref[...]
    beta_all = _sigmoid(sm[:, :GDN_HEADS])
    g_all = -jnp.exp(alog_ref[...]) * _softplus(sm[:, GDN_HEADS:2 * GDN_HEADS] + dtb_ref[...])
    if t_real < L:
        valid = lax.broadcasted_iota(jnp.int32, (L, GDN_HEADS), 0) < t_real
        beta_all = jnp.where(valid, beta_all, 0.0)
        g_all = jnp.where(valid, g_all, 0.0)
    c_all = _mxu_f32(tril, g_all, _NN)
    pairs = range(GDN_HEADS // 2)
    nn, tt, qk, kq, kdec, ec, bcol, elast, vst = [], [], [], [], [], [], [], [], []
    for j in pairs:
        h0, h1 = 2 * j, 2 * j + 1
        stack_col = lambda a: cat([a[:, h0:h0 + 1], a[:, h1:h1 + 1]], axis=0)
        c_col = stack_col(c_all)
        beta_col = stack_col(beta_all)
        c_row = jnp.sum(jnp.where(upper, stack_col(g_all), 0.0), axis=0, keepdims=True)
        decay = jnp.exp(jnp.where(incl, c_col - c_row, -jnp.inf))
        last = lambda rows: cat([jnp.broadcast_to(c_all[L - 1:L, h0:h0 + 1], (rows, 1)),
                                 jnp.broadcast_to(c_all[L - 1:L, h1:h1 + 1], (rows, 1))], axis=0)
        q0, q1 = (l2n(conv_tile(h * d)) * d ** -0.5 for h in (h0, h1))
        k0, k1 = (l2n(conv_tile(GDN_WIDTH + h * d)) for h in (h0, h1))
        vst.append(cat([conv_tile(2 * GDN_WIDTH + h0 * d), conv_tile(2 * GDN_WIDTH + h1 * d)], axis=0))
        k_st = cat([cat([k0, zl], axis=1), cat([zl, k1], axis=1)], axis=0)
        q_st = cat([cat([q0, zl], axis=1), cat([zl, q1], axis=1)], axis=0)
        both = cat([k_st, q_st], axis=0)
        full = _mxu(both, k_st, _NT)
        a = jnp.where(strict, full[:n2] * decay * beta_col, 0.0)
        nn.append(-a)
        tt.append(eye - a)
        qk.append(full[n2:] * decay)
        kq.append(both)
        kdec.append(k_st * jnp.exp(last(L) - c_col))
        ec.append(jnp.exp(c_col))
        bcol.append(beta_col)
        elast.append(jnp.exp(last(d)))
    tt = _neumann_inverse(nn, tt, n2, rounds)
    s_old, ksqs, u = [], [], []
    for j in pairs:
        s_old.append(cat([S[2 * j], S[2 * j + 1]], axis=0))
        ksqs.append(_mxu(kq[j], s_old[j], _NN))
    for j in pairs:
        u.append(_mxu(tt[j], bcol[j] * (vst[j] - ec[j] * ksqs[j][:n2]), _NN))
    nw = nw_ref[...]
    for j in pairs:
        o = ec[j] * ksqs[j][n2:] + _mxu(qk[j], u[j], _NN)
        for hh, oh in ((2 * j, o[:L]), (2 * j + 1, o[L:])):
            cols = slice(hh * d, (hh + 1) * d)
            y = oh * lax.rsqrt(jnp.mean(oh * oh, axis=-1, keepdims=True) + 1e-6) * nw
            mix_ref[:, cols] = y * _silu(cz_ref[:, cols])
        new = s_old[j] * elast[j] + _mxu(kdec[j], u[j], _TN)
        S[2 * j] = new[:d]
        S[2 * j + 1] = new[d:]
    if nchunk > 1:
        tail[...] = qkv_ref[L - 8:L, :]

    @pl.when(ci == nchunk - 1)
    def _():
        s_ref[...] = S[...]
        conv_out_ref[...] = qkv_ref[t_real - (CONV_W - 1):t_real, :]


def _gdn(h, mix_prev, grp, P, st_in, conv_in, s_prev, p):
    L = grp.chunk
    nchunk = grp.t_rows // L
    tail = (GDN_HEADS, GDN_HEAD_DIM, GDN_HEAD_DIM)
    s_ins, s_specs, s_out, s_shape = _state_io(tail, p, st_in, None, grp.bsz)
    cw = 3 * GDN_WIDTH
    full = lambda shape: pl.BlockSpec(shape, lambda b, c: (0,) * len(shape))
    ins = [h, h, h, P['gdn_conv_w'], P['gdn_a_log'].reshape(1, GDN_HEADS), P['gdn_dt_bias'].reshape(1, GDN_HEADS),
           P['gdn_norm'].reshape(1, GDN_HEAD_DIM)]
    specs = [grp.spec(L, cw, ODD_OFF[0]), grp.spec(L, GDN_WIDTH, ODD_OFF[1]), grp.spec(L, 128, ODD_OFF[2]),
             full((CONV_W, cw)), full((1, GDN_HEADS)), full((1, GDN_HEADS)), full((1, GDN_HEAD_DIM))]
    if st_in is not None:
        ins += [conv_in] + s_ins
        specs += [pl.BlockSpec((None, None, CONV_W - 1, cw), lambda b, c: (p, b, 0, 0))] + s_specs
    any_spec = pl.BlockSpec(memory_space=pl.ANY)
    alias = {}
    for prev, out_idx in ((mix_prev, 0), (s_prev, 1)):
        if prev is not None:
            alias[len(ins)] = out_idx
            ins.append(prev)
            specs.append(any_spec)
    n_prev = (mix_prev is not None) + (s_prev is not None)
    return pl.pallas_call(
        functools.partial(_gdn_kernel, L=L, nchunk=nchunk, rounds=int(math.log2(L)), t_real=grp.t_real or L,
                          has_state=st_in is not None, has_prev=n_prev),
        grid=(grp.bsz, nchunk),
        in_specs=specs,
        out_specs=[grp.spec(L, GDN_WIDTH, 0), s_out,
                   pl.BlockSpec((None, CONV_W - 1, cw), lambda b, c: (b, 0, 0))],
        out_shape=[jax.ShapeDtypeStruct((h.shape[0], MIX_WIDTH), F32), s_shape,
                   jax.ShapeDtypeStruct((grp.bsz, CONV_W - 1, cw), F32)],
        scratch_shapes=[pltpu.VMEM(tail, F32), pltpu.VMEM((8, cw), F32)],
        input_output_aliases=alias,
        compiler_params=_cparams(("parallel", "arbitrary")),
        name="gdn",
    )(*ins)


RWKV_PAIRS = RWKV_HEADS // 2
RWKV_PW = 2 * RWKV_HEAD_DIM


def _rwkv_kernel(r_ref, k_ref, v_ref, xwa_ref, gate_ref, mu_ref, w0_ref, w2_ref, a0_ref, a2_ref, kkp_ref, ka_ref,
                 rk_ref, lng_ref, lnb_ref, *rest, L, nchunk, rounds, t_real, has_state, has_prev):
    n_opt = 2 * has_state + has_prev
    mix_ref, s_ref, shift_out_ref, S, last = rest[n_opt:]
    ci = pl.program_id(1)
    n = RWKV_HEAD_DIM
    w3 = 3 * RWKV_WIDTH
    cat = jnp.concatenate
    pairs = range(RWKV_PAIRS)

    @pl.when(ci == 0)
    def _():
        last[...] = jnp.zeros_like(last)
        if has_state:
            last[7:8, :] = rest[0][...]
            zn = jnp.zeros((n, n), F32)
            for j in pairs:
                S[j] = cat([cat([rest[1][2 * j], zn], axis=1), cat([zn, rest[1][2 * j + 1]], axis=1)], axis=0)
        else:
            S[...] = jnp.zeros_like(S)

    n2 = 2 * L
    row = lax.broadcasted_iota(jnp.int32, (L, L), 0)
    col = lax.broadcasted_iota(jnp.int32, (L, L), 1)
    tril = (col <= row).astype(F32)
    r2 = lax.broadcasted_iota(jnp.int32, (n2, n2), 0)
    c2 = lax.broadcasted_iota(jnp.int32, (n2, n2), 1)
    same = (r2 >= L) == (c2 >= L)
    strict = same & (c2 < r2)
    incl = same & (c2 <= r2)
    eye = (r2 == c2).astype(F32)
    lane = lax.broadcasted_iota(jnp.int32, (L, RWKV_PW), 1)
    lo = lane < n

    def stack(x):
        return cat([jnp.where(lo, x, 0.0), jnp.where(lo, 0.0, x)], axis=0)

    row8 = lax.broadcasted_iota(jnp.int32, (8, RWKV_PW), 0)
    valid = lax.broadcasted_iota(jnp.int32, (L, RWKV_PW), 0) < t_real

    def seg_sum(x):
        s_lo = jnp.sum(jnp.where(lo, x, 0.0), axis=-1, keepdims=True)
        s_hi = jnp.sum(jnp.where(lo, 0.0, x), axis=-1, keepdims=True)
        return jnp.where(lo, s_lo, s_hi)

    def shift_mix(ref, c_src, c_all):
        x = ref[:, c_src:c_src + RWKV_PW]
        prev = _shifted(x, last[:, c_all:c_all + RWKV_PW], 1, row8)
        return x + (prev - x) * mu_ref[:, c_all:c_all + RWKV_PW]

    xwa = shift_mix(xwa_ref, 0, w3)
    lr_w = _mxu_f32(jnp.tanh(xwa), w2_ref[...], _NN)
    lr_a = _mxu_f32(xwa, a2_ref[...], _NN)
    a_ak, a_rk, a_rb, nn, tt, sread, kdbd, egl, vs, bonus = [], [], [], [], [], [], [], [], [], []
    for j in pairs:
        sl = slice(j * RWKV_PW, (j + 1) * RWKV_PW)
        r = shift_mix(r_ref, j * RWKV_PW, j * RWKV_PW)
        k = shift_mix(k_ref, j * RWKV_PW, RWKV_WIDTH + j * RWKV_PW)
        v = shift_mix(v_ref, j * RWKV_PW, 2 * RWKV_WIDTH + j * RWKV_PW)
        w_log = -_softplus(-(w0_ref[:, sl] + lr_w[:, sl])) - 0.5
        lw = -jnp.exp(w_log)
        a7 = _sigmoid(a0_ref[:, sl] + lr_a[:, sl])
        kx = k * kkp_ref[:, sl]
        kk = kx * lax.rsqrt(seg_sum(kx * kx) + 1e-6)
        k = k * (1.0 + (a7 - 1.0) * ka_ref[:, sl])
        if t_real < L:
            lw, kk, k, v = (jnp.where(valid, a, 0.0) for a in (lw, kk, k, v))
        b = kk * a7
        bonus.append(seg_sum(r * k * rk_ref[:, sl]) * v)
        vs.append(stack(v))
        g = _mxu_f32(tril, lw, _NN)
        gp = g - lw
        gm = g[L // 2 - 1:L // 2, :]
        gl = g[L - 1:L, :]
        e_neg = jnp.exp(gm - g)
        lhs = cat([stack(kk * jnp.exp(gp - gm)), stack(r * jnp.exp(g - gm))], axis=0)
        rhs = cat([stack(b * e_neg), stack(k * e_neg)], axis=0)
        full = _mxu(lhs, rhs, _NT)
        a_ab = jnp.where(strict, full[:n2, :n2], 0.0)
        a_ak.append(jnp.where(strict, full[:n2, n2:], 0.0))
        a_rb.append(jnp.where(incl, full[n2:, :n2], 0.0))
        a_rk.append(jnp.where(incl, full[n2:, n2:], 0.0))
        nn.append(-a_ab)
        tt.append(eye - a_ab)
        sread.append(cat([stack(kk * jnp.exp(gp)), stack(r * jnp.exp(g))], axis=0))
        dec = jnp.exp(gl - g)
        kdbd.append(cat([stack(k * dec), stack(-b * dec)], axis=0))
        egl.append(jnp.exp(gl))
    tt = _neumann_inverse(nn, tt, n2, rounds)
    s_old, sr, av, u = [], [], [], []
    for j in pairs:
        s_old.append(S[j])
        sr.append(_mxu(sread[j], s_old[j], _NT))
        av.append(_mxu(cat([a_ak[j], a_rk[j]], axis=0), vs[j], _NN))
    for j in pairs:
        u.append(_mxu(tt[j], sr[j][:n2] + av[j][:n2], _NN))
    for j in pairs:
        sl = slice(j * RWKV_PW, (j + 1) * RWKV_PW)
        o = sr[j][n2:] + av[j][n2:] - _mxu(a_rb[j], u[j], _NN)
        o = o[:L] + o[L:]
        oc = o - seg_sum(o) * (1.0 / n)
        gn = oc * lax.rsqrt(seg_sum(oc * oc) * (1.0 / n) + RWKV_GN_EPS)
        y = gn * lng_ref[:, sl] + lnb_ref[:, sl] + bonus[j]
        mix_ref[:, sl] = y * _silu(gate_ref[:, sl])
        S[j] = s_old[j] * egl[j] + _mxu(cat([vs[j], u[j]], axis=0), kdbd[j], _TN)
    pieces = ((r_ref, 0, RWKV_WIDTH), (k_ref, RWKV_WIDTH, RWKV_WIDTH), (v_ref, 2 * RWKV_WIDTH, RWKV_WIDTH),
              (xwa_ref, w3, RWKV_PW))
    if nchunk > 1:
        for ref, c0, wd in pieces:
            last[:, c0:c0 + wd] = ref[L - 8:L, :]

    @pl.when(ci == nchunk - 1)
    def _():
        for ref, c0, wd in pieces:
            shift_out_ref[:, c0:c0 + wd] = ref[t_real - 1:t_real, :]
        for j in pairs:
            s_ref[2 * j] = S[j][:n, :n]
            s_ref[2 * j + 1] = S[j][n:, n:]


def _rwkv(h, mix_prev, grp, P, st_in, shift_in, s_prev, p):
    L = grp.chunk
    nchunk = grp.t_rows // L
    tail = (RWKV_HEADS, RWKV_HEAD_DIM, RWKV_HEAD_DIM)
    s_ins, s_specs, s_out, s_shape = _state_io(tail, p, st_in, None, grp.bsz)
    full = lambda shape: pl.BlockSpec(shape, lambda b, c: (0,) * len(shape))
    row = lambda a: a.reshape(1, -1)
    zr = jnp.zeros((RWKV_HEAD_DIM, RWKV_WIDTH), F32)
    w2 = jnp.concatenate([P['rwkv_w2'], zr], axis=0)
    a2 = jnp.concatenate([zr, P['rwkv_a2']], axis=0)
    off = ODD_OFF[4]
    ins = [h, h, h, h, h, row(P['rwkv_mu']), row(P['rwkv_w0']), w2, row(P['rwkv_a0']), a2, row(P['rwkv_kk']),
           row(P['rwkv_ka']), row(P['rwkv_rk']), row(P['rwkv_ln_g']), row(P['rwkv_ln_b'])]
    vec = full((1, RWKV_WIDTH))
    specs = [grp.spec(L, RWKV_WIDTH, off), grp.spec(L, RWKV_WIDTH, off + RWKV_WIDTH),
             grp.spec(L, RWKV_WIDTH, off + 2 * RWKV_WIDTH), grp.spec(L, RWKV_PW, off + 3 * RWKV_WIDTH),
             grp.spec(L, RWKV_WIDTH, ODD_OFF[5]), full((1, RWKV_SHIFT_DIM)), vec, full((RWKV_PW, RWKV_WIDTH)), vec,
             full((RWKV_PW, RWKV_WIDTH)), vec, vec, vec, vec, vec]
    if st_in is not None:
        ins += [shift_in] + s_ins
        specs += [pl.BlockSpec((None, None, 1, RWKV_SHIFT_DIM), lambda b, c: (p, b, 0, 0))] + s_specs
    any_spec = pl.BlockSpec(memory_space=pl.ANY)
    alias = {}
    for prev, out_idx in ((mix_prev, 0), (s_prev, 1)):
        if prev is not None:
            alias[len(ins)] = out_idx
            ins.append(prev)
            specs.append(any_spec)
    n_prev = (mix_prev is not None) + (s_prev is not None)
    return pl.pallas_call(
        functools.partial(_rwkv_kernel, L=L, nchunk=nchunk, rounds=int(math.log2(L)), t_real=grp.t_real or L,
                          has_state=st_in is not None, has_prev=n_prev),
        grid=(grp.bsz, nchunk),
        in_specs=specs,
        out_specs=[grp.spec(L, RWKV_WIDTH, GDN_WIDTH), s_out,
                   pl.BlockSpec((None, 1, RWKV_SHIFT_DIM), lambda b, c: (b, 0, 0))],
        out_shape=[jax.ShapeDtypeStruct((h.shape[0], MIX_WIDTH), F32), s_shape,
                   jax.ShapeDtypeStruct((grp.bsz, 1, RWKV_SHIFT_DIM), F32)],
        scratch_shapes=[pltpu.VMEM((RWKV_PAIRS, RWKV_PW, RWKV_PW), F32), pltpu.VMEM((8, RWKV_SHIFT_DIM), F32)],
        input_output_aliases=alias,
        compiler_params=_cparams(("parallel", "arbitrary")),
        name="rwkv7",
    )(*ins)


def _pad_t(a, t_to):
    t = a.shape[1]
    if t == t_to:
        return a
    return jnp.pad(a, [(0, 0), (0, t_to - t)] + [(0, 0)] * (a.ndim - 2))


def _even_mix(h, mix, grp, mem_k, mem_v, layer, st_in, st_prev, conv_in, P):
    p = layer // 2
    mix, s_ssd, s_conv = _ssd(h, mix, grp, P, st_in['ssd'], conv_in, st_prev['ssd'], p)
    gla_l, gla_tb = (GLA_CHUNK, 256) if grp.t_real is None else (grp.chunk, grp.chunk)
    mix, s_gla = _gla(h, mix, grp, P, st_in['gla'], st_prev['gla'], p, gla_l, gla_tb)
    mix = _mem_attention(h, mix, grp, EVEN_OFF[8], EVEN_OFF[9], mem_k, mem_v, layer, min(512, grp.t_rows))
    return mix, dict(gla=s_gla, ssd=s_ssd), s_conv


def _odd_mix(h, mix, grp, mem_k, mem_v, layer, st_in, st_prev, conv_in, shift_in, P):
    p = layer // 2
    mix, s_gdn, s_conv = _gdn(h, mix, grp, P, st_in['gdn'], conv_in, st_prev['gdn'], p)
    mix, s_rwkv, s_shift = _rwkv(h, mix, grp, P, st_in['rwkv'], shift_in, st_prev['rwkv'], p)
    mix = _mem_attention(h, mix, grp, ODD_OFF[6], ODD_OFF[7], mem_k, mem_v, layer, min(512, grp.t_rows))
    return mix, dict(gdn=s_gdn, rwkv=s_rwkv), s_conv, s_shift.reshape(grp.bsz, RWKV_SHIFT_DIM)


def kernel(x_prompt, x_sample, mem_prompt, cache_mem_k, cache_mem_v, state_gla, state_ssd, state_ssd_conv, state_gdn, state_gdn_conv, state_rwkv, state_rwkv_shift, mem_w_kv, ev_w_in, ev_gla_w2, ev_gla_b, ev_gla_norm, ev_ssd_conv_w, ev_ssd_conv_b, ev_ssd_dt_bias, ev_ssd_a_log, ev_ssd_d, ev_ssd_norm, ev_w_out, ev_ln_g, ev_ln_b, od_w_in, od_gdn_conv_w, od_gdn_dt_bias, od_gdn_a_log, od_gdn_norm, od_rwkv_mu, od_rwkv_w0, od_rwkv_w2, od_rwkv_a0, od_rwkv_a2, od_rwkv_kk, od_rwkv_ka, od_rwkv_rk, od_rwkv_ln_g, od_rwkv_ln_b, od_w_out, od_ln_g, od_ln_b):
    ev = dict(w_in=ev_w_in, gla_w2=ev_gla_w2, gla_b=ev_gla_b, gla_norm=ev_gla_norm,
              ssd_conv_w=ev_ssd_conv_w, ssd_conv_b=ev_ssd_conv_b, ssd_dt_bias=ev_ssd_dt_bias,
              ssd_a_log=ev_ssd_a_log, ssd_d=ev_ssd_d, ssd_norm=ev_ssd_norm,
              w_out=ev_w_out, ln_g=ev_ln_g, ln_b=ev_ln_b)
    od = dict(w_in=od_w_in, gdn_conv_w=od_gdn_conv_w, gdn_dt_bias=od_gdn_dt_bias, gdn_a_log=od_gdn_a_log,
              gdn_norm=od_gdn_norm, rwkv_mu=od_rwkv_mu, rwkv_w0=od_rwkv_w0, rwkv_w2=od_rwkv_w2,
              rwkv_a0=od_rwkv_a0, rwkv_a2=od_rwkv_a2, rwkv_kk=od_rwkv_kk, rwkv_ka=od_rwkv_ka,
              rwkv_rk=od_rwkv_rk, rwkv_ln_g=od_rwkv_ln_g, rwkv_ln_b=od_rwkv_ln_b,
              w_out=od_w_out, ln_g=od_ln_g, ln_b=od_ln_b)
    bp, tp, _ = x_prompt.shape
    bs, ts, _ = x_sample.shape
    mp, ms = bp * tp, bs * SMALL_T
    grp_p = _Group(bp, tp, None, 0, GDN_CHUNK)
    grp_s = _Group(bs, SMALL_T, ts, mp, SMALL_T)

    w_kv = jnp.moveaxis(mem_w_kv, 0, 1).reshape(D_MODEL, DEPTH * 2 * MEM_WIDTH).astype(BF16)
    kv = _matmul(mem_prompt.reshape(bp * MEM_LEN, D_MODEL).astype(BF16), w_kv, 512, 1024)
    kv6 = kv.reshape(bp, MEM_LEN, DEPTH, 2, MEM_HEADS, MEM_HEAD_DIM)
    mem_k_p = jnp.moveaxis(kv6[:, :, :, 0], 2, 0)
    mem_v_p = jnp.moveaxis(kv6[:, :, :, 1], 2, 0)
    mk_s = cache_mem_k.reshape(cache_mem_k.shape[:-2] + (MEM_WIDTH,))
    mv_s = cache_mem_v.reshape(cache_mem_v.shape[:-2] + (MEM_WIDTH,))

    x = jnp.concatenate([x_prompt.reshape(mp, D_MODEL),
                         _pad_t(x_sample, SMALL_T).reshape(ms, D_MODEL)], axis=0)
    x_bf = x.astype(BF16)
    zp = lambda shape: jnp.zeros(shape, F32)
    names = ('gla', 'ssd', 'gdn', 'rwkv')
    none = {n: None for n in names}
    in_s = dict(gla=state_gla, ssd=state_ssd, gdn=state_gdn, rwkv=state_rwkv)
    shift_s = state_rwkv_shift.reshape(N_PAIRS, bs, 1, RWKV_SHIFT_DIM)
    out_p, out_s = dict(none), dict(none)
    small_p = {n: [] for n in ('ssd_conv', 'gdn_conv', 'rwkv_shift')}
    small_s = {n: [] for n in small_p}
    tm = 512
    for layer in range(DEPTH):
        p = layer // 2
        if layer % 2 == 0:
            P = {n: w[p] for n, w in ev.items()}
            h = _matmul(x_bf, _pack_w_in(P['w_in'], EVEN_SIZES, EVEN_ORDER, EVEN_N), tm, PROJ_TN)
            mix, new, c1 = _even_mix(h, None, grp_p, kv, kv, layer, none, out_p, None, P)
            out_p.update(new)
            mix, new, c2 = _even_mix(h, mix, grp_s, mk_s, mv_s, layer, in_s, out_s, state_ssd_conv, P)
            out_s.update(new)
            small_p['ssd_conv'].append(c1)
            small_s['ssd_conv'].append(c2)
            w_out = jnp.concatenate([P['w_out'][GLA_WIDTH:GLA_WIDTH + SSD_WIDTH], P['w_out'][:GLA_WIDTH],
                                     P['w_out'][GLA_WIDTH + SSD_WIDTH:]], axis=0)
        else:
            P = {n: w[p] for n, w in od.items()}
            h = _matmul(x_bf, _pack_w_in(P['w_in'], ODD_SIZES, ODD_ORDER, ODD_N), tm, PROJ_TN)
            mix, new, c1, h1 = _odd_mix(h, None, grp_p, kv, kv, layer, none, out_p, None, None, P)
            out_p.update(new)
            mix, new, c2, h2 = _odd_mix(h, mix, grp_s, mk_s, mv_s, layer, in_s, out_s, state_gdn_conv, shift_s, P)
            out_s.update(new)
            small_p['gdn_conv'].append(c1)
            small_s['gdn_conv'].append(c2)
            small_p['rwkv_shift'].append(h1)
            small_s['rwkv_shift'].append(h2)
            w_out = P['w_out']
        x, x_bf = _out_ln(mix, w_out.astype(BF16), x, P['ln_g'], P['ln_b'])

    y_prompt = x[:mp].reshape(bp, tp, D_MODEL)
    y_sample = x[mp:].reshape(bs, SMALL_T, D_MODEL)[:, :ts]
    st = lambda d, n: jnp.stack(d[n])
    return (y_prompt, y_sample, mem_k_p, mem_v_p,
            out_p['gla'], out_s['gla'], out_p['ssd'], out_s['ssd'],
            st(small_p, 'ssd_conv'), st(small_s, 'ssd_conv'), out_p['gdn'], out_s['gdn'],
            st(small_p, 'gdn_conv'), st(small_s, 'gdn_conv'), out_p['rwkv'], out_s['rwkv'],
            st(small_p, 'rwkv_shift'), st(small_s, 'rwkv_shift'))
```

```python
import functools
import math

import numpy as np
import jax
import jax.numpy as jnp
from jax import lax
from jax.experimental import pallas as pl
from jax.experimental.pallas import tpu as pltpu

F32 = jnp.float32
BF16 = jnp.bfloat16
HI = lax.Precision.HIGHEST

D_MODEL = 2048
DEPTH = 4
N_PAIRS = DEPTH // 2
CONV_W = 4
MEM_LEN = 256
MEM_HEADS = 4
MEM_HEAD_DIM = 256
MEM_WIDTH = 1024
GLA_HEADS = 4
GLA_DK = 128
GLA_DV = 256
GLA_QK = 512
GLA_WIDTH = 1024
GLA_RANK = 16
GLA_TAU = 16.0
SSD_WIDTH = 2048
SSD_HEAD_DIM = 64
SSD_HEADS = 32
SSD_GROUPS = 4
SSD_REP = 8
SSD_STATE = 128
SSD_CONV_DIM = SSD_WIDTH + 2 * SSD_GROUPS * SSD_STATE
GDN_WIDTH = 2048
GDN_HEAD_DIM = 128
GDN_HEADS = 16
RWKV_WIDTH = 1024
RWKV_HEAD_DIM = 64
RWKV_HEADS = 16
RWKV_W_RANK = 64
RWKV_A_RANK = 64
RWKV_SHIFT_DIM = 3 * RWKV_WIDTH + RWKV_W_RANK + RWKV_A_RANK
RWKV_GN_EPS = 64e-5
EVEN_SIZES = (GLA_QK, GLA_QK, GLA_WIDTH, GLA_RANK, GLA_WIDTH, SSD_WIDTH, SSD_CONV_DIM, SSD_HEADS,
              MEM_WIDTH, MEM_WIDTH)
ODD_SIZES = (3 * GDN_WIDTH, GDN_WIDTH, GDN_HEADS, GDN_HEADS, RWKV_SHIFT_DIM, RWKV_WIDTH, MEM_WIDTH, MEM_WIDTH)
MIX_WIDTH = 4096
DEEPNORM_ALPHA = (2 * DEPTH) ** 0.25

EVEN_ORDER = (5, 2, 4, 8, 9, 6, 0, 1, 3, 7)
ODD_ORDER = (0, 1, 5, 6, 7, 4, 2, 3)
PROJ_TN = 768
VMEM_LIMIT = 48 * 1024 * 1024

GLA_CHUNK = 16
SSD_CHUNK = 64
GDN_CHUNK = 64
RWKV_CHUNK = 64
SMALL_T = 8

_NN = ((1,), (0,))
_NT = ((1,), (1,))
_TN = ((0,), (0,))


def _packed_layout(sizes, order):
    offs, o = {}, 0
    for i in order:
        offs[i] = o
        o += sizes[i]
    total = -(-o // PROJ_TN) * PROJ_TN
    return offs, total


EVEN_OFF, EVEN_N = _packed_layout(EVEN_SIZES, EVEN_ORDER)
ODD_OFF, ODD_N = _packed_layout(ODD_SIZES, ODD_ORDER)


def _pack_w_in(w, sizes, order, total):
    segs = jnp.split(w, np.cumsum(sizes)[:-1].tolist(), axis=-1)
    parts = [segs[i] for i in order]
    used = sum(sizes)
    if total > used:
        parts.append(jnp.zeros((w.shape[0], total - used), w.dtype))
    return jnp.concatenate(parts, axis=-1).astype(BF16)


def _seg(h, offs, sizes, i):
    return h[..., offs[i]:offs[i] + sizes[i]]


def _cparams(sem):
    return pltpu.CompilerParams(dimension_semantics=sem, vmem_limit_bytes=VMEM_LIMIT)


def _mxu(a, b, dims):
    return lax.dot_general(a.astype(BF16), b.astype(BF16), (dims, ((), ())), preferred_element_type=F32)


def _mxu_f32(a, b, dims):
    return lax.dot_general(a, b, (dims, ((), ())), precision=HI, preferred_element_type=F32)


def _sigmoid(x):
    return 1.0 / (1.0 + jnp.exp(-x))


def _silu(x):
    return x * _sigmoid(x)


def _softplus(x):
    return jnp.maximum(x, 0.0) + jnp.log(1.0 + jnp.exp(-jnp.abs(x)))


def _shifted(u, prev8, j, row8):
    ru = pltpu.roll(u, j, 0)
    top = jnp.where(row8 < j, pltpu.roll(prev8, j, 0), ru[:8])
    return top if u.shape[0] == 8 else jnp.concatenate([top, ru[8:]], axis=0)


class _Group:
    def __init__(self, bsz, t_rows, t_real, row0, chunk):
        self.bsz, self.t_rows, self.t_real, self.row0, self.chunk = bsz, t_rows, t_real, row0, chunk

    def spec(self, rows, width, off):
        assert off % width == 0 and self.row0 % rows == 0 and self.t_rows % rows == 0
        base, per, cb = self.row0 // rows, self.t_rows // rows, off // width
        return pl.BlockSpec((rows, width), lambda b, i: (base + b * per + i, cb))


def _alias_last(n_inputs, has_prev, out_index=0):
    return {n_inputs - 1: out_index} if has_prev else {}


def _neumann_inverse(nn, tt, n, rounds):
    idx = range(len(nn))
    if rounds >= 2:
        for j in idx:
            nn[j] = _mxu(nn[j], nn[j], _NN)
        for _ in range(rounds - 2):
            for j in idx:
                both = _mxu(nn[j], jnp.concatenate([tt[j], nn[j]], axis=1), _NN)
                tt[j] = tt[j] + both[:, :n]
                nn[j] = both[:, n:]
        for j in idx:
            tt[j] = tt[j] + _mxu(nn[j], tt[j], _NN)
    return tt


def _state_io(tail, p, s_in, s_prev, bsz):
    zeros = (0,) * len(tail)
    spec = pl.BlockSpec((None, None) + tail, lambda *g: (p, g[0]) + zeros)
    ins, specs = [], []
    if s_in is not None:
        ins.append(s_in)
        specs.append(spec)
    if s_prev is not None:
        ins.append(s_prev)
        specs.append(pl.BlockSpec(memory_space=pl.ANY))
    shape = jax.ShapeDtypeStruct((N_PAIRS, bsz) + tail, F32)
    return ins, specs, spec, shape


def _mm_kernel(x_ref, w_ref, o_ref, *, precision):
    o_ref[...] = jnp.dot(x_ref[...], w_ref[...], preferred_element_type=F32, precision=precision)


def _matmul(x, w, tm, tn, precision=None):
    m, k = x.shape
    n = w.shape[1]
    assert m % tm == 0 and n % tn == 0
    return pl.pallas_call(
        functools.partial(_mm_kernel, precision=precision),
        grid=(n // tn, m // tm),
        in_specs=[pl.BlockSpec((tm, k), lambda j, i: (i, 0)),
                  pl.BlockSpec((k, tn), lambda j, i: (0, j))],
        out_specs=pl.BlockSpec((tm, tn), lambda j, i: (i, j)),
        out_shape=jax.ShapeDtypeStruct((m, n), F32),
        compiler_params=_cparams(("parallel", "parallel")),
        name="matmul",
    )(x, w)


def _out_ln_kernel(mix_ref, w_ref, x_ref, g_ref, b_ref, y_ref, ybf_ref, acc, *, nk):
    kk = pl.program_id(1)

    @pl.when(kk == 0)
    def _():
        acc[...] = jnp.zeros_like(acc)

    acc[...] += jnp.dot(mix_ref[...].astype(BF16), w_ref[...], preferred_element_type=F32)

    @pl.when(kk == nk - 1)
    def _():
        z = DEEPNORM_ALPHA * x_ref[...] + acc[...]
        zc = z - jnp.mean(z, axis=-1, keepdims=True)
        var = jnp.mean(zc * zc, axis=-1, keepdims=True)
        y = zc * lax.rsqrt(var + 1e-5) * g_ref[...] + b_ref[...]
        y_ref[...] = y
        ybf_ref[...] = y.astype(BF16)


def _out_ln(mix, w, x, g, b, tm=512, tk=1024):
    m, k = mix.shape
    d = w.shape[1]
    nk = k // tk
    return pl.pallas_call(
        functools.partial(_out_ln_kernel, nk=nk),
        grid=(m // tm, nk),
        in_specs=[pl.BlockSpec((tm, tk), lambda i, j: (i, j)),
                  pl.BlockSpec((tk, d), lambda i, j: (j, 0)),
                  pl.BlockSpec((tm, d), lambda i, j: (i, 0)),
                  pl.BlockSpec((1, d), lambda i, j: (0, 0)),
                  pl.BlockSpec((1, d), lambda i, j: (0, 0))],
        out_specs=[pl.BlockSpec((tm, d), lambda i, j: (i, 0)),
                   pl.BlockSpec((tm, d), lambda i, j: (i, 0))],
        out_shape=[jax.ShapeDtypeStruct((m, d), F32), jax.ShapeDtypeStruct((m, d), BF16)],
        scratch_shapes=[pltpu.VMEM((tm, d), F32)],
        compiler_params=_cparams(("parallel", "arbitrary")),
        name="out_ln",
    )(mix, w, x, g.reshape(1, d), b.reshape(1, d))


def _mem_kernel(q_ref, gate_ref, k_ref, v_ref, *rest):
    o_ref = rest[-1]
    for h in range(MEM_HEADS):
        sl = slice(h * MEM_HEAD_DIM, (h + 1) * MEM_HEAD_DIM)
        k = k_ref[:, sl]
        v = v_ref[:, sl]
        s = _mxu(q_ref[:, sl], k, _NT) * MEM_HEAD_DIM ** -0.5
        p = jnp.exp(s - jnp.max(s, axis=-1, keepdims=True))
        p = p / jnp.sum(p, axis=-1, keepdims=True)
        o_ref[:, sl] = _mxu(p, v, _NN) * _silu(gate_ref[:, sl])


def _mem_attention(h, mix_prev, grp, q_off, gate_off, mem_k, mem_v, layer, tq):
    if mem_k.ndim == 4:
        kv_specs = [pl.BlockSpec((None, None, MEM_LEN, MEM_WIDTH), lambda b, i: (layer, b, 0, 0))] * 2
    else:
        kv_specs = [pl.BlockSpec((MEM_LEN, MEM_WIDTH), lambda b, i: (b, 2 * layer)),
                    pl.BlockSpec((MEM_LEN, MEM_WIDTH), lambda b, i: (b, 2 * layer + 1))]
    ins = [h, h, mem_k, mem_v] + ([] if mix_prev is None else [mix_prev])
    specs = [grp.spec(tq, MEM_WIDTH, q_off), grp.spec(tq, MEM_WIDTH, gate_off)] + kv_specs
    if mix_prev is not None:
        specs.append(pl.BlockSpec(memory_space=pl.ANY))
    return pl.pallas_call(
        _mem_kernel,
        grid=(grp.bsz, grp.t_rows // tq),
        in_specs=specs,
        out_specs=grp.spec(tq, MEM_WIDTH, MIX_WIDTH - MEM_WIDTH),
        out_shape=jax.ShapeDtypeStruct((h.shape[0], MIX_WIDTH), F32),
        input_output_aliases=_alias_last(len(ins), mix_prev is not None),
        compiler_params=_cparams(("parallel", "parallel")),
        name="mem_attention",
    )(*ins)


def _gla_kernel(q_ref, k_ref, v_ref, gate_ref, sm_ref, w2_ref, gb_ref, nw_ref, *rest, L, nchunk, nblk, t_real,
                has_state, has_prev):
    n_opt = has_state + has_prev
    mix_ref, s_ref, ST, QK, B, OI = rest[n_opt:]
    tb = pl.program_id(1)
    rows_blk = q_ref.shape[0]

    @pl.when(tb == 0)
    def _():
        for h in range(GLA_HEADS):
            ST[h] = rest[0][h].T if has_state else jnp.zeros((GLA_DV, GLA_DK), F32)

    z = _mxu_f32(sm_ref[...], w2_ref[...], _NN) + gb_ref[...]
    g_all = -_softplus(-z) * (1.0 / GLA_TAU)
    t_i = lax.broadcasted_iota(jnp.int32, (rows_blk, GLA_DK), 0)
    t_c = t_i & (L - 1)
    nw = nw_ref[...]
    for h in range(GLA_HEADS):
        ks = slice(h * GLA_DK, (h + 1) * GLA_DK)
        vs = slice(h * GLA_DV, (h + 1) * GLA_DV)
        q = q_ref[:, ks] * GLA_DK ** -0.5
        k = k_ref[:, ks]
        b = g_all[:, ks]
        v = v_ref[:, vs]
        if t_real < L:
            b = jnp.where(t_i < t_real, b, 0.0)
            k = jnp.where(t_i < t_real, k, 0.0)
        sh = 1
        while sh < L:
            b = b + jnp.where(t_c >= sh, pltpu.roll(b, sh, 0), 0.0)
            sh *= 2
        o = jnp.sum(q * k, axis=-1, keepdims=True) * v
        for j in range(1, L):
            d = jnp.where(t_c >= j, b - pltpu.roll(b, j, 0), -jnp.inf)
            p = jnp.exp(d) * q * pltpu.roll(k, j, 0)
            o = o + jnp.sum(p, axis=-1, keepdims=True) * pltpu.roll(v, j, 0)
        OI[:, vs] = o
        B[:, ks] = b
        QK[:, ks] = q
        QK[:, GLA_QK + h * GLA_DK:GLA_QK + (h + 1) * GLA_DK] = k
    for c in range(rows_blk // L):
        rows = slice(c * L, (c + 1) * L)
        for h in range(GLA_HEADS):
            ks = slice(h * GLA_DK, (h + 1) * GLA_DK)
            vs = slice(h * GLA_DV, (h + 1) * GLA_DV)
            b = B[rows, ks]
            b_last = b[L - 1:L, :]
            st = ST[h]
            o = OI[rows, vs] + _mxu(QK[rows, ks] * jnp.exp(b), st, _NT)
            y = o * lax.rsqrt(jnp.mean(o * o, axis=-1, keepdims=True) + 1e-6) * nw
            mix_ref[rows, vs] = y * _silu(gate_ref[rows, vs])
            kd = QK[rows, GLA_QK + h * GLA_DK:GLA_QK + (h + 1) * GLA_DK] * jnp.exp(b_last - b)
            ST[h] = st * jnp.exp(b_last) + _mxu(v_ref[rows, vs], kd, _TN)

    @pl.when(tb == nblk - 1)
    def _():
        for h in range(GLA_HEADS):
            s_ref[h] = ST[h].T


def _gla(h, mix_prev, grp, P, st_in, s_prev, p, L, tb):
    nblk = grp.t_rows // tb
    tail = (GLA_HEADS, GLA_DK, GLA_DV)
    s_ins, s_specs, s_out, s_shape = _state_io(tail, p, st_in, None, grp.bsz)
    full = lambda shape: pl.BlockSpec(shape, lambda b, c: (0,) * len(shape))
    w2 = jnp.concatenate([P['gla_w2'], jnp.zeros((128 - GLA_RANK, GLA_QK), F32)], axis=0)
    ins = [h, h, h, h, h, w2, P['gla_b'].reshape(1, GLA_QK), P['gla_norm'].reshape(1, GLA_DV)]
    specs = [grp.spec(tb, GLA_QK, EVEN_OFF[0]), grp.spec(tb, GLA_QK, EVEN_OFF[1]), grp.spec(tb, GLA_WIDTH, EVEN_OFF[2]),
             grp.spec(tb, GLA_WIDTH, EVEN_OFF[4]), grp.spec(tb, 128, EVEN_OFF[3]),
             full((128, GLA_QK)), full((1, GLA_QK)), full((1, GLA_DV))]
    ins += s_ins
    specs += s_specs
    any_spec = pl.BlockSpec(memory_space=pl.ANY)
    alias = {}
    for prev, out_idx in ((mix_prev, 0), (s_prev, 1)):
        if prev is not None:
            alias[len(ins)] = out_idx
            ins.append(prev)
            specs.append(any_spec)
    n_prev = (mix_prev is not None) + (s_prev is not None)
    return pl.pallas_call(
        functools.partial(_gla_kernel, L=L, nchunk=tb // L, nblk=nblk, t_real=grp.t_real or L,
                          has_state=st_in is not None, has_prev=n_prev),
        grid=(grp.bsz, nblk),
        in_specs=specs,
        out_specs=[grp.spec(tb, GLA_WIDTH, SSD_WIDTH), s_out],
        out_shape=[jax.ShapeDtypeStruct((h.shape[0], MIX_WIDTH), F32), s_shape],
        scratch_shapes=[pltpu.VMEM((GLA_HEADS, GLA_DV, GLA_DK), F32), pltpu.VMEM((tb, 2 * GLA_QK), F32),
                        pltpu.VMEM((tb, GLA_QK), F32), pltpu.VMEM((tb, GLA_WIDTH), F32)],
        input_output_aliases=alias,
        compiler_params=_cparams(("parallel", "arbitrary")),
        name="gla",
    )(*ins)


SSD_GW = SSD_REP * SSD_HEAD_DIM


def _ssd_lanes(L):
    return max(SSD_REP * L, 128)


def _ssd_kernel(sz_ref, xbc_ref, sm_ref, cw_ref, cbias_ref, dtb_ref, alog_ref, dvec_ref, nw_ref, ep_ref, es_ref, *rest,
                L, nchunk, t_real, has_state, has_prev):
    n_opt = 2 * has_state + has_prev
    mix_ref, s_ref, conv_out_ref, ST, tail = rest[n_opt:]
    ci = pl.program_id(1)
    gs = _ssd_lanes(L)
    cat = jnp.concatenate

    @pl.when(ci == 0)
    def _():
        tail[...] = jnp.zeros_like(tail)
        if has_state:
            tail[8 - (CONV_W - 1):8, :] = rest[0][...]
            for g in range(SSD_GROUPS):
                ST[g] = rest[1][g * SSD_REP:(g + 1) * SSD_REP].reshape(SSD_GW, SSD_STATE).T
        else:
            ST[...] = jnp.zeros_like(ST)

    d = 128
    row8 = lax.broadcasted_iota(jnp.int32, (8, d), 0)

    def conv_tile(c0):
        u = xbc_ref[:, c0:c0 + d]
        p8 = tail[:, c0:c0 + d]
        w = cw_ref[:, c0:c0 + d]
        acc = u * w[CONV_W - 1:CONV_W] + cbias_ref[:, c0:c0 + d]
        for j in range(1, CONV_W):
            acc = acc + _shifted(u, p8, j, row8) * w[CONV_W - 1 - j:CONV_W - j]
        return _silu(acc)

    row = lax.broadcasted_iota(jnp.int32, (L, L), 0)
    col = lax.broadcasted_iota(jnp.int32, (L, L), 1)
    tril = (col <= row).astype(F32)
    dt = _softplus(sm_ref[:, GLA_RANK:GLA_RANK + SSD_HEADS] + dtb_ref[...])
    if t_real < L:
        dt = jnp.where(lax.broadcasted_iota(jnp.int32, (L, SSD_HEADS), 0) < t_real, dt, 0.0)
    c = _mxu_f32(tril, dt * -jnp.exp(alog_ref[...]), _NN)
    ep = ep_ref[...]
    dt_x = _mxu_f32(dt, ep, _NN)
    c_x = _mxu_f32(c, ep, _NN)
    c_s = _mxu_f32(c, es_ref[...], _NN)
    t_i = lax.broadcasted_iota(jnp.int32, (L, SSD_GROUPS * gs), 0)
    s_i = lax.broadcasted_iota(jnp.int32, (L, SSD_GROUPS * gs), 1) & (L - 1)
    c_src = jnp.sum(jnp.where(t_i == s_i, c_s, 0.0), axis=0, keepdims=True)
    seg = jnp.exp(jnp.where(s_i <= t_i, c_s - c_src, -jnp.inf))
    blk_r = lax.broadcasted_iota(jnp.int32, (gs, SSD_GW), 0) // L
    blk_c = lax.broadcasted_iota(jnp.int32, (gs, SSD_GW), 1) // SSD_HEAD_DIM
    diag = blk_r == blk_c
    reps = SSD_REP * L
    for g in range(SSD_GROUPS):
        gl = slice(g * SSD_GW, (g + 1) * SSD_GW)
        sx = cat([conv_tile(g * SSD_GW + i * d) for i in range(SSD_GW // d)], axis=1)
        bm = conv_tile(SSD_WIDTH + g * SSD_STATE)
        cm = conv_tile(SSD_WIDTH + SSD_GROUPS * SSD_STATE + g * SSD_STATE)
        xdt = sx * dt_x[:, gl]
        pad_rows = [] if reps == gs else [jnp.zeros((gs - reps, SSD_STATE), F32)]
        cb = _mxu(cm, cat([bm] * SSD_REP + pad_rows, axis=0), _NT)
        pad_rows = [] if reps == gs else [jnp.zeros((gs - reps, SSD_GW), F32)]
        xbd = jnp.where(diag, cat([xdt] * SSD_REP + pad_rows, axis=0), 0.0)
        st = ST[g]
        y = _mxu(cb * seg[:, g * gs:(g + 1) * gs], xbd, _NN) + _mxu(cm, st, _NN) * jnp.exp(c_x[:, gl])
        y = (y + sx * dvec_ref[:, gl]) * _silu(sz_ref[:, gl])
        y = y * lax.rsqrt(jnp.mean(y * y, axis=-1, keepdims=True) + 1e-6) * nw_ref[:, gl]
        mix_ref[:, gl] = y
        c_end = c_x[L - 1:L, gl]
        ST[g] = st * jnp.exp(c_end) + _mxu(bm, xdt * jnp.exp(c_end - c_x[:, gl]), _TN)
    if nchunk > 1:
        tail[...] = xbc_ref[L - 8:L, :]

    @pl.when(ci == nchunk - 1)
    def _():
        conv_out_ref[...] = xbc_ref[t_real - (CONV_W - 1):t_real, :]
        for g in range(SSD_GROUPS):
            s_ref[g * SSD_REP:(g + 1) * SSD_REP] = ST[g].T.reshape(SSD_REP, SSD_HEAD_DIM, SSD_STATE)


def _ssd(h, mix_prev, grp, P, st_in, conv_in, s_prev, p):
    L = grp.chunk
    nchunk = grp.t_rows // L
    gs = _ssd_lanes(L)
    tail = (SSD_HEADS, SSD_HEAD_DIM, SSD_STATE)
    s_ins, s_specs, s_out, s_shape = _state_io(tail, p, st_in, None, grp.bsz)
    full = lambda shape: pl.BlockSpec(shape, lambda b, c: (0,) * len(shape))
    heads = jnp.arange(SSD_HEADS)[:, None]
    lane_p = jnp.arange(SSD_WIDTH)[None, :]
    ep = (lane_p // SSD_HEAD_DIM == heads).astype(F32)
    lane_s = jnp.arange(SSD_GROUPS * gs)[None, :]
    in_grp = lane_s % gs
    es = ((in_grp < SSD_REP * L) & ((lane_s // gs) * SSD_REP + in_grp // L == heads)).astype(F32)
    row = lambda a: a.reshape(1, -1)
    ins = [h, h, h, P['ssd_conv_w'], row(P['ssd_conv_b']), row(P['ssd_dt_bias']), row(P['ssd_a_log']),
           row(jnp.repeat(P['ssd_d'], SSD_HEAD_DIM)), row(P['ssd_norm']), ep, es]
    specs = [grp.spec(L, SSD_WIDTH, EVEN_OFF[5]), grp.spec(L, SSD_CONV_DIM, EVEN_OFF[6]), grp.spec(L, 128, EVEN_OFF[3]),
             full((CONV_W, SSD_CONV_DIM)), full((1, SSD_CONV_DIM)), full((1, SSD_HEADS)), full((1, SSD_HEADS)),
             full((1, SSD_WIDTH)), full((1, SSD_WIDTH)), full(ep.shape), full(es.shape)]
    if st_in is not None:
        ins += [conv_in] + s_ins
        specs += [pl.BlockSpec((None, None, CONV_W - 1, SSD_CONV_DIM), lambda b, c: (p, b, 0, 0))] + s_specs
    any_spec = pl.BlockSpec(memory_space=pl.ANY)
    alias = {}
    for prev, out_idx in ((mix_prev, 0), (s_prev, 1)):
        if prev is not None:
            alias[len(ins)] = out_idx
            ins.append(prev)
            specs.append(any_spec)
    n_prev = (mix_prev is not None) + (s_prev is not None)
    return pl.pallas_call(
        functools.partial(_ssd_kernel, L=L, nchunk=nchunk, t_real=grp.t_real or L,
                          has_state=st_in is not None, has_prev=n_prev),
        grid=(grp.bsz, nchunk),
        in_specs=specs,
        out_specs=[grp.spec(L, SSD_WIDTH, 0), s_out,
                   pl.BlockSpec((None, CONV_W - 1, SSD_CONV_DIM), lambda b, c: (b, 0, 0))],
        out_shape=[jax.ShapeDtypeStruct((h.shape[0], MIX_WIDTH), F32), s_shape,
                   jax.ShapeDtypeStruct((grp.bsz, CONV_W - 1, SSD_CONV_DIM), F32)],
        scratch_shapes=[pltpu.VMEM((SSD_GROUPS, SSD_STATE, SSD_GW), F32), pltpu.VMEM((8, SSD_CONV_DIM), F32)],
        input_output_aliases=alias,
        compiler_params=_cparams(("parallel", "arbitrary")),
        name="ssd",
    )(*ins)


def _gdn_kernel(qkv_ref, cz_ref, sm_ref, cw_ref, alog_ref, dtb_ref, nw_ref, *rest, L, nchunk, rounds, t_real,
                has_state, has_prev):
    n_opt = 2 * has_state + has_prev
    mix_ref, s_ref, conv_out_ref, S, tail = rest[n_opt:]
    ci = pl.program_id(1)

    @pl.when(ci == 0)
    def _():
        tail[...] = jnp.zeros_like(tail)
        if has_state:
            tail[8 - (CONV_W - 1):8, :] = rest[0][...]
            S[...] = rest[1][...]
        else:
            S[...] = jnp.zeros_like(S)

    n2 = 2 * L
    d = GDN_HEAD_DIM
    cat = jnp.concatenate
    row8 = lax.broadcasted_iota(jnp.int32, (8, d), 0)

    def conv_tile(c0):
        u = qkv_ref[:, c0:c0 + d]
        p8 = tail[:, c0:c0 + d]
        w = cw_ref[:, c0:c0 + d]
        acc = u * w[CONV_W - 1:CONV_W]
        for j in range(1, CONV_W):
            acc = acc + _shifted(u, p8, j, row8) * w[CONV_W - 1 - j:CONV_W - j]
        return _silu(acc)

    def l2n(x):
        return x * lax.rsqrt(jnp.sum(x * x, axis=-1, keepdims=True) + 1e-6)

    row = lax.broadcasted_iota(jnp.int32, (L, L), 0)
    col = lax.broadcasted_iota(jnp.int32, (L, L), 1)
    tril = (col <= row).astype(F32)
    r2 = lax.broadcasted_iota(jnp.int32, (n2, n2), 0)
    c2 = lax.broadcasted_iota(jnp.int32, (n2, n2), 1)
    same = (r2 >= L) == (c2 >= L)
    strict = same & (c2 < r2)
    incl = same & (c2 <= r2)
    upper = same & (r2 <= c2)
    eye = (r2 == c2).astype(F32)
    zl = jnp.zeros((L, d), F32)
    sm = sm_ref[...]
    beta_all = _sigmoid(sm[:, :GDN_HEADS])
    g_all = -jnp.exp(alog_ref[...]) * _softplus(sm[:, GDN_HEADS:2 * GDN_HEADS] + dtb_ref[...])
    if t_real < L:
        valid = lax.broadcasted_iota(jnp.int32, (L, GDN_HEADS), 0) < t_real
        beta_all = jnp.where(valid, beta_all, 0.0)
        g_all = jnp.where(valid, g_all, 0.0)
    c_all = _mxu_f32(tril, g_all, _NN)
    pairs = range(GDN_HEADS // 2)
    nn, tt, qk, kq, kdec, ec, bcol, elast, vst = [], [], [], [], [], [], [], [], []
    for j in pairs:
        h0, h1 = 2 * j, 2 * j + 1
        stack_col = lambda a: cat([a[:, h0:h0 + 1], a[:, h1:h1 + 1]], axis=0)
        c_col = stack_col(c_all)
        beta_col = stack_col(beta_all)
        c_row = jnp.sum(jnp.where(upper, stack_col(g_all), 0.0), axis=0, keepdims=True)
        decay = jnp.exp(jnp.where(incl, c_col - c_row, -jnp.inf))
        last = lambda rows: cat([jnp.broadcast_to(c_all[L - 1:L, h0:h0 + 1], (rows, 1)),
                                 jnp.broadcast_to(c_all[L - 1:L, h1:h1 + 1], (rows, 1))], axis=0)
        q0, q1 = (l2n(conv_tile(h * d)) * d ** -0.5 for h in (h0, h1))
        k0, k1 = (l2n(conv_tile(GDN_WIDTH + h * d)) for h in (h0, h1))
        vst.append(cat([conv_tile(2 * GDN_WIDTH + h0 * d), conv_tile(2 * GDN_WIDTH + h1 * d)], axis=0))
        k_st = cat([cat([k0, zl], axis=1), cat([zl, k1], axis=1)], axis=0)
        q_st = cat([cat([q0, zl], axis=1), cat([zl, q1], axis=1)], axis=0)
        both = cat([k_st, q_st], axis=0)
        full = _mxu(both, k_st, _NT)
        a = jnp.where(strict, full[:n2] * decay * beta_col, 0.0)
        nn.append(-a)
        tt.append(eye - a)
        qk.append(full[n2:] * decay)
        kq.append(both)
        kdec.append(k_st * jnp.exp(last(L) - c_col))
        ec.append(jnp.exp(c_col))
        bcol.append(beta_col)
        elast.append(jnp.exp(last(d)))
    tt = _neumann_inverse(nn, tt, n2, rounds)
    s_old, ksqs, u = [], [], []
    for j in pairs:
        s_old.append(cat([S[2 * j], S[2 * j + 1]], axis=0))
        ksqs.append(_mxu(kq[j], s_old[j], _NN))
    for j in pairs:
        u.append(_mxu(tt[j], bcol[j] * (vst[j] - ec[j] * ksqs[j][:n2]), _NN))
    nw = nw_ref[...]
    for j in pairs:
        o = ec[j] * ksqs[j][n2:] + _mxu(qk[j], u[j], _NN)
        for hh, oh in ((2 * j, o[:L]), (2 * j + 1, o[L:])):
            cols = slice(hh * d, (hh + 1) * d)
            y = oh * lax.rsqrt(jnp.mean(oh * oh, axis=-1, keepdims=True) + 1e-6) * nw
            mix_ref[:, cols] = y * _silu(cz_ref[:, cols])
        new = s_old[j] * elast[j] + _mxu(kdec[j], u[j], _TN)
        S[2 * j] = new[:d]
        S[2 * j + 1] = new[d:]
    if nchunk > 1:
        tail[...] = qkv_ref[L - 8:L, :]

    @pl.when(ci == nchunk - 1)
    def _():
        s_ref[...] = S[...]
        conv_out_ref[...] = qkv_ref[t_real - (CONV_W - 1):t_real, :]


def _gdn(h, mix_prev, grp, P, st_in, conv_in, s_prev, p):
    L = grp.chunk
    nchunk = grp.t_rows // L
    tail = (GDN_HEADS, GDN_HEAD_DIM, GDN_HEAD_DIM)
    s_ins, s_specs, s_out, s_shape = _state_io(tail, p, st_in, None, grp.bsz)
    cw = 3 * GDN_WIDTH
    full = lambda shape: pl.BlockSpec(shape, lambda b, c: (0,) * len(shape))
    ins = [h, h, h, P['gdn_conv_w'], P['gdn_a_log'].reshape(1, GDN_HEADS), P['gdn_dt_bias'].reshape(1, GDN_HEADS),
           P['gdn_norm'].reshape(1, GDN_HEAD_DIM)]
    specs = [grp.spec(L, cw, ODD_OFF[0]), grp.spec(L, GDN_WIDTH, ODD_OFF[1]), grp.spec(L, 128, ODD_OFF[2]),
             full((CONV_W, cw)), full((1, GDN_HEADS)), full((1, GDN_HEADS)), full((1, GDN_HEAD_DIM))]
    if st_in is not None:
        ins += [conv_in] + s_ins
        specs += [pl.BlockSpec((None, None, CONV_W - 1, cw), lambda b, c: (p, b, 0, 0))] + s_specs
    any_spec = pl.BlockSpec(memory_space=pl.ANY)
    alias = {}
    for prev, out_idx in ((mix_prev, 0), (s_prev, 1)):
        if prev is not None:
            alias[len(ins)] = out_idx
            ins.append(prev)
            specs.append(any_spec)
    n_prev = (mix_prev is not None) + (s_prev is not None)
    return pl.pallas_call(
        functools.partial(_gdn_kernel, L=L, nchunk=nchunk, rounds=int(math.log2(L)), t_real=grp.t_real or L,
                          has_state=st_in is not None, has_prev=n_prev),
        grid=(grp.bsz, nchunk),
        in_specs=specs,
        out_specs=[grp.spec(L, GDN_WIDTH, 0), s_out,
                   pl.BlockSpec((None, CONV_W - 1, cw), lambda b, c: (b, 0, 0))],
        out_shape=[jax.ShapeDtypeStruct((h.shape[0], MIX_WIDTH), F32), s_shape,
                   jax.ShapeDtypeStruct((grp.bsz, CONV_W - 1, cw), F32)],
        scratch_shapes=[pltpu.VMEM(tail, F32), pltpu.VMEM((8, cw), F32)],
        input_output_aliases=alias,
        compiler_params=_cparams(("parallel", "arbitrary")),
        name="gdn",
    )(*ins)


RWKV_PAIRS = RWKV_HEADS // 2
RWKV_PW = 2 * RWKV_HEAD_DIM


def _rwkv_kernel(r_ref, k_ref, v_ref, xwa_ref, gate_ref, mu_ref, w0_ref, w2_ref, a0_ref, a2_ref, kkp_ref, ka_ref,
                 rk_ref, lng_ref, lnb_ref, *rest, L, nchunk, rounds, t_real, has_state, has_prev):
    n_opt = 2 * has_state + has_prev
    mix_ref, s_ref, shift_out_ref, S, last = rest[n_opt:]
    ci = pl.program_id(1)
    n = RWKV_HEAD_DIM
    w3 = 3 * RWKV_WIDTH
    cat = jnp.concatenate
    pairs = range(RWKV_PAIRS)

    @pl.when(ci == 0)
    def _():
        last[...] = jnp.zeros_like(last)
        if has_state:
            last[7:8, :] = rest[0][...]
            zn = jnp.zeros((n, n), F32)
            for j in pairs:
                S[j] = cat([cat([rest[1][2 * j], zn], axis=1), cat([zn, rest[1][2 * j + 1]], axis=1)], axis=0)
        else:
            S[...] = jnp.zeros_like(S)

    n2 = 2 * L
    row = lax.broadcasted_iota(jnp.int32, (L, L), 0)
    col = lax.broadcasted_iota(jnp.int32, (L, L), 1)
    tril = (col <= row).astype(F32)
    r2 = lax.broadcasted_iota(jnp.int32, (n2, n2), 0)
    c2 = lax.broadcasted_iota(jnp.int32, (n2, n2), 1)
    same = (r2 >= L) == (c2 >= L)
    strict = same & (c2 < r2)
    incl = same & (c2 <= r2)
    eye = (r2 == c2).astype(F32)
    lane = lax.broadcasted_iota(jnp.int32, (L, RWKV_PW), 1)
    lo = lane < n

    def stack(x):
        return cat([jnp.where(lo, x, 0.0), jnp.where(lo, 0.0, x)], axis=0)

    row8 = lax.broadcasted_iota(jnp.int32, (8, RWKV_PW), 0)
    valid = lax.broadcasted_iota(jnp.int32, (L, RWKV_PW), 0) < t_real

    def seg_sum(x):
        s_lo = jnp.sum(jnp.where(lo, x, 0.0), axis=-1, keepdims=True)
        s_hi = jnp.sum(jnp.where(lo, 0.0, x), axis=-1, keepdims=True)
        return jnp.where(lo, s_lo, s_hi)

    def shift_mix(ref, c_src, c_all):
        x = ref[:, c_src:c_src + RWKV_PW]
        prev = _shifted(x, last[:, c_all:c_all + RWKV_PW], 1, row8)
        return x + (prev - x) * mu_ref[:, c_all:c_all + RWKV_PW]

    xwa = shift_mix(xwa_ref, 0, w3)
    lr_w = _mxu_f32(jnp.tanh(xwa), w2_ref[...], _NN)
    lr_a = _mxu_f32(xwa, a2_ref[...], _NN)
    a_ak, a_rk, a_rb, nn, tt, sread, kdbd, egl, vs, bonus = [], [], [], [], [], [], [], [], [], []
    for j in pairs:
        sl = slice(j * RWKV_PW, (j + 1) * RWKV_PW)
        r = shift_mix(r_ref, j * RWKV_PW, j * RWKV_PW)
        k = shift_mix(k_ref, j * RWKV_PW, RWKV_WIDTH + j * RWKV_PW)
        v = shift_mix(v_ref, j * RWKV_PW, 2 * RWKV_WIDTH + j * RWKV_PW)
        w_log = -_softplus(-(w0_ref[:, sl] + lr_w[:, sl])) - 0.5
        lw = -jnp.exp(w_log)
        a7 = _sigmoid(a0_ref[:, sl] + lr_a[:, sl])
        kx = k * kkp_ref[:, sl]
        kk = kx * lax.rsqrt(seg_sum(kx * kx) + 1e-6)
        k = k * (1.0 + (a7 - 1.0) * ka_ref[:, sl])
        if t_real < L:
            lw, kk, k, v = (jnp.where(valid, a, 0.0) for a in (lw, kk, k, v))
        b = kk * a7
        bonus.append(seg_sum(r * k * rk_ref[:, sl]) * v)
        vs.append(stack(v))
        g = _mxu_f32(tril, lw, _NN)
        gp = g - lw
        gm = g[L // 2 - 1:L // 2, :]
        gl = g[L - 1:L, :]
        e_neg = jnp.exp(gm - g)
        lhs = cat([stack(kk * jnp.exp(gp - gm)), stack(r * jnp.exp(g - gm))], axis=0)
        rhs = cat([stack(b * e_neg), stack(k * e_neg)], axis=0)
        full = _mxu(lhs, rhs, _NT)
        a_ab = jnp.where(strict, full[:n2, :n2], 0.0)
        a_ak.append(jnp.where(strict, full[:n2, n2:], 0.0))
        a_rb.append(jnp.where(incl, full[n2:, :n2], 0.0))
        a_rk.append(jnp.where(incl, full[n2:, n2:], 0.0))
        nn.append(-a_ab)
        tt.append(eye - a_ab)
        sread.append(cat([stack(kk * jnp.exp(gp)), stack(r * jnp.exp(g))], axis=0))
        dec = jnp.exp(gl - g)
        kdbd.append(cat([stack(k * dec), stack(-b * dec)], axis=0))
        egl.append(jnp.exp(gl))
    tt = _neumann_inverse(nn, tt, n2, rounds)
    s_old, sr, av, u = [], [], [], []
    for j in pairs:
        s_old.append(S[j])
        sr.append(_mxu(sread[j], s_old[j], _NT))
        av.append(_mxu(cat([a_ak[j], a_rk[j]], axis=0), vs[j], _NN))
    for j in pairs:
        u.append(_mxu(tt[j], sr[j][:n2] + av[j][:n2], _NN))
    for j in pairs:
        sl = slice(j * RWKV_PW, (j + 1) * RWKV_PW)
        o = sr[j][n2:] + av[j][n2:] - _mxu(a_rb[j], u[j], _NN)
        o = o[:L] + o[L:]
        oc = o - seg_sum(o) * (1.0 / n)
        gn = oc * lax.rsqrt(seg_sum(oc * oc) * (1.0 / n) + RWKV_GN_EPS)
        y = gn * lng_ref[:, sl] + lnb_ref[:, sl] + bonus[j]
        mix_ref[:, sl] = y * _silu(gate_ref[:, sl])
        S[j] = s_old[j] * egl[j] + _mxu(cat([vs[j], u[j]], axis=0), kdbd[j], _TN)
    pieces = ((r_ref, 0, RWKV_WIDTH), (k_ref, RWKV_WIDTH, RWKV_WIDTH), (v_ref, 2 * RWKV_WIDTH, RWKV_WIDTH),
              (xwa_ref, w3, RWKV_PW))
    if nchunk > 1:
        for ref, c0, wd in pieces:
            last[:, c0:c0 + wd] = ref[L - 8:L, :]

    @pl.when(ci == nchunk - 1)
    def _():
        for ref, c0, wd in pieces:
            shift_out_ref[:, c0:c0 + wd] = ref[t_real - 1:t_real, :]
        for j in pairs:
            s_ref[2 * j] = S[j][:n, :n]
            s_ref[2 * j + 1] = S[j][n:, n:]


def _rwkv(h, mix_prev, grp, P, st_in, shift_in, s_prev, p):
    L = grp.chunk
    nchunk = grp.t_rows // L
    tail = (RWKV_HEADS, RWKV_HEAD_DIM, RWKV_HEAD_DIM)
    s_ins, s_specs, s_out, s_shape = _state_io(tail, p, st_in, None, grp.bsz)
    full = lambda shape: pl.BlockSpec(shape, lambda b, c: (0,) * len(shape))
    row = lambda a: a.reshape(1, -1)
    zr = jnp.zeros((RWKV_HEAD_DIM, RWKV_WIDTH), F32)
    w2 = jnp.concatenate([P['rwkv_w2'], zr], axis=0)
    a2 = jnp.concatenate([zr, P['rwkv_a2']], axis=0)
    off = ODD_OFF[4]
    ins = [h, h, h, h, h, row(P['rwkv_mu']), row(P['rwkv_w0']), w2, row(P['rwkv_a0']), a2, row(P['rwkv_kk']),
           row(P['rwkv_ka']), row(P['rwkv_rk']), row(P['rwkv_ln_g']), row(P['rwkv_ln_b'])]
    vec = full((1, RWKV_WIDTH))
    specs = [grp.spec(L, RWKV_WIDTH, off), grp.spec(L, RWKV_WIDTH, off + RWKV_WIDTH),
             grp.spec(L, RWKV_WIDTH, off + 2 * RWKV_WIDTH), grp.spec(L, RWKV_PW, off + 3 * RWKV_WIDTH),
             grp.spec(L, RWKV_WIDTH, ODD_OFF[5]), full((1, RWKV_SHIFT_DIM)), vec, full((RWKV_PW, RWKV_WIDTH)), vec,
             full((RWKV_PW, RWKV_WIDTH)), vec, vec, vec, vec, vec]
    if st_in is not None:
        ins += [shift_in] + s_ins
        specs += [pl.BlockSpec((None, None, 1, RWKV_SHIFT_DIM), lambda b, c: (p, b, 0, 0))] + s_specs
    any_spec = pl.BlockSpec(memory_space=pl.ANY)
    alias = {}
    for prev, out_idx in ((mix_prev, 0), (s_prev, 1)):
        if prev is not None:
            alias[len(ins)] = out_idx
            ins.append(prev)
            specs.append(any_spec)
    n_prev = (mix_prev is not None) + (s_prev is not None)
    return pl.pallas_call(
        functools.partial(_rwkv_kernel, L=L, nchunk=nchunk, rounds=int(math.log2(L)), t_real=grp.t_real or L,
                          has_state=st_in is not None, has_prev=n_prev),
        grid=(grp.bsz, nchunk),
        in_specs=specs,
        out_specs=[grp.spec(L, RWKV_WIDTH, GDN_WIDTH), s_out,
                   pl.BlockSpec((None, 1, RWKV_SHIFT_DIM), lambda b, c: (b, 0, 0))],
        out_shape=[jax.ShapeDtypeStruct((h.shape[0], MIX_WIDTH), F32), s_shape,
                   jax.ShapeDtypeStruct((grp.bsz, 1, RWKV_SHIFT_DIM), F32)],
        scratch_shapes=[pltpu.VMEM((RWKV_PAIRS, RWKV_PW, RWKV_PW), F32), pltpu.VMEM((8, RWKV_SHIFT_DIM), F32)],
        input_output_aliases=alias,
        compiler_params=_cparams(("parallel", "arbitrary")),
        name="rwkv7",
    )(*ins)


def _pad_t(a, t_to):
    t = a.shape[1]
    if t == t_to:
        return a
    return jnp.pad(a, [(0, 0), (0, t_to - t)] + [(0, 0)] * (a.ndim - 2))


def _even_mix(h, mix, grp, mem_k, mem_v, layer, st_in, st_prev, conv_in, P):
    p = layer // 2
    mix, s_ssd, s_conv = _ssd(h, mix, grp, P, st_in['ssd'], conv_in, st_prev['ssd'], p)
    gla_l, gla_tb = (GLA_CHUNK, 256) if grp.t_real is None else (grp.chunk, grp.chunk)
    mix, s_gla = _gla(h, mix, grp, P, st_in['gla'], st_prev['gla'], p, gla_l, gla_tb)
    mix = _mem_attention(h, mix, grp, EVEN_OFF[8], EVEN_OFF[9], mem_k, mem_v, layer, min(512, grp.t_rows))
    return mix, dict(gla=s_gla, ssd=s_ssd), s_conv


def _odd_mix(h, mix, grp, mem_k, mem_v, layer, st_in, st_prev, conv_in, shift_in, P):
    p = layer // 2
    mix, s_gdn, s_conv = _gdn(h, mix, grp, P, st_in['gdn'], conv_in, st_prev['gdn'], p)
    mix, s_rwkv, s_shift = _rwkv(h, mix, grp, P, st_in['rwkv'], shift_in, st_prev['rwkv'], p)
    mix = _mem_attention(h, mix, grp, ODD_OFF[6], ODD_OFF[7], mem_k, mem_v, layer, min(512, grp.t_rows))
    return mix, dict(gdn=s_gdn, rwkv=s_rwkv), s_conv, s_shift.reshape(grp.bsz, RWKV_SHIFT_DIM)


def kernel(x_prompt, x_sample, mem_prompt, cache_mem_k, cache_mem_v, state_gla, state_ssd, state_ssd_conv, state_gdn, state_gdn_conv, state_rwkv, state_rwkv_shift, mem_w_kv, ev_w_in, ev_gla_w2, ev_gla_b, ev_gla_norm, ev_ssd_conv_w, ev_ssd_conv_b, ev_ssd_dt_bias, ev_ssd_a_log, ev_ssd_d, ev_ssd_norm, ev_w_out, ev_ln_g, ev_ln_b, od_w_in, od_gdn_conv_w, od_gdn_dt_bias, od_gdn_a_log, od_gdn_norm, od_rwkv_mu, od_rwkv_w0, od_rwkv_w2, od_rwkv_a0, od_rwkv_a2, od_rwkv_kk, od_rwkv_ka, od_rwkv_rk, od_rwkv_ln_g, od_rwkv_ln_b, od_w_out, od_ln_g, od_ln_b):
    ev = dict(w_in=ev_w_in, gla_w2=ev_gla_w2, gla_b=ev_gla_b, gla_norm=ev_gla_norm,
              ssd_conv_w=ev_ssd_conv_w, ssd_conv_b=ev_ssd_conv_b, ssd_dt_bias=ev_ssd_dt_bias,
              ssd_a_log=ev_ssd_a_log, ssd_d=ev_ssd_d, ssd_norm=ev_ssd_norm,
              w_out=ev_w_out, ln_g=ev_ln_g, ln_b=ev_ln_b)
    od = dict(w_in=od_w_in, gdn_conv_w=od_gdn_conv_w, gdn_dt_bias=od_gdn_dt_bias, gdn_a_log=od_gdn_a_log,
              gdn_norm=od_gdn_norm, rwkv_mu=od_rwkv_mu, rwkv_w0=od_rwkv_w0, rwkv_w2=od_rwkv_w2,
              rwkv_a0=od_rwkv_a0, rwkv_a2=od_rwkv_a2, rwkv_kk=od_rwkv_kk, rwkv_ka=od_rwkv_ka,
              rwkv_rk=od_rwkv_rk, rwkv_ln_g=od_rwkv_ln_g, rwkv_ln_b=od_rwkv_ln_b,
              w_out=od_w_out, ln_g=od_ln_g, ln_b=od_ln_b)
    bp, tp, _ = x_prompt.shape
    bs, ts, _ = x_sample.shape
    mp, ms = bp * tp, bs * SMALL_T
    grp_p = _Group(bp, tp, None, 0, GDN_CHUNK)
    grp_s = _Group(bs, SMALL_T, ts, mp, SMALL_T)

    w_kv = jnp.moveaxis(mem_w_kv, 0, 1).reshape(D_MODEL, DEPTH * 2 * MEM_WIDTH).astype(BF16)
    kv = _matmul(mem_prompt.reshape(bp * MEM_LEN, D_MODEL).astype(BF16), w_kv, 512, 1024)
    kv6 = kv.reshape(bp, MEM_LEN, DEPTH, 2, MEM_HEADS, MEM_HEAD_DIM)
    mem_k_p = jnp.moveaxis(kv6[:, :, :, 0], 2, 0)
    mem_v_p = jnp.moveaxis(kv6[:, :, :, 1], 2, 0)
    mk_s = cache_mem_k.reshape(cache_mem_k.shape[:-2] + (MEM_WIDTH,))
    mv_s = cache_mem_v.reshape(cache_mem_v.shape[:-2] + (MEM_WIDTH,))

    x = jnp.concatenate([x_prompt.reshape(mp, D_MODEL),
                         _pad_t(x_sample, SMALL_T).reshape(ms, D_MODEL)], axis=0)
    x_bf = x.astype(BF16)
    zp = lambda shape: jnp.zeros(shape, F32)
    names = ('gla', 'ssd', 'gdn', 'rwkv')
    none = {n: None for n in names}
    in_s = dict(gla=state_gla, ssd=state_ssd, gdn=state_gdn, rwkv=state_rwkv)
    shift_s = state_rwkv_shift.reshape(N_PAIRS, bs, 1, RWKV_SHIFT_DIM)
    out_p, out_s = dict(none), dict(none)
    small_p = {n: [] for n in ('ssd_conv', 'gdn_conv', 'rwkv_shift')}
    small_s = {n: [] for n in small_p}
    tm = 1024
    for layer in range(DEPTH):
        p = layer // 2
        if layer % 2 == 0:
            P = {n: w[p] for n, w in ev.items()}
            h = _matmul(x_bf, _pack_w_in(P['w_in'], EVEN_SIZES, EVEN_ORDER, EVEN_N), tm, PROJ_TN)
            mix, new, c1 = _even_mix(h, None, grp_p, kv, kv, layer, none, out_p, None, P)
            out_p.update(new)
            mix, new, c2 = _even_mix(h, mix, grp_s, mk_s, mv_s, layer, in_s, out_s, state_ssd_conv, P)
            out_s.update(new)
            small_p['ssd_conv'].append(c1)
            small_s['ssd_conv'].append(c2)
            w_out = jnp.concatenate([P['w_out'][GLA_WIDTH:GLA_WIDTH + SSD_WIDTH], P['w_out'][:GLA_WIDTH],
                                     P['w_out'][GLA_WIDTH + SSD_WIDTH:]], axis=0)
        else:
            P = {n: w[p] for n, w in od.items()}
            h = _matmul(x_bf, _pack_w_in(P['w_in'], ODD_SIZES, ODD_ORDER, ODD_N), tm, PROJ_TN)
            mix, new, c1, h1 = _odd_mix(h, None, grp_p, kv, kv, layer, none, out_p, None, None, P)
            out_p.update(new)
            mix, new, c2, h2 = _odd_mix(h, mix, grp_s, mk_s, mv_s, layer, in_s, out_s, state_gdn_conv, shift_s, P)
            out_s.update(new)
            small_p['gdn_conv'].append(c1)
            small_s['gdn_conv'].append(c2)
            small_p['rwkv_shift'].append(h1)
            small_s['rwkv_shift'].append(h2)
            w_out = P['w_out']
        x, x_bf = _out_ln(mix, w_out.astype(BF16), x, P['ln_g'], P['ln_b'])

    y_prompt = x[:mp].reshape(bp, tp, D_MODEL)
    y_sample = x[mp:].reshape(bs, SMALL_T, D_MODEL)[:, :ts]
    st = lambda d, n: jnp.stack(d[n])
    return (y_prompt, y_sample, mem_k_p, mem_v_p,
            out_p['gla'], out_s['gla'], out_p['ssd'], out_s['ssd'],
            st(small_p, 'ssd_conv'), st(small_s, 'ssd_conv'), out_p['gdn'], out_s['gdn'],
            st(small_p, 'gdn_conv'), st(small_s, 'gdn_conv'), out_p['rwkv'], out_s['rwkv'],
            st(small_p, 'rwkv_shift'), st(small_s, 'rwkv_shift'))
```

```python
import functools
import math

import numpy as np
import jax
import jax.numpy as jnp
from jax import lax
from jax.experimental import pallas as pl
from jax.experimental.pallas import tpu as pltpu

F32 = jnp.float32
BF16 = jnp.bfloat16
HI = lax.Precision.HIGHEST

D_MODEL = 2048
DEPTH = 4
N_PAIRS = DEPTH // 2
CONV_W = 4
MEM_LEN = 256
MEM_HEADS = 4
MEM_HEAD_DIM = 256
MEM_WIDTH = 1024
GLA_HEADS = 4
GLA_DK = 128
GLA_DV = 256
GLA_QK = 512
GLA_WIDTH = 1024
GLA_RANK = 16
GLA_TAU = 16.0
SSD_WIDTH = 2048
SSD_HEAD_DIM = 64
SSD_HEADS = 32
SSD_GROUPS = 4
SSD_REP = 8
SSD_STATE = 128
SSD_CONV_DIM = SSD_WIDTH + 2 * SSD_GROUPS * SSD_STATE
GDN_WIDTH = 2048
GDN_HEAD_DIM = 128
GDN_HEADS = 16
RWKV_WIDTH = 1024
RWKV_HEAD_DIM = 64
RWKV_HEADS = 16
RWKV_W_RANK = 64
RWKV_A_RANK = 64
RWKV_SHIFT_DIM = 3 * RWKV_WIDTH + RWKV_W_RANK + RWKV_A_RANK
RWKV_GN_EPS = 64e-5
EVEN_SIZES = (GLA_QK, GLA_QK, GLA_WIDTH, GLA_RANK, GLA_WIDTH, SSD_WIDTH, SSD_CONV_DIM, SSD_HEADS,
              MEM_WIDTH, MEM_WIDTH)
ODD_SIZES = (3 * GDN_WIDTH, GDN_WIDTH, GDN_HEADS, GDN_HEADS, RWKV_SHIFT_DIM, RWKV_WIDTH, MEM_WIDTH, MEM_WIDTH)
MIX_WIDTH = 4096
DEEPNORM_ALPHA = (2 * DEPTH) ** 0.25

EVEN_ORDER = (5, 2, 4, 8, 9, 6, 0, 1, 3, 7)
ODD_ORDER = (0, 1, 5, 6, 7, 4, 2, 3)
PROJ_TN = 768
VMEM_LIMIT = 48 * 1024 * 1024

GLA_CHUNK = 16
SSD_CHUNK = 64
GDN_CHUNK = 64
RWKV_CHUNK = 64
SMALL_T = 8

_NN = ((1,), (0,))
_NT = ((1,), (1,))
_TN = ((0,), (0,))


def _packed_layout(sizes, order):
    offs, o = {}, 0
    for i in order:
        offs[i] = o
        o += sizes[i]
    total = -(-o // PROJ_TN) * PROJ_TN
    return offs, total


EVEN_OFF, EVEN_N = _packed_layout(EVEN_SIZES, EVEN_ORDER)
ODD_OFF, ODD_N = _packed_layout(ODD_SIZES, ODD_ORDER)


def _pack_w_in(w, sizes, order, total):
    segs = jnp.split(w, np.cumsum(sizes)[:-1].tolist(), axis=-1)
    parts = [segs[i] for i in order]
    used = sum(sizes)
    if total > used:
        parts.append(jnp.zeros((w.shape[0], total - used), w.dtype))
    return jnp.concatenate(parts, axis=-1).astype(BF16)


def _seg(h, offs, sizes, i):
    return h[..., offs[i]:offs[i] + sizes[i]]


def _cparams(sem):
    return pltpu.CompilerParams(dimension_semantics=sem, vmem_limit_bytes=VMEM_LIMIT)


def _mxu(a, b, dims):
    return lax.dot_general(a.astype(BF16), b.astype(BF16), (dims, ((), ())), preferred_element_type=F32)


def _mxu_f32(a, b, dims):
    return lax.dot_general(a, b, (dims, ((), ())), precision=HI, preferred_element_type=F32)


def _sigmoid(x):
    return 1.0 / (1.0 + jnp.exp(-x))


def _silu(x):
    return x * _sigmoid(x)


def _softplus(x):
    return jnp.maximum(x, 0.0) + jnp.log(1.0 + jnp.exp(-jnp.abs(x)))


def _shifted(u, prev8, j, row8):
    ru = pltpu.roll(u, j, 0)
    top = jnp.where(row8 < j, pltpu.roll(prev8, j, 0), ru[:8])
    return top if u.shape[0] == 8 else jnp.concatenate([top, ru[8:]], axis=0)


class _Group:
    def __init__(self, bsz, t_rows, t_real, row0, chunk):
        self.bsz, self.t_rows, self.t_real, self.row0, self.chunk = bsz, t_rows, t_real, row0, chunk

    def spec(self, rows, width, off):
        assert off % width == 0 and self.row0 % rows == 0 and self.t_rows % rows == 0
        base, per, cb = self.row0 // rows, self.t_rows // rows, off // width
        return pl.BlockSpec((rows, width), lambda b, i: (base + b * per + i, cb))


def _alias_last(n_inputs, has_prev, out_index=0):
    return {n_inputs - 1: out_index} if has_prev else {}


def _neumann_inverse(nn, tt, n, rounds):
    idx = range(len(nn))
    if rounds >= 2:
        for j in idx:
            nn[j] = _mxu(nn[j], nn[j], _NN)
        for _ in range(rounds - 2):
            for j in idx:
                both = _mxu(nn[j], jnp.concatenate([tt[j], nn[j]], axis=1), _NN)
                tt[j] = tt[j] + both[:, :n]
                nn[j] = both[:, n:]
        for j in idx:
            tt[j] = tt[j] + _mxu(nn[j], tt[j], _NN)
    return tt


def _state_io(tail, p, s_in, s_prev, bsz):
    zeros = (0,) * len(tail)
    spec = pl.BlockSpec((None, None) + tail, lambda *g: (p, g[0]) + zeros)
    ins, specs = [], []
    if s_in is not None:
        ins.append(s_in)
        specs.append(spec)
    if s_prev is not None:
        ins.append(s_prev)
        specs.append(pl.BlockSpec(memory_space=pl.ANY))
    shape = jax.ShapeDtypeStruct((N_PAIRS, bsz) + tail, F32)
    return ins, specs, spec, shape


def _mm_kernel(x_ref, w_ref, o_ref, *, precision):
    o_ref[...] = jnp.dot(x_ref[...], w_ref[...], preferred_element_type=F32, precision=precision)


def _matmul(x, w, tm, tn, precision=None):
    m, k = x.shape
    n = w.shape[1]
    assert m % tm == 0 and n % tn == 0
    return pl.pallas_call(
        functools.partial(_mm_kernel, precision=precision),
        grid=(n // tn, m // tm),
        in_specs=[pl.BlockSpec((tm, k), lambda j, i: (i, 0)),
                  pl.BlockSpec((k, tn), lambda j, i: (0, j))],
        out_specs=pl.BlockSpec((tm, tn), lambda j, i: (i, j)),
        out_shape=jax.ShapeDtypeStruct((m, n), F32),
        compiler_params=_cparams(("parallel", "parallel")),
        name="matmul",
    )(x, w)


def _out_ln_kernel(mix_ref, w_ref, x_ref, g_ref, b_ref, y_ref, ybf_ref, acc, *, nk):
    kk = pl.program_id(1)

    @pl.when(kk == 0)
    def _():
        acc[...] = jnp.zeros_like(acc)

    acc[...] += jnp.dot(mix_ref[...].astype(BF16), w_ref[...], preferred_element_type=F32)

    @pl.when(kk == nk - 1)
    def _():
        z = DEEPNORM_ALPHA * x_ref[...] + acc[...]
        zc = z - jnp.mean(z, axis=-1, keepdims=True)
        var = jnp.mean(zc * zc, axis=-1, keepdims=True)
        y = zc * lax.rsqrt(var + 1e-5) * g_ref[...] + b_ref[...]
        y_ref[...] = y
        ybf_ref[...] = y.astype(BF16)


def _out_ln(mix, w, x, g, b, tm=512, tk=1024):
    m, k = mix.shape
    d = w.shape[1]
    nk = k // tk
    return pl.pallas_call(
        functools.partial(_out_ln_kernel, nk=nk),
        grid=(m // tm, nk),
        in_specs=[pl.BlockSpec((tm, tk), lambda i, j: (i, j)),
                  pl.BlockSpec((tk, d), lambda i, j: (j, 0)),
                  pl.BlockSpec((tm, d), lambda i, j: (i, 0)),
                  pl.BlockSpec((1, d), lambda i, j: (0, 0)),
                  pl.BlockSpec((1, d), lambda i, j: (0, 0))],
        out_specs=[pl.BlockSpec((tm, d), lambda i, j: (i, 0)),
                   pl.BlockSpec((tm, d), lambda i, j: (i, 0))],
        out_shape=[jax.ShapeDtypeStruct((m, d), F32), jax.ShapeDtypeStruct((m, d), BF16)],
        scratch_shapes=[pltpu.VMEM((tm, d), F32)],
        compiler_params=_cparams(("parallel", "arbitrary")),
        name="out_ln",
    )(mix, w, x, g.reshape(1, d), b.reshape(1, d))


def _mem_kernel(q_ref, gate_ref, k_ref, v_ref, *rest):
    o_ref = rest[-1]
    for h in range(MEM_HEADS):
        sl = slice(h * MEM_HEAD_DIM, (h + 1) * MEM_HEAD_DIM)
        k = k_ref[:, sl]
        v = v_ref[:, sl]
        s = _mxu(q_ref[:, sl], k, _NT) * MEM_HEAD_DIM ** -0.5
        p = jnp.exp(s - jnp.max(s, axis=-1, keepdims=True))
        p = p / jnp.sum(p, axis=-1, keepdims=True)
        o_ref[:, sl] = _mxu(p, v, _NN) * _silu(gate_ref[:, sl])


MEM_DT = MEM_HEAD_DIM // 128
MEM_ROWS = MEM_LEN * MEM_DT * MEM_HEADS


def _cache_view(c):
    d, b = c.shape[:2]
    c = c.reshape(d, b, MEM_LEN, MEM_HEADS, MEM_DT, 128)
    return jnp.transpose(c, (0, 1, 2, 4, 3, 5)).reshape(d, b, MEM_ROWS, 128)


def _mem_cache_kernel(q_ref, gate_ref, k_ref, v_ref, *rest):
    o_ref = rest[-1]
    t = q_ref.shape[0]
    grp = MEM_DT * MEM_HEADS
    k = k_ref[...]
    v = v_ref[...]
    qs = [jnp.concatenate([q_ref[:, h * MEM_HEAD_DIM + dt * 128:h * MEM_HEAD_DIM + (dt + 1) * 128]
                           for h in range(MEM_HEADS)], axis=0) for dt in range(MEM_DT)]
    s = _mxu(qs[0], k, _NT)
    for dt in range(1, MEM_DT):
        s = s + pltpu.roll(_mxu(qs[dt], k, _NT), MEM_ROWS - dt * MEM_HEADS, 1)
    col = lax.broadcasted_iota(jnp.int32, (MEM_HEADS * t, MEM_ROWS), 1) % grp
    head = lax.broadcasted_iota(jnp.int32, (MEM_HEADS * t, MEM_ROWS), 0) // t
    s = jnp.where(col == head, s * MEM_HEAD_DIM ** -0.5, -jnp.inf)
    p = jnp.exp(s - jnp.max(s, axis=-1, keepdims=True))
    p = p / jnp.sum(p, axis=-1, keepdims=True)
    for dt in range(MEM_DT):
        o = _mxu(p if dt == 0 else pltpu.roll(p, dt * MEM_HEADS, 1), v, _NN)
        for h in range(MEM_HEADS):
            sl = slice(h * MEM_HEAD_DIM + dt * 128, h * MEM_HEAD_DIM + (dt + 1) * 128)
            o_ref[:, sl] = o[h * t:(h + 1) * t] * _silu(gate_ref[:, sl])


def _mem_attention(h, mix_prev, grp, q_off, gate_off, mem_k, mem_v, layer, tq):
    cached = mem_k.ndim == 4
    if cached:
        kv_specs = [pl.BlockSpec((None, None, MEM_ROWS, 128), lambda b, i: (layer, b, 0, 0))] * 2
    else:
        kv_specs = [pl.BlockSpec((MEM_LEN, MEM_WIDTH), lambda b, i: (b, 2 * layer)),
                    pl.BlockSpec((MEM_LEN, MEM_WIDTH), lambda b, i: (b, 2 * layer + 1))]
    ins = [h, h, mem_k, mem_v] + ([] if mix_prev is None else [mix_prev])
    specs = [grp.spec(tq, MEM_WIDTH, q_off), grp.spec(tq, MEM_WIDTH, gate_off)] + kv_specs
    if mix_prev is not None:
        specs.append(pl.BlockSpec(memory_space=pl.ANY))
    return pl.pallas_call(
        _mem_cache_kernel if cached else _mem_kernel,
        grid=(grp.bsz, grp.t_rows // tq),
        in_specs=specs,
        out_specs=grp.spec(tq, MEM_WIDTH, MIX_WIDTH - MEM_WIDTH),
        out_shape=jax.ShapeDtypeStruct((h.shape[0], MIX_WIDTH), F32),
        input_output_aliases=_alias_last(len(ins), mix_prev is not None),
        compiler_params=_cparams(("parallel", "parallel")),
        name="mem_attention",
    )(*ins)


def _gla_kernel(q_ref, k_ref, v_ref, gate_ref, sm_ref, w2_ref, gb_ref, nw_ref, *rest, L, nchunk, nblk, t_real,
                has_state, has_prev):
    n_opt = has_state + has_prev
    mix_ref, s_ref, ST, QK, B, OI = rest[n_opt:]
    tb = pl.program_id(1)
    rows_blk = q_ref.shape[0]

    @pl.when(tb == 0)
    def _():
        for h in range(GLA_HEADS):
            ST[h] = rest[0][h].T if has_state else jnp.zeros((GLA_DV, GLA_DK), F32)

    z = _mxu_f32(sm_ref[...], w2_ref[...], _NN) + gb_ref[...]
    g_all = -_softplus(-z) * (1.0 / GLA_TAU)
    t_i = lax.broadcasted_iota(jnp.int32, (rows_blk, GLA_DK), 0)
    t_c = t_i & (L - 1)
    nw = nw_ref[...]
    for h in range(GLA_HEADS):
        ks = slice(h * GLA_DK, (h + 1) * GLA_DK)
        vs = slice(h * GLA_DV, (h + 1) * GLA_DV)
        q = q_ref[:, ks] * GLA_DK ** -0.5
        k = k_ref[:, ks]
        b = g_all[:, ks]
        v = v_ref[:, vs]
        if t_real < L:
            b = jnp.where(t_i < t_real, b, 0.0)
            k = jnp.where(t_i < t_real, k, 0.0)
        sh = 1
        while sh < L:
            b = b + jnp.where(t_c >= sh, pltpu.roll(b, sh, 0), 0.0)
            sh *= 2
        o = jnp.sum(q * k, axis=-1, keepdims=True) * v
        for j in range(1, L):
            d = jnp.where(t_c >= j, b - pltpu.roll(b, j, 0), -jnp.inf)
            p = jnp.exp(d) * q * pltpu.roll(k, j, 0)
            o = o + jnp.sum(p, axis=-1, keepdims=True) * pltpu.roll(v, j, 0)
        OI[:, vs] = o
        B[:, ks] = b
        QK[:, ks] = q
        QK[:, GLA_QK + h * GLA_DK:GLA_QK + (h + 1) * GLA_DK] = k
    for c in range(rows_blk // L):
        rows = slice(c * L, (c + 1) * L)
        for h in range(GLA_HEADS):
            ks = slice(h * GLA_DK, (h + 1) * GLA_DK)
            vs = slice(h * GLA_DV, (h + 1) * GLA_DV)
            b = B[rows, ks]
            b_last = b[L - 1:L, :]
            st = ST[h]
            o = OI[rows, vs] + _mxu(QK[rows, ks] * jnp.exp(b), st, _NT)
            y = o * lax.rsqrt(jnp.mean(o * o, axis=-1, keepdims=True) + 1e-6) * nw
            mix_ref[rows, vs] = y * _silu(gate_ref[rows, vs])
            kd = QK[rows, GLA_QK + h * GLA_DK:GLA_QK + (h + 1) * GLA_DK] * jnp.exp(b_last - b)
            ST[h] = st * jnp.exp(b_last) + _mxu(v_ref[rows, vs], kd, _TN)

    @pl.when(tb == nblk - 1)
    def _():
        for h in range(GLA_HEADS):
            s_ref[h] = ST[h].T


def _gla(h, mix_prev, grp, P, st_in, s_prev, p, L, tb):
    nblk = grp.t_rows // tb
    tail = (GLA_HEADS, GLA_DK, GLA_DV)
    s_ins, s_specs, s_out, s_shape = _state_io(tail, p, st_in, None, grp.bsz)
    full = lambda shape: pl.BlockSpec(shape, lambda b, c: (0,) * len(shape))
    w2 = jnp.concatenate([P['gla_w2'], jnp.zeros((128 - GLA_RANK, GLA_QK), F32)], axis=0)
    ins = [h, h, h, h, h, w2, P['gla_b'].reshape(1, GLA_QK), P['gla_norm'].reshape(1, GLA_DV)]
    specs = [grp.spec(tb, GLA_QK, EVEN_OFF[0]), grp.spec(tb, GLA_QK, EVEN_OFF[1]), grp.spec(tb, GLA_WIDTH, EVEN_OFF[2]),
             grp.spec(tb, GLA_WIDTH, EVEN_OFF[4]), grp.spec(tb, 128, EVEN_OFF[3]),
             full((128, GLA_QK)), full((1, GLA_QK)), full((1, GLA_DV))]
    ins += s_ins
    specs += s_specs
    any_spec = pl.BlockSpec(memory_space=pl.ANY)
    alias = {}
    for prev, out_idx in ((mix_prev, 0), (s_prev, 1)):
        if prev is not None:
            alias[len(ins)] = out_idx
            ins.append(prev)
            specs.append(any_spec)
    n_prev = (mix_prev is not None) + (s_prev is not None)
    return pl.pallas_call(
        functools.partial(_gla_kernel, L=L, nchunk=tb // L, nblk=nblk, t_real=grp.t_real or L,
                          has_state=st_in is not None, has_prev=n_prev),
        grid=(grp.bsz, nblk),
        in_specs=specs,
        out_specs=[grp.spec(tb, GLA_WIDTH, SSD_WIDTH), s_out],
        out_shape=[jax.ShapeDtypeStruct((h.shape[0], MIX_WIDTH), F32), s_shape],
        scratch_shapes=[pltpu.VMEM((GLA_HEADS, GLA_DV, GLA_DK), F32), pltpu.VMEM((tb, 2 * GLA_QK), F32),
                        pltpu.VMEM((tb, GLA_QK), F32), pltpu.VMEM((tb, GLA_WIDTH), F32)],
        input_output_aliases=alias,
        compiler_params=_cparams(("parallel", "arbitrary")),
        name="gla",
    )(*ins)


SSD_GW = SSD_REP * SSD_HEAD_DIM


def _ssd_lanes(L):
    return max(SSD_REP * L, 128)


def _ssd_kernel(sz_ref, xbc_ref, sm_ref, cw_ref, cbias_ref, dtb_ref, alog_ref, dvec_ref, nw_ref, ep_ref, es_ref, *rest,
                L, nchunk, t_real, has_state, has_prev):
    n_opt = 2 * has_state + has_prev
    mix_ref, s_ref, conv_out_ref, ST, tail = rest[n_opt:]
    ci = pl.program_id(1)
    gs = _ssd_lanes(L)
    cat = jnp.concatenate

    @pl.when(ci == 0)
    def _():
        tail[...] = jnp.zeros_like(tail)
        if has_state:
            tail[8 - (CONV_W - 1):8, :] = rest[0][...]
            for g in range(SSD_GROUPS):
                ST[g] = rest[1][g * SSD_REP:(g + 1) * SSD_REP].reshape(SSD_GW, SSD_STATE).T
        else:
            ST[...] = jnp.zeros_like(ST)

    d = 128
    row8 = lax.broadcasted_iota(jnp.int32, (8, d), 0)

    def conv_tile(c0):
        u = xbc_ref[:, c0:c0 + d]
        p8 = tail[:, c0:c0 + d]
        w = cw_ref[:, c0:c0 + d]
        acc = u * w[CONV_W - 1:CONV_W] + cbias_ref[:, c0:c0 + d]
        for j in range(1, CONV_W):
            acc = acc + _shifted(u, p8, j, row8) * w[CONV_W - 1 - j:CONV_W - j]
        return _silu(acc)

    row = lax.broadcasted_iota(jnp.int32, (L, L), 0)
    col = lax.broadcasted_iota(jnp.int32, (L, L), 1)
    tril = (col <= row).astype(F32)
    dt = _softplus(sm_ref[:, GLA_RANK:GLA_RANK + SSD_HEADS] + dtb_ref[...])
    if t_real < L:
        dt = jnp.where(lax.broadcasted_iota(jnp.int32, (L, SSD_HEADS), 0) < t_real, dt, 0.0)
    c = _mxu_f32(tril, dt * -jnp.exp(alog_ref[...]), _NN)
    ep = ep_ref[...]
    dt_x = _mxu_f32(dt, ep, _NN)
    c_x = _mxu_f32(c, ep, _NN)
    c_s = _mxu_f32(c, es_ref[...], _NN)
    t_i = lax.broadcasted_iota(jnp.int32, (L, SSD_GROUPS * gs), 0)
    s_i = lax.broadcasted_iota(jnp.int32, (L, SSD_GROUPS * gs), 1) & (L - 1)
    c_src = jnp.sum(jnp.where(t_i == s_i, c_s, 0.0), axis=0, keepdims=True)
    seg = jnp.exp(jnp.where(s_i <= t_i, c_s - c_src, -jnp.inf))
    blk_r = lax.broadcasted_iota(jnp.int32, (gs, SSD_GW), 0) // L
    blk_c = lax.broadcasted_iota(jnp.int32, (gs, SSD_GW), 1) // SSD_HEAD_DIM
    diag = blk_r == blk_c
    reps = SSD_REP * L
    for g in range(SSD_GROUPS):
        gl = slice(g * SSD_GW, (g + 1) * SSD_GW)
        sx = cat([conv_tile(g * SSD_GW + i * d) for i in range(SSD_GW // d)], axis=1)
        bm = conv_tile(SSD_WIDTH + g * SSD_STATE)
        cm = conv_tile(SSD_WIDTH + SSD_GROUPS * SSD_STATE + g * SSD_STATE)
        xdt = sx * dt_x[:, gl]
        pad_rows = [] if reps == gs else [jnp.zeros((gs - reps, SSD_STATE), F32)]
        cb = _mxu(cm, cat([bm] * SSD_REP + pad_rows, axis=0), _NT)
        pad_rows = [] if reps == gs else [jnp.zeros((gs - reps, SSD_GW), F32)]
        xbd = jnp.where(diag, cat([xdt] * SSD_REP + pad_rows, axis=0), 0.0)
        st = ST[g]
        y = _mxu(cb * seg[:, g * gs:(g + 1) * gs], xbd, _NN) + _mxu(cm, st, _NN) * jnp.exp(c_x[:, gl])
        y = (y + sx * dvec_ref[:, gl]) * _silu(sz_ref[:, gl])
        y = y * lax.rsqrt(jnp.mean(y * y, axis=-1, keepdims=True) + 1e-6) * nw_ref[:, gl]
        mix_ref[:, gl] = y
        c_end = c_x[L - 1:L, gl]
        ST[g] = st * jnp.exp(c_end) + _mxu(bm, xdt * jnp.exp(c_end - c_x[:, gl]), _TN)
    if nchunk > 1:
        tail[...] = xbc_ref[L - 8:L, :]

    @pl.when(ci == nchunk - 1)
    def _():
        conv_out_ref[...] = xbc_ref[t_real - (CONV_W - 1):t_real, :]
        for g in range(SSD_GROUPS):
            s_ref[g * SSD_REP:(g + 1) * SSD_REP] = ST[g].T.reshape(SSD_REP, SSD_HEAD_DIM, SSD_STATE)


def _ssd(h, mix_prev, grp, P, st_in, conv_in, s_prev, p):
    L = grp.chunk
    nchunk = grp.t_rows // L
    gs = _ssd_lanes(L)
    tail = (SSD_HEADS, SSD_HEAD_DIM, SSD_STATE)
    s_ins, s_specs, s_out, s_shape = _state_io(tail, p, st_in, None, grp.bsz)
    full = lambda shape: pl.BlockSpec(shape, lambda b, c: (0,) * len(shape))
    heads = jnp.arange(SSD_HEADS)[:, None]
    lane_p = jnp.arange(SSD_WIDTH)[None, :]
    ep = (lane_p // SSD_HEAD_DIM == heads).astype(F32)
    lane_s = jnp.arange(SSD_GROUPS * gs)[None, :]
    in_grp = lane_s % gs
    es = ((in_grp < SSD_REP * L) & ((lane_s // gs) * SSD_REP + in_grp // L == heads)).astype(F32)
    row = lambda a: a.reshape(1, -1)
    ins = [h, h, h, P['ssd_conv_w'], row(P['ssd_conv_b']), row(P['ssd_dt_bias']), row(P['ssd_a_log']),
           row(jnp.repeat(P['ssd_d'], SSD_HEAD_DIM)), row(P['ssd_norm']), ep, es]
    specs = [grp.spec(L, SSD_WIDTH, EVEN_OFF[5]), grp.spec(L, SSD_CONV_DIM, EVEN_OFF[6]), grp.spec(L, 128, EVEN_OFF[3]),
             full((CONV_W, SSD_CONV_DIM)), full((1, SSD_CONV_DIM)), full((1, SSD_HEADS)), full((1, SSD_HEADS)),
             full((1, SSD_WIDTH)), full((1, SSD_WIDTH)), full(ep.shape), full(es.shape)]
    if st_in is not None:
        ins += [conv_in] + s_ins
        specs += [pl.BlockSpec((None, None, CONV_W - 1, SSD_CONV_DIM), lambda b, c: (p, b, 0, 0))] + s_specs
    any_spec = pl.BlockSpec(memory_space=pl.ANY)
    alias = {}
    for prev, out_idx in ((mix_prev, 0), (s_prev, 1)):
        if prev is not None:
            alias[len(ins)] = out_idx
            ins.append(prev)
            specs.append(any_spec)
    n_prev = (mix_prev is not None) + (s_prev is not None)
    return pl.pallas_call(
        functools.partial(_ssd_kernel, L=L, nchunk=nchunk, t_real=grp.t_real or L,
                          has_state=st_in is not None, has_prev=n_prev),
        grid=(grp.bsz, nchunk),
        in_specs=specs,
        out_specs=[grp.spec(L, SSD_WIDTH, 0), s_out,
                   pl.BlockSpec((None, CONV_W - 1, SSD_CONV_DIM), lambda b, c: (b, 0, 0))],
        out_shape=[jax.ShapeDtypeStruct((h.shape[0], MIX_WIDTH), F32), s_shape,
                   jax.ShapeDtypeStruct((grp.bsz, CONV_W - 1, SSD_CONV_DIM), F32)],
        scratch_shapes=[pltpu.VMEM((SSD_GROUPS, SSD_STATE, SSD_GW), F32), pltpu.VMEM((8, SSD_CONV_DIM), F32)],
        input_output_aliases=alias,
        compiler_params=_cparams(("parallel", "arbitrary")),
        name="ssd",
    )(*ins)


def _gdn_kernel(qkv_ref, cz_ref, sm_ref, cw_ref, alog_ref, dtb_ref, nw_ref, *rest, L, nchunk, rounds, t_real,
                has_state, has_prev):
    n_opt = 2 * has_state + has_prev
    mix_ref, s_ref, conv_out_ref, S, tail = rest[n_opt:]
    ci = pl.program_id(1)

    @pl.when(ci == 0)
    def _():
        tail[...] = jnp.zeros_like(tail)
        if has_state:
            tail[8 - (CONV_W - 1):8, :] = rest[0][...]
            S[...] = rest[1][...]
        else:
            S[...] = jnp.zeros_like(S)

    n2 = 2 * L
    d = GDN_HEAD_DIM
    cat = jnp.concatenate
    row8 = lax.broadcasted_iota(jnp.int32, (8, d), 0)

    def conv_tile(c0):
        u = qkv_ref[:, c0:c0 + d]
        p8 = tail[:, c0:c0 + d]
        w = cw_ref[:, c0:c0 + d]
        acc = u * w[CONV_W - 1:CONV_W]
        for j in range(1, CONV_W):
            acc = acc + _shifted(u, p8, j, row8) * w[CONV_W - 1 - j:CONV_W - j]
        return _silu(acc)

    def l2n(x):
        return x * lax.rsqrt(jnp.sum(x * x, axis=-1, keepdims=True) + 1e-6)

    row = lax.broadcasted_iota(jnp.int32, (L, L), 0)
    col = lax.broadcasted_iota(jnp.int32, (L, L), 1)
    tril = (col <= row).astype(F32)
    r2 = lax.broadcasted_iota(jnp.int32, (n2, n2), 0)
    c2 = lax.broadcasted_iota(jnp.int32, (n2, n2), 1)
    same = (r2 >= L) == (c2 >= L)
    strict = same & (c2 < r2)
    incl = same & (c2 <= r2)
    upper = same & (r2 <= c2)
    eye = (r2 == c2).astype(F32)
    zl = jnp.zeros((L, d), F32)
    sm = sm_ref[...]
    beta_all = _sigmoid(sm[:, :GDN_HEADS])
    g_all = -jnp.exp(alog_ref[...]) * _softplus(sm[:, GDN_HEADS:2 * GDN_HEADS] + dtb_ref[...])
    if t_real < L:
        valid = lax.broadcasted_iota(jnp.int32, (L, GDN_HEADS), 0) < t_real
        beta_all = jnp.where(valid, beta_all, 0.0)
        g_all = jnp.where(valid, g_all, 0.0)
    c_all = _mxu_f32(tril, g_all, _NN)
    pairs = range(GDN_HEADS // 2)
    nn, tt, qk, kq, kdec, ec, bcol, elast, vst = [], [], [], [], [], [], [], [], []
    for j in pairs:
        h0, h1 = 2 * j, 2 * j + 1
        stack_col = lambda a: cat([a[:, h0:h0 + 1], a[:, h1:h1 + 1]], axis=0)
        c_col = stack_col(c_all)
        beta_col = stack_col(beta_all)
        c_row = jnp.sum(jnp.where(upper, stack_col(g_all), 0.0), axis=0, keepdims=True)
        decay = jnp.exp(jnp.where(incl, c_col - c_row, -jnp.inf))
        last = lambda rows: cat([jnp.broadcast_to(c_all[L - 1:L, h0:h0 + 1], (rows, 1)),
                                 jnp.broadcast_to(c_all[L - 1:L, h1:h1 + 1], (rows, 1))], axis=0)
        q0, q1 = (l2n(conv_tile(h * d)) * d ** -0.5 for h in (h0, h1))
        k0, k1 = (l2n(conv_tile(GDN_WIDTH + h * d)) for h in (h0, h1))
        vst.append(cat([conv_tile(2 * GDN_WIDTH + h0 * d), conv_tile(2 * GDN_WIDTH + h1 * d)], axis=0))
        k_st = cat([cat([k0, zl], axis=1), cat([zl, k1], axis=1)], axis=0)
        q_st = cat([cat([q0, zl], axis=1), cat([zl, q1], axis=1)], axis=0)
        both = cat([k_st, q_st], axis=0)
        full = _mxu(both, k_st, _NT)
        a = jnp.where(strict, full[:n2] * decay * beta_col, 0.0)
        nn.append(-a)
        tt.append(eye - a)
        qk.append(full[n2:] * decay)
        kq.append(both)
        kdec.append(k_st * jnp.exp(last(L) - c_col))
        ec.append(jnp.exp(c_col))
        bcol.append(beta_col)
        elast.append(jnp.exp(last(d)))
    tt = _neumann_inverse(nn, tt, n2, rounds)
    s_old, ksqs, u = [], [], []
    for j in pairs:
        s_old.append(cat([S[2 * j], S[2 * j + 1]], axis=0))
        ksqs.append(_mxu(kq[j], s_old[j], _NN))
    for j in pairs:
        u.append(_mxu(tt[j], bcol[j] * (vst[j] - ec[j] * ksqs[j][:n2]), _NN))
    nw = nw_ref[...]
    for j in pairs:
        o = ec[j] * ksqs[j][n2:] + _mxu(qk[j], u[j], _NN)
        for hh, oh in ((2 * j, o[:L]), (2 * j + 1, o[L:])):
            cols = slice(hh * d, (hh + 1) * d)
            y = oh * lax.rsqrt(jnp.mean(oh * oh, axis=-1, keepdims=True) + 1e-6) * nw
            mix_ref[:, cols] = y * _silu(cz_ref[:, cols])
        new = s_old[j] * elast[j] + _mxu(kdec[j], u[j], _TN)
        S[2 * j] = new[:d]
        S[2 * j + 1] = new[d:]
    if nchunk > 1:
        tail[...] = qkv_ref[L - 8:L, :]

    @pl.when(ci == nchunk - 1)
    def _():
        s_ref[...] = S[...]
        conv_out_ref[...] = qkv_ref[t_real - (CONV_W - 1):t_real, :]


def _gdn(h, mix_prev, grp, P, st_in, conv_in, s_prev, p):
    L = grp.chunk
    nchunk = grp.t_rows // L
    tail = (GDN_HEADS, GDN_HEAD_DIM, GDN_HEAD_DIM)
    s_ins, s_specs, s_out, s_shape = _state_io(tail, p, st_in, None, grp.bsz)
    cw = 3 * GDN_WIDTH
    full = lambda shape: pl.BlockSpec(shape, lambda b, c: (0,) * len(shape))
    ins = [h, h, h, P['gdn_conv_w'], P['gdn_a_log'].reshape(1, GDN_HEADS), P['gdn_dt_bias'].reshape(1, GDN_HEADS),
           P['gdn_norm'].reshape(1, GDN_HEAD_DIM)]
    specs = [grp.spec(L, cw, ODD_OFF[0]), grp.spec(L, GDN_WIDTH, ODD_OFF[1]), grp.spec(L, 128, ODD_OFF[2]),
             full((CONV_W, cw)), full((1, GDN_HEADS)), full((1, GDN_HEADS)), full((1, GDN_HEAD_DIM))]
    if st_in is not None:
        ins += [conv_in] + s_ins
        specs += [pl.BlockSpec((None, None, CONV_W - 1, cw), lambda b, c: (p, b, 0, 0))] + s_specs
    any_spec = pl.BlockSpec(memory_space=pl.ANY)
    alias = {}
    for prev, out_idx in ((mix_prev, 0), (s_prev, 1)):
        if prev is not None:
            alias[len(ins)] = out_idx
            ins.append(prev)
            specs.append(any_spec)
    n_prev = (mix_prev is not None) + (s_prev is not None)
    return pl.pallas_call(
        functools.partial(_gdn_kernel, L=L, nchunk=nchunk, rounds=int(math.log2(L)), t_real=grp.t_real or L,
                          has_state=st_in is not None, has_prev=n_prev),
        grid=(grp.bsz, nchunk),
        in_specs=specs,
        out_specs=[grp.spec(L, GDN_WIDTH, 0), s_out,
                   pl.BlockSpec((None, CONV_W - 1, cw), lambda b, c: (b, 0, 0))],
        out_shape=[jax.ShapeDtypeStruct((h.shape[0], MIX_WIDTH), F32), s_shape,
                   jax.ShapeDtypeStruct((grp.bsz, CONV_W - 1, cw), F32)],
        scratch_shapes=[pltpu.VMEM(tail, F32), pltpu.VMEM((8, cw), F32)],
        input_output_aliases=alias,
        compiler_params=_cparams(("parallel", "arbitrary")),
        name="gdn",
    )(*ins)


RWKV_PAIRS = RWKV_HEADS // 2
RWKV_PW = 2 * RWKV_HEAD_DIM


def _rwkv_kernel(r_ref, k_ref, v_ref, xwa_ref, gate_ref, mu_ref, w0_ref, w2_ref, a0_ref, a2_ref, kkp_ref, ka_ref,
                 rk_ref, lng_ref, lnb_ref, *rest, L, nchunk, rounds, t_real, has_state, has_prev):
    n_opt = 2 * has_state + has_prev
    mix_ref, s_ref, shift_out_ref, S, last = rest[n_opt:]
    ci = pl.program_id(1)
    n = RWKV_HEAD_DIM
    w3 = 3 * RWKV_WIDTH
    cat = jnp.concatenate
    pairs = range(RWKV_PAIRS)

    @pl.when(ci == 0)
    def _():
        last[...] = jnp.zeros_like(last)
        if has_state:
            last[7:8, :] = rest[0][...]
            zn = jnp.zeros((n, n), F32)
            for j in pairs:
                S[j] = cat([cat([rest[1][2 * j], zn], axis=1), cat([zn, rest[1][2 * j + 1]], axis=1)], axis=0)
        else:
            S[...] = jnp.zeros_like(S)

    n2 = 2 * L
    row = lax.broadcasted_iota(jnp.int32, (L, L), 0)
    col = lax.broadcasted_iota(jnp.int32, (L, L), 1)
    tril = (col <= row).astype(F32)
    r2 = lax.broadcasted_iota(jnp.int32, (n2, n2), 0)
    c2 = lax.broadcasted_iota(jnp.int32, (n2, n2), 1)
    same = (r2 >= L) == (c2 >= L)
    strict = same & (c2 < r2)
    incl = same & (c2 <= r2)
    eye = (r2 == c2).astype(F32)
    lane = lax.broadcasted_iota(jnp.int32, (L, RWKV_PW), 1)
    lo = lane < n

    def stack(x):
        return cat([jnp.where(lo, x, 0.0), jnp.where(lo, 0.0, x)], axis=0)

    row8 = lax.broadcasted_iota(jnp.int32, (8, RWKV_PW), 0)
    valid = lax.broadcasted_iota(jnp.int32, (L, RWKV_PW), 0) < t_real

    def seg_sum(x):
        s_lo = jnp.sum(jnp.where(lo, x, 0.0), axis=-1, keepdims=True)
        s_hi = jnp.sum(jnp.where(lo, 0.0, x), axis=-1, keepdims=True)
        return jnp.where(lo, s_lo, s_hi)

    def shift_mix(ref, c_src, c_all):
        x = ref[:, c_src:c_src + RWKV_PW]
        prev = _shifted(x, last[:, c_all:c_all + RWKV_PW], 1, row8)
        return x + (prev - x) * mu_ref[:, c_all:c_all + RWKV_PW]

    xwa = shift_mix(xwa_ref, 0, w3)
    lr_w = _mxu_f32(jnp.tanh(xwa), w2_ref[...], _NN)
    lr_a = _mxu_f32(xwa, a2_ref[...], _NN)
    a_ak, a_rk, a_rb, nn, tt, sread, kdbd, egl, vs, bonus = [], [], [], [], [], [], [], [], [], []
    for j in pairs:
        sl = slice(j * RWKV_PW, (j + 1) * RWKV_PW)
        r = shift_mix(r_ref, j * RWKV_PW, j * RWKV_PW)
        k = shift_mix(k_ref, j * RWKV_PW, RWKV_WIDTH + j * RWKV_PW)
        v = shift_mix(v_ref, j * RWKV_PW, 2 * RWKV_WIDTH + j * RWKV_PW)
        w_log = -_softplus(-(w0_ref[:, sl] + lr_w[:, sl])) - 0.5
        lw = -jnp.exp(w_log)
        a7 = _sigmoid(a0_ref[:, sl] + lr_a[:, sl])
        kx = k * kkp_ref[:, sl]
        kk = kx * lax.rsqrt(seg_sum(kx * kx) + 1e-6)
        k = k * (1.0 + (a7 - 1.0) * ka_ref[:, sl])
        if t_real < L:
            lw, kk, k, v = (jnp.where(valid, a, 0.0) for a in (lw, kk, k, v))
        b = kk * a7
        bonus.append(seg_sum(r * k * rk_ref[:, sl]) * v)
        vs.append(stack(v))
        g = _mxu_f32(tril, lw, _NN)
        gp = g - lw
        gm = g[L // 2 - 1:L // 2, :]
        gl = g[L - 1:L, :]
        e_neg = jnp.exp(gm - g)
        lhs = cat([stack(kk * jnp.exp(gp - gm)), stack(r * jnp.exp(g - gm))], axis=0)
        rhs = cat([stack(b * e_neg), stack(k * e_neg)], axis=0)
        full = _mxu(lhs, rhs, _NT)
        a_ab = jnp.where(strict, full[:n2, :n2], 0.0)
        a_ak.append(jnp.where(strict, full[:n2, n2:], 0.0))
        a_rb.append(jnp.where(incl, full[n2:, :n2], 0.0))
        a_rk.append(jnp.where(incl, full[n2:, n2:], 0.0))
        nn.append(-a_ab)
        tt.append(eye - a_ab)
        sread.append(cat([stack(kk * jnp.exp(gp)), stack(r * jnp.exp(g))], axis=0))
        dec = jnp.exp(gl - g)
        kdbd.append(cat([stack(k * dec), stack(-b * dec)], axis=0))
        egl.append(jnp.exp(gl))
    tt = _neumann_inverse(nn, tt, n2, rounds)
    s_old, sr, av, u = [], [], [], []
    for j in pairs:
        s_old.append(S[j])
        sr.append(_mxu(sread[j], s_old[j], _NT))
        av.append(_mxu(cat([a_ak[j], a_rk[j]], axis=0), vs[j], _NN))
    for j in pairs:
        u.append(_mxu(tt[j], sr[j][:n2] + av[j][:n2], _NN))
    for j in pairs:
        sl = slice(j * RWKV_PW, (j + 1) * RWKV_PW)
        o = sr[j][n2:] + av[j][n2:] - _mxu(a_rb[j], u[j], _NN)
        o = o[:L] + o[L:]
        oc = o - seg_sum(o) * (1.0 / n)
        gn = oc * lax.rsqrt(seg_sum(oc * oc) * (1.0 / n) + RWKV_GN_EPS)
        y = gn * lng_ref[:, sl] + lnb_ref[:, sl] + bonus[j]
        mix_ref[:, sl] = y * _silu(gate_ref[:, sl])
        S[j] = s_old[j] * egl[j] + _mxu(cat([vs[j], u[j]], axis=0), kdbd[j], _TN)
    pieces = ((r_ref, 0, RWKV_WIDTH), (k_ref, RWKV_WIDTH, RWKV_WIDTH), (v_ref, 2 * RWKV_WIDTH, RWKV_WIDTH),
              (xwa_ref, w3, RWKV_PW))
    if nchunk > 1:
        for ref, c0, wd in pieces:
            last[:, c0:c0 + wd] = ref[L - 8:L, :]

    @pl.when(ci == nchunk - 1)
    def _():
        for ref, c0, wd in pieces:
            shift_out_ref[:, c0:c0 + wd] = ref[t_real - 1:t_real, :]
        for j in pairs:
            s_ref[2 * j] = S[j][:n, :n]
            s_ref[2 * j + 1] = S[j][n:, n:]


def _rwkv(h, mix_prev, grp, P, st_in, shift_in, s_prev, p):
    L = grp.chunk
    nchunk = grp.t_rows // L
    tail = (RWKV_HEADS, RWKV_HEAD_DIM, RWKV_HEAD_DIM)
    s_ins, s_specs, s_out, s_shape = _state_io(tail, p, st_in, None, grp.bsz)
    full = lambda shape: pl.BlockSpec(shape, lambda b, c: (0,) * len(shape))
    row = lambda a: a.reshape(1, -1)
    zr = jnp.zeros((RWKV_HEAD_DIM, RWKV_WIDTH), F32)
    w2 = jnp.concatenate([P['rwkv_w2'], zr], axis=0)
    a2 = jnp.concatenate([zr, P['rwkv_a2']], axis=0)
    off = ODD_OFF[4]
    ins = [h, h, h, h, h, row(P['rwkv_mu']), row(P['rwkv_w0']), w2, row(P['rwkv_a0']), a2, row(P['rwkv_kk']),
           row(P['rwkv_ka']), row(P['rwkv_rk']), row(P['rwkv_ln_g']), row(P['rwkv_ln_b'])]
    vec = full((1, RWKV_WIDTH))
    specs = [grp.spec(L, RWKV_WIDTH, off), grp.spec(L, RWKV_WIDTH, off + RWKV_WIDTH),
             grp.spec(L, RWKV_WIDTH, off + 2 * RWKV_WIDTH), grp.spec(L, RWKV_PW, off + 3 * RWKV_WIDTH),
             grp.spec(L, RWKV_WIDTH, ODD_OFF[5]), full((1, RWKV_SHIFT_DIM)), vec, full((RWKV_PW, RWKV_WIDTH)), vec,
             full((RWKV_PW, RWKV_WIDTH)), vec, vec, vec, vec, vec]
    if st_in is not None:
        ins += [shift_in] + s_ins
        specs += [pl.BlockSpec((None, None, 1, RWKV_SHIFT_DIM), lambda b, c: (p, b, 0, 0))] + s_specs
    any_spec = pl.BlockSpec(memory_space=pl.ANY)
    alias = {}
    for prev, out_idx in ((mix_prev, 0), (s_prev, 1)):
        if prev is not None:
            alias[len(ins)] = out_idx
            ins.append(prev)
            specs.append(any_spec)
    n_prev = (mix_prev is not None) + (s_prev is not None)
    return pl.pallas_call(
        functools.partial(_rwkv_kernel, L=L, nchunk=nchunk, rounds=int(math.log2(L)), t_real=grp.t_real or L,
                          has_state=st_in is not None, has_prev=n_prev),
        grid=(grp.bsz, nchunk),
        in_specs=specs,
        out_specs=[grp.spec(L, RWKV_WIDTH, GDN_WIDTH), s_out,
                   pl.BlockSpec((None, 1, RWKV_SHIFT_DIM), lambda b, c: (b, 0, 0))],
        out_shape=[jax.ShapeDtypeStruct((h.shape[0], MIX_WIDTH), F32), s_shape,
                   jax.ShapeDtypeStruct((grp.bsz, 1, RWKV_SHIFT_DIM), F32)],
        scratch_shapes=[pltpu.VMEM((RWKV_PAIRS, RWKV_PW, RWKV_PW), F32), pltpu.VMEM((8, RWKV_SHIFT_DIM), F32)],
        input_output_aliases=alias,
        compiler_params=_cparams(("parallel", "arbitrary")),
        name="rwkv7",
    )(*ins)


def _pad_t(a, t_to):
    t = a.shape[1]
    if t == t_to:
        return a
    return jnp.pad(a, [(0, 0), (0, t_to - t)] + [(0, 0)] * (a.ndim - 2))


def _even_mix(h, mix, grp, mem_k, mem_v, layer, st_in, st_prev, conv_in, P):
    p = layer // 2
    mix, s_ssd, s_conv = _ssd(h, mix, grp, P, st_in['ssd'], conv_in, st_prev['ssd'], p)
    gla_l, gla_tb = (GLA_CHUNK, 256) if grp.t_real is None else (grp.chunk, grp.chunk)
    mix, s_gla = _gla(h, mix, grp, P, st_in['gla'], st_prev['gla'], p, gla_l, gla_tb)
    mix = _mem_attention(h, mix, grp, EVEN_OFF[8], EVEN_OFF[9], mem_k, mem_v, layer, min(512, grp.t_rows))
    return mix, dict(gla=s_gla, ssd=s_ssd), s_conv


def _odd_mix(h, mix, grp, mem_k, mem_v, layer, st_in, st_prev, conv_in, shift_in, P):
    p = layer // 2
    mix, s_gdn, s_conv = _gdn(h, mix, grp, P, st_in['gdn'], conv_in, st_prev['gdn'], p)
    mix, s_rwkv, s_shift = _rwkv(h, mix, grp, P, st_in['rwkv'], shift_in, st_prev['rwkv'], p)
    mix = _mem_attention(h, mix, grp, ODD_OFF[6], ODD_OFF[7], mem_k, mem_v, layer, min(512, grp.t_rows))
    return mix, dict(gdn=s_gdn, rwkv=s_rwkv), s_conv, s_shift.reshape(grp.bsz, RWKV_SHIFT_DIM)


def kernel(x_prompt, x_sample, mem_prompt, cache_mem_k, cache_mem_v, state_gla, state_ssd, state_ssd_conv, state_gdn, state_gdn_conv, state_rwkv, state_rwkv_shift, mem_w_kv, ev_w_in, ev_gla_w2, ev_gla_b, ev_gla_norm, ev_ssd_conv_w, ev_ssd_conv_b, ev_ssd_dt_bias, ev_ssd_a_log, ev_ssd_d, ev_ssd_norm, ev_w_out, ev_ln_g, ev_ln_b, od_w_in, od_gdn_conv_w, od_gdn_dt_bias, od_gdn_a_log, od_gdn_norm, od_rwkv_mu, od_rwkv_w0, od_rwkv_w2, od_rwkv_a0, od_rwkv_a2, od_rwkv_kk, od_rwkv_ka, od_rwkv_rk, od_rwkv_ln_g, od_rwkv_ln_b, od_w_out, od_ln_g, od_ln_b):
    ev = dict(w_in=ev_w_in, gla_w2=ev_gla_w2, gla_b=ev_gla_b, gla_norm=ev_gla_norm,
              ssd_conv_w=ev_ssd_conv_w, ssd_conv_b=ev_ssd_conv_b, ssd_dt_bias=ev_ssd_dt_bias,
              ssd_a_log=ev_ssd_a_log, ssd_d=ev_ssd_d, ssd_norm=ev_ssd_norm,
              w_out=ev_w_out, ln_g=ev_ln_g, ln_b=ev_ln_b)
    od = dict(w_in=od_w_in, gdn_conv_w=od_gdn_conv_w, gdn_dt_bias=od_gdn_dt_bias, gdn_a_log=od_gdn_a_log,
              gdn_norm=od_gdn_norm, rwkv_mu=od_rwkv_mu, rwkv_w0=od_rwkv_w0, rwkv_w2=od_rwkv_w2,
              rwkv_a0=od_rwkv_a0, rwkv_a2=od_rwkv_a2, rwkv_kk=od_rwkv_kk, rwkv_ka=od_rwkv_ka,
              rwkv_rk=od_rwkv_rk, rwkv_ln_g=od_rwkv_ln_g, rwkv_ln_b=od_rwkv_ln_b,
              w_out=od_w_out, ln_g=od_ln_g, ln_b=od_ln_b)
    bp, tp, _ = x_prompt.shape
    bs, ts, _ = x_sample.shape
    mp, ms = bp * tp, bs * SMALL_T
    grp_p = _Group(bp, tp, None, 0, GDN_CHUNK)
    grp_s = _Group(bs, SMALL_T, ts, mp, SMALL_T)

    w_kv = jnp.moveaxis(mem_w_kv, 0, 1).reshape(D_MODEL, DEPTH * 2 * MEM_WIDTH).astype(BF16)
    kv = _matmul(mem_prompt.reshape(bp * MEM_LEN, D_MODEL).astype(BF16), w_kv, 512, 1024)
    kv6 = kv.reshape(bp, MEM_LEN, DEPTH, 2, MEM_HEADS, MEM_HEAD_DIM)
    mem_k_p = jnp.moveaxis(kv6[:, :, :, 0], 2, 0)
    mem_v_p = jnp.moveaxis(kv6[:, :, :, 1], 2, 0)
    mk_s, mv_s = _cache_view(cache_mem_k), _cache_view(cache_mem_v)

    x = jnp.concatenate([x_prompt.reshape(mp, D_MODEL),
                         _pad_t(x_sample, SMALL_T).reshape(ms, D_MODEL)], axis=0)
    x_bf = x.astype(BF16)
    zp = lambda shape: jnp.zeros(shape, F32)
    names = ('gla', 'ssd', 'gdn', 'rwkv')
    none = {n: None for n in names}
    in_s = dict(gla=state_gla, ssd=state_ssd, gdn=state_gdn, rwkv=state_rwkv)
    shift_s = state_rwkv_shift.reshape(N_PAIRS, bs, 1, RWKV_SHIFT_DIM)
    out_p, out_s = dict(none), dict(none)
    small_p = {n: [] for n in ('ssd_conv', 'gdn_conv', 'rwkv_shift')}
    small_s = {n: [] for n in small_p}
    tm = 1024
    for layer in range(DEPTH):
        p = layer // 2
        if layer % 2 == 0:
            P = {n: w[p] for n, w in ev.items()}
            h = _matmul(x_bf, _pack_w_in(P['w_in'], EVEN_SIZES, EVEN_ORDER, EVEN_N), tm, PROJ_TN)
            mix, new, c1 = _even_mix(h, None, grp_p, kv, kv, layer, none, out_p, None, P)
            out_p.update(new)
            mix, new, c2 = _even_mix(h, mix, grp_s, mk_s, mv_s, layer, in_s, out_s, state_ssd_conv, P)
            out_s.update(new)
            small_p['ssd_conv'].append(c1)
            small_s['ssd_conv'].append(c2)
            w_out = jnp.concatenate([P['w_out'][GLA_WIDTH:GLA_WIDTH + SSD_WIDTH], P['w_out'][:GLA_WIDTH],
                                     P['w_out'][GLA_WIDTH + SSD_WIDTH:]], axis=0)
        else:
            P = {n: w[p] for n, w in od.items()}
            h = _matmul(x_bf, _pack_w_in(P['w_in'], ODD_SIZES, ODD_ORDER, ODD_N), tm, PROJ_TN)
            mix, new, c1, h1 = _odd_mix(h, None, grp_p, kv, kv, layer, none, out_p, None, None, P)
            out_p.update(new)
            mix, new, c2, h2 = _odd_mix(h, mix, grp_s, mk_s, mv_s, layer, in_s, out_s, state_gdn_conv, shift_s, P)
            out_s.update(new)
            small_p['gdn_conv'].append(c1)
            small_s['gdn_conv'].append(c2)
            small_p['rwkv_shift'].append(h1)
            small_s['rwkv_shift'].append(h2)
            w_out = P['w_out']
        x, x_bf = _out_ln(mix, w_out.astype(BF16), x, P['ln_g'], P['ln_b'])

    y_prompt = x[:mp].reshape(bp, tp, D_MODEL)
    y_sample = x[mp:].reshape(bs, SMALL_T, D_MODEL)[:, :ts]
    st = lambda d, n: jnp.stack(d[n])
    return (y_prompt, y_sample, mem_k_p, mem_v_p,
            out_p['gla'], out_s['gla'], out_p['ssd'], out_s['ssd'],
            st(small_p, 'ssd_conv'), st(small_s, 'ssd_conv'), out_p['gdn'], out_s['gdn'],
            st(small_p, 'gdn_conv'), st(small_s, 'gdn_conv'), out_p['rwkv'], out_s['rwkv'],
            st(small_p, 'rwkv_shift'), st(small_s, 'rwkv_shift'))
```

```python
import functools
import math

import numpy as np
import jax
import jax.numpy as jnp
from jax import lax
from jax.experimental import pallas as pl
from jax.experimental.pallas import tpu as pltpu

F32 = jnp.float32
BF16 = jnp.bfloat16
HI = lax.Precision.HIGHEST

D_MODEL = 2048
DEPTH = 4
N_PAIRS = DEPTH // 2
CONV_W = 4
MEM_LEN = 256
MEM_HEADS = 4
MEM_HEAD_DIM = 256
MEM_WIDTH = 1024
GLA_HEADS = 4
GLA_DK = 128
GLA_DV = 256
GLA_QK = 512
GLA_WIDTH = 1024
GLA_RANK = 16
GLA_TAU = 16.0
SSD_WIDTH = 2048
SSD_HEAD_DIM = 64
SSD_HEADS = 32
SSD_GROUPS = 4
SSD_REP = 8
SSD_STATE = 128
SSD_CONV_DIM = SSD_WIDTH + 2 * SSD_GROUPS * SSD_STATE
GDN_WIDTH = 2048
GDN_HEAD_DIM = 128
GDN_HEADS = 16
RWKV_WIDTH = 1024
RWKV_HEAD_DIM = 64
RWKV_HEADS = 16
RWKV_W_RANK = 64
RWKV_A_RANK = 64
RWKV_SHIFT_DIM = 3 * RWKV_WIDTH + RWKV_W_RANK + RWKV_A_RANK
RWKV_GN_EPS = 64e-5
EVEN_SIZES = (GLA_QK, GLA_QK, GLA_WIDTH, GLA_RANK, GLA_WIDTH, SSD_WIDTH, SSD_CONV_DIM, SSD_HEADS,
              MEM_WIDTH, MEM_WIDTH)
ODD_SIZES = (3 * GDN_WIDTH, GDN_WIDTH, GDN_HEADS, GDN_HEADS, RWKV_SHIFT_DIM, RWKV_WIDTH, MEM_WIDTH, MEM_WIDTH)
MIX_WIDTH = 4096
DEEPNORM_ALPHA = (2 * DEPTH) ** 0.25

EVEN_ORDER = (5, 2, 4, 8, 9, 6, 0, 1, 3, 7)
ODD_ORDER = (0, 1, 5, 6, 7, 4, 2, 3)
PROJ_TN = 768
VMEM_LIMIT = 48 * 1024 * 1024

GLA_CHUNK = 16
SSD_CHUNK = 64
GDN_CHUNK = 64
RWKV_CHUNK = 64
SMALL_T = 8
SAMPLE_NSEQ = 4

_NN = ((1,), (0,))
_NT = ((1,), (1,))
_TN = ((0,), (0,))


def _packed_layout(sizes, order):
    offs, o = {}, 0
    for i in order:
        offs[i] = o
        o += sizes[i]
    total = -(-o // PROJ_TN) * PROJ_TN
    return offs, total


EVEN_OFF, EVEN_N = _packed_layout(EVEN_SIZES, EVEN_ORDER)
ODD_OFF, ODD_N = _packed_layout(ODD_SIZES, ODD_ORDER)


def _pack_w_in(w, sizes, order, total):
    segs = jnp.split(w, np.cumsum(sizes)[:-1].tolist(), axis=-1)
    parts = [segs[i] for i in order]
    used = sum(sizes)
    if total > used:
        parts.append(jnp.zeros((w.shape[0], total - used), w.dtype))
    return jnp.concatenate(parts, axis=-1).astype(BF16)


def _seg(h, offs, sizes, i):
    return h[..., offs[i]:offs[i] + sizes[i]]


def _cparams(sem):
    return pltpu.CompilerParams(dimension_semantics=sem, vmem_limit_bytes=VMEM_LIMIT)


def _mxu(a, b, dims):
    return lax.dot_general(a.astype(BF16), b.astype(BF16), (dims, ((), ())), preferred_element_type=F32)


def _mxu_f32(a, b, dims):
    return lax.dot_general(a, b, (dims, ((), ())), precision=HI, preferred_element_type=F32)


def _sigmoid(x):
    return 1.0 / (1.0 + jnp.exp(-x))


def _silu(x):
    return x * _sigmoid(x)


def _softplus(x):
    return jnp.maximum(x, 0.0) + jnp.log(1.0 + jnp.exp(-jnp.abs(x)))


def _shifted(u, prev8, j, row8):
    ru = pltpu.roll(u, j, 0)
    top = jnp.where(row8 < j, pltpu.roll(prev8, j, 0), ru[:8])
    return top if u.shape[0] == 8 else jnp.concatenate([top, ru[8:]], axis=0)


class _Group:
    def __init__(self, bsz, t_rows, t_real, row0, chunk, nseq=1):
        assert nseq == 1 or t_rows == chunk
        self.bsz, self.t_rows, self.t_real, self.row0, self.chunk, self.nseq = bsz, t_rows, t_real, row0, chunk, nseq
        self.steps = bsz // nseq

    def spec(self, rows, width, off):
        rows = rows * self.nseq
        assert off % width == 0 and self.row0 % rows == 0 and (self.t_rows * self.nseq) % rows == 0
        base, per, cb = self.row0 // rows, self.t_rows * self.nseq // rows, off // width
        return pl.BlockSpec((rows, width), lambda b, i: (base + b * per + i, cb))


def _alias_last(n_inputs, has_prev, out_index=0):
    return {n_inputs - 1: out_index} if has_prev else {}


def _neumann_inverse(nn, tt, n, rounds):
    idx = range(len(nn))
    if rounds >= 2:
        for j in idx:
            nn[j] = _mxu(nn[j], nn[j], _NN)
        for _ in range(rounds - 2):
            for j in idx:
                both = _mxu(nn[j], jnp.concatenate([tt[j], nn[j]], axis=1), _NN)
                tt[j] = tt[j] + both[:, :n]
                nn[j] = both[:, n:]
        for j in idx:
            tt[j] = tt[j] + _mxu(nn[j], tt[j], _NN)
    return tt


def _state_io(tail, p, s_in, grp):
    zeros = (0,) * len(tail)
    spec = pl.BlockSpec((None, grp.nseq) + tail, lambda *g: (p, g[0]) + zeros)
    ins, specs = ([s_in], [spec]) if s_in is not None else ([], [])
    shape = jax.ShapeDtypeStruct((N_PAIRS, grp.bsz) + tail, F32)
    return ins, specs, spec, shape


def _mm_kernel(x_ref, w_ref, o_ref, *, precision):
    o_ref[...] = jnp.dot(x_ref[...], w_ref[...], preferred_element_type=F32, precision=precision)


def _matmul(x, w, tm, tn, precision=None):
    m, k = x.shape
    n = w.shape[1]
    assert m % tm == 0 and n % tn == 0
    return pl.pallas_call(
        functools.partial(_mm_kernel, precision=precision),
        grid=(n // tn, m // tm),
        in_specs=[pl.BlockSpec((tm, k), lambda j, i: (i, 0)),
                  pl.BlockSpec((k, tn), lambda j, i: (0, j))],
        out_specs=pl.BlockSpec((tm, tn), lambda j, i: (i, j)),
        out_shape=jax.ShapeDtypeStruct((m, n), F32),
        compiler_params=_cparams(("parallel", "parallel")),
        name="matmul",
    )(x, w)


def _out_ln_kernel(mix_ref, w_ref, x_ref, g_ref, b_ref, y_ref, ybf_ref, acc, *, nk):
    kk = pl.program_id(1)

    @pl.when(kk == 0)
    def _():
        acc[...] = jnp.zeros_like(acc)

    acc[...] += jnp.dot(mix_ref[...].astype(BF16), w_ref[...], preferred_element_type=F32)

    @pl.when(kk == nk - 1)
    def _():
        z = DEEPNORM_ALPHA * x_ref[...] + acc[...]
        zc = z - jnp.mean(z, axis=-1, keepdims=True)
        var = jnp.mean(zc * zc, axis=-1, keepdims=True)
        y = zc * lax.rsqrt(var + 1e-5) * g_ref[...] + b_ref[...]
        y_ref[...] = y
        ybf_ref[...] = y.astype(BF16)


def _out_ln(mix, w, x, g, b, tm=512, tk=1024):
    m, k = mix.shape
    d = w.shape[1]
    nk = k // tk
    return pl.pallas_call(
        functools.partial(_out_ln_kernel, nk=nk),
        grid=(m // tm, nk),
        in_specs=[pl.BlockSpec((tm, tk), lambda i, j: (i, j)),
                  pl.BlockSpec((tk, d), lambda i, j: (j, 0)),
                  pl.BlockSpec((tm, d), lambda i, j: (i, 0)),
                  pl.BlockSpec((1, d), lambda i, j: (0, 0)),
                  pl.BlockSpec((1, d), lambda i, j: (0, 0))],
        out_specs=[pl.BlockSpec((tm, d), lambda i, j: (i, 0)),
                   pl.BlockSpec((tm, d), lambda i, j: (i, 0))],
        out_shape=[jax.ShapeDtypeStruct((m, d), F32), jax.ShapeDtypeStruct((m, d), BF16)],
        scratch_shapes=[pltpu.VMEM((tm, d), F32)],
        compiler_params=_cparams(("parallel", "arbitrary")),
        name="out_ln",
    )(mix, w, x, g.reshape(1, d), b.reshape(1, d))


def _mem_kernel(q_ref, gate_ref, k_ref, v_ref, *rest):
    o_ref = rest[-1]
    for h in range(MEM_HEADS):
        sl = slice(h * MEM_HEAD_DIM, (h + 1) * MEM_HEAD_DIM)
        k = k_ref[:, sl]
        v = v_ref[:, sl]
        s = _mxu(q_ref[:, sl], k, _NT) * MEM_HEAD_DIM ** -0.5
        p = jnp.exp(s - jnp.max(s, axis=-1, keepdims=True))
        p = p / jnp.sum(p, axis=-1, keepdims=True)
        o_ref[:, sl] = _mxu(p, v, _NN) * _silu(gate_ref[:, sl])


MEM_DT = MEM_HEAD_DIM // 128
MEM_ROWS = MEM_LEN * MEM_DT * MEM_HEADS


def _cache_view(c):
    d, b = c.shape[:2]
    c = c.reshape(d, b, MEM_LEN, MEM_HEADS, MEM_DT, 128)
    return jnp.transpose(c, (0, 1, 2, 4, 3, 5)).reshape(d, b, MEM_ROWS, 128)


def _mem_cache_kernel(q_ref, gate_ref, k_ref, v_ref, *rest):
    o_ref = rest[-1]
    nseq = k_ref.shape[0]
    t = q_ref.shape[0] // nseq
    grp = MEM_DT * MEM_HEADS
    col = lax.broadcasted_iota(jnp.int32, (MEM_HEADS * t, MEM_ROWS), 1) % grp
    head = lax.broadcasted_iota(jnp.int32, (MEM_HEADS * t, MEM_ROWS), 0) // t
    for i in range(nseq):
        rows = slice(i * t, (i + 1) * t)
        k = k_ref[i]
        v = v_ref[i]
        qs = [jnp.concatenate([q_ref[rows, h * MEM_HEAD_DIM + dt * 128:h * MEM_HEAD_DIM + (dt + 1) * 128]
                               for h in range(MEM_HEADS)], axis=0) for dt in range(MEM_DT)]
        s = _mxu(qs[0], k, _NT)
        for dt in range(1, MEM_DT):
            s = s + pltpu.roll(_mxu(qs[dt], k, _NT), MEM_ROWS - dt * MEM_HEADS, 1)
        s = jnp.where(col == head, s * MEM_HEAD_DIM ** -0.5, -jnp.inf)
        p = jnp.exp(s - jnp.max(s, axis=-1, keepdims=True))
        p = p / jnp.sum(p, axis=-1, keepdims=True)
        for dt in range(MEM_DT):
            o = _mxu(p if dt == 0 else pltpu.roll(p, dt * MEM_HEADS, 1), v, _NN)
            for h in range(MEM_HEADS):
                sl = slice(h * MEM_HEAD_DIM + dt * 128, h * MEM_HEAD_DIM + (dt + 1) * 128)
                o_ref[rows, sl] = o[h * t:(h + 1) * t] * _silu(gate_ref[rows, sl])


def _mem_attention(h, mix_prev, grp, q_off, gate_off, mem_k, mem_v, layer, tq):
    cached = mem_k.ndim == 4
    if cached:
        kv_specs = [pl.BlockSpec((None, grp.nseq, MEM_ROWS, 128), lambda b, i: (layer, b, 0, 0))] * 2
    else:
        kv_specs = [pl.BlockSpec((MEM_LEN, MEM_WIDTH), lambda b, i: (b, 2 * layer)),
                    pl.BlockSpec((MEM_LEN, MEM_WIDTH), lambda b, i: (b, 2 * layer + 1))]
    ins = [h, h, mem_k, mem_v] + ([] if mix_prev is None else [mix_prev])
    specs = [grp.spec(tq, MEM_WIDTH, q_off), grp.spec(tq, MEM_WIDTH, gate_off)] + kv_specs
    if mix_prev is not None:
        specs.append(pl.BlockSpec(memory_space=pl.ANY))
    return pl.pallas_call(
        _mem_cache_kernel if cached else _mem_kernel,
        grid=(grp.steps, grp.t_rows // tq),
        in_specs=specs,
        out_specs=grp.spec(tq, MEM_WIDTH, MIX_WIDTH - MEM_WIDTH),
        out_shape=jax.ShapeDtypeStruct((h.shape[0], MIX_WIDTH), F32),
        input_output_aliases=_alias_last(len(ins), mix_prev is not None),
        compiler_params=_cparams(("parallel", "parallel")),
        name="mem_attention",
    )(*ins)


def _gla_kernel(q_ref, k_ref, v_ref, gate_ref, sm_ref, w2_ref, gb_ref, nw_ref, *rest, L, nblk, t_real, nseq,
                has_state, has_prev):
    n_opt = has_state + has_prev
    mix_ref, s_ref, ST, QK, B, OI = rest[n_opt:]
    tb = pl.program_id(1)
    rows_blk = q_ref.shape[0]

    @pl.when(tb == 0)
    def _():
        for i in range(nseq):
            for h in range(GLA_HEADS):
                ST[i * GLA_HEADS + h] = rest[0][i, h].T if has_state else jnp.zeros((GLA_DV, GLA_DK), F32)

    z = _mxu_f32(sm_ref[...], w2_ref[...], _NN) + gb_ref[...]
    g_all = -_softplus(-z) * (1.0 / GLA_TAU)
    t_i = lax.broadcasted_iota(jnp.int32, (rows_blk, GLA_DK), 0)
    t_c = t_i & (L - 1)
    nw = nw_ref[...]
    for h in range(GLA_HEADS):
        ks = slice(h * GLA_DK, (h + 1) * GLA_DK)
        vs = slice(h * GLA_DV, (h + 1) * GLA_DV)
        q = q_ref[:, ks] * GLA_DK ** -0.5
        k = k_ref[:, ks]
        b = g_all[:, ks]
        v = v_ref[:, vs]
        if t_real < L:
            b = jnp.where(t_c < t_real, b, 0.0)
            k = jnp.where(t_c < t_real, k, 0.0)
        sh = 1
        while sh < L:
            b = b + jnp.where(t_c >= sh, pltpu.roll(b, sh, 0), 0.0)
            sh *= 2
        o = jnp.sum(q * k, axis=-1, keepdims=True) * v
        for j in range(1, L):
            d = jnp.where(t_c >= j, b - pltpu.roll(b, j, 0), -jnp.inf)
            p = jnp.exp(d) * q * pltpu.roll(k, j, 0)
            o = o + jnp.sum(p, axis=-1, keepdims=True) * pltpu.roll(v, j, 0)
        OI[:, vs] = o
        B[:, ks] = b
        QK[:, ks] = q
        QK[:, GLA_QK + h * GLA_DK:GLA_QK + (h + 1) * GLA_DK] = k
    for c in range(rows_blk // L):
        rows = slice(c * L, (c + 1) * L)
        for h in range(GLA_HEADS):
            ks = slice(h * GLA_DK, (h + 1) * GLA_DK)
            vs = slice(h * GLA_DV, (h + 1) * GLA_DV)
            si = (c if nseq > 1 else 0) * GLA_HEADS + h
            b = B[rows, ks]
            b_last = b[L - 1:L, :]
            st = ST[si]
            o = OI[rows, vs] + _mxu(QK[rows, ks] * jnp.exp(b), st, _NT)
            y = o * lax.rsqrt(jnp.mean(o * o, axis=-1, keepdims=True) + 1e-6) * nw
            mix_ref[rows, vs] = y * _silu(gate_ref[rows, vs])
            kd = QK[rows, GLA_QK + h * GLA_DK:GLA_QK + (h + 1) * GLA_DK] * jnp.exp(b_last - b)
            ST[si] = st * jnp.exp(b_last) + _mxu(v_ref[rows, vs], kd, _TN)

    @pl.when(tb == nblk - 1)
    def _():
        for i in range(nseq):
            for h in range(GLA_HEADS):
                s_ref[i, h] = ST[i * GLA_HEADS + h].T


def _gla(h, mix_prev, grp, P, st_in, s_prev, p, L, tb):
    nblk = grp.t_rows // tb
    tail = (GLA_HEADS, GLA_DK, GLA_DV)
    rows_blk = tb * grp.nseq
    s_ins, s_specs, s_out, s_shape = _state_io(tail, p, st_in, grp)
    full = lambda shape: pl.BlockSpec(shape, lambda b, c: (0,) * len(shape))
    w2 = jnp.concatenate([P['gla_w2'], jnp.zeros((128 - GLA_RANK, GLA_QK), F32)], axis=0)
    ins = [h, h, h, h, h, w2, P['gla_b'].reshape(1, GLA_QK), P['gla_norm'].reshape(1, GLA_DV)]
    specs = [grp.spec(tb, GLA_QK, EVEN_OFF[0]), grp.spec(tb, GLA_QK, EVEN_OFF[1]), grp.spec(tb, GLA_WIDTH, EVEN_OFF[2]),
             grp.spec(tb, GLA_WIDTH, EVEN_OFF[4]), grp.spec(tb, 128, EVEN_OFF[3]),
             full((128, GLA_QK)), full((1, GLA_QK)), full((1, GLA_DV))]
    ins += s_ins
    specs += s_specs
    any_spec = pl.BlockSpec(memory_space=pl.ANY)
    alias = {}
    for prev, out_idx in ((mix_prev, 0), (s_prev, 1)):
        if prev is not None:
            alias[len(ins)] = out_idx
            ins.append(prev)
            specs.append(any_spec)
    n_prev = (mix_prev is not None) + (s_prev is not None)
    return pl.pallas_call(
        functools.partial(_gla_kernel, L=L, nblk=nblk, t_real=grp.t_real or L, nseq=grp.nseq,
                          has_state=st_in is not None, has_prev=n_prev),
        grid=(grp.steps, nblk),
        in_specs=specs,
        out_specs=[grp.spec(tb, GLA_WIDTH, SSD_WIDTH), s_out],
        out_shape=[jax.ShapeDtypeStruct((h.shape[0], MIX_WIDTH), F32), s_shape],
        scratch_shapes=[pltpu.VMEM((grp.nseq * GLA_HEADS, GLA_DV, GLA_DK), F32),
                        pltpu.VMEM((rows_blk, 2 * GLA_QK), F32), pltpu.VMEM((rows_blk, GLA_QK), F32),
                        pltpu.VMEM((rows_blk, GLA_WIDTH), F32)],
        input_output_aliases=alias,
        compiler_params=_cparams(("parallel", "arbitrary")),
        name="gla",
    )(*ins)


SSD_GW = SSD_REP * SSD_HEAD_DIM


def _ssd_lanes(L):
    return max(SSD_REP * L, 128)


def _ssd_kernel(sz_ref, xbc_ref, sm_ref, cw_ref, cbias_ref, dtb_ref, alog_ref, dvec_ref, nw_ref, ep_ref, es_ref, *rest,
                L, nchunk, t_real, nseq, has_state, has_prev):
    n_opt = 2 * has_state + has_prev
    mix_ref, s_ref, conv_out_ref, ST, tail = rest[n_opt:]
    ci = pl.program_id(1)
    gs = _ssd_lanes(L)
    cat = jnp.concatenate

    @pl.when(ci == 0)
    def _():
        tail[...] = jnp.zeros_like(tail)
        for i in range(nseq):
            for g in range(SSD_GROUPS):
                if has_state:
                    ST[i * SSD_GROUPS + g] = rest[1][i, g * SSD_REP:(g + 1) * SSD_REP].reshape(SSD_GW, SSD_STATE).T
                else:
                    ST[i * SSD_GROUPS + g] = jnp.zeros((SSD_STATE, SSD_GW), F32)
            if has_state:
                tail[i, 8 - (CONV_W - 1):8, :] = rest[0][i]

    d = 128
    row8 = lax.broadcasted_iota(jnp.int32, (8, d), 0)
    row = lax.broadcasted_iota(jnp.int32, (L, L), 0)
    col = lax.broadcasted_iota(jnp.int32, (L, L), 1)
    tril = (col <= row).astype(F32)
    ep = ep_ref[...]
    t_i = lax.broadcasted_iota(jnp.int32, (L, SSD_GROUPS * gs), 0)
    s_i = lax.broadcasted_iota(jnp.int32, (L, SSD_GROUPS * gs), 1) & (L - 1)
    blk_r = lax.broadcasted_iota(jnp.int32, (gs, SSD_GW), 0) // L
    blk_c = lax.broadcasted_iota(jnp.int32, (gs, SSD_GW), 1) // SSD_HEAD_DIM
    diag = blk_r == blk_c
    reps = SSD_REP * L
    for i in range(nseq):
        rows = slice(i * L, (i + 1) * L)

        def conv_tile(c0):
            u = xbc_ref[rows, c0:c0 + d]
            p8 = tail[i, :, c0:c0 + d]
            w = cw_ref[:, c0:c0 + d]
            acc = u * w[CONV_W - 1:CONV_W] + cbias_ref[:, c0:c0 + d]
            for j in range(1, CONV_W):
                acc = acc + _shifted(u, p8, j, row8) * w[CONV_W - 1 - j:CONV_W - j]
            return _silu(acc)

        dt = _softplus(sm_ref[rows, GLA_RANK:GLA_RANK + SSD_HEADS] + dtb_ref[...])
        if t_real < L:
            dt = jnp.where(lax.broadcasted_iota(jnp.int32, (L, SSD_HEADS), 0) < t_real, dt, 0.0)
        c = _mxu_f32(tril, dt * -jnp.exp(alog_ref[...]), _NN)
        dt_x = _mxu_f32(dt, ep, _NN)
        c_x = _mxu_f32(c, ep, _NN)
        c_s = _mxu_f32(c, es_ref[...], _NN)
        c_src = jnp.sum(jnp.where(t_i == s_i, c_s, 0.0), axis=0, keepdims=True)
        seg = jnp.exp(jnp.where(s_i <= t_i, c_s - c_src, -jnp.inf))
        for g in range(SSD_GROUPS):
            gl = slice(g * SSD_GW, (g + 1) * SSD_GW)
            sx = cat([conv_tile(g * SSD_GW + n * d) for n in range(SSD_GW // d)], axis=1)
            bm = conv_tile(SSD_WIDTH + g * SSD_STATE)
            cm = conv_tile(SSD_WIDTH + SSD_GROUPS * SSD_STATE + g * SSD_STATE)
            xdt = sx * dt_x[:, gl]
            pad_rows = [] if reps == gs else [jnp.zeros((gs - reps, SSD_STATE), F32)]
            cb = _mxu(cm, cat([bm] * SSD_REP + pad_rows, axis=0), _NT)
            pad_rows = [] if reps == gs else [jnp.zeros((gs - reps, SSD_GW), F32)]
            xbd = jnp.where(diag, cat([xdt] * SSD_REP + pad_rows, axis=0), 0.0)
            st = ST[i * SSD_GROUPS + g]
            y = _mxu(cb * seg[:, g * gs:(g + 1) * gs], xbd, _NN) + _mxu(cm, st, _NN) * jnp.exp(c_x[:, gl])
            y = (y + sx * dvec_ref[:, gl]) * _silu(sz_ref[rows, gl])
            y = y * lax.rsqrt(jnp.mean(y * y, axis=-1, keepdims=True) + 1e-6) * nw_ref[:, gl]
            mix_ref[rows, gl] = y
            c_end = c_x[L - 1:L, gl]
            ST[i * SSD_GROUPS + g] = st * jnp.exp(c_end) + _mxu(bm, xdt * jnp.exp(c_end - c_x[:, gl]), _TN)
        if nchunk > 1:
            tail[i] = xbc_ref[i * L + L - 8:(i + 1) * L, :]

    @pl.when(ci == nchunk - 1)
    def _():
        for i in range(nseq):
            conv_out_ref[i] = xbc_ref[i * L + t_real - (CONV_W - 1):i * L + t_real, :]
            for g in range(SSD_GROUPS):
                s_ref[i, g * SSD_REP:(g + 1) * SSD_REP] = ST[i * SSD_GROUPS + g].T.reshape(
                    SSD_REP, SSD_HEAD_DIM, SSD_STATE)


def _ssd(h, mix_prev, grp, P, st_in, conv_in, s_prev, p):
    L = grp.chunk
    nchunk = grp.t_rows // L
    gs = _ssd_lanes(L)
    tail = (SSD_HEADS, SSD_HEAD_DIM, SSD_STATE)
    ns = grp.nseq
    s_ins, s_specs, s_out, s_shape = _state_io(tail, p, st_in, grp)
    full = lambda shape: pl.BlockSpec(shape, lambda b, c: (0,) * len(shape))
    heads = jnp.arange(SSD_HEADS)[:, None]
    lane_p = jnp.arange(SSD_WIDTH)[None, :]
    ep = (lane_p // SSD_HEAD_DIM == heads).astype(F32)
    lane_s = jnp.arange(SSD_GROUPS * gs)[None, :]
    in_grp = lane_s % gs
    es = ((in_grp < SSD_REP * L) & ((lane_s // gs) * SSD_REP + in_grp // L == heads)).astype(F32)
    row = lambda a: a.reshape(1, -1)
    ins = [h, h, h, P['ssd_conv_w'], row(P['ssd_conv_b']), row(P['ssd_dt_bias']), row(P['ssd_a_log']),
           row(jnp.repeat(P['ssd_d'], SSD_HEAD_DIM)), row(P['ssd_norm']), ep, es]
    specs = [grp.spec(L, SSD_WIDTH, EVEN_OFF[5]), grp.spec(L, SSD_CONV_DIM, EVEN_OFF[6]), grp.spec(L, 128, EVEN_OFF[3]),
             full((CONV_W, SSD_CONV_DIM)), full((1, SSD_CONV_DIM)), full((1, SSD_HEADS)), full((1, SSD_HEADS)),
             full((1, SSD_WIDTH)), full((1, SSD_WIDTH)), full(ep.shape), full(es.shape)]
    if st_in is not None:
        ins += [conv_in] + s_ins
        specs += [pl.BlockSpec((None, ns, CONV_W - 1, SSD_CONV_DIM), lambda b, c: (p, b, 0, 0))] + s_specs
    any_spec = pl.BlockSpec(memory_space=pl.ANY)
    alias = {}
    for prev, out_idx in ((mix_prev, 0), (s_prev, 1)):
        if prev is not None:
            alias[len(ins)] = out_idx
            ins.append(prev)
            specs.append(any_spec)
    n_prev = (mix_prev is not None) + (s_prev is not None)
    return pl.pallas_call(
        functools.partial(_ssd_kernel, L=L, nchunk=nchunk, t_real=grp.t_real or L, nseq=ns,
                          has_state=st_in is not None, has_prev=n_prev),
        grid=(grp.steps, nchunk),
        in_specs=specs,
        out_specs=[grp.spec(L, SSD_WIDTH, 0), s_out,
                   pl.BlockSpec((ns, CONV_W - 1, SSD_CONV_DIM), lambda b, c: (b, 0, 0))],
        out_shape=[jax.ShapeDtypeStruct((h.shape[0], MIX_WIDTH), F32), s_shape,
                   jax.ShapeDtypeStruct((grp.bsz, CONV_W - 1, SSD_CONV_DIM), F32)],
        scratch_shapes=[pltpu.VMEM((ns * SSD_GROUPS, SSD_STATE, SSD_GW), F32),
                        pltpu.VMEM((ns, 8, SSD_CONV_DIM), F32)],
        input_output_aliases=alias,
        compiler_params=_cparams(("parallel", "arbitrary")),
        name="ssd",
    )(*ins)


def _gdn_kernel(qkv_ref, cz_ref, sm_ref, cw_ref, alog_ref, dtb_ref, nw_ref, *rest, L, nchunk, rounds, t_real, nseq,
                has_state, has_prev):
    n_opt = 2 * has_state + has_prev
    mix_ref, s_ref, conv_out_ref, S, tail = rest[n_opt:]
    ci = pl.program_id(1)

    @pl.when(ci == 0)
    def _():
        tail[...] = jnp.zeros_like(tail)
        if has_state:
            for i in range(nseq):
                tail[i, 8 - (CONV_W - 1):8, :] = rest[0][i]
            S[...] = rest[1][...]
        else:
            S[...] = jnp.zeros_like(S)

    n2 = 2 * L
    d = GDN_HEAD_DIM
    cat = jnp.concatenate
    row8 = lax.broadcasted_iota(jnp.int32, (8, d), 0)

    def conv_tile(i, c0):
        u = qkv_ref[i * L:(i + 1) * L, c0:c0 + d]
        p8 = tail[i, :, c0:c0 + d]
        w = cw_ref[:, c0:c0 + d]
        acc = u * w[CONV_W - 1:CONV_W]
        for j in range(1, CONV_W):
            acc = acc + _shifted(u, p8, j, row8) * w[CONV_W - 1 - j:CONV_W - j]
        return _silu(acc)

    def l2n(x):
        return x * lax.rsqrt(jnp.sum(x * x, axis=-1, keepdims=True) + 1e-6)

    row = lax.broadcasted_iota(jnp.int32, (L, L), 0)
    col = lax.broadcasted_iota(jnp.int32, (L, L), 1)
    tril = (col <= row).astype(F32)
    r2 = lax.broadcasted_iota(jnp.int32, (n2, n2), 0)
    c2 = lax.broadcasted_iota(jnp.int32, (n2, n2), 1)
    same = (r2 >= L) == (c2 >= L)
    strict = same & (c2 < r2)
    incl = same & (c2 <= r2)
    upper = same & (r2 <= c2)
    eye = (r2 == c2).astype(F32)
    zl = jnp.zeros((L, d), F32)
    units = [(i, j) for i in range(nseq) for j in range(GDN_HEADS // 2)]
    nn, tt, qk, kq, kdec, ec, bcol, elast, vst = [], [], [], [], [], [], [], [], []
    for i in range(nseq):
        sm = sm_ref[i * L:(i + 1) * L, :]
        beta_all = _sigmoid(sm[:, :GDN_HEADS])
        g_all = -jnp.exp(alog_ref[...]) * _softplus(sm[:, GDN_HEADS:2 * GDN_HEADS] + dtb_ref[...])
        if t_real < L:
            valid = lax.broadcasted_iota(jnp.int32, (L, GDN_HEADS), 0) < t_real
            beta_all = jnp.where(valid, beta_all, 0.0)
            g_all = jnp.where(valid, g_all, 0.0)
        c_all = _mxu_f32(tril, g_all, _NN)
        for j in range(GDN_HEADS // 2):
            h0, h1 = 2 * j, 2 * j + 1
            stack_col = lambda a: cat([a[:, h0:h0 + 1], a[:, h1:h1 + 1]], axis=0)
            c_col = stack_col(c_all)
            beta_col = stack_col(beta_all)
            c_row = jnp.sum(jnp.where(upper, stack_col(g_all), 0.0), axis=0, keepdims=True)
            decay = jnp.exp(jnp.where(incl, c_col - c_row, -jnp.inf))
            last = lambda rows: cat([jnp.broadcast_to(c_all[L - 1:L, h0:h0 + 1], (rows, 1)),
                                     jnp.broadcast_to(c_all[L - 1:L, h1:h1 + 1], (rows, 1))], axis=0)
            q0, q1 = (l2n(conv_tile(i, h * d)) * d ** -0.5 for h in (h0, h1))
            k0, k1 = (l2n(conv_tile(i, GDN_WIDTH + h * d)) for h in (h0, h1))
            vst.append(cat([conv_tile(i, 2 * GDN_WIDTH + h0 * d), conv_tile(i, 2 * GDN_WIDTH + h1 * d)], axis=0))
            k_st = cat([cat([k0, zl], axis=1), cat([zl, k1], axis=1)], axis=0)
            q_st = cat([cat([q0, zl], axis=1), cat([zl, q1], axis=1)], axis=0)
            both = cat([k_st, q_st], axis=0)
            full = _mxu(both, k_st, _NT)
            a = jnp.where(strict, full[:n2] * decay * beta_col, 0.0)
            nn.append(-a)
            tt.append(eye - a)
            qk.append(full[n2:] * decay)
            kq.append(both)
            kdec.append(k_st * jnp.exp(last(L) - c_col))
            ec.append(jnp.exp(c_col))
            bcol.append(beta_col)
            elast.append(jnp.exp(last(d)))
    tt = _neumann_inverse(nn, tt, n2, rounds)
    s_old, ksqs, u = [], [], []
    for n, (i, j) in enumerate(units):
        s_old.append(cat([S[i, 2 * j], S[i, 2 * j + 1]], axis=0))
        ksqs.append(_mxu(kq[n], s_old[n], _NN))
    for n in range(len(units)):
        u.append(_mxu(tt[n], bcol[n] * (vst[n] - ec[n] * ksqs[n][:n2]), _NN))
    nw = nw_ref[...]
    for n, (i, j) in enumerate(units):
        rows = slice(i * L, (i + 1) * L)
        o = ec[n] * ksqs[n][n2:] + _mxu(qk[n], u[n], _NN)
        for hh, oh in ((2 * j, o[:L]), (2 * j + 1, o[L:])):
            cols = slice(hh * d, (hh + 1) * d)
            y = oh * lax.rsqrt(jnp.mean(oh * oh, axis=-1, keepdims=True) + 1e-6) * nw
            mix_ref[rows, cols] = y * _silu(cz_ref[rows, cols])
        new = s_old[n] * elast[n] + _mxu(kdec[n], u[n], _TN)
        S[i, 2 * j] = new[:d]
        S[i, 2 * j + 1] = new[d:]
    if nchunk > 1:
        for i in range(nseq):
            tail[i] = qkv_ref[i * L + L - 8:(i + 1) * L, :]

    @pl.when(ci == nchunk - 1)
    def _():
        s_ref[...] = S[...]
        for i in range(nseq):
            conv_out_ref[i] = qkv_ref[i * L + t_real - (CONV_W - 1):i * L + t_real, :]


def _gdn(h, mix_prev, grp, P, st_in, conv_in, s_prev, p):
    L = grp.chunk
    nchunk = grp.t_rows // L
    tail = (GDN_HEADS, GDN_HEAD_DIM, GDN_HEAD_DIM)
    ns = grp.nseq
    s_ins, s_specs, s_out, s_shape = _state_io(tail, p, st_in, grp)
    cw = 3 * GDN_WIDTH
    full = lambda shape: pl.BlockSpec(shape, lambda b, c: (0,) * len(shape))
    ins = [h, h, h, P['gdn_conv_w'], P['gdn_a_log'].reshape(1, GDN_HEADS), P['gdn_dt_bias'].reshape(1, GDN_HEADS),
           P['gdn_norm'].reshape(1, GDN_HEAD_DIM)]
    specs = [grp.spec(L, cw, ODD_OFF[0]), grp.spec(L, GDN_WIDTH, ODD_OFF[1]), grp.spec(L, 128, ODD_OFF[2]),
             full((CONV_W, cw)), full((1, GDN_HEADS)), full((1, GDN_HEADS)), full((1, GDN_HEAD_DIM))]
    if st_in is not None:
        ins += [conv_in] + s_ins
        specs += [pl.BlockSpec((None, ns, CONV_W - 1, cw), lambda b, c: (p, b, 0, 0))] + s_specs
    any_spec = pl.BlockSpec(memory_space=pl.ANY)
    alias = {}
    for prev, out_idx in ((mix_prev, 0), (s_prev, 1)):
        if prev is not None:
            alias[len(ins)] = out_idx
            ins.append(prev)
            specs.append(any_spec)
    n_prev = (mix_prev is not None) + (s_prev is not None)
    return pl.pallas_call(
        functools.partial(_gdn_kernel, L=L, nchunk=nchunk, rounds=int(math.log2(L)), t_real=grp.t_real or L,
                          nseq=ns, has_state=st_in is not None, has_prev=n_prev),
        grid=(grp.steps, nchunk),
        in_specs=specs,
        out_specs=[grp.spec(L, GDN_WIDTH, 0), s_out,
                   pl.BlockSpec((ns, CONV_W - 1, cw), lambda b, c: (b, 0, 0))],
        out_shape=[jax.ShapeDtypeStruct((h.shape[0], MIX_WIDTH), F32), s_shape,
                   jax.ShapeDtypeStruct((grp.bsz, CONV_W - 1, cw), F32)],
        scratch_shapes=[pltpu.VMEM((ns,) + tail, F32), pltpu.VMEM((ns, 8, cw), F32)],
        input_output_aliases=alias,
        compiler_params=_cparams(("parallel", "arbitrary")),
        name="gdn",
    )(*ins)


RWKV_PAIRS = RWKV_HEADS // 2
RWKV_PW = 2 * RWKV_HEAD_DIM


def _rwkv_kernel(r_ref, k_ref, v_ref, xwa_ref, gate_ref, mu_ref, w0_ref, w2_ref, a0_ref, a2_ref, kkp_ref, ka_ref,
                 rk_ref, lng_ref, lnb_ref, *rest, L, nchunk, rounds, t_real, nseq, has_state, has_prev):
    n_opt = 2 * has_state + has_prev
    mix_ref, s_ref, shift_out_ref, S, last = rest[n_opt:]
    ci = pl.program_id(1)
    n = RWKV_HEAD_DIM
    w3 = 3 * RWKV_WIDTH
    cat = jnp.concatenate
    pairs = range(RWKV_PAIRS)

    @pl.when(ci == 0)
    def _():
        last[...] = jnp.zeros_like(last)
        if has_state:
            zn = jnp.zeros((n, n), F32)
            for i in range(nseq):
                last[i, 7:8, :] = rest[0][i]
                for j in pairs:
                    S[i * RWKV_PAIRS + j] = cat([cat([rest[1][i, 2 * j], zn], axis=1),
                                                 cat([zn, rest[1][i, 2 * j + 1]], axis=1)], axis=0)
        else:
            S[...] = jnp.zeros_like(S)

    n2 = 2 * L
    row = lax.broadcasted_iota(jnp.int32, (L, L), 0)
    col = lax.broadcasted_iota(jnp.int32, (L, L), 1)
    tril = (col <= row).astype(F32)
    r2 = lax.broadcasted_iota(jnp.int32, (n2, n2), 0)
    c2 = lax.broadcasted_iota(jnp.int32, (n2, n2), 1)
    same = (r2 >= L) == (c2 >= L)
    strict = same & (c2 < r2)
    incl = same & (c2 <= r2)
    eye = (r2 == c2).astype(F32)
    lane = lax.broadcasted_iota(jnp.int32, (L, RWKV_PW), 1)
    lo = lane < n

    def stack(x):
        return cat([jnp.where(lo, x, 0.0), jnp.where(lo, 0.0, x)], axis=0)

    row8 = lax.broadcasted_iota(jnp.int32, (8, RWKV_PW), 0)
    valid = lax.broadcasted_iota(jnp.int32, (L, RWKV_PW), 0) < t_real

    def seg_sum(x):
        s_lo = jnp.sum(jnp.where(lo, x, 0.0), axis=-1, keepdims=True)
        s_hi = jnp.sum(jnp.where(lo, 0.0, x), axis=-1, keepdims=True)
        return jnp.where(lo, s_lo, s_hi)

    def shift_mix(ref, i, c_src, c_all):
        x = ref[i * L:(i + 1) * L, c_src:c_src + RWKV_PW]
        prev = _shifted(x, last[i, :, c_all:c_all + RWKV_PW], 1, row8)
        return x + (prev - x) * mu_ref[:, c_all:c_all + RWKV_PW]

    xwa = cat([shift_mix(xwa_ref, i, 0, w3) for i in range(nseq)], axis=0)
    lr_w_all = _mxu_f32(jnp.tanh(xwa), w2_ref[...], _NN)
    lr_a_all = _mxu_f32(xwa, a2_ref[...], _NN)
    units = [(i, j) for i in range(nseq) for j in pairs]
    a_ak, a_rk, a_rb, nn, tt, sread, kdbd, egl, vs, bonus = [], [], [], [], [], [], [], [], [], []
    for i, j in units:
        sl = slice(j * RWKV_PW, (j + 1) * RWKV_PW)
        r = shift_mix(r_ref, i, j * RWKV_PW, j * RWKV_PW)
        k = shift_mix(k_ref, i, j * RWKV_PW, RWKV_WIDTH + j * RWKV_PW)
        v = shift_mix(v_ref, i, j * RWKV_PW, 2 * RWKV_WIDTH + j * RWKV_PW)
        w_log = -_softplus(-(w0_ref[:, sl] + lr_w_all[i * L:(i + 1) * L, sl])) - 0.5
        lw = -jnp.exp(w_log)
        a7 = _sigmoid(a0_ref[:, sl] + lr_a_all[i * L:(i + 1) * L, sl])
        kx = k * kkp_ref[:, sl]
        kk = kx * lax.rsqrt(seg_sum(kx * kx) + 1e-6)
        k = k * (1.0 + (a7 - 1.0) * ka_ref[:, sl])
        if t_real < L:
            lw, kk, k, v = (jnp.where(valid, a, 0.0) for a in (lw, kk, k, v))
        b = kk * a7
        bonus.append(seg_sum(r * k * rk_ref[:, sl]) * v)
        vs.append(stack(v))
        g = _mxu_f32(tril, lw, _NN)
        gp = g - lw
        gm = g[L // 2 - 1:L // 2, :]
        gl = g[L - 1:L, :]
        e_neg = jnp.exp(gm - g)
        lhs = cat([stack(kk * jnp.exp(gp - gm)), stack(r * jnp.exp(g - gm))], axis=0)
        rhs = cat([stack(b * e_neg), stack(k * e_neg)], axis=0)
        full = _mxu(lhs, rhs, _NT)
        a_ab = jnp.where(strict, full[:n2, :n2], 0.0)
        a_ak.append(jnp.where(strict, full[:n2, n2:], 0.0))
        a_rb.append(jnp.where(incl, full[n2:, :n2], 0.0))
        a_rk.append(jnp.where(incl, full[n2:, n2:], 0.0))
        nn.append(-a_ab)
        tt.append(eye - a_ab)
        sread.append(cat([stack(kk * jnp.exp(gp)), stack(r * jnp.exp(g))], axis=0))
        dec = jnp.exp(gl - g)
        kdbd.append(cat([stack(k * dec), stack(-b * dec)], axis=0))
        egl.append(jnp.exp(gl))
    tt = _neumann_inverse(nn, tt, n2, rounds)
    s_old, sr, av, u = [], [], [], []
    for m, (i, j) in enumerate(units):
        s_old.append(S[i * RWKV_PAIRS + j])
        sr.append(_mxu(sread[m], s_old[m], _NT))
        av.append(_mxu(cat([a_ak[m], a_rk[m]], axis=0), vs[m], _NN))
    for m in range(len(units)):
        u.append(_mxu(tt[m], sr[m][:n2] + av[m][:n2], _NN))
    for m, (i, j) in enumerate(units):
        sl = slice(j * RWKV_PW, (j + 1) * RWKV_PW)
        rows = slice(i * L, (i + 1) * L)
        o = sr[m][n2:] + av[m][n2:] - _mxu(a_rb[m], u[m], _NN)
        o = o[:L] + o[L:]
        oc = o - seg_sum(o) * (1.0 / n)
        gn = oc * lax.rsqrt(seg_sum(oc * oc) * (1.0 / n) + RWKV_GN_EPS)
        y = gn * lng_ref[:, sl] + lnb_ref[:, sl] + bonus[m]
        mix_ref[rows, sl] = y * _silu(gate_ref[rows, sl])
        S[i * RWKV_PAIRS + j] = s_old[m] * egl[m] + _mxu(cat([vs[m], u[m]], axis=0), kdbd[m], _TN)
    pieces = ((r_ref, 0, RWKV_WIDTH), (k_ref, RWKV_WIDTH, RWKV_WIDTH), (v_ref, 2 * RWKV_WIDTH, RWKV_WIDTH),
              (xwa_ref, w3, RWKV_PW))
    if nchunk > 1:
        for i in range(nseq):
            for ref, c0, wd in pieces:
                last[i, :, c0:c0 + wd] = ref[i * L + L - 8:(i + 1) * L, :]

    @pl.when(ci == nchunk - 1)
    def _():
        for i in range(nseq):
            for ref, c0, wd in pieces:
                shift_out_ref[i, :, c0:c0 + wd] = ref[i * L + t_real - 1:i * L + t_real, :]
            for j in pairs:
                s_ref[i, 2 * j] = S[i * RWKV_PAIRS + j][:n, :n]
                s_ref[i, 2 * j + 1] = S[i * RWKV_PAIRS + j][n:, n:]


def _rwkv(h, mix_prev, grp, P, st_in, shift_in, s_prev, p):
    L = grp.chunk
    nchunk = grp.t_rows // L
    tail = (RWKV_HEADS, RWKV_HEAD_DIM, RWKV_HEAD_DIM)
    ns = grp.nseq
    s_ins, s_specs, s_out, s_shape = _state_io(tail, p, st_in, grp)
    full = lambda shape: pl.BlockSpec(shape, lambda b, c: (0,) * len(shape))
    row = lambda a: a.reshape(1, -1)
    zr = jnp.zeros((RWKV_HEAD_DIM, RWKV_WIDTH), F32)
    w2 = jnp.concatenate([P['rwkv_w2'], zr], axis=0)
    a2 = jnp.concatenate([zr, P['rwkv_a2']], axis=0)
    off = ODD_OFF[4]
    ins = [h, h, h, h, h, row(P['rwkv_mu']), row(P['rwkv_w0']), w2, row(P['rwkv_a0']), a2, row(P['rwkv_kk']),
           row(P['rwkv_ka']), row(P['rwkv_rk']), row(P['rwkv_ln_g']), row(P['rwkv_ln_b'])]
    vec = full((1, RWKV_WIDTH))
    specs = [grp.spec(L, RWKV_WIDTH, off), grp.spec(L, RWKV_WIDTH, off + RWKV_WIDTH),
             grp.spec(L, RWKV_WIDTH, off + 2 * RWKV_WIDTH), grp.spec(L, RWKV_PW, off + 3 * RWKV_WIDTH),
             grp.spec(L, RWKV_WIDTH, ODD_OFF[5]), full((1, RWKV_SHIFT_DIM)), vec, full((RWKV_PW, RWKV_WIDTH)), vec,
             full((RWKV_PW, RWKV_WIDTH)), vec, vec, vec, vec, vec]
    if st_in is not None:
        ins += [shift_in] + s_ins
        specs += [pl.BlockSpec((None, ns, 1, RWKV_SHIFT_DIM), lambda b, c: (p, b, 0, 0))] + s_specs
    any_spec = pl.BlockSpec(memory_space=pl.ANY)
    alias = {}
    for prev, out_idx in ((mix_prev, 0), (s_prev, 1)):
        if prev is not None:
            alias[len(ins)] = out_idx
            ins.append(prev)
            specs.append(any_spec)
    n_prev = (mix_prev is not None) + (s_prev is not None)
    return pl.pallas_call(
        functools.partial(_rwkv_kernel, L=L, nchunk=nchunk, rounds=int(math.log2(L)), t_real=grp.t_real or L,
                          nseq=ns, has_state=st_in is not None, has_prev=n_prev),
        grid=(grp.steps, nchunk),
        in_specs=specs,
        out_specs=[grp.spec(L, RWKV_WIDTH, GDN_WIDTH), s_out,
                   pl.BlockSpec((ns, 1, RWKV_SHIFT_DIM), lambda b, c: (b, 0, 0))],
        out_shape=[jax.ShapeDtypeStruct((h.shape[0], MIX_WIDTH), F32), s_shape,
                   jax.ShapeDtypeStruct((grp.bsz, 1, RWKV_SHIFT_DIM), F32)],
        scratch_shapes=[pltpu.VMEM((ns * RWKV_PAIRS, RWKV_PW, RWKV_PW), F32),
                        pltpu.VMEM((ns, 8, RWKV_SHIFT_DIM), F32)],
        input_output_aliases=alias,
        compiler_params=_cparams(("parallel", "arbitrary")),
        name="rwkv7",
    )(*ins)


def _pad_t(a, t_to):
    t = a.shape[1]
    if t == t_to:
        return a
    return jnp.pad(a, [(0, 0), (0, t_to - t)] + [(0, 0)] * (a.ndim - 2))


def _even_mix(h, mix, grp, mem_k, mem_v, layer, st_in, st_prev, conv_in, P):
    p = layer // 2
    mix, s_ssd, s_conv = _ssd(h, mix, grp, P, st_in['ssd'], conv_in, st_prev['ssd'], p)
    gla_l, gla_tb = (GLA_CHUNK, 256) if grp.t_real is None else (grp.chunk, grp.chunk)
    mix, s_gla = _gla(h, mix, grp, P, st_in['gla'], st_prev['gla'], p, gla_l, gla_tb)
    mix = _mem_attention(h, mix, grp, EVEN_OFF[8], EVEN_OFF[9], mem_k, mem_v, layer, min(512, grp.t_rows))
    return mix, dict(gla=s_gla, ssd=s_ssd), s_conv


def _odd_mix(h, mix, grp, mem_k, mem_v, layer, st_in, st_prev, conv_in, shift_in, P):
    p = layer // 2
    mix, s_gdn, s_conv = _gdn(h, mix, grp, P, st_in['gdn'], conv_in, st_prev['gdn'], p)
    mix, s_rwkv, s_shift = _rwkv(h, mix, grp, P, st_in['rwkv'], shift_in, st_prev['rwkv'], p)
    mix = _mem_attention(h, mix, grp, ODD_OFF[6], ODD_OFF[7], mem_k, mem_v, layer, min(512, grp.t_rows))
    return mix, dict(gdn=s_gdn, rwkv=s_rwkv), s_conv, s_shift.reshape(grp.bsz, RWKV_SHIFT_DIM)


def kernel(x_prompt, x_sample, mem_prompt, cache_mem_k, cache_mem_v, state_gla, state_ssd, state_ssd_conv, state_gdn, state_gdn_conv, state_rwkv, state_rwkv_shift, mem_w_kv, ev_w_in, ev_gla_w2, ev_gla_b, ev_gla_norm, ev_ssd_conv_w, ev_ssd_conv_b, ev_ssd_dt_bias, ev_ssd_a_log, ev_ssd_d, ev_ssd_norm, ev_w_out, ev_ln_g, ev_ln_b, od_w_in, od_gdn_conv_w, od_gdn_dt_bias, od_gdn_a_log, od_gdn_norm, od_rwkv_mu, od_rwkv_w0, od_rwkv_w2, od_rwkv_a0, od_rwkv_a2, od_rwkv_kk, od_rwkv_ka, od_rwkv_rk, od_rwkv_ln_g, od_rwkv_ln_b, od_w_out, od_ln_g, od_ln_b):
    ev = dict(w_in=ev_w_in, gla_w2=ev_gla_w2, gla_b=ev_gla_b, gla_norm=ev_gla_norm,
              ssd_conv_w=ev_ssd_conv_w, ssd_conv_b=ev_ssd_conv_b, ssd_dt_bias=ev_ssd_dt_bias,
              ssd_a_log=ev_ssd_a_log, ssd_d=ev_ssd_d, ssd_norm=ev_ssd_norm,
              w_out=ev_w_out, ln_g=ev_ln_g, ln_b=ev_ln_b)
    od = dict(w_in=od_w_in, gdn_conv_w=od_gdn_conv_w, gdn_dt_bias=od_gdn_dt_bias, gdn_a_log=od_gdn_a_log,
              gdn_norm=od_gdn_norm, rwkv_mu=od_rwkv_mu, rwkv_w0=od_rwkv_w0, rwkv_w2=od_rwkv_w2,
              rwkv_a0=od_rwkv_a0, rwkv_a2=od_rwkv_a2, rwkv_kk=od_rwkv_kk, rwkv_ka=od_rwkv_ka,
              rwkv_rk=od_rwkv_rk, rwkv_ln_g=od_rwkv_ln_g, rwkv_ln_b=od_rwkv_ln_b,
              w_out=od_w_out, ln_g=od_ln_g, ln_b=od_ln_b)
    bp, tp, _ = x_prompt.shape
    bs, ts, _ = x_sample.shape
    mp, ms = bp * tp, bs * SMALL_T
    grp_p = _Group(bp, tp, None, 0, GDN_CHUNK)
    grp_s = _Group(bs, SMALL_T, ts, mp, SMALL_T, nseq=SAMPLE_NSEQ)

    w_kv = jnp.moveaxis(mem_w_kv, 0, 1).reshape(D_MODEL, DEPTH * 2 * MEM_WIDTH).astype(BF16)
    kv = _matmul(mem_prompt.reshape(bp * MEM_LEN, D_MODEL).astype(BF16), w_kv, 512, 1024)
    kv6 = kv.reshape(bp, MEM_LEN, DEPTH, 2, MEM_HEADS, MEM_HEAD_DIM)
    mem_k_p = jnp.moveaxis(kv6[:, :, :, 0], 2, 0)
    mem_v_p = jnp.moveaxis(kv6[:, :, :, 1], 2, 0)
    mk_s, mv_s = _cache_view(cache_mem_k), _cache_view(cache_mem_v)

    x = jnp.concatenate([x_prompt.reshape(mp, D_MODEL),
                         _pad_t(x_sample, SMALL_T).reshape(ms, D_MODEL)], axis=0)
    x_bf = x.astype(BF16)
    zp = lambda shape: jnp.zeros(shape, F32)
    names = ('gla', 'ssd', 'gdn', 'rwkv')
    none = {n: None for n in names}
    in_s = dict(gla=state_gla, ssd=state_ssd, gdn=state_gdn, rwkv=state_rwkv)
    shift_s = state_rwkv_shift.reshape(N_PAIRS, bs, 1, RWKV_SHIFT_DIM)
    out_p, out_s = dict(none), dict(none)
    small_p = {n: [] for n in ('ssd_conv', 'gdn_conv', 'rwkv_shift')}
    small_s = {n: [] for n in small_p}
    tm = 1024
    for layer in range(DEPTH):
        p = layer // 2
        if layer % 2 == 0:
            P = {n: w[p] for n, w in ev.items()}
            h = _matmul(x_bf, _pack_w_in(P['w_in'], EVEN_SIZES, EVEN_ORDER, EVEN_N), tm, PROJ_TN)
            mix, new, c1 = _even_mix(h, None, grp_p, kv, kv, layer, none, out_p, None, P)
            out_p.update(new)
            mix, new, c2 = _even_mix(h, mix, grp_s, mk_s, mv_s, layer, in_s, out_s, state_ssd_conv, P)
            out_s.update(new)
            small_p['ssd_conv'].append(c1)
            small_s['ssd_conv'].append(c2)
            w_out = jnp.concatenate([P['w_out'][GLA_WIDTH:GLA_WIDTH + SSD_WIDTH], P['w_out'][:GLA_WIDTH],
                                     P['w_out'][GLA_WIDTH + SSD_WIDTH:]], axis=0)
        else:
            P = {n: w[p] for n, w in od.items()}
            h = _matmul(x_bf, _pack_w_in(P['w_in'], ODD_SIZES, ODD_ORDER, ODD_N), tm, PROJ_TN)
            mix, new, c1, h1 = _odd_mix(h, None, grp_p, kv, kv, layer, none, out_p, None, None, P)
            out_p.update(new)
            mix, new, c2, h2 = _odd_mix(h, mix, grp_s, mk_s, mv_s, layer, in_s, out_s, state_gdn_conv, shift_s, P)
            out_s.update(new)
            small_p['gdn_conv'].append(c1)
            small_s['gdn_conv'].append(c2)
            small_p['rwkv_shift'].append(h1)
            small_s['rwkv_shift'].append(h2)
            w_out = P['w_out']
        x, x_bf = _out_ln(mix, w_out.astype(BF16), x, P['ln_g'], P['ln_b'])

    y_prompt = x[:mp].reshape(bp, tp, D_MODEL)
    y_sample = x[mp:].reshape(bs, SMALL_T, D_MODEL)[:, :ts]
    st = lambda d, n: jnp.stack(d[n])
    return (y_prompt, y_sample, mem_k_p, mem_v_p,
            out_p['gla'], out_s['gla'], out_p['ssd'], out_s['ssd'],
            st(small_p, 'ssd_conv'), st(small_s, 'ssd_conv'), out_p['gdn'], out_s['gdn'],
            st(small_p, 'gdn_conv'), st(small_s, 'gdn_conv'), out_p['rwkv'], out_s['rwkv'],
            st(small_p, 'rwkv_shift'), st(small_s, 'rwkv_shift'))
```

```python
import functools
import math

import numpy as np
import jax
import jax.numpy as jnp
from jax import lax
from jax.experimental import pallas as pl
from jax.experimental.pallas import tpu as pltpu

F32 = jnp.float32
BF16 = jnp.bfloat16
HI = lax.Precision.HIGHEST

D_MODEL = 2048
DEPTH = 4
N_PAIRS = DEPTH // 2
CONV_W = 4
MEM_LEN = 256
MEM_HEADS = 4
MEM_HEAD_DIM = 256
MEM_WIDTH = 1024
GLA_HEADS = 4
GLA_DK = 128
GLA_DV = 256
GLA_QK = 512
GLA_WIDTH = 1024
GLA_RANK = 16
GLA_TAU = 16.0
SSD_WIDTH = 2048
SSD_HEAD_DIM = 64
SSD_HEADS = 32
SSD_GROUPS = 4
SSD_REP = 8
SSD_STATE = 128
SSD_CONV_DIM = SSD_WIDTH + 2 * SSD_GROUPS * SSD_STATE
GDN_WIDTH = 2048
GDN_HEAD_DIM = 128
GDN_HEADS = 16
RWKV_WIDTH = 1024
RWKV_HEAD_DIM = 64
RWKV_HEADS = 16
RWKV_W_RANK = 64
RWKV_A_RANK = 64
RWKV_SHIFT_DIM = 3 * RWKV_WIDTH + RWKV_W_RANK + RWKV_A_RANK
RWKV_GN_EPS = 64e-5
EVEN_SIZES = (GLA_QK, GLA_QK, GLA_WIDTH, GLA_RANK, GLA_WIDTH, SSD_WIDTH, SSD_CONV_DIM, SSD_HEADS,
              MEM_WIDTH, MEM_WIDTH)
ODD_SIZES = (3 * GDN_WIDTH, GDN_WIDTH, GDN_HEADS, GDN_HEADS, RWKV_SHIFT_DIM, RWKV_WIDTH, MEM_WIDTH, MEM_WIDTH)
MIX_WIDTH = 4096
DEEPNORM_ALPHA = (2 * DEPTH) ** 0.25

EVEN_ORDER = (5, 2, 4, 8, 9, 6, 0, 1, 3, 7)
ODD_ORDER = (0, 1, 5, 6, 7, 4, 2, 3)
PROJ_TN = 768
VMEM_LIMIT = 60 * 1024 * 1024

GLA_CHUNK = 16
SSD_CHUNK = 64
GDN_CHUNK = 64
RWKV_CHUNK = 64
SMALL_T = 8
SAMPLE_NSEQ = 4

_NN = ((1,), (0,))
_NT = ((1,), (1,))
_TN = ((0,), (0,))


def _packed_layout(sizes, order):
    offs, o = {}, 0
    for i in order:
        offs[i] = o
        o += sizes[i]
    total = -(-o // PROJ_TN) * PROJ_TN
    return offs, total


EVEN_OFF, EVEN_N = _packed_layout(EVEN_SIZES, EVEN_ORDER)
ODD_OFF, ODD_N = _packed_layout(ODD_SIZES, ODD_ORDER)


def _pack_w_in(w, sizes, order, total):
    segs = jnp.split(w, np.cumsum(sizes)[:-1].tolist(), axis=-1)
    parts = [segs[i] for i in order]
    used = sum(sizes)
    if total > used:
        parts.append(jnp.zeros((w.shape[0], total - used), w.dtype))
    return jnp.concatenate(parts, axis=-1).astype(BF16)


def _seg(h, offs, sizes, i):
    return h[..., offs[i]:offs[i] + sizes[i]]


def _cparams(sem):
    return pltpu.CompilerParams(dimension_semantics=sem, vmem_limit_bytes=VMEM_LIMIT)


def _mxu(a, b, dims):
    return lax.dot_general(a.astype(BF16), b.astype(BF16), (dims, ((), ())), preferred_element_type=F32)


def _mxu_f32(a, b, dims):
    return lax.dot_general(a, b, (dims, ((), ())), precision=HI, preferred_element_type=F32)


def _sigmoid(x):
    return 1.0 / (1.0 + jnp.exp(-x))


def _silu(x):
    return x * _sigmoid(x)


def _softplus(x):
    return jnp.maximum(x, 0.0) + jnp.log(1.0 + jnp.exp(-jnp.abs(x)))


def _shifted(u, prev8, j, row8):
    ru = pltpu.roll(u, j, 0)
    top = jnp.where(row8 < j, pltpu.roll(prev8, j, 0), ru[:8])
    return top if u.shape[0] == 8 else jnp.concatenate([top, ru[8:]], axis=0)


class _Group:
    def __init__(self, bsz, t_rows, t_real, row0, chunk, nseq=1):
        assert nseq == 1 or t_rows == chunk
        self.bsz, self.t_rows, self.t_real, self.row0, self.chunk, self.nseq = bsz, t_rows, t_real, row0, chunk, nseq
        self.steps = bsz // nseq

    def spec(self, rows, width, off):
        rows = rows * self.nseq
        assert off % width == 0 and self.row0 % rows == 0 and (self.t_rows * self.nseq) % rows == 0
        base, per, cb = self.row0 // rows, self.t_rows * self.nseq // rows, off // width
        return pl.BlockSpec((rows, width), lambda b, i: (base + b * per + i, cb))


def _alias_last(n_inputs, has_prev, out_index=0):
    return {n_inputs - 1: out_index} if has_prev else {}


def _neumann_inverse(nn, tt, n, rounds):
    idx = range(len(nn))
    if rounds >= 2:
        for j in idx:
            nn[j] = _mxu(nn[j], nn[j], _NN)
        for _ in range(rounds - 2):
            for j in idx:
                both = _mxu(nn[j], jnp.concatenate([tt[j], nn[j]], axis=1), _NN)
                tt[j] = tt[j] + both[:, :n]
                nn[j] = both[:, n:]
        for j in idx:
            tt[j] = tt[j] + _mxu(nn[j], tt[j], _NN)
    return tt


def _state_io(tail, p, s_in, grp):
    zeros = (0,) * len(tail)
    spec = pl.BlockSpec((None, grp.nseq) + tail, lambda *g: (p, g[0]) + zeros)
    ins, specs = ([s_in], [spec]) if s_in is not None else ([], [])
    shape = jax.ShapeDtypeStruct((N_PAIRS, grp.bsz) + tail, F32)
    return ins, specs, spec, shape


def _mm_kernel(x_ref, w_ref, o_ref, *, precision):
    o_ref[...] = jnp.dot(x_ref[...], w_ref[...], preferred_element_type=F32, precision=precision)


def _matmul(x, w, tm, tn, precision=None):
    m, k = x.shape
    n = w.shape[1]
    assert m % tm == 0 and n % tn == 0
    return pl.pallas_call(
        functools.partial(_mm_kernel, precision=precision),
        grid=(n // tn, m // tm),
        in_specs=[pl.BlockSpec((tm, k), lambda j, i: (i, 0)),
                  pl.BlockSpec((k, tn), lambda j, i: (0, j))],
        out_specs=pl.BlockSpec((tm, tn), lambda j, i: (i, j)),
        out_shape=jax.ShapeDtypeStruct((m, n), F32),
        compiler_params=_cparams(("parallel", "parallel")),
        name="matmul",
    )(x, w)


def _out_ln_kernel(mix_ref, w_ref, x_ref, g_ref, b_ref, y_ref, ybf_ref, acc, *, nk):
    kk = pl.program_id(1)

    @pl.when(kk == 0)
    def _():
        acc[...] = jnp.zeros_like(acc)

    acc[...] += jnp.dot(mix_ref[...].astype(BF16), w_ref[...], preferred_element_type=F32)

    @pl.when(kk == nk - 1)
    def _():
        z = DEEPNORM_ALPHA * x_ref[...] + acc[...]
        zc = z - jnp.mean(z, axis=-1, keepdims=True)
        var = jnp.mean(zc * zc, axis=-1, keepdims=True)
        y = zc * lax.rsqrt(var + 1e-5) * g_ref[...] + b_ref[...]
        y_ref[...] = y
        ybf_ref[...] = y.astype(BF16)


def _out_ln(mix, w, x, g, b, tm=512, tk=2048):
    m, k = mix.shape
    d = w.shape[1]
    nk = k // tk
    return pl.pallas_call(
        functools.partial(_out_ln_kernel, nk=nk),
        grid=(m // tm, nk),
        in_specs=[pl.BlockSpec((tm, tk), lambda i, j: (i, j)),
                  pl.BlockSpec((tk, d), lambda i, j: (j, 0)),
                  pl.BlockSpec((tm, d), lambda i, j: (i, 0)),
                  pl.BlockSpec((1, d), lambda i, j: (0, 0)),
                  pl.BlockSpec((1, d), lambda i, j: (0, 0))],
        out_specs=[pl.BlockSpec((tm, d), lambda i, j: (i, 0)),
                   pl.BlockSpec((tm, d), lambda i, j: (i, 0))],
        out_shape=[jax.ShapeDtypeStruct((m, d), F32), jax.ShapeDtypeStruct((m, d), BF16)],
        scratch_shapes=[pltpu.VMEM((tm, d), F32)],
        compiler_params=_cparams(("parallel", "arbitrary")),
        name="out_ln",
    )(mix, w, x, g.reshape(1, d), b.reshape(1, d))


def _mem_kernel(q_ref, gate_ref, k_ref, v_ref, *rest):
    o_ref = rest[-1]
    for h in range(MEM_HEADS):
        sl = slice(h * MEM_HEAD_DIM, (h + 1) * MEM_HEAD_DIM)
        k = k_ref[:, sl]
        v = v_ref[:, sl]
        s = _mxu(q_ref[:, sl], k, _NT) * MEM_HEAD_DIM ** -0.5
        p = jnp.exp(s - jnp.max(s, axis=-1, keepdims=True))
        p = p / jnp.sum(p, axis=-1, keepdims=True)
        o_ref[:, sl] = (_mxu(p, v, _NN) * _silu(gate_ref[:, sl])).astype(BF16)


MEM_DT = MEM_HEAD_DIM // 128
MEM_ROWS = MEM_LEN * MEM_DT * MEM_HEADS


def _cache_view(c):
    d, b = c.shape[:2]
    c = c.reshape(d, b, MEM_LEN, MEM_HEADS, MEM_DT, 128)
    return jnp.transpose(c, (0, 1, 2, 4, 3, 5)).reshape(d, b, MEM_ROWS, 128)


def _mem_cache_kernel(q_ref, gate_ref, k_ref, v_ref, *rest):
    o_ref = rest[-1]
    nseq = k_ref.shape[0]
    t = q_ref.shape[0] // nseq
    grp = MEM_DT * MEM_HEADS
    col = lax.broadcasted_iota(jnp.int32, (MEM_HEADS * t, MEM_ROWS), 1) % grp
    head = lax.broadcasted_iota(jnp.int32, (MEM_HEADS * t, MEM_ROWS), 0) // t
    for i in range(nseq):
        rows = slice(i * t, (i + 1) * t)
        k = k_ref[i]
        v = v_ref[i]
        qs = [jnp.concatenate([q_ref[rows, h * MEM_HEAD_DIM + dt * 128:h * MEM_HEAD_DIM + (dt + 1) * 128]
                               for h in range(MEM_HEADS)], axis=0) for dt in range(MEM_DT)]
        s = _mxu(qs[0], k, _NT)
        for dt in range(1, MEM_DT):
            s = s + pltpu.roll(_mxu(qs[dt], k, _NT), MEM_ROWS - dt * MEM_HEADS, 1)
        s = jnp.where(col == head, s * MEM_HEAD_DIM ** -0.5, -jnp.inf)
        p = jnp.exp(s - jnp.max(s, axis=-1, keepdims=True))
        p = p / jnp.sum(p, axis=-1, keepdims=True)
        for dt in range(MEM_DT):
            o = _mxu(p if dt == 0 else pltpu.roll(p, dt * MEM_HEADS, 1), v, _NN)
            for h in range(MEM_HEADS):
                sl = slice(h * MEM_HEAD_DIM + dt * 128, h * MEM_HEAD_DIM + (dt + 1) * 128)
                o_ref[rows, sl] = (o[h * t:(h + 1) * t] * _silu(gate_ref[rows, sl])).astype(BF16)


def _mem_attention(h, mix_prev, grp, q_off, gate_off, mem_k, mem_v, layer, tq):
    cached = mem_k.ndim == 4
    if cached:
        kv_specs = [pl.BlockSpec((None, grp.nseq, MEM_ROWS, 128), lambda b, i: (layer, b, 0, 0))] * 2
    else:
        kv_specs = [pl.BlockSpec((MEM_LEN, MEM_WIDTH), lambda b, i: (b, 2 * layer)),
                    pl.BlockSpec((MEM_LEN, MEM_WIDTH), lambda b, i: (b, 2 * layer + 1))]
    ins = [h, h, mem_k, mem_v] + ([] if mix_prev is None else [mix_prev])
    specs = [grp.spec(tq, MEM_WIDTH, q_off), grp.spec(tq, MEM_WIDTH, gate_off)] + kv_specs
    if mix_prev is not None:
        specs.append(pl.BlockSpec(memory_space=pl.ANY))
    return pl.pallas_call(
        _mem_cache_kernel if cached else _mem_kernel,
        grid=(grp.steps, grp.t_rows // tq),
        in_specs=specs,
        out_specs=grp.spec(tq, MEM_WIDTH, MIX_WIDTH - MEM_WIDTH),
        out_shape=jax.ShapeDtypeStruct((h.shape[0], MIX_WIDTH), BF16),
        input_output_aliases=_alias_last(len(ins), mix_prev is not None),
        compiler_params=_cparams(("parallel", "parallel")),
        name="mem_attention",
    )(*ins)


def _gla_kernel(q_ref, k_ref, v_ref, gate_ref, sm_ref, w2_ref, gb_ref, nw_ref, *rest, L, nblk, t_real, nseq,
                has_state, has_prev):
    n_opt = has_state + has_prev
    mix_ref, s_ref, ST, QK, B, OI = rest[n_opt:]
    tb = pl.program_id(1)
    rows_blk = q_ref.shape[0]

    @pl.when(tb == 0)
    def _():
        for i in range(nseq):
            for h in range(GLA_HEADS):
                ST[i * GLA_HEADS + h] = rest[0][i, h].T if has_state else jnp.zeros((GLA_DV, GLA_DK), F32)

    z = _mxu_f32(sm_ref[...], w2_ref[...], _NN) + gb_ref[...]
    g_all = -_softplus(-z) * (1.0 / GLA_TAU)
    t_i = lax.broadcasted_iota(jnp.int32, (rows_blk, GLA_DK), 0)
    t_c = t_i & (L - 1)
    nw = nw_ref[...]
    for h in range(GLA_HEADS):
        ks = slice(h * GLA_DK, (h + 1) * GLA_DK)
        vs = slice(h * GLA_DV, (h + 1) * GLA_DV)
        q = q_ref[:, ks] * GLA_DK ** -0.5
        k = k_ref[:, ks]
        b = g_all[:, ks]
        v = v_ref[:, vs]
        if t_real < L:
            b = jnp.where(t_c < t_real, b, 0.0)
            k = jnp.where(t_c < t_real, k, 0.0)
        sh = 1
        while sh < L:
            b = b + jnp.where(t_c >= sh, pltpu.roll(b, sh, 0), 0.0)
            sh *= 2
        o = jnp.sum(q * k, axis=-1, keepdims=True) * v
        for j in range(1, L):
            d = jnp.where(t_c >= j, b - pltpu.roll(b, j, 0), -jnp.inf)
            p = jnp.exp(d) * q * pltpu.roll(k, j, 0)
            o = o + jnp.sum(p, axis=-1, keepdims=True) * pltpu.roll(v, j, 0)
        OI[:, vs] = o
        B[:, ks] = b
        QK[:, ks] = q
        QK[:, GLA_QK + h * GLA_DK:GLA_QK + (h + 1) * GLA_DK] = k
    for c in range(rows_blk // L):
        rows = slice(c * L, (c + 1) * L)
        for h in range(GLA_HEADS):
            ks = slice(h * GLA_DK, (h + 1) * GLA_DK)
            vs = slice(h * GLA_DV, (h + 1) * GLA_DV)
            si = (c if nseq > 1 else 0) * GLA_HEADS + h
            b = B[rows, ks]
            b_last = b[L - 1:L, :]
            st = ST[si]
            o = OI[rows, vs] + _mxu(QK[rows, ks] * jnp.exp(b), st, _NT)
            y = o * lax.rsqrt(jnp.mean(o * o, axis=-1, keepdims=True) + 1e-6) * nw
            mix_ref[rows, vs] = (y * _silu(gate_ref[rows, vs])).astype(BF16)
            kd = QK[rows, GLA_QK + h * GLA_DK:GLA_QK + (h + 1) * GLA_DK] * jnp.exp(b_last - b)
            ST[si] = st * jnp.exp(b_last) + _mxu(v_ref[rows, vs], kd, _TN)

    @pl.when(tb == nblk - 1)
    def _():
        for i in range(nseq):
            for h in range(GLA_HEADS):
                s_ref[i, h] = ST[i * GLA_HEADS + h].T


def _gla(h, mix_prev, grp, P, st_in, s_prev, p, L, tb):
    nblk = grp.t_rows // tb
    tail = (GLA_HEADS, GLA_DK, GLA_DV)
    rows_blk = tb * grp.nseq
    s_ins, s_specs, s_out, s_shape = _state_io(tail, p, st_in, grp)
    full = lambda shape: pl.BlockSpec(shape, lambda b, c: (0,) * len(shape))
    w2 = jnp.concatenate([P['gla_w2'], jnp.zeros((128 - GLA_RANK, GLA_QK), F32)], axis=0)
    ins = [h, h, h, h, h, w2, P['gla_b'].reshape(1, GLA_QK), P['gla_norm'].reshape(1, GLA_DV)]
    specs = [grp.spec(tb, GLA_QK, EVEN_OFF[0]), grp.spec(tb, GLA_QK, EVEN_OFF[1]), grp.spec(tb, GLA_WIDTH, EVEN_OFF[2]),
             grp.spec(tb, GLA_WIDTH, EVEN_OFF[4]), grp.spec(tb, 128, EVEN_OFF[3]),
             full((128, GLA_QK)), full((1, GLA_QK)), full((1, GLA_DV))]
    ins += s_ins
    specs += s_specs
    any_spec = pl.BlockSpec(memory_space=pl.ANY)
    alias = {}
    for prev, out_idx in ((mix_prev, 0), (s_prev, 1)):
        if prev is not None:
            alias[len(ins)] = out_idx
            ins.append(prev)
            specs.append(any_spec)
    n_prev = (mix_prev is not None) + (s_prev is not None)
    return pl.pallas_call(
        functools.partial(_gla_kernel, L=L, nblk=nblk, t_real=grp.t_real or L, nseq=grp.nseq,
                          has_state=st_in is not None, has_prev=n_prev),
        grid=(grp.steps, nblk),
        in_specs=specs,
        out_specs=[grp.spec(tb, GLA_WIDTH, SSD_WIDTH), s_out],
        out_shape=[jax.ShapeDtypeStruct((h.shape[0], MIX_WIDTH), BF16), s_shape],
        scratch_shapes=[pltpu.VMEM((grp.nseq * GLA_HEADS, GLA_DV, GLA_DK), F32),
                        pltpu.VMEM((rows_blk, 2 * GLA_QK), F32), pltpu.VMEM((rows_blk, GLA_QK), F32),
                        pltpu.VMEM((rows_blk, GLA_WIDTH), F32)],
        input_output_aliases=alias,
        compiler_params=_cparams(("parallel", "arbitrary")),
        name="gla",
    )(*ins)


SSD_GW = SSD_REP * SSD_HEAD_DIM


def _ssd_lanes(L):
    return max(SSD_REP * L, 128)


def _ssd_kernel(sz_ref, xbc_ref, sm_ref, cw_ref, cbias_ref, dtb_ref, alog_ref, dvec_ref, nw_ref, ep_ref, es_ref, *rest,
                L, nchunk, t_real, nseq, has_state, has_prev):
    n_opt = 2 * has_state + has_prev
    mix_ref, s_ref, conv_out_ref, ST, tail = rest[n_opt:]
    ci = pl.program_id(1)
    gs = _ssd_lanes(L)
    cat = jnp.concatenate

    @pl.when(ci == 0)
    def _():
        tail[...] = jnp.zeros_like(tail)
        for i in range(nseq):
            for g in range(SSD_GROUPS):
                if has_state:
                    ST[i * SSD_GROUPS + g] = rest[1][i, g * SSD_REP:(g + 1) * SSD_REP].reshape(SSD_GW, SSD_STATE).T
                else:
                    ST[i * SSD_GROUPS + g] = jnp.zeros((SSD_STATE, SSD_GW), F32)
            if has_state:
                tail[i, 8 - (CONV_W - 1):8, :] = rest[0][i]

    d = 128
    row8 = lax.broadcasted_iota(jnp.int32, (8, d), 0)
    row = lax.broadcasted_iota(jnp.int32, (L, L), 0)
    col = lax.broadcasted_iota(jnp.int32, (L, L), 1)
    tril = (col <= row).astype(F32)
    ep = ep_ref[...]
    t_i = lax.broadcasted_iota(jnp.int32, (L, SSD_GROUPS * gs), 0)
    s_i = lax.broadcasted_iota(jnp.int32, (L, SSD_GROUPS * gs), 1) & (L - 1)
    blk_r = lax.broadcasted_iota(jnp.int32, (gs, SSD_GW), 0) // L
    blk_c = lax.broadcasted_iota(jnp.int32, (gs, SSD_GW), 1) // SSD_HEAD_DIM
    diag = blk_r == blk_c
    reps = SSD_REP * L
    for i in range(nseq):
        rows = slice(i * L, (i + 1) * L)

        def conv_tile(c0):
            u = xbc_ref[rows, c0:c0 + d]
            p8 = tail[i, :, c0:c0 + d]
            w = cw_ref[:, c0:c0 + d]
            acc = u * w[CONV_W - 1:CONV_W] + cbias_ref[:, c0:c0 + d]
            for j in range(1, CONV_W):
                acc = acc + _shifted(u, p8, j, row8) * w[CONV_W - 1 - j:CONV_W - j]
            return _silu(acc)

        dt = _softplus(sm_ref[rows, GLA_RANK:GLA_RANK + SSD_HEADS] + dtb_ref[...])
        if t_real < L:
            dt = jnp.where(lax.broadcasted_iota(jnp.int32, (L, SSD_HEADS), 0) < t_real, dt, 0.0)
        c = _mxu_f32(tril, dt * -jnp.exp(alog_ref[...]), _NN)
        dt_x = _mxu_f32(dt, ep, _NN)
        c_x = _mxu_f32(c, ep, _NN)
        c_s = _mxu_f32(c, es_ref[...], _NN)
        c_src = jnp.sum(jnp.where(t_i == s_i, c_s, 0.0), axis=0, keepdims=True)
        seg = jnp.exp(jnp.where(s_i <= t_i, c_s - c_src, -jnp.inf))
        for g in range(SSD_GROUPS):
            gl = slice(g * SSD_GW, (g + 1) * SSD_GW)
            sx = cat([conv_tile(g * SSD_GW + n * d) for n in range(SSD_GW // d)], axis=1)
            bm = conv_tile(SSD_WIDTH + g * SSD_STATE)
            cm = conv_tile(SSD_WIDTH + SSD_GROUPS * SSD_STATE + g * SSD_STATE)
            xdt = sx * dt_x[:, gl]
            pad_rows = [] if reps == gs else [jnp.zeros((gs - reps, SSD_STATE), F32)]
            cb = _mxu(cm, cat([bm] * SSD_REP + pad_rows, axis=0), _NT)
            pad_rows = [] if reps == gs else [jnp.zeros((gs - reps, SSD_GW), F32)]
            xbd = jnp.where(diag, cat([xdt] * SSD_REP + pad_rows, axis=0), 0.0)
            st = ST[i * SSD_GROUPS + g]
            y = _mxu(cb * seg[:, g * gs:(g + 1) * gs], xbd, _NN) + _mxu(cm, st, _NN) * jnp.exp(c_x[:, gl])
            y = (y + sx * dvec_ref[:, gl]) * _silu(sz_ref[rows, gl])
            y = y * lax.rsqrt(jnp.mean(y * y, axis=-1, keepdims=True) + 1e-6) * nw_ref[:, gl]
            mix_ref[rows, gl] = y.astype(BF16)
            c_end = c_x[L - 1:L, gl]
            ST[i * SSD_GROUPS + g] = st * jnp.exp(c_end) + _mxu(bm, xdt * jnp.exp(c_end - c_x[:, gl]), _TN)
        if nchunk > 1:
            tail[i] = xbc_ref[i * L + L - 8:(i + 1) * L, :]

    @pl.when(ci == nchunk - 1)
    def _():
        for i in range(nseq):
            conv_out_ref[i] = xbc_ref[i * L + t_real - (CONV_W - 1):i * L + t_real, :]
            for g in range(SSD_GROUPS):
                s_ref[i, g * SSD_REP:(g + 1) * SSD_REP] = ST[i * SSD_GROUPS + g].T.reshape(
                    SSD_REP, SSD_HEAD_DIM, SSD_STATE)


def _ssd(h, mix_prev, grp, P, st_in, conv_in, s_prev, p):
    L = grp.chunk
    nchunk = grp.t_rows // L
    gs = _ssd_lanes(L)
    tail = (SSD_HEADS, SSD_HEAD_DIM, SSD_STATE)
    ns = grp.nseq
    s_ins, s_specs, s_out, s_shape = _state_io(tail, p, st_in, grp)
    full = lambda shape: pl.BlockSpec(shape, lambda b, c: (0,) * len(shape))
    heads = jnp.arange(SSD_HEADS)[:, None]
    lane_p = jnp.arange(SSD_WIDTH)[None, :]
    ep = (lane_p // SSD_HEAD_DIM == heads).astype(F32)
    lane_s = jnp.arange(SSD_GROUPS * gs)[None, :]
    in_grp = lane_s % gs
    es = ((in_grp < SSD_REP * L) & ((lane_s // gs) * SSD_REP + in_grp // L == heads)).astype(F32)
    row = lambda a: a.reshape(1, -1)
    ins = [h, h, h, P['ssd_conv_w'], row(P['ssd_conv_b']), row(P['ssd_dt_bias']), row(P['ssd_a_log']),
           row(jnp.repeat(P['ssd_d'], SSD_HEAD_DIM)), row(P['ssd_norm']), ep, es]
    specs = [grp.spec(L, SSD_WIDTH, EVEN_OFF[5]), grp.spec(L, SSD_CONV_DIM, EVEN_OFF[6]), grp.spec(L, 128, EVEN_OFF[3]),
             full((CONV_W, SSD_CONV_DIM)), full((1, SSD_CONV_DIM)), full((1, SSD_HEADS)), full((1, SSD_HEADS)),
             full((1, SSD_WIDTH)), full((1, SSD_WIDTH)), full(ep.shape), full(es.shape)]
    if st_in is not None:
        ins += [conv_in] + s_ins
        specs += [pl.BlockSpec((None, ns, CONV_W - 1, SSD_CONV_DIM), lambda b, c: (p, b, 0, 0))] + s_specs
    any_spec = pl.BlockSpec(memory_space=pl.ANY)
    alias = {}
    for prev, out_idx in ((mix_prev, 0), (s_prev, 1)):
        if prev is not None:
            alias[len(ins)] = out_idx
            ins.append(prev)
            specs.append(any_spec)
    n_prev = (mix_prev is not None) + (s_prev is not None)
    return pl.pallas_call(
        functools.partial(_ssd_kernel, L=L, nchunk=nchunk, t_real=grp.t_real or L, nseq=ns,
                          has_state=st_in is not None, has_prev=n_prev),
        grid=(grp.steps, nchunk),
        in_specs=specs,
        out_specs=[grp.spec(L, SSD_WIDTH, 0), s_out,
                   pl.BlockSpec((ns, CONV_W - 1, SSD_CONV_DIM), lambda b, c: (b, 0, 0))],
        out_shape=[jax.ShapeDtypeStruct((h.shape[0], MIX_WIDTH), BF16), s_shape,
                   jax.ShapeDtypeStruct((grp.bsz, CONV_W - 1, SSD_CONV_DIM), F32)],
        scratch_shapes=[pltpu.VMEM((ns * SSD_GROUPS, SSD_STATE, SSD_GW), F32),
                        pltpu.VMEM((ns, 8, SSD_CONV_DIM), F32)],
        input_output_aliases=alias,
        compiler_params=_cparams(("parallel", "arbitrary")),
        name="ssd",
    )(*ins)


def _gdn_kernel(qkv_ref, cz_ref, sm_ref, cw_ref, alog_ref, dtb_ref, nw_ref, *rest, L, nchunk, rounds, t_real, nseq,
                has_state, has_prev):
    n_opt = 2 * has_state + has_prev
    mix_ref, s_ref, conv_out_ref, S, tail = rest[n_opt:]
    ci = pl.program_id(1)

    @pl.when(ci == 0)
    def _():
        tail[...] = jnp.zeros_like(tail)
        if has_state:
            for i in range(nseq):
                tail[i, 8 - (CONV_W - 1):8, :] = rest[0][i]
            S[...] = rest[1][...]
        else:
            S[...] = jnp.zeros_like(S)

    n2 = 2 * L
    d = GDN_HEAD_DIM
    cat = jnp.concatenate
    row8 = lax.broadcasted_iota(jnp.int32, (8, d), 0)

    def conv_tile(i, c0):
        u = qkv_ref[i * L:(i + 1) * L, c0:c0 + d]
        p8 = tail[i, :, c0:c0 + d]
        w = cw_ref[:, c0:c0 + d]
        acc = u * w[CONV_W - 1:CONV_W]
        for j in range(1, CONV_W):
            acc = acc + _shifted(u, p8, j, row8) * w[CONV_W - 1 - j:CONV_W - j]
        return _silu(acc)

    def l2n(x):
        return x * lax.rsqrt(jnp.sum(x * x, axis=-1, keepdims=True) + 1e-6)

    row = lax.broadcasted_iota(jnp.int32, (L, L), 0)
    col = lax.broadcasted_iota(jnp.int32, (L, L), 1)
    tril = (col <= row).astype(F32)
    r2 = lax.broadcasted_iota(jnp.int32, (n2, n2), 0)
    c2 = lax.broadcasted_iota(jnp.int32, (n2, n2), 1)
    same = (r2 >= L) == (c2 >= L)
    strict = same & (c2 < r2)
    incl = same & (c2 <= r2)
    upper = same & (r2 <= c2)
    eye = (r2 == c2).astype(F32)
    zl = jnp.zeros((L, d), F32)
    units = [(i, j) for i in range(nseq) for j in range(GDN_HEADS // 2)]
    nn, tt, qk, kq, kdec, ec, bcol, elast, vst = [], [], [], [], [], [], [], [], []
    for i in range(nseq):
        sm = sm_ref[i * L:(i + 1) * L, :]
        beta_all = _sigmoid(sm[:, :GDN_HEADS])
        g_all = -jnp.exp(alog_ref[...]) * _softplus(sm[:, GDN_HEADS:2 * GDN_HEADS] + dtb_ref[...])
        if t_real < L:
            valid = lax.broadcasted_iota(jnp.int32, (L, GDN_HEADS), 0) < t_real
            beta_all = jnp.where(valid, beta_all, 0.0)
            g_all = jnp.where(valid, g_all, 0.0)
        c_all = _mxu_f32(tril, g_all, _NN)
        for j in range(GDN_HEADS // 2):
            h0, h1 = 2 * j, 2 * j + 1
            stack_col = lambda a: cat([a[:, h0:h0 + 1], a[:, h1:h1 + 1]], axis=0)
            c_col = stack_col(c_all)
            beta_col = stack_col(beta_all)
            c_row = jnp.sum(jnp.where(upper, stack_col(g_all), 0.0), axis=0, keepdims=True)
            decay = jnp.exp(jnp.where(incl, c_col - c_row, -jnp.inf))
            last = lambda rows: cat([jnp.broadcast_to(c_all[L - 1:L, h0:h0 + 1], (rows, 1)),
                                     jnp.broadcast_to(c_all[L - 1:L, h1:h1 + 1], (rows, 1))], axis=0)
            q0, q1 = (l2n(conv_tile(i, h * d)) * d ** -0.5 for h in (h0, h1))
            k0, k1 = (l2n(conv_tile(i, GDN_WIDTH + h * d)) for h in (h0, h1))
            vst.append(cat([conv_tile(i, 2 * GDN_WIDTH + h0 * d), conv_tile(i, 2 * GDN_WIDTH + h1 * d)], axis=0))
            k_st = cat([cat([k0, zl], axis=1), cat([zl, k1], axis=1)], axis=0)
            q_st = cat([cat([q0, zl], axis=1), cat([zl, q1], axis=1)], axis=0)
            both = cat([k_st, q_st], axis=0)
            full = _mxu(both, k_st, _NT)
            a = jnp.where(strict, full[:n2] * decay * beta_col, 0.0)
            nn.append(-a)
            tt.append(eye - a)
            qk.append(full[n2:] * decay)
            kq.append(both)
            kdec.append(k_st * jnp.exp(last(L) - c_col))
            ec.append(jnp.exp(c_col))
            bcol.append(beta_col)
            elast.append(jnp.exp(last(d)))
    tt = _neumann_inverse(nn, tt, n2, rounds)
    s_old, ksqs, u = [], [], []
    for n, (i, j) in enumerate(units):
        s_old.append(cat([S[i, 2 * j], S[i, 2 * j + 1]], axis=0))
        ksqs.append(_mxu(kq[n], s_old[n], _NN))
    for n in range(len(units)):
        u.append(_mxu(tt[n], bcol[n] * (vst[n] - ec[n] * ksqs[n][:n2]), _NN))
    nw = nw_ref[...]
    for n, (i, j) in enumerate(units):
        rows = slice(i * L, (i + 1) * L)
        o = ec[n] * ksqs[n][n2:] + _mxu(qk[n], u[n], _NN)
        for hh, oh in ((2 * j, o[:L]), (2 * j + 1, o[L:])):
            cols = slice(hh * d, (hh + 1) * d)
            y = oh * lax.rsqrt(jnp.mean(oh * oh, axis=-1, keepdims=True) + 1e-6) * nw
            mix_ref[rows, cols] = (y * _silu(cz_ref[rows, cols])).astype(BF16)
        new = s_old[n] * elast[n] + _mxu(kdec[n], u[n], _TN)
        S[i, 2 * j] = new[:d]
        S[i, 2 * j + 1] = new[d:]
    if nchunk > 1:
        for i in range(nseq):
            tail[i] = qkv_ref[i * L + L - 8:(i + 1) * L, :]

    @pl.when(ci == nchunk - 1)
    def _():
        s_ref[...] = S[...]
        for i in range(nseq):
            conv_out_ref[i] = qkv_ref[i * L + t_real - (CONV_W - 1):i * L + t_real, :]


def _gdn(h, mix_prev, grp, P, st_in, conv_in, s_prev, p):
    L = grp.chunk
    nchunk = grp.t_rows // L
    tail = (GDN_HEADS, GDN_HEAD_DIM, GDN_HEAD_DIM)
    ns = grp.nseq
    s_ins, s_specs, s_out, s_shape = _state_io(tail, p, st_in, grp)
    cw = 3 * GDN_WIDTH
    full = lambda shape: pl.BlockSpec(shape, lambda b, c: (0,) * len(shape))
    ins = [h, h, h, P['gdn_conv_w'], P['gdn_a_log'].reshape(1, GDN_HEADS), P['gdn_dt_bias'].reshape(1, GDN_HEADS),
           P['gdn_norm'].reshape(1, GDN_HEAD_DIM)]
    specs = [grp.spec(L, cw, ODD_OFF[0]), grp.spec(L, GDN_WIDTH, ODD_OFF[1]), grp.spec(L, 128, ODD_OFF[2]),
             full((CONV_W, cw)), full((1, GDN_HEADS)), full((1, GDN_HEADS)), full((1, GDN_HEAD_DIM))]
    if st_in is not None:
        ins += [conv_in] + s_ins
        specs += [pl.BlockSpec((None, ns, CONV_W - 1, cw), lambda b, c: (p, b, 0, 0))] + s_specs
    any_spec = pl.BlockSpec(memory_space=pl.ANY)
    alias = {}
    for prev, out_idx in ((mix_prev, 0), (s_prev, 1)):
        if prev is not None:
            alias[len(ins)] = out_idx
            ins.append(prev)
            specs.append(any_spec)
    n_prev = (mix_prev is not None) + (s_prev is not None)
    return pl.pallas_call(
        functools.partial(_gdn_kernel, L=L, nchunk=nchunk, rounds=int(math.log2(L)), t_real=grp.t_real or L,
                          nseq=ns, has_state=st_in is not None, has_prev=n_prev),
        grid=(grp.steps, nchunk),
        in_specs=specs,
        out_specs=[grp.spec(L, GDN_WIDTH, 0), s_out,
                   pl.BlockSpec((ns, CONV_W - 1, cw), lambda b, c: (b, 0, 0))],
        out_shape=[jax.ShapeDtypeStruct((h.shape[0], MIX_WIDTH), BF16), s_shape,
                   jax.ShapeDtypeStruct((grp.bsz, CONV_W - 1, cw), F32)],
        scratch_shapes=[pltpu.VMEM((ns,) + tail, F32), pltpu.VMEM((ns, 8, cw), F32)],
        input_output_aliases=alias,
        compiler_params=_cparams(("parallel", "arbitrary")),
        name="gdn",
    )(*ins)


RWKV_PAIRS = RWKV_HEADS // 2
RWKV_PW = 2 * RWKV_HEAD_DIM


def _rwkv_kernel(r_ref, k_ref, v_ref, xwa_ref, gate_ref, mu_ref, w0_ref, w2_ref, a0_ref, a2_ref, kkp_ref, ka_ref,
                 rk_ref, lng_ref, lnb_ref, *rest, L, nchunk, rounds, t_real, nseq, has_state, has_prev):
    n_opt = 2 * has_state + has_prev
    mix_ref, s_ref, shift_out_ref, S, last = rest[n_opt:]
    ci = pl.program_id(1)
    n = RWKV_HEAD_DIM
    w3 = 3 * RWKV_WIDTH
    cat = jnp.concatenate
    pairs = range(RWKV_PAIRS)

    @pl.when(ci == 0)
    def _():
        last[...] = jnp.zeros_like(last)
        if has_state:
            zn = jnp.zeros((n, n), F32)
            for i in range(nseq):
                last[i, 7:8, :] = rest[0][i]
                for j in pairs:
                    S[i * RWKV_PAIRS + j] = cat([cat([rest[1][i, 2 * j], zn], axis=1),
                                                 cat([zn, rest[1][i, 2 * j + 1]], axis=1)], axis=0)
        else:
            S[...] = jnp.zeros_like(S)

    n2 = 2 * L
    row = lax.broadcasted_iota(jnp.int32, (L, L), 0)
    col = lax.broadcasted_iota(jnp.int32, (L, L), 1)
    tril = (col <= row).astype(F32)
    r2 = lax.broadcasted_iota(jnp.int32, (n2, n2), 0)
    c2 = lax.broadcasted_iota(jnp.int32, (n2, n2), 1)
    same = (r2 >= L) == (c2 >= L)
    strict = same & (c2 < r2)
    incl = same & (c2 <= r2)
    eye = (r2 == c2).astype(F32)
    lane = lax.broadcasted_iota(jnp.int32, (L, RWKV_PW), 1)
    lo = lane < n

    def stack(x):
        return cat([jnp.where(lo, x, 0.0), jnp.where(lo, 0.0, x)], axis=0)

    row8 = lax.broadcasted_iota(jnp.int32, (8, RWKV_PW), 0)
    valid = lax.broadcasted_iota(jnp.int32, (L, RWKV_PW), 0) < t_real

    def seg_sum(x):
        s_lo = jnp.sum(jnp.where(lo, x, 0.0), axis=-1, keepdims=True)
        s_hi = jnp.sum(jnp.where(lo, 0.0, x), axis=-1, keepdims=True)
        return jnp.where(lo, s_lo, s_hi)

    def shift_mix(ref, i, c_src, c_all):
        x = ref[i * L:(i + 1) * L, c_src:c_src + RWKV_PW]
        prev = _shifted(x, last[i, :, c_all:c_all + RWKV_PW], 1, row8)
        return x + (prev - x) * mu_ref[:, c_all:c_all + RWKV_PW]

    xwa = cat([shift_mix(xwa_ref, i, 0, w3) for i in range(nseq)], axis=0)
    lr_w_all = _mxu_f32(jnp.tanh(xwa), w2_ref[...], _NN)
    lr_a_all = _mxu_f32(xwa, a2_ref[...], _NN)
    units = [(i, j) for i in range(nseq) for j in pairs]
    a_ak, a_rk, a_rb, nn, tt, sread, kdbd, egl, vs, bonus = [], [], [], [], [], [], [], [], [], []
    for i, j in units:
        sl = slice(j * RWKV_PW, (j + 1) * RWKV_PW)
        r = shift_mix(r_ref, i, j * RWKV_PW, j * RWKV_PW)
        k = shift_mix(k_ref, i, j * RWKV_PW, RWKV_WIDTH + j * RWKV_PW)
        v = shift_mix(v_ref, i, j * RWKV_PW, 2 * RWKV_WIDTH + j * RWKV_PW)
        w_log = -_softplus(-(w0_ref[:, sl] + lr_w_all[i * L:(i + 1) * L, sl])) - 0.5
        lw = -jnp.exp(w_log)
        a7 = _sigmoid(a0_ref[:, sl] + lr_a_all[i * L:(i + 1) * L, sl])
        kx = k * kkp_ref[:, sl]
        kk = kx * lax.rsqrt(seg_sum(kx * kx) + 1e-6)
        k = k * (1.0 + (a7 - 1.0) * ka_ref[:, sl])
        if t_real < L:
            lw, kk, k, v = (jnp.where(valid, a, 0.0) for a in (lw, kk, k, v))
        b = kk * a7
        bonus.append(seg_sum(r * k * rk_ref[:, sl]) * v)
        vs.append(stack(v))
        g = _mxu_f32(tril, lw, _NN)
        gp = g - lw
        gm = g[L // 2 - 1:L // 2, :]
        gl = g[L - 1:L, :]
        e_neg = jnp.exp(gm - g)
        lhs = cat([stack(kk * jnp.exp(gp - gm)), stack(r * jnp.exp(g - gm))], axis=0)
        rhs = cat([stack(b * e_neg), stack(k * e_neg)], axis=0)
        full = _mxu(lhs, rhs, _NT)
        a_ab = jnp.where(strict, full[:n2, :n2], 0.0)
        a_ak.append(jnp.where(strict, full[:n2, n2:], 0.0))
        a_rb.append(jnp.where(incl, full[n2:, :n2], 0.0))
        a_rk.append(jnp.where(incl, full[n2:, n2:], 0.0))
        nn.append(-a_ab)
        tt.append(eye - a_ab)
        sread.append(cat([stack(kk * jnp.exp(gp)), stack(r * jnp.exp(g))], axis=0))
        dec = jnp.exp(gl - g)
        kdbd.append(cat([stack(k * dec), stack(-b * dec)], axis=0))
        egl.append(jnp.exp(gl))
    tt = _neumann_inverse(nn, tt, n2, rounds)
    s_old, sr, av, u = [], [], [], []
    for m, (i, j) in enumerate(units):
        s_old.append(S[i * RWKV_PAIRS + j])
        sr.append(_mxu(sread[m], s_old[m], _NT))
        av.append(_mxu(cat([a_ak[m], a_rk[m]], axis=0), vs[m], _NN))
    for m in range(len(units)):
        u.append(_mxu(tt[m], sr[m][:n2] + av[m][:n2], _NN))
    for m, (i, j) in enumerate(units):
        sl = slice(j * RWKV_PW, (j + 1) * RWKV_PW)
        rows = slice(i * L, (i + 1) * L)
        o = sr[m][n2:] + av[m][n2:] - _mxu(a_rb[m], u[m], _NN)
        o = o[:L] + o[L:]
        oc = o - seg_sum(o) * (1.0 / n)
        gn = oc * lax.rsqrt(seg_sum(oc * oc) * (1.0 / n) + RWKV_GN_EPS)
        y = gn * lng_ref[:, sl] + lnb_ref[:, sl] + bonus[m]
        mix_ref[rows, sl] = (y * _silu(gate_ref[rows, sl])).astype(BF16)
        S[i * RWKV_PAIRS + j] = s_old[m] * egl[m] + _mxu(cat([vs[m], u[m]], axis=0), kdbd[m], _TN)
    pieces = ((r_ref, 0, RWKV_WIDTH), (k_ref, RWKV_WIDTH, RWKV_WIDTH), (v_ref, 2 * RWKV_WIDTH, RWKV_WIDTH),
              (xwa_ref, w3, RWKV_PW))
    if nchunk > 1:
        for i in range(nseq):
            for ref, c0, wd in pieces:
                last[i, :, c0:c0 + wd] = ref[i * L + L - 8:(i + 1) * L, :]

    @pl.when(ci == nchunk - 1)
    def _():
        for i in range(nseq):
            for ref, c0, wd in pieces:
                shift_out_ref[i, :, c0:c0 + wd] = ref[i * L + t_real - 1:i * L + t_real, :]
            for j in pairs:
                s_ref[i, 2 * j] = S[i * RWKV_PAIRS + j][:n, :n]
                s_ref[i, 2 * j + 1] = S[i * RWKV_PAIRS + j][n:, n:]


def _rwkv(h, mix_prev, grp, P, st_in, shift_in, s_prev, p):
    L = grp.chunk
    nchunk = grp.t_rows // L
    tail = (RWKV_HEADS, RWKV_HEAD_DIM, RWKV_HEAD_DIM)
    ns = grp.nseq
    s_ins, s_specs, s_out, s_shape = _state_io(tail, p, st_in, grp)
    full = lambda shape: pl.BlockSpec(shape, lambda b, c: (0,) * len(shape))
    row = lambda a: a.reshape(1, -1)
    zr = jnp.zeros((RWKV_HEAD_DIM, RWKV_WIDTH), F32)
    w2 = jnp.concatenate([P['rwkv_w2'], zr], axis=0)
    a2 = jnp.concatenate([zr, P['rwkv_a2']], axis=0)
    off = ODD_OFF[4]
    ins = [h, h, h, h, h, row(P['rwkv_mu']), row(P['rwkv_w0']), w2, row(P['rwkv_a0']), a2, row(P['rwkv_kk']),
           row(P['rwkv_ka']), row(P['rwkv_rk']), row(P['rwkv_ln_g']), row(P['rwkv_ln_b'])]
    vec = full((1, RWKV_WIDTH))
    specs = [grp.spec(L, RWKV_WIDTH, off), grp.spec(L, RWKV_WIDTH, off + RWKV_WIDTH),
             grp.spec(L, RWKV_WIDTH, off + 2 * RWKV_WIDTH), grp.spec(L, RWKV_PW, off + 3 * RWKV_WIDTH),
             grp.spec(L, RWKV_WIDTH, ODD_OFF[5]), full((1, RWKV_SHIFT_DIM)), vec, full((RWKV_PW, RWKV_WIDTH)), vec,
             full((RWKV_PW, RWKV_WIDTH)), vec, vec, vec, vec, vec]
    if st_in is not None:
        ins += [shift_in] + s_ins
        specs += [pl.BlockSpec((None, ns, 1, RWKV_SHIFT_DIM), lambda b, c: (p, b, 0, 0))] + s_specs
    any_spec = pl.BlockSpec(memory_space=pl.ANY)
    alias = {}
    for prev, out_idx in ((mix_prev, 0), (s_prev, 1)):
        if prev is not None:
            alias[len(ins)] = out_idx
            ins.append(prev)
            specs.append(any_spec)
    n_prev = (mix_prev is not None) + (s_prev is not None)
    return pl.pallas_call(
        functools.partial(_rwkv_kernel, L=L, nchunk=nchunk, rounds=int(math.log2(L)), t_real=grp.t_real or L,
                          nseq=ns, has_state=st_in is not None, has_prev=n_prev),
        grid=(grp.steps, nchunk),
        in_specs=specs,
        out_specs=[grp.spec(L, RWKV_WIDTH, GDN_WIDTH), s_out,
                   pl.BlockSpec((ns, 1, RWKV_SHIFT_DIM), lambda b, c: (b, 0, 0))],
        out_shape=[jax.ShapeDtypeStruct((h.shape[0], MIX_WIDTH), BF16), s_shape,
                   jax.ShapeDtypeStruct((grp.bsz, 1, RWKV_SHIFT_DIM), F32)],
        scratch_shapes=[pltpu.VMEM((ns * RWKV_PAIRS, RWKV_PW, RWKV_PW), F32),
                        pltpu.VMEM((ns, 8, RWKV_SHIFT_DIM), F32)],
        input_output_aliases=alias,
        compiler_params=_cparams(("parallel", "arbitrary")),
        name="rwkv7",
    )(*ins)


def _pad_t(a, t_to):
    t = a.shape[1]
    if t == t_to:
        return a
    return jnp.pad(a, [(0, 0), (0, t_to - t)] + [(0, 0)] * (a.ndim - 2))


def _even_mix(h, mix, grp, mem_k, mem_v, layer, st_in, st_prev, conv_in, P):
    p = layer // 2
    mix, s_ssd, s_conv = _ssd(h, mix, grp, P, st_in['ssd'], conv_in, st_prev['ssd'], p)
    gla_l, gla_tb = (GLA_CHUNK, 256) if grp.t_real is None else (grp.chunk, grp.chunk)
    mix, s_gla = _gla(h, mix, grp, P, st_in['gla'], st_prev['gla'], p, gla_l, gla_tb)
    mix = _mem_attention(h, mix, grp, EVEN_OFF[8], EVEN_OFF[9], mem_k, mem_v, layer, min(512, grp.t_rows))
    return mix, dict(gla=s_gla, ssd=s_ssd), s_conv


def _odd_mix(h, mix, grp, mem_k, mem_v, layer, st_in, st_prev, conv_in, shift_in, P):
    p = layer // 2
    mix, s_gdn, s_conv = _gdn(h, mix, grp, P, st_in['gdn'], conv_in, st_prev['gdn'], p)
    mix, s_rwkv, s_shift = _rwkv(h, mix, grp, P, st_in['rwkv'], shift_in, st_prev['rwkv'], p)
    mix = _mem_attention(h, mix, grp, ODD_OFF[6], ODD_OFF[7], mem_k, mem_v, layer, min(512, grp.t_rows))
    return mix, dict(gdn=s_gdn, rwkv=s_rwkv), s_conv, s_shift.reshape(grp.bsz, RWKV_SHIFT_DIM)


def kernel(x_prompt, x_sample, mem_prompt, cache_mem_k, cache_mem_v, state_gla, state_ssd, state_ssd_conv, state_gdn, state_gdn_conv, state_rwkv, state_rwkv_shift, mem_w_kv, ev_w_in, ev_gla_w2, ev_gla_b, ev_gla_norm, ev_ssd_conv_w, ev_ssd_conv_b, ev_ssd_dt_bias, ev_ssd_a_log, ev_ssd_d, ev_ssd_norm, ev_w_out, ev_ln_g, ev_ln_b, od_w_in, od_gdn_conv_w, od_gdn_dt_bias, od_gdn_a_log, od_gdn_norm, od_rwkv_mu, od_rwkv_w0, od_rwkv_w2, od_rwkv_a0, od_rwkv_a2, od_rwkv_kk, od_rwkv_ka, od_rwkv_rk, od_rwkv_ln_g, od_rwkv_ln_b, od_w_out, od_ln_g, od_ln_b):
    ev = dict(w_in=ev_w_in, gla_w2=ev_gla_w2, gla_b=ev_gla_b, gla_norm=ev_gla_norm,
              ssd_conv_w=ev_ssd_conv_w, ssd_conv_b=ev_ssd_conv_b, ssd_dt_bias=ev_ssd_dt_bias,
              ssd_a_log=ev_ssd_a_log, ssd_d=ev_ssd_d, ssd_norm=ev_ssd_norm,
              w_out=ev_w_out, ln_g=ev_ln_g, ln_b=ev_ln_b)
    od = dict(w_in=od_w_in, gdn_conv_w=od_gdn_conv_w, gdn_dt_bias=od_gdn_dt_bias, gdn_a_log=od_gdn_a_log,
              gdn_norm=od_gdn_norm, rwkv_mu=od_rwkv_mu, rwkv_w0=od_rwkv_w0, rwkv_w2=od_rwkv_w2,
              rwkv_a0=od_rwkv_a0, rwkv_a2=od_rwkv_a2, rwkv_kk=od_rwkv_kk, rwkv_ka=od_rwkv_ka,
              rwkv_rk=od_rwkv_rk, rwkv_ln_g=od_rwkv_ln_g, rwkv_ln_b=od_rwkv_ln_b,
              w_out=od_w_out, ln_g=od_ln_g, ln_b=od_ln_b)
    bp, tp, _ = x_prompt.shape
    bs, ts, _ = x_sample.shape
    mp, ms = bp * tp, bs * SMALL_T
    grp_p = _Group(bp, tp, None, 0, GDN_CHUNK)
    grp_s = _Group(bs, SMALL_T, ts, mp, SMALL_T, nseq=SAMPLE_NSEQ)

    w_kv = jnp.moveaxis(mem_w_kv, 0, 1).reshape(D_MODEL, DEPTH * 2 * MEM_WIDTH).astype(BF16)
    kv = _matmul(mem_prompt.reshape(bp * MEM_LEN, D_MODEL).astype(BF16), w_kv, 512, 1024)
    kv6 = kv.reshape(bp, MEM_LEN, DEPTH, 2, MEM_HEADS, MEM_HEAD_DIM)
    mem_k_p = jnp.moveaxis(kv6[:, :, :, 0], 2, 0)
    mem_v_p = jnp.moveaxis(kv6[:, :, :, 1], 2, 0)
    mk_s, mv_s = _cache_view(cache_mem_k), _cache_view(cache_mem_v)

    x = jnp.concatenate([x_prompt.reshape(mp, D_MODEL),
                         _pad_t(x_sample, SMALL_T).reshape(ms, D_MODEL)], axis=0)
    x_bf = x.astype(BF16)
    zp = lambda shape: jnp.zeros(shape, F32)
    names = ('gla', 'ssd', 'gdn', 'rwkv')
    none = {n: None for n in names}
    in_s = dict(gla=state_gla, ssd=state_ssd, gdn=state_gdn, rwkv=state_rwkv)
    shift_s = state_rwkv_shift.reshape(N_PAIRS, bs, 1, RWKV_SHIFT_DIM)
    out_p, out_s = dict(none), dict(none)
    small_p = {n: [] for n in ('ssd_conv', 'gdn_conv', 'rwkv_shift')}
    small_s = {n: [] for n in small_p}
    tm = 1024
    for layer in range(DEPTH):
        p = layer // 2
        if layer % 2 == 0:
            P = {n: w[p] for n, w in ev.items()}
            h = _matmul(x_bf, _pack_w_in(P['w_in'], EVEN_SIZES, EVEN_ORDER, EVEN_N), tm, PROJ_TN)
            mix, new, c1 = _even_mix(h, None, grp_p, kv, kv, layer, none, out_p, None, P)
            out_p.update(new)
            mix, new, c2 = _even_mix(h, mix, grp_s, mk_s, mv_s, layer, in_s, out_s, state_ssd_conv, P)
            out_s.update(new)
            small_p['ssd_conv'].append(c1)
            small_s['ssd_conv'].append(c2)
            w_out = jnp.concatenate([P['w_out'][GLA_WIDTH:GLA_WIDTH + SSD_WIDTH], P['w_out'][:GLA_WIDTH],
                                     P['w_out'][GLA_WIDTH + SSD_WIDTH:]], axis=0)
        else:
            P = {n: w[p] for n, w in od.items()}
            h = _matmul(x_bf, _pack_w_in(P['w_in'], ODD_SIZES, ODD_ORDER, ODD_N), tm, PROJ_TN)
            mix, new, c1, h1 = _odd_mix(h, None, grp_p, kv, kv, layer, none, out_p, None, None, P)
            out_p.update(new)
            mix, new, c2, h2 = _odd_mix(h, mix, grp_s, mk_s, mv_s, layer, in_s, out_s, state_gdn_conv, shift_s, P)
            out_s.update(new)
            small_p['gdn_conv'].append(c1)
            small_s['gdn_conv'].append(c2)
            small_p['rwkv_shift'].append(h1)
            small_s['rwkv_shift'].append(h2)
            w_out = P['w_out']
        x, x_bf = _out_ln(mix, w_out.astype(BF16), x, P['ln_g'], P['ln_b'])

    y_prompt = x[:mp].reshape(bp, tp, D_MODEL)
    y_sample = x[mp:].reshape(bs, SMALL_T, D_MODEL)[:, :ts]
    st = lambda d, n: jnp.stack(d[n])
    return (y_prompt, y_sample, mem_k_p, mem_v_p,
            out_p['gla'], out_s['gla'], out_p['ssd'], out_s['ssd'],
            st(small_p, 'ssd_conv'), st(small_s, 'ssd_conv'), out_p['gdn'], out_s['gdn'],
            st(small_p, 'gdn_conv'), st(small_s, 'gdn_conv'), out_p['rwkv'], out_s['rwkv'],
            st(small_p, 'rwkv_shift'), st(small_s, 'rwkv_shift'))
```

```python
import functools
import math

import numpy as np
import jax
import jax.numpy as jnp
from jax import lax
from jax.experimental import pallas as pl
from jax.experimental.pallas import tpu as pltpu

F32 = jnp.float32
BF16 = jnp.bfloat16
HI = lax.Precision.HIGHEST

D_MODEL = 2048
DEPTH = 4
N_PAIRS = DEPTH // 2
CONV_W = 4
MEM_LEN = 256
MEM_HEADS = 4
MEM_HEAD_DIM = 256
MEM_WIDTH = 1024
GLA_HEADS = 4
GLA_DK = 128
GLA_DV = 256
GLA_QK = 512
GLA_WIDTH = 1024
GLA_RANK = 16
GLA_TAU = 16.0
SSD_WIDTH = 2048
SSD_HEAD_DIM = 64
SSD_HEADS = 32
SSD_GROUPS = 4
SSD_REP = 8
SSD_STATE = 128
SSD_CONV_DIM = SSD_WIDTH + 2 * SSD_GROUPS * SSD_STATE
GDN_WIDTH = 2048
GDN_HEAD_DIM = 128
GDN_HEADS = 16
RWKV_WIDTH = 1024
RWKV_HEAD_DIM = 64
RWKV_HEADS = 16
RWKV_W_RANK = 64
RWKV_A_RANK = 64
RWKV_SHIFT_DIM = 3 * RWKV_WIDTH + RWKV_W_RANK + RWKV_A_RANK
RWKV_GN_EPS = 64e-5
EVEN_SIZES = (GLA_QK, GLA_QK, GLA_WIDTH, GLA_RANK, GLA_WIDTH, SSD_WIDTH, SSD_CONV_DIM, SSD_HEADS,
              MEM_WIDTH, MEM_WIDTH)
ODD_SIZES = (3 * GDN_WIDTH, GDN_WIDTH, GDN_HEADS, GDN_HEADS, RWKV_SHIFT_DIM, RWKV_WIDTH, MEM_WIDTH, MEM_WIDTH)
MIX_WIDTH = 4096
DEEPNORM_ALPHA = (2 * DEPTH) ** 0.25

EVEN_ORDER = (5, 2, 4, 8, 9, 6, 0, 1, 3, 7)
ODD_ORDER = (0, 1, 5, 6, 7, 4, 2, 3)
PROJ_TN = 768
VMEM_LIMIT = 60 * 1024 * 1024

GLA_CHUNK = 16
SSD_CHUNK = 64
GDN_CHUNK = 64
RWKV_CHUNK = 64
SMALL_T = 8
SAMPLE_NSEQ = 4

_NN = ((1,), (0,))
_NT = ((1,), (1,))
_TN = ((0,), (0,))


def _packed_layout(sizes, order):
    offs, o = {}, 0
    for i in order:
        offs[i] = o
        o += sizes[i]
    total = -(-o // PROJ_TN) * PROJ_TN
    return offs, total


EVEN_OFF, EVEN_N = _packed_layout(EVEN_SIZES, EVEN_ORDER)
ODD_OFF, ODD_N = _packed_layout(ODD_SIZES, ODD_ORDER)


def _pack_w_in(w, sizes, order, total):
    ends = np.cumsum(sizes)
    parts = [w[..., ends[i] - sizes[i]:ends[i]].astype(BF16) for i in order]
    used = sum(sizes)
    if total > used:
        parts.append(jnp.zeros(w.shape[:-1] + (total - used,), BF16))
    return jnp.concatenate(parts, axis=-1)


def _seg(h, offs, sizes, i):
    return h[..., offs[i]:offs[i] + sizes[i]]


def _cparams(sem):
    return pltpu.CompilerParams(dimension_semantics=sem, vmem_limit_bytes=VMEM_LIMIT)


def _mxu(a, b, dims):
    return lax.dot_general(a.astype(BF16), b.astype(BF16), (dims, ((), ())), preferred_element_type=F32)


def _mxu_f32(a, b, dims):
    return lax.dot_general(a, b, (dims, ((), ())), precision=HI, preferred_element_type=F32)


def _sigmoid(x):
    return 1.0 / (1.0 + jnp.exp(-x))


def _silu(x):
    return x * _sigmoid(x)


def _softplus(x):
    return jnp.maximum(x, 0.0) + jnp.log(1.0 + jnp.exp(-jnp.abs(x)))


def _shifted(u, prev8, j, row8):
    ru = pltpu.roll(u, j, 0)
    top = jnp.where(row8 < j, pltpu.roll(prev8, j, 0), ru[:8])
    return top if u.shape[0] == 8 else jnp.concatenate([top, ru[8:]], axis=0)


class _Group:
    def __init__(self, bsz, t_rows, t_real, row0, chunk, nseq=1):
        assert nseq == 1 or t_rows == chunk
        self.bsz, self.t_rows, self.t_real, self.row0, self.chunk, self.nseq = bsz, t_rows, t_real, row0, chunk, nseq
        self.steps = bsz // nseq

    def spec(self, rows, width, off):
        rows = rows * self.nseq
        assert off % width == 0 and self.row0 % rows == 0 and (self.t_rows * self.nseq) % rows == 0
        base, per, cb = self.row0 // rows, self.t_rows * self.nseq // rows, off // width
        return pl.BlockSpec((rows, width), lambda b, i: (base + b * per + i, cb))


def _alias_last(n_inputs, has_prev, out_index=0):
    return {n_inputs - 1: out_index} if has_prev else {}


def _neumann_inverse(nn, tt, n, rounds):
    idx = range(len(nn))
    if rounds >= 2:
        for j in idx:
            nn[j] = _mxu(nn[j], nn[j], _NN)
        for _ in range(rounds - 2):
            for j in idx:
                both = _mxu(nn[j], jnp.concatenate([tt[j], nn[j]], axis=1), _NN)
                tt[j] = tt[j] + both[:, :n]
                nn[j] = both[:, n:]
        for j in idx:
            tt[j] = tt[j] + _mxu(nn[j], tt[j], _NN)
    return tt


def _state_io(tail, p, s_in, grp):
    zeros = (0,) * len(tail)
    spec = pl.BlockSpec((None, grp.nseq) + tail, lambda *g: (p, g[0]) + zeros)
    ins, specs = ([s_in], [spec]) if s_in is not None else ([], [])
    shape = jax.ShapeDtypeStruct((N_PAIRS, grp.bsz) + tail, F32)
    return ins, specs, spec, shape


def _mm_kernel(x_ref, w_ref, o_ref, *, precision):
    o_ref[...] = jnp.dot(x_ref[...], w_ref[...], preferred_element_type=F32, precision=precision)


def _matmul(x, w, tm, tn, precision=None, p=None):
    m, k = x.shape
    n = w.shape[-1]
    assert m % tm == 0 and n % tn == 0
    w_spec = (pl.BlockSpec((k, tn), lambda j, i: (0, j)) if p is None
              else pl.BlockSpec((None, k, tn), lambda j, i: (p, 0, j)))
    return pl.pallas_call(
        functools.partial(_mm_kernel, precision=precision),
        grid=(n // tn, m // tm),
        in_specs=[pl.BlockSpec((tm, k), lambda j, i: (i, 0)), w_spec],
        out_specs=pl.BlockSpec((tm, tn), lambda j, i: (i, j)),
        out_shape=jax.ShapeDtypeStruct((m, n), F32),
        compiler_params=_cparams(("parallel", "parallel")),
        name="matmul",
    )(x, w)


def _out_ln_kernel(mix_ref, w_ref, x_ref, g_ref, b_ref, y_ref, ybf_ref, acc, *, nk):
    kk = pl.program_id(1)

    @pl.when(kk == 0)
    def _():
        acc[...] = jnp.zeros_like(acc)

    acc[...] += jnp.dot(mix_ref[...].astype(BF16), w_ref[...], preferred_element_type=F32)

    @pl.when(kk == nk - 1)
    def _():
        z = DEEPNORM_ALPHA * x_ref[...] + acc[...]
        zc = z - jnp.mean(z, axis=-1, keepdims=True)
        var = jnp.mean(zc * zc, axis=-1, keepdims=True)
        y = zc * lax.rsqrt(var + 1e-5) * g_ref[...] + b_ref[...]
        y_ref[...] = y
        ybf_ref[...] = y.astype(BF16)


def _out_ln(mix, w, p, x, g, b, tm=512, tk=2048):
    m, k = mix.shape
    d = w.shape[-1]
    nk = k // tk
    return pl.pallas_call(
        functools.partial(_out_ln_kernel, nk=nk),
        grid=(m // tm, nk),
        in_specs=[pl.BlockSpec((tm, tk), lambda i, j: (i, j)),
                  pl.BlockSpec((None, tk, d), lambda i, j: (p, j, 0)),
                  pl.BlockSpec((tm, d), lambda i, j: (i, 0)),
                  pl.BlockSpec((1, d), lambda i, j: (0, 0)),
                  pl.BlockSpec((1, d), lambda i, j: (0, 0))],
        out_specs=[pl.BlockSpec((tm, d), lambda i, j: (i, 0)),
                   pl.BlockSpec((tm, d), lambda i, j: (i, 0))],
        out_shape=[jax.ShapeDtypeStruct((m, d), F32), jax.ShapeDtypeStruct((m, d), BF16)],
        scratch_shapes=[pltpu.VMEM((tm, d), F32)],
        compiler_params=_cparams(("parallel", "arbitrary")),
        name="out_ln",
    )(mix, w, x, g.reshape(1, d), b.reshape(1, d))


def _mem_kernel(q_ref, gate_ref, k_ref, v_ref, *rest):
    o_ref = rest[-1]
    for h in range(MEM_HEADS):
        sl = slice(h * MEM_HEAD_DIM, (h + 1) * MEM_HEAD_DIM)
        k = k_ref[:, sl]
        v = v_ref[:, sl]
        s = _mxu(q_ref[:, sl], k, _NT) * MEM_HEAD_DIM ** -0.5
        p = jnp.exp(s - jnp.max(s, axis=-1, keepdims=True))
        p = p / jnp.sum(p, axis=-1, keepdims=True)
        o_ref[:, sl] = (_mxu(p, v, _NN) * _silu(gate_ref[:, sl])).astype(BF16)


MEM_DT = MEM_HEAD_DIM // 128
MEM_ROWS = MEM_LEN * MEM_DT * MEM_HEADS


def _cache_view(c):
    d, b = c.shape[:2]
    c = c.reshape(d, b, MEM_LEN, MEM_HEADS, MEM_DT, 128)
    return jnp.transpose(c, (0, 1, 2, 4, 3, 5)).reshape(d, b, MEM_ROWS, 128)


def _mem_cache_kernel(q_ref, gate_ref, k_ref, v_ref, *rest):
    o_ref = rest[-1]
    nseq = k_ref.shape[0]
    t = q_ref.shape[0] // nseq
    grp = MEM_DT * MEM_HEADS
    col = lax.broadcasted_iota(jnp.int32, (MEM_HEADS * t, MEM_ROWS), 1) % grp
    head = lax.broadcasted_iota(jnp.int32, (MEM_HEADS * t, MEM_ROWS), 0) // t
    for i in range(nseq):
        rows = slice(i * t, (i + 1) * t)
        k = k_ref[i]
        v = v_ref[i]
        qs = [jnp.concatenate([q_ref[rows, h * MEM_HEAD_DIM + dt * 128:h * MEM_HEAD_DIM + (dt + 1) * 128]
                               for h in range(MEM_HEADS)], axis=0) for dt in range(MEM_DT)]
        s = _mxu(qs[0], k, _NT)
        for dt in range(1, MEM_DT):
            s = s + pltpu.roll(_mxu(qs[dt], k, _NT), MEM_ROWS - dt * MEM_HEADS, 1)
        s = jnp.where(col == head, s * MEM_HEAD_DIM ** -0.5, -jnp.inf)
        p = jnp.exp(s - jnp.max(s, axis=-1, keepdims=True))
        p = p / jnp.sum(p, axis=-1, keepdims=True)
        for dt in range(MEM_DT):
            o = _mxu(p if dt == 0 else pltpu.roll(p, dt * MEM_HEADS, 1), v, _NN)
            for h in range(MEM_HEADS):
                sl = slice(h * MEM_HEAD_DIM + dt * 128, h * MEM_HEAD_DIM + (dt + 1) * 128)
                o_ref[rows, sl] = (o[h * t:(h + 1) * t] * _silu(gate_ref[rows, sl])).astype(BF16)


def _mem_attention(h, mix_prev, grp, q_off, gate_off, mem_k, mem_v, layer, tq):
    cached = mem_k.ndim == 4
    if cached:
        kv_specs = [pl.BlockSpec((None, grp.nseq, MEM_ROWS, 128), lambda b, i: (layer, b, 0, 0))] * 2
    else:
        kv_specs = [pl.BlockSpec((MEM_LEN, MEM_WIDTH), lambda b, i: (b, 2 * layer)),
                    pl.BlockSpec((MEM_LEN, MEM_WIDTH), lambda b, i: (b, 2 * layer + 1))]
    ins = [h, h, mem_k, mem_v] + ([] if mix_prev is None else [mix_prev])
    specs = [grp.spec(tq, MEM_WIDTH, q_off), grp.spec(tq, MEM_WIDTH, gate_off)] + kv_specs
    if mix_prev is not None:
        specs.append(pl.BlockSpec(memory_space=pl.ANY))
    return pl.pallas_call(
        _mem_cache_kernel if cached else _mem_kernel,
        grid=(grp.steps, grp.t_rows // tq),
        in_specs=specs,
        out_specs=grp.spec(tq, MEM_WIDTH, MIX_WIDTH - MEM_WIDTH),
        out_shape=jax.ShapeDtypeStruct((h.shape[0], MIX_WIDTH), BF16),
        input_output_aliases=_alias_last(len(ins), mix_prev is not None),
        compiler_params=_cparams(("parallel", "parallel")),
        name="mem_attention",
    )(*ins)


def _gla_kernel(q_ref, k_ref, v_ref, gate_ref, sm_ref, w2_ref, gb_ref, nw_ref, *rest, L, nblk, t_real, nseq,
                has_state, has_prev):
    n_opt = has_state + has_prev
    mix_ref, s_ref, ST, QK, B, OI = rest[n_opt:]
    tb = pl.program_id(1)
    rows_blk = q_ref.shape[0]

    @pl.when(tb == 0)
    def _():
        for i in range(nseq):
            for h in range(GLA_HEADS):
                ST[i * GLA_HEADS + h] = rest[0][i, h].T if has_state else jnp.zeros((GLA_DV, GLA_DK), F32)

    z = _mxu_f32(sm_ref[...], w2_ref[...], _NN) + gb_ref[...]
    g_all = -_softplus(-z) * (1.0 / GLA_TAU)
    t_i = lax.broadcasted_iota(jnp.int32, (rows_blk, GLA_DK), 0)
    t_c = t_i & (L - 1)
    nw = nw_ref[...]
    for h in range(GLA_HEADS):
        ks = slice(h * GLA_DK, (h + 1) * GLA_DK)
        vs = slice(h * GLA_DV, (h + 1) * GLA_DV)
        q = q_ref[:, ks] * GLA_DK ** -0.5
        k = k_ref[:, ks]
        b = g_all[:, ks]
        v = v_ref[:, vs]
        if t_real < L:
            b = jnp.where(t_c < t_real, b, 0.0)
            k = jnp.where(t_c < t_real, k, 0.0)
        sh = 1
        while sh < L:
            b = b + jnp.where(t_c >= sh, pltpu.roll(b, sh, 0), 0.0)
            sh *= 2
        o = jnp.sum(q * k, axis=-1, keepdims=True) * v
        for j in range(1, L):
            d = jnp.where(t_c >= j, b - pltpu.roll(b, j, 0), -jnp.inf)
            p = jnp.exp(d) * q * pltpu.roll(k, j, 0)
            o = o + jnp.sum(p, axis=-1, keepdims=True) * pltpu.roll(v, j, 0)
        OI[:, vs] = o
        B[:, ks] = b
        QK[:, ks] = q
        QK[:, GLA_QK + h * GLA_DK:GLA_QK + (h + 1) * GLA_DK] = k
    for c in range(rows_blk // L):
        rows = slice(c * L, (c + 1) * L)
        for h in range(GLA_HEADS):
            ks = slice(h * GLA_DK, (h + 1) * GLA_DK)
            vs = slice(h * GLA_DV, (h + 1) * GLA_DV)
            si = (c if nseq > 1 else 0) * GLA_HEADS + h
            b = B[rows, ks]
            b_last = b[L - 1:L, :]
            st = ST[si]
            o = OI[rows, vs] + _mxu(QK[rows, ks] * jnp.exp(b), st, _NT)
            y = o * lax.rsqrt(jnp.mean(o * o, axis=-1, keepdims=True) + 1e-6) * nw
            mix_ref[rows, vs] = (y * _silu(gate_ref[rows, vs])).astype(BF16)
            kd = QK[rows, GLA_QK + h * GLA_DK:GLA_QK + (h + 1) * GLA_DK] * jnp.exp(b_last - b)
            ST[si] = st * jnp.exp(b_last) + _mxu(v_ref[rows, vs], kd, _TN)

    @pl.when(tb == nblk - 1)
    def _():
        for i in range(nseq):
            for h in range(GLA_HEADS):
                s_ref[i, h] = ST[i * GLA_HEADS + h].T


def _gla(h, mix_prev, grp, P, st_in, s_prev, p, L, tb):
    nblk = grp.t_rows // tb
    tail = (GLA_HEADS, GLA_DK, GLA_DV)
    rows_blk = tb * grp.nseq
    s_ins, s_specs, s_out, s_shape = _state_io(tail, p, st_in, grp)
    full = lambda shape: pl.BlockSpec(shape, lambda b, c: (0,) * len(shape))
    w2 = jnp.concatenate([P['gla_w2'], jnp.zeros((128 - GLA_RANK, GLA_QK), F32)], axis=0)
    ins = [h, h, h, h, h, w2, P['gla_b'].reshape(1, GLA_QK), P['gla_norm'].reshape(1, GLA_DV)]
    specs = [grp.spec(tb, GLA_QK, EVEN_OFF[0]), grp.spec(tb, GLA_QK, EVEN_OFF[1]), grp.spec(tb, GLA_WIDTH, EVEN_OFF[2]),
             grp.spec(tb, GLA_WIDTH, EVEN_OFF[4]), grp.spec(tb, 128, EVEN_OFF[3]),
             full((128, GLA_QK)), full((1, GLA_QK)), full((1, GLA_DV))]
    ins += s_ins
    specs += s_specs
    any_spec = pl.BlockSpec(memory_space=pl.ANY)
    alias = {}
    for prev, out_idx in ((mix_prev, 0), (s_prev, 1)):
        if prev is not None:
            alias[len(ins)] = out_idx
            ins.append(prev)
            specs.append(any_spec)
    n_prev = (mix_prev is not None) + (s_prev is not None)
    return pl.pallas_call(
        functools.partial(_gla_kernel, L=L, nblk=nblk, t_real=grp.t_real or L, nseq=grp.nseq,
                          has_state=st_in is not None, has_prev=n_prev),
        grid=(grp.steps, nblk),
        in_specs=specs,
        out_specs=[grp.spec(tb, GLA_WIDTH, SSD_WIDTH), s_out],
        out_shape=[jax.ShapeDtypeStruct((h.shape[0], MIX_WIDTH), BF16), s_shape],
        scratch_shapes=[pltpu.VMEM((grp.nseq * GLA_HEADS, GLA_DV, GLA_DK), F32),
                        pltpu.VMEM((rows_blk, 2 * GLA_QK), F32), pltpu.VMEM((rows_blk, GLA_QK), F32),
                        pltpu.VMEM((rows_blk, GLA_WIDTH), F32)],
        input_output_aliases=alias,
        compiler_params=_cparams(("parallel", "arbitrary")),
        name="gla",
    )(*ins)


SSD_GW = SSD_REP * SSD_HEAD_DIM


def _ssd_lanes(L):
    return max(SSD_REP * L, 128)


def _ssd_kernel(sz_ref, xbc_ref, sm_ref, cw_ref, cbias_ref, dtb_ref, alog_ref, dvec_ref, nw_ref, ep_ref, es_ref, *rest,
                L, nchunk, t_real, nseq, has_state, has_prev):
    n_opt = 2 * has_state + has_prev
    mix_ref, s_ref, conv_out_ref, ST, tail = rest[n_opt:]
    ci = pl.program_id(1)
    gs = _ssd_lanes(L)
    cat = jnp.concatenate

    @pl.when(ci == 0)
    def _():
        tail[...] = jnp.zeros_like(tail)
        for i in range(nseq):
            for g in range(SSD_GROUPS):
                if has_state:
                    ST[i * SSD_GROUPS + g] = rest[1][i, g * SSD_REP:(g + 1) * SSD_REP].reshape(SSD_GW, SSD_STATE).T
                else:
                    ST[i * SSD_GROUPS + g] = jnp.zeros((SSD_STATE, SSD_GW), F32)
            if has_state:
                tail[i, 8 - (CONV_W - 1):8, :] = rest[0][i]

    d = 128
    row8 = lax.broadcasted_iota(jnp.int32, (8, d), 0)
    row = lax.broadcasted_iota(jnp.int32, (L, L), 0)
    col = lax.broadcasted_iota(jnp.int32, (L, L), 1)
    tril = (col <= row).astype(F32)
    ep = ep_ref[...]
    t_i = lax.broadcasted_iota(jnp.int32, (L, SSD_GROUPS * gs), 0)
    s_i = lax.broadcasted_iota(jnp.int32, (L, SSD_GROUPS * gs), 1) & (L - 1)
    blk_r = lax.broadcasted_iota(jnp.int32, (gs, SSD_GW), 0) // L
    blk_c = lax.broadcasted_iota(jnp.int32, (gs, SSD_GW), 1) // SSD_HEAD_DIM
    diag = blk_r == blk_c
    reps = SSD_REP * L
    for i in range(nseq):
        rows = slice(i * L, (i + 1) * L)

        def conv_tile(c0):
            u = xbc_ref[rows, c0:c0 + d]
            p8 = tail[i, :, c0:c0 + d]
            w = cw_ref[:, c0:c0 + d]
            acc = u * w[CONV_W - 1:CONV_W] + cbias_ref[:, c0:c0 + d]
            for j in range(1, CONV_W):
                acc = acc + _shifted(u, p8, j, row8) * w[CONV_W - 1 - j:CONV_W - j]
            return _silu(acc)

        dt = _softplus(sm_ref[rows, GLA_RANK:GLA_RANK + SSD_HEADS] + dtb_ref[...])
        if t_real < L:
            dt = jnp.where(lax.broadcasted_iota(jnp.int32, (L, SSD_HEADS), 0) < t_real, dt, 0.0)
        c = _mxu_f32(tril, dt * -jnp.exp(alog_ref[...]), _NN)
        dt_x = _mxu_f32(dt, ep, _NN)
        c_x = _mxu_f32(c, ep, _NN)
        c_s = _mxu_f32(c, es_ref[...], _NN)
        c_src = jnp.sum(jnp.where(t_i == s_i, c_s, 0.0), axis=0, keepdims=True)
        seg = jnp.exp(jnp.where(s_i <= t_i, c_s - c_src, -jnp.inf))
        for g in range(SSD_GROUPS):
            gl = slice(g * SSD_GW, (g + 1) * SSD_GW)
            sx = cat([conv_tile(g * SSD_GW + n * d) for n in range(SSD_GW // d)], axis=1)
            bm = conv_tile(SSD_WIDTH + g * SSD_STATE)
            cm = conv_tile(SSD_WIDTH + SSD_GROUPS * SSD_STATE + g * SSD_STATE)
            xdt = sx * dt_x[:, gl]
            pad_rows = [] if reps == gs else [jnp.zeros((gs - reps, SSD_STATE), F32)]
            cb = _mxu(cm, cat([bm] * SSD_REP + pad_rows, axis=0), _NT)
            pad_rows = [] if reps == gs else [jnp.zeros((gs - reps, SSD_GW), F32)]
            xbd = jnp.where(diag, cat([xdt] * SSD_REP + pad_rows, axis=0), 0.0)
            st = ST[i * SSD_GROUPS + g]
            y = _mxu(cb * seg[:, g * gs:(g + 1) * gs], xbd, _NN) + _mxu(cm, st, _NN) * jnp.exp(c_x[:, gl])
            y = (y + sx * dvec_ref[:, gl]) * _silu(sz_ref[rows, gl])
            y = y * lax.rsqrt(jnp.mean(y * y, axis=-1, keepdims=True) + 1e-6) * nw_ref[:, gl]
            mix_ref[rows, gl] = y.astype(BF16)
            c_end = c_x[L - 1:L, gl]
            ST[i * SSD_GROUPS + g] = st * jnp.exp(c_end) + _mxu(bm, xdt * jnp.exp(c_end - c_x[:, gl]), _TN)
        if nchunk > 1:
            tail[i] = xbc_ref[i * L + L - 8:(i + 1) * L, :]

    @pl.when(ci == nchunk - 1)
    def _():
        for i in range(nseq):
            conv_out_ref[i] = xbc_ref[i * L + t_real - (CONV_W - 1):i * L + t_real, :]
            for g in range(SSD_GROUPS):
                s_ref[i, g * SSD_REP:(g + 1) * SSD_REP] = ST[i * SSD_GROUPS + g].T.reshape(
                    SSD_REP, SSD_HEAD_DIM, SSD_STATE)


def _ssd(h, mix_prev, grp, P, st_in, conv_in, s_prev, p):
    L = grp.chunk
    nchunk = grp.t_rows // L
    gs = _ssd_lanes(L)
    tail = (SSD_HEADS, SSD_HEAD_DIM, SSD_STATE)
    ns = grp.nseq
    s_ins, s_specs, s_out, s_shape = _state_io(tail, p, st_in, grp)
    full = lambda shape: pl.BlockSpec(shape, lambda b, c: (0,) * len(shape))
    heads = jnp.arange(SSD_HEADS)[:, None]
    lane_p = jnp.arange(SSD_WIDTH)[None, :]
    ep = (lane_p // SSD_HEAD_DIM == heads).astype(F32)
    lane_s = jnp.arange(SSD_GROUPS * gs)[None, :]
    in_grp = lane_s % gs
    es = ((in_grp < SSD_REP * L) & ((lane_s // gs) * SSD_REP + in_grp // L == heads)).astype(F32)
    row = lambda a: a.reshape(1, -1)
    ins = [h, h, h, P['ssd_conv_w'], row(P['ssd_conv_b']), row(P['ssd_dt_bias']), row(P['ssd_a_log']),
           row(jnp.repeat(P['ssd_d'], SSD_HEAD_DIM)), row(P['ssd_norm']), ep, es]
    specs = [grp.spec(L, SSD_WIDTH, EVEN_OFF[5]), grp.spec(L, SSD_CONV_DIM, EVEN_OFF[6]), grp.spec(L, 128, EVEN_OFF[3]),
             full((CONV_W, SSD_CONV_DIM)), full((1, SSD_CONV_DIM)), full((1, SSD_HEADS)), full((1, SSD_HEADS)),
             full((1, SSD_WIDTH)), full((1, SSD_WIDTH)), full(ep.shape), full(es.shape)]
    if st_in is not None:
        ins += [conv_in] + s_ins
        specs += [pl.BlockSpec((None, ns, CONV_W - 1, SSD_CONV_DIM), lambda b, c: (p, b, 0, 0))] + s_specs
    any_spec = pl.BlockSpec(memory_space=pl.ANY)
    alias = {}
    for prev, out_idx in ((mix_prev, 0), (s_prev, 1)):
        if prev is not None:
            alias[len(ins)] = out_idx
            ins.append(prev)
            specs.append(any_spec)
    n_prev = (mix_prev is not None) + (s_prev is not None)
    return pl.pallas_call(
        functools.partial(_ssd_kernel, L=L, nchunk=nchunk, t_real=grp.t_real or L, nseq=ns,
                          has_state=st_in is not None, has_prev=n_prev),
        grid=(grp.steps, nchunk),
        in_specs=specs,
        out_specs=[grp.spec(L, SSD_WIDTH, 0), s_out,
                   pl.BlockSpec((ns, CONV_W - 1, SSD_CONV_DIM), lambda b, c: (b, 0, 0))],
        out_shape=[jax.ShapeDtypeStruct((h.shape[0], MIX_WIDTH), BF16), s_shape,
                   jax.ShapeDtypeStruct((grp.bsz, CONV_W - 1, SSD_CONV_DIM), F32)],
        scratch_shapes=[pltpu.VMEM((ns * SSD_GROUPS, SSD_STATE, SSD_GW), F32),
                        pltpu.VMEM((ns, 8, SSD_CONV_DIM), F32)],
        input_output_aliases=alias,
        compiler_params=_cparams(("parallel", "arbitrary")),
        name="ssd",
    )(*ins)


def _gdn_kernel(qkv_ref, cz_ref, sm_ref, cw_ref, alog_ref, dtb_ref, nw_ref, *rest, L, nchunk, rounds, t_real, nseq,
                has_state, has_prev):
    n_opt = 2 * has_state + has_prev
    mix_ref, s_ref, conv_out_ref, S, tail = rest[n_opt:]
    ci = pl.program_id(1)

    @pl.when(ci == 0)
    def _():
        tail[...] = jnp.zeros_like(tail)
        if has_state:
            for i in range(nseq):
                tail[i, 8 - (CONV_W - 1):8, :] = rest[0][i]
            S[...] = rest[1][...]
        else:
            S[...] = jnp.zeros_like(S)

    n2 = 2 * L
    d = GDN_HEAD_DIM
    cat = jnp.concatenate
    row8 = lax.broadcasted_iota(jnp.int32, (8, d), 0)

    def conv_tile(i, c0):
        u = qkv_ref[i * L:(i + 1) * L, c0:c0 + d]
        p8 = tail[i, :, c0:c0 + d]
        w = cw_ref[:, c0:c0 + d]
        acc = u * w[CONV_W - 1:CONV_W]
        for j in range(1, CONV_W):
            acc = acc + _shifted(u, p8, j, row8) * w[CONV_W - 1 - j:CONV_W - j]
        return _silu(acc)

    def l2n(x):
        return x * lax.rsqrt(jnp.sum(x * x, axis=-1, keepdims=True) + 1e-6)

    row = lax.broadcasted_iota(jnp.int32, (L, L), 0)
    col = lax.broadcasted_iota(jnp.int32, (L, L), 1)
    tril = (col <= row).astype(F32)
    r2 = lax.broadcasted_iota(jnp.int32, (n2, n2), 0)
    c2 = lax.broadcasted_iota(jnp.int32, (n2, n2), 1)
    same = (r2 >= L) == (c2 >= L)
    strict = same & (c2 < r2)
    incl = same & (c2 <= r2)
    upper = same & (r2 <= c2)
    eye = (r2 == c2).astype(F32)
    zl = jnp.zeros((L, d), F32)
    units = [(i, j) for i in range(nseq) for j in range(GDN_HEADS // 2)]
    nn, tt, qk, kq, kdec, ec, bcol, elast, vst = [], [], [], [], [], [], [], [], []
    for i in range(nseq):
        sm = sm_ref[i * L:(i + 1) * L, :]
        beta_all = _sigmoid(sm[:, :GDN_HEADS])
        g_all = -jnp.exp(alog_ref[...]) * _softplus(sm[:, GDN_HEADS:2 * GDN_HEADS] + dtb_ref[...])
        if t_real < L:
            valid = lax.broadcasted_iota(jnp.int32, (L, GDN_HEADS), 0) < t_real
            beta_all = jnp.where(valid, beta_all, 0.0)
            g_all = jnp.where(valid, g_all, 0.0)
        c_all = _mxu_f32(tril, g_all, _NN)
        for j in range(GDN_HEADS // 2):
            h0, h1 = 2 * j, 2 * j + 1
            stack_col = lambda a: cat([a[:, h0:h0 + 1], a[:, h1:h1 + 1]], axis=0)
            c_col = stack_col(c_all)
            beta_col = stack_col(beta_all)
            c_row = jnp.sum(jnp.where(upper, stack_col(g_all), 0.0), axis=0, keepdims=True)
            decay = jnp.exp(jnp.where(incl, c_col - c_row, -jnp.inf))
            last = lambda rows: cat([jnp.broadcast_to(c_all[L - 1:L, h0:h0 + 1], (rows, 1)),
                                     jnp.broadcast_to(c_all[L - 1:L, h1:h1 + 1], (rows, 1))], axis=0)
            q0, q1 = (l2n(conv_tile(i, h * d)) * d ** -0.5 for h in (h0, h1))
            k0, k1 = (l2n(conv_tile(i, GDN_WIDTH + h * d)) for h in (h0, h1))
            vst.append(cat([conv_tile(i, 2 * GDN_WIDTH + h0 * d), conv_tile(i, 2 * GDN_WIDTH + h1 * d)], axis=0))
            k_st = cat([cat([k0, zl], axis=1), cat([zl, k1], axis=1)], axis=0)
            q_st = cat([cat([q0, zl], axis=1), cat([zl, q1], axis=1)], axis=0)
            both = cat([k_st, q_st], axis=0)
            full = _mxu(both, k_st, _NT)
            a = jnp.where(strict, full[:n2] * decay * beta_col, 0.0)
            nn.append(-a)
            tt.append(eye - a)
            qk.append(full[n2:] * decay)
            kq.append(both)
            kdec.append(k_st * jnp.exp(last(L) - c_col))
            ec.append(jnp.exp(c_col))
            bcol.append(beta_col)
            elast.append(jnp.exp(last(d)))
    tt = _neumann_inverse(nn, tt, n2, rounds)
    s_old, ksqs, u = [], [], []
    for n, (i, j) in enumerate(units):
        s_old.append(cat([S[i, 2 * j], S[i, 2 * j + 1]], axis=0))
        ksqs.append(_mxu(kq[n], s_old[n], _NN))
    for n in range(len(units)):
        u.append(_mxu(tt[n], bcol[n] * (vst[n] - ec[n] * ksqs[n][:n2]), _NN))
    nw = nw_ref[...]
    for n, (i, j) in enumerate(units):
        rows = slice(i * L, (i + 1) * L)
        o = ec[n] * ksqs[n][n2:] + _mxu(qk[n], u[n], _NN)
        for hh, oh in ((2 * j, o[:L]), (2 * j + 1, o[L:])):
            cols = slice(hh * d, (hh + 1) * d)
            y = oh * lax.rsqrt(jnp.mean(oh * oh, axis=-1, keepdims=True) + 1e-6) * nw
            mix_ref[rows, cols] = (y * _silu(cz_ref[rows, cols])).astype(BF16)
        new = s_old[n] * elast[n] + _mxu(kdec[n], u[n], _TN)
        S[i, 2 * j] = new[:d]
        S[i, 2 * j + 1] = new[d:]
    if nchunk > 1:
        for i in range(nseq):
            tail[i] = qkv_ref[i * L + L - 8:(i + 1) * L, :]

    @pl.when(ci == nchunk - 1)
    def _():
        s_ref[...] = S[...]
        for i in range(nseq):
            conv_out_ref[i] = qkv_ref[i * L + t_real - (CONV_W - 1):i * L + t_real, :]


def _gdn(h, mix_prev, grp, P, st_in, conv_in, s_prev, p):
    L = grp.chunk
    nchunk = grp.t_rows // L
    tail = (GDN_HEADS, GDN_HEAD_DIM, GDN_HEAD_DIM)
    ns = grp.nseq
    s_ins, s_specs, s_out, s_shape = _state_io(tail, p, st_in, grp)
    cw = 3 * GDN_WIDTH
    full = lambda shape: pl.BlockSpec(shape, lambda b, c: (0,) * len(shape))
    ins = [h, h, h, P['gdn_conv_w'], P['gdn_a_log'].reshape(1, GDN_HEADS), P['gdn_dt_bias'].reshape(1, GDN_HEADS),
           P['gdn_norm'].reshape(1, GDN_HEAD_DIM)]
    specs = [grp.spec(L, cw, ODD_OFF[0]), grp.spec(L, GDN_WIDTH, ODD_OFF[1]), grp.spec(L, 128, ODD_OFF[2]),
             full((CONV_W, cw)), full((1, GDN_HEADS)), full((1, GDN_HEADS)), full((1, GDN_HEAD_DIM))]
    if st_in is not None:
        ins += [conv_in] + s_ins
        specs += [pl.BlockSpec((None, ns, CONV_W - 1, cw), lambda b, c: (p, b, 0, 0))] + s_specs
    any_spec = pl.BlockSpec(memory_space=pl.ANY)
    alias = {}
    for prev, out_idx in ((mix_prev, 0), (s_prev, 1)):
        if prev is not None:
            alias[len(ins)] = out_idx
            ins.append(prev)
            specs.append(any_spec)
    n_prev = (mix_prev is not None) + (s_prev is not None)
    return pl.pallas_call(
        functools.partial(_gdn_kernel, L=L, nchunk=nchunk, rounds=int(math.log2(L)), t_real=grp.t_real or L,
                          nseq=ns, has_state=st_in is not None, has_prev=n_prev),
        grid=(grp.steps, nchunk),
        in_specs=specs,
        out_specs=[grp.spec(L, GDN_WIDTH, 0), s_out,
                   pl.BlockSpec((ns, CONV_W - 1, cw), lambda b, c: (b, 0, 0))],
        out_shape=[jax.ShapeDtypeStruct((h.shape[0], MIX_WIDTH), BF16), s_shape,
                   jax.ShapeDtypeStruct((grp.bsz, CONV_W - 1, cw), F32)],
        scratch_shapes=[pltpu.VMEM((ns,) + tail, F32), pltpu.VMEM((ns, 8, cw), F32)],
        input_output_aliases=alias,
        compiler_params=_cparams(("parallel", "arbitrary")),
        name="gdn",
    )(*ins)


RWKV_PAIRS = RWKV_HEADS // 2
RWKV_PW = 2 * RWKV_HEAD_DIM


def _rwkv_kernel(r_ref, k_ref, v_ref, xwa_ref, gate_ref, mu_ref, w0_ref, w2_ref, a0_ref, a2_ref, kkp_ref, ka_ref,
                 rk_ref, lng_ref, lnb_ref, *rest, L, nchunk, rounds, t_real, nseq, has_state, has_prev):
    n_opt = 2 * has_state + has_prev
    mix_ref, s_ref, shift_out_ref, S, last = rest[n_opt:]
    ci = pl.program_id(1)
    n = RWKV_HEAD_DIM
    w3 = 3 * RWKV_WIDTH
    cat = jnp.concatenate
    pairs = range(RWKV_PAIRS)

    @pl.when(ci == 0)
    def _():
        last[...] = jnp.zeros_like(last)
        if has_state:
            zn = jnp.zeros((n, n), F32)
            for i in range(nseq):
                last[i, 7:8, :] = rest[0][i]
                for j in pairs:
                    S[i * RWKV_PAIRS + j] = cat([cat([rest[1][i, 2 * j], zn], axis=1),
                                                 cat([zn, rest[1][i, 2 * j + 1]], axis=1)], axis=0)
        else:
            S[...] = jnp.zeros_like(S)

    n2 = 2 * L
    row = lax.broadcasted_iota(jnp.int32, (L, L), 0)
    col = lax.broadcasted_iota(jnp.int32, (L, L), 1)
    tril = (col <= row).astype(F32)
    r2 = lax.broadcasted_iota(jnp.int32, (n2, n2), 0)
    c2 = lax.broadcasted_iota(jnp.int32, (n2, n2), 1)
    same = (r2 >= L) == (c2 >= L)
    strict = same & (c2 < r2)
    incl = same & (c2 <= r2)
    eye = (r2 == c2).astype(F32)
    lane = lax.broadcasted_iota(jnp.int32, (L, RWKV_PW), 1)
    lo = lane < n

    def stack(x):
        return cat([jnp.where(lo, x, 0.0), jnp.where(lo, 0.0, x)], axis=0)

    row8 = lax.broadcasted_iota(jnp.int32, (8, RWKV_PW), 0)
    valid = lax.broadcasted_iota(jnp.int32, (L, RWKV_PW), 0) < t_real

    def seg_sum(x):
        s_lo = jnp.sum(jnp.where(lo, x, 0.0), axis=-1, keepdims=True)
        s_hi = jnp.sum(jnp.where(lo, 0.0, x), axis=-1, keepdims=True)
        return jnp.where(lo, s_lo, s_hi)

    def shift_mix(ref, i, c_src, c_all):
        x = ref[i * L:(i + 1) * L, c_src:c_src + RWKV_PW]
        prev = _shifted(x, last[i, :, c_all:c_all + RWKV_PW], 1, row8)
        return x + (prev - x) * mu_ref[:, c_all:c_all + RWKV_PW]

    xwa = cat([shift_mix(xwa_ref, i, 0, w3) for i in range(nseq)], axis=0)
    lr_w_all = _mxu_f32(jnp.tanh(xwa), w2_ref[...], _NN)
    lr_a_all = _mxu_f32(xwa, a2_ref[...], _NN)
    units = [(i, j) for i in range(nseq) for j in pairs]
    a_ak, a_rk, a_rb, nn, tt, sread, kdbd, egl, vs, bonus = [], [], [], [], [], [], [], [], [], []
    for i, j in units:
        sl = slice(j * RWKV_PW, (j + 1) * RWKV_PW)
        r = shift_mix(r_ref, i, j * RWKV_PW, j * RWKV_PW)
        k = shift_mix(k_ref, i, j * RWKV_PW, RWKV_WIDTH + j * RWKV_PW)
        v = shift_mix(v_ref, i, j * RWKV_PW, 2 * RWKV_WIDTH + j * RWKV_PW)
        w_log = -_softplus(-(w0_ref[:, sl] + lr_w_all[i * L:(i + 1) * L, sl])) - 0.5
        lw = -jnp.exp(w_log)
        a7 = _sigmoid(a0_ref[:, sl] + lr_a_all[i * L:(i + 1) * L, sl])
        kx = k * kkp_ref[:, sl]
        kk = kx * lax.rsqrt(seg_sum(kx * kx) + 1e-6)
        k = k * (1.0 + (a7 - 1.0) * ka_ref[:, sl])
        if t_real < L:
            lw, kk, k, v = (jnp.where(valid, a, 0.0) for a in (lw, kk, k, v))
        b = kk * a7
        bonus.append(seg_sum(r * k * rk_ref[:, sl]) * v)
        vs.append(stack(v))
        g = _mxu_f32(tril, lw, _NN)
        gp = g - lw
        gm = g[L // 2 - 1:L // 2, :]
        gl = g[L - 1:L, :]
        e_neg = jnp.exp(gm - g)
        lhs = cat([stack(kk * jnp.exp(gp - gm)), stack(r * jnp.exp(g - gm))], axis=0)
        rhs = cat([stack(b * e_neg), stack(k * e_neg)], axis=0)
        full = _mxu(lhs, rhs, _NT)
        a_ab = jnp.where(strict, full[:n2, :n2], 0.0)
        a_ak.append(jnp.where(strict, full[:n2, n2:], 0.0))
        a_rb.append(jnp.where(incl, full[n2:, :n2], 0.0))
        a_rk.append(jnp.where(incl, full[n2:, n2:], 0.0))
        nn.append(-a_ab)
        tt.append(eye - a_ab)
        sread.append(cat([stack(kk * jnp.exp(gp)), stack(r * jnp.exp(g))], axis=0))
        dec = jnp.exp(gl - g)
        kdbd.append(cat([stack(k * dec), stack(-b * dec)], axis=0))
        egl.append(jnp.exp(gl))
    tt = _neumann_inverse(nn, tt, n2, rounds)
    s_old, sr, av, u = [], [], [], []
    for m, (i, j) in enumerate(units):
        s_old.append(S[i * RWKV_PAIRS + j])
        sr.append(_mxu(sread[m], s_old[m], _NT))
        av.append(_mxu(cat([a_ak[m], a_rk[m]], axis=0), vs[m], _NN))
    for m in range(len(units)):
        u.append(_mxu(tt[m], sr[m][:n2] + av[m][:n2], _NN))
    for m, (i, j) in enumerate(units):
        sl = slice(j * RWKV_PW, (j + 1) * RWKV_PW)
        rows = slice(i * L, (i + 1) * L)
        o = sr[m][n2:] + av[m][n2:] - _mxu(a_rb[m], u[m], _NN)
        o = o[:L] + o[L:]
        oc = o - seg_sum(o) * (1.0 / n)
        gn = oc * lax.rsqrt(seg_sum(oc * oc) * (1.0 / n) + RWKV_GN_EPS)
        y = gn * lng_ref[:, sl] + lnb_ref[:, sl] + bonus[m]
        mix_ref[rows, sl] = (y * _silu(gate_ref[rows, sl])).astype(BF16)
        S[i * RWKV_PAIRS + j] = s_old[m] * egl[m] + _mxu(cat([vs[m], u[m]], axis=0), kdbd[m], _TN)
    pieces = ((r_ref, 0, RWKV_WIDTH), (k_ref, RWKV_WIDTH, RWKV_WIDTH), (v_ref, 2 * RWKV_WIDTH, RWKV_WIDTH),
              (xwa_ref, w3, RWKV_PW))
    if nchunk > 1:
        for i in range(nseq):
            for ref, c0, wd in pieces:
                last[i, :, c0:c0 + wd] = ref[i * L + L - 8:(i + 1) * L, :]

    @pl.when(ci == nchunk - 1)
    def _():
        for i in range(nseq):
            for ref, c0, wd in pieces:
                shift_out_ref[i, :, c0:c0 + wd] = ref[i * L + t_real - 1:i * L + t_real, :]
            for j in pairs:
                s_ref[i, 2 * j] = S[i * RWKV_PAIRS + j][:n, :n]
                s_ref[i, 2 * j + 1] = S[i * RWKV_PAIRS + j][n:, n:]


def _rwkv(h, mix_prev, grp, P, st_in, shift_in, s_prev, p):
    L = grp.chunk
    nchunk = grp.t_rows // L
    tail = (RWKV_HEADS, RWKV_HEAD_DIM, RWKV_HEAD_DIM)
    ns = grp.nseq
    s_ins, s_specs, s_out, s_shape = _state_io(tail, p, st_in, grp)
    full = lambda shape: pl.BlockSpec(shape, lambda b, c: (0,) * len(shape))
    row = lambda a: a.reshape(1, -1)
    zr = jnp.zeros((RWKV_HEAD_DIM, RWKV_WIDTH), F32)
    w2 = jnp.concatenate([P['rwkv_w2'], zr], axis=0)
    a2 = jnp.concatenate([zr, P['rwkv_a2']], axis=0)
    off = ODD_OFF[4]
    ins = [h, h, h, h, h, row(P['rwkv_mu']), row(P['rwkv_w0']), w2, row(P['rwkv_a0']), a2, row(P['rwkv_kk']),
           row(P['rwkv_ka']), row(P['rwkv_rk']), row(P['rwkv_ln_g']), row(P['rwkv_ln_b'])]
    vec = full((1, RWKV_WIDTH))
    specs = [grp.spec(L, RWKV_WIDTH, off), grp.spec(L, RWKV_WIDTH, off + RWKV_WIDTH),
             grp.spec(L, RWKV_WIDTH, off + 2 * RWKV_WIDTH), grp.spec(L, RWKV_PW, off + 3 * RWKV_WIDTH),
             grp.spec(L, RWKV_WIDTH, ODD_OFF[5]), full((1, RWKV_SHIFT_DIM)), vec, full((RWKV_PW, RWKV_WIDTH)), vec,
             full((RWKV_PW, RWKV_WIDTH)), vec, vec, vec, vec, vec]
    if st_in is not None:
        ins += [shift_in] + s_ins
        specs += [pl.BlockSpec((None, ns, 1, RWKV_SHIFT_DIM), lambda b, c: (p, b, 0, 0))] + s_specs
    any_spec = pl.BlockSpec(memory_space=pl.ANY)
    alias = {}
    for prev, out_idx in ((mix_prev, 0), (s_prev, 1)):
        if prev is not None:
            alias[len(ins)] = out_idx
            ins.append(prev)
            specs.append(any_spec)
    n_prev = (mix_prev is not None) + (s_prev is not None)
    return pl.pallas_call(
        functools.partial(_rwkv_kernel, L=L, nchunk=nchunk, rounds=int(math.log2(L)), t_real=grp.t_real or L,
                          nseq=ns, has_state=st_in is not None, has_prev=n_prev),
        grid=(grp.steps, nchunk),
        in_specs=specs,
        out_specs=[grp.spec(L, RWKV_WIDTH, GDN_WIDTH), s_out,
                   pl.BlockSpec((ns, 1, RWKV_SHIFT_DIM), lambda b, c: (b, 0, 0))],
        out_shape=[jax.ShapeDtypeStruct((h.shape[0], MIX_WIDTH), BF16), s_shape,
                   jax.ShapeDtypeStruct((grp.bsz, 1, RWKV_SHIFT_DIM), F32)],
        scratch_shapes=[pltpu.VMEM((ns * RWKV_PAIRS, RWKV_PW, RWKV_PW), F32),
                        pltpu.VMEM((ns, 8, RWKV_SHIFT_DIM), F32)],
        input_output_aliases=alias,
        compiler_params=_cparams(("parallel", "arbitrary")),
        name="rwkv7",
    )(*ins)


def _pad_t(a, t_to):
    t = a.shape[1]
    if t == t_to:
        return a
    return jnp.pad(a, [(0, 0), (0, t_to - t)] + [(0, 0)] * (a.ndim - 2))


def _even_mix(h, mix, grp, mem_k, mem_v, layer, st_in, st_prev, conv_in, P):
    p = layer // 2
    mix, s_ssd, s_conv = _ssd(h, mix, grp, P, st_in['ssd'], conv_in, st_prev['ssd'], p)
    gla_l, gla_tb = (GLA_CHUNK, 256) if grp.t_real is None else (grp.chunk, grp.chunk)
    mix, s_gla = _gla(h, mix, grp, P, st_in['gla'], st_prev['gla'], p, gla_l, gla_tb)
    mix = _mem_attention(h, mix, grp, EVEN_OFF[8], EVEN_OFF[9], mem_k, mem_v, layer, min(512, grp.t_rows))
    return mix, dict(gla=s_gla, ssd=s_ssd), s_conv


def _odd_mix(h, mix, grp, mem_k, mem_v, layer, st_in, st_prev, conv_in, shift_in, P):
    p = layer // 2
    mix, s_gdn, s_conv = _gdn(h, mix, grp, P, st_in['gdn'], conv_in, st_prev['gdn'], p)
    mix, s_rwkv, s_shift = _rwkv(h, mix, grp, P, st_in['rwkv'], shift_in, st_prev['rwkv'], p)
    mix = _mem_attention(h, mix, grp, ODD_OFF[6], ODD_OFF[7], mem_k, mem_v, layer, min(512, grp.t_rows))
    return mix, dict(gdn=s_gdn, rwkv=s_rwkv), s_conv, s_shift.reshape(grp.bsz, RWKV_SHIFT_DIM)


def kernel(x_prompt, x_sample, mem_prompt, cache_mem_k, cache_mem_v, state_gla, state_ssd, state_ssd_conv, state_gdn, state_gdn_conv, state_rwkv, state_rwkv_shift, mem_w_kv, ev_w_in, ev_gla_w2, ev_gla_b, ev_gla_norm, ev_ssd_conv_w, ev_ssd_conv_b, ev_ssd_dt_bias, ev_ssd_a_log, ev_ssd_d, ev_ssd_norm, ev_w_out, ev_ln_g, ev_ln_b, od_w_in, od_gdn_conv_w, od_gdn_dt_bias, od_gdn_a_log, od_gdn_norm, od_rwkv_mu, od_rwkv_w0, od_rwkv_w2, od_rwkv_a0, od_rwkv_a2, od_rwkv_kk, od_rwkv_ka, od_rwkv_rk, od_rwkv_ln_g, od_rwkv_ln_b, od_w_out, od_ln_g, od_ln_b):
    ev = dict(w_in=ev_w_in, gla_w2=ev_gla_w2, gla_b=ev_gla_b, gla_norm=ev_gla_norm,
              ssd_conv_w=ev_ssd_conv_w, ssd_conv_b=ev_ssd_conv_b, ssd_dt_bias=ev_ssd_dt_bias,
              ssd_a_log=ev_ssd_a_log, ssd_d=ev_ssd_d, ssd_norm=ev_ssd_norm,
              w_out=ev_w_out, ln_g=ev_ln_g, ln_b=ev_ln_b)
    od = dict(w_in=od_w_in, gdn_conv_w=od_gdn_conv_w, gdn_dt_bias=od_gdn_dt_bias, gdn_a_log=od_gdn_a_log,
              gdn_norm=od_gdn_norm, rwkv_mu=od_rwkv_mu, rwkv_w0=od_rwkv_w0, rwkv_w2=od_rwkv_w2,
              rwkv_a0=od_rwkv_a0, rwkv_a2=od_rwkv_a2, rwkv_kk=od_rwkv_kk, rwkv_ka=od_rwkv_ka,
              rwkv_rk=od_rwkv_rk, rwkv_ln_g=od_rwkv_ln_g, rwkv_ln_b=od_rwkv_ln_b,
              w_out=od_w_out, ln_g=od_ln_g, ln_b=od_ln_b)
    bp, tp, _ = x_prompt.shape
    bs, ts, _ = x_sample.shape
    mp, ms = bp * tp, bs * SMALL_T
    grp_p = _Group(bp, tp, None, 0, GDN_CHUNK)
    grp_s = _Group(bs, SMALL_T, ts, mp, SMALL_T, nseq=SAMPLE_NSEQ)

    w_kv = jnp.moveaxis(mem_w_kv, 0, 1).reshape(D_MODEL, DEPTH * 2 * MEM_WIDTH).astype(BF16)
    kv = _matmul(mem_prompt.reshape(bp * MEM_LEN, D_MODEL).astype(BF16), w_kv, 512, 1024)
    kv6 = kv.reshape(bp, MEM_LEN, DEPTH, 2, MEM_HEADS, MEM_HEAD_DIM)
    mem_k_p = jnp.moveaxis(kv6[:, :, :, 0], 2, 0)
    mem_v_p = jnp.moveaxis(kv6[:, :, :, 1], 2, 0)
    mk_s, mv_s = _cache_view(cache_mem_k), _cache_view(cache_mem_v)

    x = jnp.concatenate([x_prompt.reshape(mp, D_MODEL),
                         _pad_t(x_sample, SMALL_T).reshape(ms, D_MODEL)], axis=0)
    x_bf = x.astype(BF16)
    zp = lambda shape: jnp.zeros(shape, F32)
    names = ('gla', 'ssd', 'gdn', 'rwkv')
    none = {n: None for n in names}
    in_s = dict(gla=state_gla, ssd=state_ssd, gdn=state_gdn, rwkv=state_rwkv)
    shift_s = state_rwkv_shift.reshape(N_PAIRS, bs, 1, RWKV_SHIFT_DIM)
    out_p, out_s = dict(none), dict(none)
    small_p = {n: [] for n in ('ssd_conv', 'gdn_conv', 'rwkv_shift')}
    small_s = {n: [] for n in small_p}
    tm = 1024
    w_in_ev = _pack_w_in(ev_w_in, EVEN_SIZES, EVEN_ORDER, EVEN_N)
    w_in_od = _pack_w_in(od_w_in, ODD_SIZES, ODD_ORDER, ODD_N)
    w_out_ev = jnp.concatenate([ev_w_out[:, GLA_WIDTH:GLA_WIDTH + SSD_WIDTH].astype(BF16),
                                ev_w_out[:, :GLA_WIDTH].astype(BF16),
                                ev_w_out[:, GLA_WIDTH + SSD_WIDTH:].astype(BF16)], axis=1)
    w_out_od = od_w_out.astype(BF16)
    small = lambda d: {n: w[p] for n, w in d.items() if n not in ('w_in', 'w_out')}
    for layer in range(DEPTH):
        p = layer // 2
        if layer % 2 == 0:
            P = small(ev)
            h = _matmul(x_bf, w_in_ev, tm, PROJ_TN, p=p)
            mix, new, c1 = _even_mix(h, None, grp_p, kv, kv, layer, none, out_p, None, P)
            out_p.update(new)
            mix, new, c2 = _even_mix(h, mix, grp_s, mk_s, mv_s, layer, in_s, out_s, state_ssd_conv, P)
            out_s.update(new)
            small_p['ssd_conv'].append(c1)
            small_s['ssd_conv'].append(c2)
            w_out = w_out_ev
        else:
            P = small(od)
            h = _matmul(x_bf, w_in_od, tm, PROJ_TN, p=p)
            mix, new, c1, h1 = _odd_mix(h, None, grp_p, kv, kv, layer, none, out_p, None, None, P)
            out_p.update(new)
            mix, new, c2, h2 = _odd_mix(h, mix, grp_s, mk_s, mv_s, layer, in_s, out_s, state_gdn_conv, shift_s, P)
            out_s.update(new)
            small_p['gdn_conv'].append(c1)
            small_s['gdn_conv'].append(c2)
            small_p['rwkv_shift'].append(h1)
            small_s['rwkv_shift'].append(h2)
            w_out = w_out_od
        x, x_bf = _out_ln(mix, w_out, p, x, P['ln_g'], P['ln_b'])

    y_prompt = x[:mp].reshape(bp, tp, D_MODEL)
    y_sample = x[mp:].reshape(bs, SMALL_T, D_MODEL)[:, :ts]
    st = lambda d, n: jnp.stack(d[n])
    return (y_prompt, y_sample, mem_k_p, mem_v_p,
            out_p['gla'], out_s['gla'], out_p['ssd'], out_s['ssd'],
            st(small_p, 'ssd_conv'), st(small_s, 'ssd_conv'), out_p['gdn'], out_s['gdn'],
            st(small_p, 'gdn_conv'), st(small_s, 'gdn_conv'), out_p['rwkv'], out_s['rwkv'],
            st(small_p, 'rwkv_shift'), st(small_s, 'rwkv_shift'))
```

```python
import functools
import math

import numpy as np
import jax
import jax.numpy as jnp
from jax import lax
from jax.experimental import pallas as pl
from jax.experimental.pallas import tpu as pltpu

F32 = jnp.float32
BF16 = jnp.bfloat16
HI = lax.Precision.HIGHEST

D_MODEL = 2048
DEPTH = 4
N_PAIRS = DEPTH // 2
CONV_W = 4
MEM_LEN = 256
MEM_HEADS = 4
MEM_HEAD_DIM = 256
MEM_WIDTH = 1024
GLA_HEADS = 4
GLA_DK = 128
GLA_DV = 256
GLA_QK = 512
GLA_WIDTH = 1024
GLA_RANK = 16
GLA_TAU = 16.0
SSD_WIDTH = 2048
SSD_HEAD_DIM = 64
SSD_HEADS = 32
SSD_GROUPS = 4
SSD_REP = 8
SSD_STATE = 128
SSD_CONV_DIM = SSD_WIDTH + 2 * SSD_GROUPS * SSD_STATE
GDN_WIDTH = 2048
GDN_HEAD_DIM = 128
GDN_HEADS = 16
RWKV_WIDTH = 1024
RWKV_HEAD_DIM = 64
RWKV_HEADS = 16
RWKV_W_RANK = 64
RWKV_A_RANK = 64
RWKV_SHIFT_DIM = 3 * RWKV_WIDTH + RWKV_W_RANK + RWKV_A_RANK
RWKV_GN_EPS = 64e-5
EVEN_SIZES = (GLA_QK, GLA_QK, GLA_WIDTH, GLA_RANK, GLA_WIDTH, SSD_WIDTH, SSD_CONV_DIM, SSD_HEADS,
              MEM_WIDTH, MEM_WIDTH)
ODD_SIZES = (3 * GDN_WIDTH, GDN_WIDTH, GDN_HEADS, GDN_HEADS, RWKV_SHIFT_DIM, RWKV_WIDTH, MEM_WIDTH, MEM_WIDTH)
MIX_WIDTH = 4096
DEEPNORM_ALPHA = (2 * DEPTH) ** 0.25

EVEN_ORDER = (5, 2, 4, 8, 9, 6, 0, 1, 3, 7)
ODD_ORDER = (0, 1, 5, 6, 7, 4, 2, 3)
PROJ_TN = 768
VMEM_LIMIT = 60 * 1024 * 1024

GLA_CHUNK = 16
SSD_CHUNK = 64
GDN_CHUNK = 64
RWKV_CHUNK = 64
SMALL_T = 8
SAMPLE_NSEQ = 4

_NN = ((1,), (0,))
_NT = ((1,), (1,))
_TN = ((0,), (0,))


def _packed_layout(sizes, order):
    offs, o = {}, 0
    for i in order:
        offs[i] = o
        o += sizes[i]
    total = -(-o // PROJ_TN) * PROJ_TN
    return offs, total


EVEN_OFF, EVEN_N = _packed_layout(EVEN_SIZES, EVEN_ORDER)
ODD_OFF, ODD_N = _packed_layout(ODD_SIZES, ODD_ORDER)


def _pack_w_in(w, sizes, order, total):
    ends = np.cumsum(sizes)
    parts = [w[..., ends[i] - sizes[i]:ends[i]].astype(BF16) for i in order]
    used = sum(sizes)
    if total > used:
        parts.append(jnp.zeros(w.shape[:-1] + (total - used,), BF16))
    return jnp.concatenate(parts, axis=-1)


def _seg(h, offs, sizes, i):
    return h[..., offs[i]:offs[i] + sizes[i]]


def _cparams(sem):
    return pltpu.CompilerParams(dimension_semantics=sem, vmem_limit_bytes=VMEM_LIMIT)


def _mxu(a, b, dims):
    return lax.dot_general(a.astype(BF16), b.astype(BF16), (dims, ((), ())), preferred_element_type=F32)


def _mxu_f32(a, b, dims):
    return lax.dot_general(a, b, (dims, ((), ())), precision=HI, preferred_element_type=F32)


def _sigmoid(x):
    return 1.0 / (1.0 + jnp.exp(-x))


def _silu(x):
    return x * _sigmoid(x)


def _softplus(x):
    return jnp.maximum(x, 0.0) + jnp.log(1.0 + jnp.exp(-jnp.abs(x)))


def _shifted(u, prev8, j, row8):
    ru = pltpu.roll(u, j, 0)
    top = jnp.where(row8 < j, pltpu.roll(prev8, j, 0), ru[:8])
    return top if u.shape[0] == 8 else jnp.concatenate([top, ru[8:]], axis=0)


class _Group:
    def __init__(self, bsz, t_rows, t_real, row0, chunk, nseq=1):
        assert nseq == 1 or t_rows == chunk
        self.bsz, self.t_rows, self.t_real, self.row0, self.chunk, self.nseq = bsz, t_rows, t_real, row0, chunk, nseq
        self.steps = bsz // nseq

    def spec(self, rows, width, off):
        rows = rows * self.nseq
        assert off % width == 0 and self.row0 % rows == 0 and (self.t_rows * self.nseq) % rows == 0
        base, per, cb = self.row0 // rows, self.t_rows * self.nseq // rows, off // width
        return pl.BlockSpec((rows, width), lambda b, i: (base + b * per + i, cb))


def _alias_last(n_inputs, has_prev, out_index=0):
    return {n_inputs - 1: out_index} if has_prev else {}


TRI_BASE = 4


def _unit_lower_inverse(a, L, eye, r_in, c_in):
    idx = range(len(a))
    base = (r_in // TRI_BASE) == (c_in // TRI_BASE)
    n1 = [-jnp.where(base, a[j], 0.0) for j in idx]
    n2 = [_mxu(n1[j], n1[j], _NN) for j in idx]
    tt = [eye + n1[j] for j in idx]
    tt = [tt[j] + _mxu(tt[j], n2[j], _NN) for j in idx]
    b = TRI_BASE
    while b < L:
        rb, cb = r_in // b, c_in // b
        lower_left = (rb // 2 == cb // 2) & (rb % 2 == 1) & (cb % 2 == 0)
        for j in idx:
            x = _mxu(tt[j], jnp.where(lower_left, a[j], 0.0), _NN)
            tt[j] = tt[j] - _mxu(x, tt[j], _NN)
        b *= 2
    return tt


def _state_io(tail, p, s_in, grp):
    zeros = (0,) * len(tail)
    spec = pl.BlockSpec((None, grp.nseq) + tail, lambda *g: (p, g[0]) + zeros)
    ins, specs = ([s_in], [spec]) if s_in is not None else ([], [])
    shape = jax.ShapeDtypeStruct((N_PAIRS, grp.bsz) + tail, F32)
    return ins, specs, spec, shape


def _mm_kernel(x_ref, w_ref, o_ref, *, precision):
    o_ref[...] = jnp.dot(x_ref[...], w_ref[...], preferred_element_type=F32, precision=precision)


def _matmul(x, w, tm, tn, precision=None, p=None):
    m, k = x.shape
    n = w.shape[-1]
    assert m % tm == 0 and n % tn == 0
    w_spec = (pl.BlockSpec((k, tn), lambda j, i: (0, j)) if p is None
              else pl.BlockSpec((None, k, tn), lambda j, i: (p, 0, j)))
    return pl.pallas_call(
        functools.partial(_mm_kernel, precision=precision),
        grid=(n // tn, m // tm),
        in_specs=[pl.BlockSpec((tm, k), lambda j, i: (i, 0)), w_spec],
        out_specs=pl.BlockSpec((tm, tn), lambda j, i: (i, j)),
        out_shape=jax.ShapeDtypeStruct((m, n), F32),
        compiler_params=_cparams(("parallel", "parallel")),
        name="matmul",
    )(x, w)


def _out_ln_kernel(mix_ref, w_ref, x_ref, g_ref, b_ref, y_ref, ybf_ref, acc, *, nk):
    kk = pl.program_id(1)

    @pl.when(kk == 0)
    def _():
        acc[...] = jnp.zeros_like(acc)

    acc[...] += jnp.dot(mix_ref[...].astype(BF16), w_ref[...], preferred_element_type=F32)

    @pl.when(kk == nk - 1)
    def _():
        z = DEEPNORM_ALPHA * x_ref[...] + acc[...]
        zc = z - jnp.mean(z, axis=-1, keepdims=True)
        var = jnp.mean(zc * zc, axis=-1, keepdims=True)
        y = zc * lax.rsqrt(var + 1e-5) * g_ref[...] + b_ref[...]
        y_ref[...] = y
        ybf_ref[...] = y.astype(BF16)


def _out_ln(mix, w, p, x, g, b, tm=512, tk=2048):
    m, k = mix.shape
    d = w.shape[-1]
    nk = k // tk
    return pl.pallas_call(
        functools.partial(_out_ln_kernel, nk=nk),
        grid=(m // tm, nk),
        in_specs=[pl.BlockSpec((tm, tk), lambda i, j: (i, j)),
                  pl.BlockSpec((None, tk, d), lambda i, j: (p, j, 0)),
                  pl.BlockSpec((tm, d), lambda i, j: (i, 0)),
                  pl.BlockSpec((1, d), lambda i, j: (0, 0)),
                  pl.BlockSpec((1, d), lambda i, j: (0, 0))],
        out_specs=[pl.BlockSpec((tm, d), lambda i, j: (i, 0)),
                   pl.BlockSpec((tm, d), lambda i, j: (i, 0))],
        out_shape=[jax.ShapeDtypeStruct((m, d), F32), jax.ShapeDtypeStruct((m, d), BF16)],
        scratch_shapes=[pltpu.VMEM((tm, d), F32)],
        compiler_params=_cparams(("parallel", "arbitrary")),
        name="out_ln",
    )(mix, w, x, g.reshape(1, d), b.reshape(1, d))


def _mem_kernel(q_ref, gate_ref, k_ref, v_ref, *rest):
    o_ref = rest[-1]
    for h in range(MEM_HEADS):
        sl = slice(h * MEM_HEAD_DIM, (h + 1) * MEM_HEAD_DIM)
        k = k_ref[:, sl]
        v = v_ref[:, sl]
        s = _mxu(q_ref[:, sl], k, _NT) * MEM_HEAD_DIM ** -0.5
        p = jnp.exp(s - jnp.max(s, axis=-1, keepdims=True))
        p = p / jnp.sum(p, axis=-1, keepdims=True)
        o_ref[:, sl] = (_mxu(p, v, _NN) * _silu(gate_ref[:, sl])).astype(BF16)


MEM_DT = MEM_HEAD_DIM // 128
MEM_ROWS = MEM_LEN * MEM_DT * MEM_HEADS


def _cache_view(c):
    d, b = c.shape[:2]
    c = c.reshape(d, b, MEM_LEN, MEM_HEADS, MEM_DT, 128)
    return jnp.transpose(c, (0, 1, 2, 4, 3, 5)).reshape(d, b, MEM_ROWS, 128)


def _mem_cache_kernel(q_ref, gate_ref, k_ref, v_ref, *rest):
    o_ref = rest[-1]
    nseq = k_ref.shape[0]
    t = q_ref.shape[0] // nseq
    grp = MEM_DT * MEM_HEADS
    col = lax.broadcasted_iota(jnp.int32, (MEM_HEADS * t, MEM_ROWS), 1) % grp
    head = lax.broadcasted_iota(jnp.int32, (MEM_HEADS * t, MEM_ROWS), 0) // t
    for i in range(nseq):
        rows = slice(i * t, (i + 1) * t)
        k = k_ref[i]
        v = v_ref[i]
        qs = [jnp.concatenate([q_ref[rows, h * MEM_HEAD_DIM + dt * 128:h * MEM_HEAD_DIM + (dt + 1) * 128]
                               for h in range(MEM_HEADS)], axis=0) for dt in range(MEM_DT)]
        s = _mxu(qs[0], k, _NT)
        for dt in range(1, MEM_DT):
            s = s + pltpu.roll(_mxu(qs[dt], k, _NT), MEM_ROWS - dt * MEM_HEADS, 1)
        s = jnp.where(col == head, s * MEM_HEAD_DIM ** -0.5, -jnp.inf)
        p = jnp.exp(s - jnp.max(s, axis=-1, keepdims=True))
        p = p / jnp.sum(p, axis=-1, keepdims=True)
        for dt in range(MEM_DT):
            o = _mxu(p if dt == 0 else pltpu.roll(p, dt * MEM_HEADS, 1), v, _NN)
            for h in range(MEM_HEADS):
                sl = slice(h * MEM_HEAD_DIM + dt * 128, h * MEM_HEAD_DIM + (dt + 1) * 128)
                o_ref[rows, sl] = (o[h * t:(h + 1) * t] * _silu(gate_ref[rows, sl])).astype(BF16)


def _mem_attention(h, mix_prev, grp, q_off, gate_off, mem_k, mem_v, layer, tq):
    cached = mem_k.ndim == 4
    if cached:
        kv_specs = [pl.BlockSpec((None, grp.nseq, MEM_ROWS, 128), lambda b, i: (layer, b, 0, 0))] * 2
    else:
        kv_specs = [pl.BlockSpec((MEM_LEN, MEM_WIDTH), lambda b, i: (b, 2 * layer)),
                    pl.BlockSpec((MEM_LEN, MEM_WIDTH), lambda b, i: (b, 2 * layer + 1))]
    ins = [h, h, mem_k, mem_v] + ([] if mix_prev is None else [mix_prev])
    specs = [grp.spec(tq, MEM_WIDTH, q_off), grp.spec(tq, MEM_WIDTH, gate_off)] + kv_specs
    if mix_prev is not None:
        specs.append(pl.BlockSpec(memory_space=pl.ANY))
    return pl.pallas_call(
        _mem_cache_kernel if cached else _mem_kernel,
        grid=(grp.steps, grp.t_rows // tq),
        in_specs=specs,
        out_specs=grp.spec(tq, MEM_WIDTH, MIX_WIDTH - MEM_WIDTH),
        out_shape=jax.ShapeDtypeStruct((h.shape[0], MIX_WIDTH), BF16),
        input_output_aliases=_alias_last(len(ins), mix_prev is not None),
        compiler_params=_cparams(("parallel", "parallel")),
        name="mem_attention",
    )(*ins)


def _gla_kernel(q_ref, k_ref, v_ref, gate_ref, sm_ref, w2_ref, gb_ref, nw_ref, *rest, L, nblk, t_real, nseq,
                has_state, has_prev):
    n_opt = has_state + has_prev
    mix_ref, s_ref, ST, QK, B, OI = rest[n_opt:]
    tb = pl.program_id(1)
    rows_blk = q_ref.shape[0]

    @pl.when(tb == 0)
    def _():
        for i in range(nseq):
            for h in range(GLA_HEADS):
                ST[i * GLA_HEADS + h] = rest[0][i, h].T if has_state else jnp.zeros((GLA_DV, GLA_DK), F32)

    z = _mxu_f32(sm_ref[...], w2_ref[...], _NN) + gb_ref[...]
    g_all = -_softplus(-z) * (1.0 / GLA_TAU)
    t_i = lax.broadcasted_iota(jnp.int32, (rows_blk, GLA_DK), 0)
    t_c = t_i & (L - 1)
    nw = nw_ref[...]
    for h in range(GLA_HEADS):
        ks = slice(h * GLA_DK, (h + 1) * GLA_DK)
        vs = slice(h * GLA_DV, (h + 1) * GLA_DV)
        q = q_ref[:, ks] * GLA_DK ** -0.5
        k = k_ref[:, ks]
        b = g_all[:, ks]
        v = v_ref[:, vs]
        if t_real < L:
            b = jnp.where(t_c < t_real, b, 0.0)
            k = jnp.where(t_c < t_real, k, 0.0)
        sh = 1
        while sh < L:
            b = b + jnp.where(t_c >= sh, pltpu.roll(b, sh, 0), 0.0)
            sh *= 2
        o = jnp.sum(q * k, axis=-1, keepdims=True) * v
        for j in range(1, L):
            d = jnp.where(t_c >= j, b - pltpu.roll(b, j, 0), -jnp.inf)
            p = jnp.exp(d) * q * pltpu.roll(k, j, 0)
            o = o + jnp.sum(p, axis=-1, keepdims=True) * pltpu.roll(v, j, 0)
        OI[:, vs] = o
        B[:, ks] = b
        QK[:, ks] = q
        QK[:, GLA_QK + h * GLA_DK:GLA_QK + (h + 1) * GLA_DK] = k
    for c in range(rows_blk // L):
        rows = slice(c * L, (c + 1) * L)
        for h in range(GLA_HEADS):
            ks = slice(h * GLA_DK, (h + 1) * GLA_DK)
            vs = slice(h * GLA_DV, (h + 1) * GLA_DV)
            si = (c if nseq > 1 else 0) * GLA_HEADS + h
            b = B[rows, ks]
            b_last = b[L - 1:L, :]
            st = ST[si]
            o = OI[rows, vs] + _mxu(QK[rows, ks] * jnp.exp(b), st, _NT)
            y = o * lax.rsqrt(jnp.mean(o * o, axis=-1, keepdims=True) + 1e-6) * nw
            mix_ref[rows, vs] = (y * _silu(gate_ref[rows, vs])).astype(BF16)
            kd = QK[rows, GLA_QK + h * GLA_DK:GLA_QK + (h + 1) * GLA_DK] * jnp.exp(b_last - b)
            ST[si] = st * jnp.exp(b_last) + _mxu(v_ref[rows, vs], kd, _TN)

    @pl.when(tb == nblk - 1)
    def _():
        for i in range(nseq):
            for h in range(GLA_HEADS):
                s_ref[i, h] = ST[i * GLA_HEADS + h].T


def _gla(h, mix_prev, grp, P, st_in, s_prev, p, L, tb):
    nblk = grp.t_rows // tb
    tail = (GLA_HEADS, GLA_DK, GLA_DV)
    rows_blk = tb * grp.nseq
    s_ins, s_specs, s_out, s_shape = _state_io(tail, p, st_in, grp)
    full = lambda shape: pl.BlockSpec(shape, lambda b, c: (0,) * len(shape))
    w2 = jnp.concatenate([P['gla_w2'], jnp.zeros((128 - GLA_RANK, GLA_QK), F32)], axis=0)
    ins = [h, h, h, h, h, w2, P['gla_b'].reshape(1, GLA_QK), P['gla_norm'].reshape(1, GLA_DV)]
    specs = [grp.spec(tb, GLA_QK, EVEN_OFF[0]), grp.spec(tb, GLA_QK, EVEN_OFF[1]), grp.spec(tb, GLA_WIDTH, EVEN_OFF[2]),
             grp.spec(tb, GLA_WIDTH, EVEN_OFF[4]), grp.spec(tb, 128, EVEN_OFF[3]),
             full((128, GLA_QK)), full((1, GLA_QK)), full((1, GLA_DV))]
    ins += s_ins
    specs += s_specs
    any_spec = pl.BlockSpec(memory_space=pl.ANY)
    alias = {}
    for prev, out_idx in ((mix_prev, 0), (s_prev, 1)):
        if prev is not None:
            alias[len(ins)] = out_idx
            ins.append(prev)
            specs.append(any_spec)
    n_prev = (mix_prev is not None) + (s_prev is not None)
    return pl.pallas_call(
        functools.partial(_gla_kernel, L=L, nblk=nblk, t_real=grp.t_real or L, nseq=grp.nseq,
                          has_state=st_in is not None, has_prev=n_prev),
        grid=(grp.steps, nblk),
        in_specs=specs,
        out_specs=[grp.spec(tb, GLA_WIDTH, SSD_WIDTH), s_out],
        out_shape=[jax.ShapeDtypeStruct((h.shape[0], MIX_WIDTH), BF16), s_shape],
        scratch_shapes=[pltpu.VMEM((grp.nseq * GLA_HEADS, GLA_DV, GLA_DK), F32),
                        pltpu.VMEM((rows_blk, 2 * GLA_QK), F32), pltpu.VMEM((rows_blk, GLA_QK), F32),
                        pltpu.VMEM((rows_blk, GLA_WIDTH), F32)],
        input_output_aliases=alias,
        compiler_params=_cparams(("parallel", "arbitrary")),
        name="gla",
    )(*ins)


SSD_GW = SSD_REP * SSD_HEAD_DIM


def _ssd_lanes(L):
    return max(SSD_REP * L, 128)


def _ssd_kernel(sz_ref, xbc_ref, sm_ref, cw_ref, cbias_ref, dtb_ref, alog_ref, dvec_ref, nw_ref, ep_ref, es_ref, *rest,
                L, nchunk, t_real, nseq, has_state, has_prev):
    n_opt = 2 * has_state + has_prev
    mix_ref, s_ref, conv_out_ref, ST, tail = rest[n_opt:]
    ci = pl.program_id(1)
    gs = _ssd_lanes(L)
    cat = jnp.concatenate

    @pl.when(ci == 0)
    def _():
        tail[...] = jnp.zeros_like(tail)
        for i in range(nseq):
            for g in range(SSD_GROUPS):
                if has_state:
                    ST[i * SSD_GROUPS + g] = rest[1][i, g * SSD_REP:(g + 1) * SSD_REP].reshape(SSD_GW, SSD_STATE).T
                else:
                    ST[i * SSD_GROUPS + g] = jnp.zeros((SSD_STATE, SSD_GW), F32)
            if has_state:
                tail[i, 8 - (CONV_W - 1):8, :] = rest[0][i]

    d = 128
    row8 = lax.broadcasted_iota(jnp.int32, (8, d), 0)
    row = lax.broadcasted_iota(jnp.int32, (L, L), 0)
    col = lax.broadcasted_iota(jnp.int32, (L, L), 1)
    tril = (col <= row).astype(F32)
    ep = ep_ref[...]
    t_i = lax.broadcasted_iota(jnp.int32, (L, SSD_GROUPS * gs), 0)
    s_i = lax.broadcasted_iota(jnp.int32, (L, SSD_GROUPS * gs), 1) & (L - 1)
    blk_r = lax.broadcasted_iota(jnp.int32, (gs, SSD_GW), 0) // L
    blk_c = lax.broadcasted_iota(jnp.int32, (gs, SSD_GW), 1) // SSD_HEAD_DIM
    diag = blk_r == blk_c
    reps = SSD_REP * L
    for i in range(nseq):
        rows = slice(i * L, (i + 1) * L)

        def conv_tile(c0):
            u = xbc_ref[rows, c0:c0 + d]
            p8 = tail[i, :, c0:c0 + d]
            w = cw_ref[:, c0:c0 + d]
            acc = u * w[CONV_W - 1:CONV_W] + cbias_ref[:, c0:c0 + d]
            for j in range(1, CONV_W):
                acc = acc + _shifted(u, p8, j, row8) * w[CONV_W - 1 - j:CONV_W - j]
            return _silu(acc)

        dt = _softplus(sm_ref[rows, GLA_RANK:GLA_RANK + SSD_HEADS] + dtb_ref[...])
        if t_real < L:
            dt = jnp.where(lax.broadcasted_iota(jnp.int32, (L, SSD_HEADS), 0) < t_real, dt, 0.0)
        c = _mxu_f32(tril, dt * -jnp.exp(alog_ref[...]), _NN)
        dt_x = _mxu_f32(dt, ep, _NN)
        c_x = _mxu_f32(c, ep, _NN)
        c_s = _mxu_f32(c, es_ref[...], _NN)
        c_src = jnp.sum(jnp.where(t_i == s_i, c_s, 0.0), axis=0, keepdims=True)
        seg = jnp.exp(jnp.where(s_i <= t_i, c_s - c_src, -jnp.inf))
        for g in range(SSD_GROUPS):
            gl = slice(g * SSD_GW, (g + 1) * SSD_GW)
            sx = cat([conv_tile(g * SSD_GW + n * d) for n in range(SSD_GW // d)], axis=1)
            bm = conv_tile(SSD_WIDTH + g * SSD_STATE)
            cm = conv_tile(SSD_WIDTH + SSD_GROUPS * SSD_STATE + g * SSD_STATE)
            xdt = sx * dt_x[:, gl]
            pad_rows = [] if reps == gs else [jnp.zeros((gs - reps, SSD_STATE), F32)]
            cb = _mxu(cm, cat([bm] * SSD_REP + pad_rows, axis=0), _NT)
            pad_rows = [] if reps == gs else [jnp.zeros((gs - reps, SSD_GW), F32)]
            xbd = jnp.where(diag, cat([xdt] * SSD_REP + pad_rows, axis=0), 0.0)
            st = ST[i * SSD_GROUPS + g]
            y = _mxu(cb * seg[:, g * gs:(g + 1) * gs], xbd, _NN) + _mxu(cm, st, _NN) * jnp.exp(c_x[:, gl])
            y = (y + sx * dvec_ref[:, gl]) * _silu(sz_ref[rows, gl])
            y = y * lax.rsqrt(jnp.mean(y * y, axis=-1, keepdims=True) + 1e-6) * nw_ref[:, gl]
            mix_ref[rows, gl] = y.astype(BF16)
            c_end = c_x[L - 1:L, gl]
            ST[i * SSD_GROUPS + g] = st * jnp.exp(c_end) + _mxu(bm, xdt * jnp.exp(c_end - c_x[:, gl]), _TN)
        if nchunk > 1:
            tail[i] = xbc_ref[i * L + L - 8:(i + 1) * L, :]

    @pl.when(ci == nchunk - 1)
    def _():
        for i in range(nseq):
            conv_out_ref[i] = xbc_ref[i * L + t_real - (CONV_W - 1):i * L + t_real, :]
            for g in range(SSD_GROUPS):
                s_ref[i, g * SSD_REP:(g + 1) * SSD_REP] = ST[i * SSD_GROUPS + g].T.reshape(
                    SSD_REP, SSD_HEAD_DIM, SSD_STATE)


def _ssd(h, mix_prev, grp, P, st_in, conv_in, s_prev, p):
    L = grp.chunk
    nchunk = grp.t_rows // L
    gs = _ssd_lanes(L)
    tail = (SSD_HEADS, SSD_HEAD_DIM, SSD_STATE)
    ns = grp.nseq
    s_ins, s_specs, s_out, s_shape = _state_io(tail, p, st_in, grp)
    full = lambda shape: pl.BlockSpec(shape, lambda b, c: (0,) * len(shape))
    heads = jnp.arange(SSD_HEADS)[:, None]
    lane_p = jnp.arange(SSD_WIDTH)[None, :]
    ep = (lane_p // SSD_HEAD_DIM == heads).astype(F32)
    lane_s = jnp.arange(SSD_GROUPS * gs)[None, :]
    in_grp = lane_s % gs
    es = ((in_grp < SSD_REP * L) & ((lane_s // gs) * SSD_REP + in_grp // L == heads)).astype(F32)
    row = lambda a: a.reshape(1, -1)
    ins = [h, h, h, P['ssd_conv_w'], row(P['ssd_conv_b']), row(P['ssd_dt_bias']), row(P['ssd_a_log']),
           row(jnp.repeat(P['ssd_d'], SSD_HEAD_DIM)), row(P['ssd_norm']), ep, es]
    specs = [grp.spec(L, SSD_WIDTH, EVEN_OFF[5]), grp.spec(L, SSD_CONV_DIM, EVEN_OFF[6]), grp.spec(L, 128, EVEN_OFF[3]),
             full((CONV_W, SSD_CONV_DIM)), full((1, SSD_CONV_DIM)), full((1, SSD_HEADS)), full((1, SSD_HEADS)),
             full((1, SSD_WIDTH)), full((1, SSD_WIDTH)), full(ep.shape), full(es.shape)]
    if st_in is not None:
        ins += [conv_in] + s_ins
        specs += [pl.BlockSpec((None, ns, CONV_W - 1, SSD_CONV_DIM), lambda b, c: (p, b, 0, 0))] + s_specs
    any_spec = pl.BlockSpec(memory_space=pl.ANY)
    alias = {}
    for prev, out_idx in ((mix_prev, 0), (s_prev, 1)):
        if prev is not None:
            alias[len(ins)] = out_idx
            ins.append(prev)
            specs.append(any_spec)
    n_prev = (mix_prev is not None) + (s_prev is not None)
    return pl.pallas_call(
        functools.partial(_ssd_kernel, L=L, nchunk=nchunk, t_real=grp.t_real or L, nseq=ns,
                          has_state=st_in is not None, has_prev=n_prev),
        grid=(grp.steps, nchunk),
        in_specs=specs,
        out_specs=[grp.spec(L, SSD_WIDTH, 0), s_out,
                   pl.BlockSpec((ns, CONV_W - 1, SSD_CONV_DIM), lambda b, c: (b, 0, 0))],
        out_shape=[jax.ShapeDtypeStruct((h.shape[0], MIX_WIDTH), BF16), s_shape,
                   jax.ShapeDtypeStruct((grp.bsz, CONV_W - 1, SSD_CONV_DIM), F32)],
        scratch_shapes=[pltpu.VMEM((ns * SSD_GROUPS, SSD_STATE, SSD_GW), F32),
                        pltpu.VMEM((ns, 8, SSD_CONV_DIM), F32)],
        input_output_aliases=alias,
        compiler_params=_cparams(("parallel", "arbitrary")),
        name="ssd",
    )(*ins)


def _gdn_kernel(qkv_ref, cz_ref, sm_ref, cw_ref, alog_ref, dtb_ref, nw_ref, *rest, L, nchunk, rounds, t_real, nseq,
                has_state, has_prev):
    n_opt = 2 * has_state + has_prev
    mix_ref, s_ref, conv_out_ref, S, tail = rest[n_opt:]
    ci = pl.program_id(1)

    @pl.when(ci == 0)
    def _():
        tail[...] = jnp.zeros_like(tail)
        if has_state:
            for i in range(nseq):
                tail[i, 8 - (CONV_W - 1):8, :] = rest[0][i]
            S[...] = rest[1][...]
        else:
            S[...] = jnp.zeros_like(S)

    n2 = 2 * L
    d = GDN_HEAD_DIM
    cat = jnp.concatenate
    row8 = lax.broadcasted_iota(jnp.int32, (8, d), 0)

    def conv_tile(i, c0):
        u = qkv_ref[i * L:(i + 1) * L, c0:c0 + d]
        p8 = tail[i, :, c0:c0 + d]
        w = cw_ref[:, c0:c0 + d]
        acc = u * w[CONV_W - 1:CONV_W]
        for j in range(1, CONV_W):
            acc = acc + _shifted(u, p8, j, row8) * w[CONV_W - 1 - j:CONV_W - j]
        return _silu(acc)

    def l2n(x):
        return x * lax.rsqrt(jnp.sum(x * x, axis=-1, keepdims=True) + 1e-6)

    row = lax.broadcasted_iota(jnp.int32, (L, L), 0)
    col = lax.broadcasted_iota(jnp.int32, (L, L), 1)
    tril = (col <= row).astype(F32)
    r2 = lax.broadcasted_iota(jnp.int32, (n2, n2), 0)
    c2 = lax.broadcasted_iota(jnp.int32, (n2, n2), 1)
    same = (r2 >= L) == (c2 >= L)
    strict = same & (c2 < r2)
    incl = same & (c2 <= r2)
    upper = same & (r2 <= c2)
    eye = (r2 == c2).astype(F32)
    zl = jnp.zeros((L, d), F32)
    units = [(i, j) for i in range(nseq) for j in range(GDN_HEADS // 2)]
    nn, tt, qk, kq, kdec, ec, bcol, elast, vst = [], [], [], [], [], [], [], [], []
    for i in range(nseq):
        sm = sm_ref[i * L:(i + 1) * L, :]
        beta_all = _sigmoid(sm[:, :GDN_HEADS])
        g_all = -jnp.exp(alog_ref[...]) * _softplus(sm[:, GDN_HEADS:2 * GDN_HEADS] + dtb_ref[...])
        if t_real < L:
            valid = lax.broadcasted_iota(jnp.int32, (L, GDN_HEADS), 0) < t_real
            beta_all = jnp.where(valid, beta_all, 0.0)
            g_all = jnp.where(valid, g_all, 0.0)
        c_all = _mxu_f32(tril, g_all, _NN)
        for j in range(GDN_HEADS // 2):
            h0, h1 = 2 * j, 2 * j + 1
            stack_col = lambda a: cat([a[:, h0:h0 + 1], a[:, h1:h1 + 1]], axis=0)
            c_col = stack_col(c_all)
            beta_col = stack_col(beta_all)
            c_row = jnp.sum(jnp.where(upper, stack_col(g_all), 0.0), axis=0, keepdims=True)
            decay = jnp.exp(jnp.where(incl, c_col - c_row, -jnp.inf))
            last = lambda rows: cat([jnp.broadcast_to(c_all[L - 1:L, h0:h0 + 1], (rows, 1)),
                                     jnp.broadcast_to(c_all[L - 1:L, h1:h1 + 1], (rows, 1))], axis=0)
            q0, q1 = (l2n(conv_tile(i, h * d)) * d ** -0.5 for h in (h0, h1))
            k0, k1 = (l2n(conv_tile(i, GDN_WIDTH + h * d)) for h in (h0, h1))
            vst.append(cat([conv_tile(i, 2 * GDN_WIDTH + h0 * d), conv_tile(i, 2 * GDN_WIDTH + h1 * d)], axis=0))
            k_st = cat([cat([k0, zl], axis=1), cat([zl, k1], axis=1)], axis=0)
            q_st = cat([cat([q0, zl], axis=1), cat([zl, q1], axis=1)], axis=0)
            both = cat([k_st, q_st], axis=0)
            full = _mxu(both, k_st, _NT)
            a = jnp.where(strict, full[:n2] * decay * beta_col, 0.0)
            tt.append(a)
            qk.append(full[n2:] * decay)
            kq.append(both)
            kdec.append(k_st * jnp.exp(last(L) - c_col))
            ec.append(jnp.exp(c_col))
            bcol.append(beta_col)
            elast.append(jnp.exp(last(d)))
    tt = _unit_lower_inverse(tt, L, eye, r2 & (L - 1), c2 & (L - 1))
    s_old, ksqs, u = [], [], []
    for n, (i, j) in enumerate(units):
        s_old.append(cat([S[i, 2 * j], S[i, 2 * j + 1]], axis=0))
        ksqs.append(_mxu(kq[n], s_old[n], _NN))
    for n in range(len(units)):
        u.append(_mxu(tt[n], bcol[n] * (vst[n] - ec[n] * ksqs[n][:n2]), _NN))
    nw = nw_ref[...]
    for n, (i, j) in enumerate(units):
        rows = slice(i * L, (i + 1) * L)
        o = ec[n] * ksqs[n][n2:] + _mxu(qk[n], u[n], _NN)
        for hh, oh in ((2 * j, o[:L]), (2 * j + 1, o[L:])):
            cols = slice(hh * d, (hh + 1) * d)
            y = oh * lax.rsqrt(jnp.mean(oh * oh, axis=-1, keepdims=True) + 1e-6) * nw
            mix_ref[rows, cols] = (y * _silu(cz_ref[rows, cols])).astype(BF16)
        new = s_old[n] * elast[n] + _mxu(kdec[n], u[n], _TN)
        S[i, 2 * j] = new[:d]
        S[i, 2 * j + 1] = new[d:]
    if nchunk > 1:
        for i in range(nseq):
            tail[i] = qkv_ref[i * L + L - 8:(i + 1) * L, :]

    @pl.when(ci == nchunk - 1)
    def _():
        s_ref[...] = S[...]
        for i in range(nseq):
            conv_out_ref[i] = qkv_ref[i * L + t_real - (CONV_W - 1):i * L + t_real, :]


def _gdn(h, mix_prev, grp, P, st_in, conv_in, s_prev, p):
    L = grp.chunk
    nchunk = grp.t_rows // L
    tail = (GDN_HEADS, GDN_HEAD_DIM, GDN_HEAD_DIM)
    ns = grp.nseq
    s_ins, s_specs, s_out, s_shape = _state_io(tail, p, st_in, grp)
    cw = 3 * GDN_WIDTH
    full = lambda shape: pl.BlockSpec(shape, lambda b, c: (0,) * len(shape))
    ins = [h, h, h, P['gdn_conv_w'], P['gdn_a_log'].reshape(1, GDN_HEADS), P['gdn_dt_bias'].reshape(1, GDN_HEADS),
           P['gdn_norm'].reshape(1, GDN_HEAD_DIM)]
    specs = [grp.spec(L, cw, ODD_OFF[0]), grp.spec(L, GDN_WIDTH, ODD_OFF[1]), grp.spec(L, 128, ODD_OFF[2]),
             full((CONV_W, cw)), full((1, GDN_HEADS)), full((1, GDN_HEADS)), full((1, GDN_HEAD_DIM))]
    if st_in is not None:
        ins += [conv_in] + s_ins
        specs += [pl.BlockSpec((None, ns, CONV_W - 1, cw), lambda b, c: (p, b, 0, 0))] + s_specs
    any_spec = pl.BlockSpec(memory_space=pl.ANY)
    alias = {}
    for prev, out_idx in ((mix_prev, 0), (s_prev, 1)):
        if prev is not None:
            alias[len(ins)] = out_idx
            ins.append(prev)
            specs.append(any_spec)
    n_prev = (mix_prev is not None) + (s_prev is not None)
    return pl.pallas_call(
        functools.partial(_gdn_kernel, L=L, nchunk=nchunk, rounds=int(math.log2(L)), t_real=grp.t_real or L,
                          nseq=ns, has_state=st_in is not None, has_prev=n_prev),
        grid=(grp.steps, nchunk),
        in_specs=specs,
        out_specs=[grp.spec(L, GDN_WIDTH, 0), s_out,
                   pl.BlockSpec((ns, CONV_W - 1, cw), lambda b, c: (b, 0, 0))],
        out_shape=[jax.ShapeDtypeStruct((h.shape[0], MIX_WIDTH), BF16), s_shape,
                   jax.ShapeDtypeStruct((grp.bsz, CONV_W - 1, cw), F32)],
        scratch_shapes=[pltpu.VMEM((ns,) + tail, F32), pltpu.VMEM((ns, 8, cw), F32)],
        input_output_aliases=alias,
        compiler_params=_cparams(("parallel", "arbitrary")),
        name="gdn",
    )(*ins)


RWKV_PAIRS = RWKV_HEADS // 2
RWKV_PW = 2 * RWKV_HEAD_DIM


def _rwkv_kernel(r_ref, k_ref, v_ref, xwa_ref, gate_ref, mu_ref, w0_ref, w2_ref, a0_ref, a2_ref, kkp_ref, ka_ref,
                 rk_ref, lng_ref, lnb_ref, *rest, L, nchunk, rounds, t_real, nseq, has_state, has_prev):
    n_opt = 2 * has_state + has_prev
    mix_ref, s_ref, shift_out_ref, S, last = rest[n_opt:]
    ci = pl.program_id(1)
    n = RWKV_HEAD_DIM
    w3 = 3 * RWKV_WIDTH
    cat = jnp.concatenate
    pairs = range(RWKV_PAIRS)

    @pl.when(ci == 0)
    def _():
        last[...] = jnp.zeros_like(last)
        if has_state:
            zn = jnp.zeros((n, n), F32)
            for i in range(nseq):
                last[i, 7:8, :] = rest[0][i]
                for j in pairs:
                    S[i * RWKV_PAIRS + j] = cat([cat([rest[1][i, 2 * j], zn], axis=1),
                                                 cat([zn, rest[1][i, 2 * j + 1]], axis=1)], axis=0)
        else:
            S[...] = jnp.zeros_like(S)

    n2 = 2 * L
    row = lax.broadcasted_iota(jnp.int32, (L, L), 0)
    col = lax.broadcasted_iota(jnp.int32, (L, L), 1)
    tril = (col <= row).astype(F32)
    r2 = lax.broadcasted_iota(jnp.int32, (n2, n2), 0)
    c2 = lax.broadcasted_iota(jnp.int32, (n2, n2), 1)
    same = (r2 >= L) == (c2 >= L)
    strict = same & (c2 < r2)
    incl = same & (c2 <= r2)
    eye = (r2 == c2).astype(F32)
    lane = lax.broadcasted_iota(jnp.int32, (L, RWKV_PW), 1)
    lo = lane < n

    def stack(x):
        return cat([jnp.where(lo, x, 0.0), jnp.where(lo, 0.0, x)], axis=0)

    row8 = lax.broadcasted_iota(jnp.int32, (8, RWKV_PW), 0)
    valid = lax.broadcasted_iota(jnp.int32, (L, RWKV_PW), 0) < t_real

    def seg_sum(x):
        s_lo = jnp.sum(jnp.where(lo, x, 0.0), axis=-1, keepdims=True)
        s_hi = jnp.sum(jnp.where(lo, 0.0, x), axis=-1, keepdims=True)
        return jnp.where(lo, s_lo, s_hi)

    def shift_mix(ref, i, c_src, c_all):
        x = ref[i * L:(i + 1) * L, c_src:c_src + RWKV_PW]
        prev = _shifted(x, last[i, :, c_all:c_all + RWKV_PW], 1, row8)
        return x + (prev - x) * mu_ref[:, c_all:c_all + RWKV_PW]

    xwa = cat([shift_mix(xwa_ref, i, 0, w3) for i in range(nseq)], axis=0)
    lr_w_all = _mxu_f32(jnp.tanh(xwa), w2_ref[...], _NN)
    lr_a_all = _mxu_f32(xwa, a2_ref[...], _NN)
    units = [(i, j) for i in range(nseq) for j in pairs]
    a_ak, a_rk, a_rb, nn, tt, sread, kdbd, egl, vs, bonus = [], [], [], [], [], [], [], [], [], []
    for i, j in units:
        sl = slice(j * RWKV_PW, (j + 1) * RWKV_PW)
        r = shift_mix(r_ref, i, j * RWKV_PW, j * RWKV_PW)
        k = shift_mix(k_ref, i, j * RWKV_PW, RWKV_WIDTH + j * RWKV_PW)
        v = shift_mix(v_ref, i, j * RWKV_PW, 2 * RWKV_WIDTH + j * RWKV_PW)
        w_log = -_softplus(-(w0_ref[:, sl] + lr_w_all[i * L:(i + 1) * L, sl])) - 0.5
        lw = -jnp.exp(w_log)
        a7 = _sigmoid(a0_ref[:, sl] + lr_a_all[i * L:(i + 1) * L, sl])
        kx = k * kkp_ref[:, sl]
        kk = kx * lax.rsqrt(seg_sum(kx * kx) + 1e-6)
        k = k * (1.0 + (a7 - 1.0) * ka_ref[:, sl])
        if t_real < L:
            lw, kk, k, v = (jnp.where(valid, a, 0.0) for a in (lw, kk, k, v))
        b = kk * a7
        bonus.append(seg_sum(r * k * rk_ref[:, sl]) * v)
        vs.append(stack(v))
        g = _mxu_f32(tril, lw, _NN)
        gp = g - lw
        gm = g[L // 2 - 1:L // 2, :]
        gl = g[L - 1:L, :]
        e_neg = jnp.exp(gm - g)
        lhs = cat([stack(kk * jnp.exp(gp - gm)), stack(r * jnp.exp(g - gm))], axis=0)
        rhs = cat([stack(b * e_neg), stack(k * e_neg)], axis=0)
        full = _mxu(lhs, rhs, _NT)
        a_ab = jnp.where(strict, full[:n2, :n2], 0.0)
        a_ak.append(jnp.where(strict, full[:n2, n2:], 0.0))
        a_rb.append(jnp.where(incl, full[n2:, :n2], 0.0))
        a_rk.append(jnp.where(incl, full[n2:, n2:], 0.0))
        tt.append(a_ab)
        sread.append(cat([stack(kk * jnp.exp(gp)), stack(r * jnp.exp(g))], axis=0))
        dec = jnp.exp(gl - g)
        kdbd.append(cat([stack(k * dec), stack(-b * dec)], axis=0))
        egl.append(jnp.exp(gl))
    tt = _unit_lower_inverse(tt, L, eye, r2 & (L - 1), c2 & (L - 1))
    s_old, sr, av, u = [], [], [], []
    for m, (i, j) in enumerate(units):
        s_old.append(S[i * RWKV_PAIRS + j])
        sr.append(_mxu(sread[m], s_old[m], _NT))
        av.append(_mxu(cat([a_ak[m], a_rk[m]], axis=0), vs[m], _NN))
    for m in range(len(units)):
        u.append(_mxu(tt[m], sr[m][:n2] + av[m][:n2], _NN))
    for m, (i, j) in enumerate(units):
        sl = slice(j * RWKV_PW, (j + 1) * RWKV_PW)
        rows = slice(i * L, (i + 1) * L)
        o = sr[m][n2:] + av[m][n2:] - _mxu(a_rb[m], u[m], _NN)
        o = o[:L] + o[L:]
        oc = o - seg_sum(o) * (1.0 / n)
        gn = oc * lax.rsqrt(seg_sum(oc * oc) * (1.0 / n) + RWKV_GN_EPS)
        y = gn * lng_ref[:, sl] + lnb_ref[:, sl] + bonus[m]
        mix_ref[rows, sl] = (y * _silu(gate_ref[rows, sl])).astype(BF16)
        S[i * RWKV_PAIRS + j] = s_old[m] * egl[m] + _mxu(cat([vs[m], u[m]], axis=0), kdbd[m], _TN)
    pieces = ((r_ref, 0, RWKV_WIDTH), (k_ref, RWKV_WIDTH, RWKV_WIDTH), (v_ref, 2 * RWKV_WIDTH, RWKV_WIDTH),
              (xwa_ref, w3, RWKV_PW))
    if nchunk > 1:
        for i in range(nseq):
            for ref, c0, wd in pieces:
                last[i, :, c0:c0 + wd] = ref[i * L + L - 8:(i + 1) * L, :]

    @pl.when(ci == nchunk - 1)
    def _():
        for i in range(nseq):
            for ref, c0, wd in pieces:
                shift_out_ref[i, :, c0:c0 + wd] = ref[i * L + t_real - 1:i * L + t_real, :]
            for j in pairs:
                s_ref[i, 2 * j] = S[i * RWKV_PAIRS + j][:n, :n]
                s_ref[i, 2 * j + 1] = S[i * RWKV_PAIRS + j][n:, n:]


def _rwkv(h, mix_prev, grp, P, st_in, shift_in, s_prev, p):
    L = grp.chunk
    nchunk = grp.t_rows // L
    tail = (RWKV_HEADS, RWKV_HEAD_DIM, RWKV_HEAD_DIM)
    ns = grp.nseq
    s_ins, s_specs, s_out, s_shape = _state_io(tail, p, st_in, grp)
    full = lambda shape: pl.BlockSpec(shape, lambda b, c: (0,) * len(shape))
    row = lambda a: a.reshape(1, -1)
    zr = jnp.zeros((RWKV_HEAD_DIM, RWKV_WIDTH), F32)
    w2 = jnp.concatenate([P['rwkv_w2'], zr], axis=0)
    a2 = jnp.concatenate([zr, P['rwkv_a2']], axis=0)
    off = ODD_OFF[4]
    ins = [h, h, h, h, h, row(P['rwkv_mu']), row(P['rwkv_w0']), w2, row(P['rwkv_a0']), a2, row(P['rwkv_kk']),
           row(P['rwkv_ka']), row(P['rwkv_rk']), row(P['rwkv_ln_g']), row(P['rwkv_ln_b'])]
    vec = full((1, RWKV_WIDTH))
    specs = [grp.spec(L, RWKV_WIDTH, off), grp.spec(L, RWKV_WIDTH, off + RWKV_WIDTH),
             grp.spec(L, RWKV_WIDTH, off + 2 * RWKV_WIDTH), grp.spec(L, RWKV_PW, off + 3 * RWKV_WIDTH),
             grp.spec(L, RWKV_WIDTH, ODD_OFF[5]), full((1, RWKV_SHIFT_DIM)), vec, full((RWKV_PW, RWKV_WIDTH)), vec,
             full((RWKV_PW, RWKV_WIDTH)), vec, vec, vec, vec, vec]
    if st_in is not None:
        ins += [shift_in] + s_ins
        specs += [pl.BlockSpec((None, ns, 1, RWKV_SHIFT_DIM), lambda b, c: (p, b, 0, 0))] + s_specs
    any_spec = pl.BlockSpec(memory_space=pl.ANY)
    alias = {}
    for prev, out_idx in ((mix_prev, 0), (s_prev, 1)):
        if prev is not None:
            alias[len(ins)] = out_idx
            ins.append(prev)
            specs.append(any_spec)
    n_prev = (mix_prev is not None) + (s_prev is not None)
    return pl.pallas_call(
        functools.partial(_rwkv_kernel, L=L, nchunk=nchunk, rounds=int(math.log2(L)), t_real=grp.t_real or L,
                          nseq=ns, has_state=st_in is not None, has_prev=n_prev),
        grid=(grp.steps, nchunk),
        in_specs=specs,
        out_specs=[grp.spec(L, RWKV_WIDTH, GDN_WIDTH), s_out,
                   pl.BlockSpec((ns, 1, RWKV_SHIFT_DIM), lambda b, c: (b, 0, 0))],
        out_shape=[jax.ShapeDtypeStruct((h.shape[0], MIX_WIDTH), BF16), s_shape,
                   jax.ShapeDtypeStruct((grp.bsz, 1, RWKV_SHIFT_DIM), F32)],
        scratch_shapes=[pltpu.VMEM((ns * RWKV_PAIRS, RWKV_PW, RWKV_PW), F32),
                        pltpu.VMEM((ns, 8, RWKV_SHIFT_DIM), F32)],
        input_output_aliases=alias,
        compiler_params=_cparams(("parallel", "arbitrary")),
        name="rwkv7",
    )(*ins)


def _pad_t(a, t_to):
    t = a.shape[1]
    if t == t_to:
        return a
    return jnp.pad(a, [(0, 0), (0, t_to - t)] + [(0, 0)] * (a.ndim - 2))


def _even_mix(h, mix, grp, mem_k, mem_v, layer, st_in, st_prev, conv_in, P):
    p = layer // 2
    mix, s_ssd, s_conv = _ssd(h, mix, grp, P, st_in['ssd'], conv_in, st_prev['ssd'], p)
    gla_l, gla_tb = (GLA_CHUNK, 256) if grp.t_real is None else (grp.chunk, grp.chunk)
    mix, s_gla = _gla(h, mix, grp, P, st_in['gla'], st_prev['gla'], p, gla_l, gla_tb)
    mix = _mem_attention(h, mix, grp, EVEN_OFF[8], EVEN_OFF[9], mem_k, mem_v, layer, min(512, grp.t_rows))
    return mix, dict(gla=s_gla, ssd=s_ssd), s_conv


def _odd_mix(h, mix, grp, mem_k, mem_v, layer, st_in, st_prev, conv_in, shift_in, P):
    p = layer // 2
    mix, s_gdn, s_conv = _gdn(h, mix, grp, P, st_in['gdn'], conv_in, st_prev['gdn'], p)
    mix, s_rwkv, s_shift = _rwkv(h, mix, grp, P, st_in['rwkv'], shift_in, st_prev['rwkv'], p)
    mix = _mem_attention(h, mix, grp, ODD_OFF[6], ODD_OFF[7], mem_k, mem_v, layer, min(512, grp.t_rows))
    return mix, dict(gdn=s_gdn, rwkv=s_rwkv), s_conv, s_shift.reshape(grp.bsz, RWKV_SHIFT_DIM)


def kernel(x_prompt, x_sample, mem_prompt, cache_mem_k, cache_mem_v, state_gla, state_ssd, state_ssd_conv, state_gdn, state_gdn_conv, state_rwkv, state_rwkv_shift, mem_w_kv, ev_w_in, ev_gla_w2, ev_gla_b, ev_gla_norm, ev_ssd_conv_w, ev_ssd_conv_b, ev_ssd_dt_bias, ev_ssd_a_log, ev_ssd_d, ev_ssd_norm, ev_w_out, ev_ln_g, ev_ln_b, od_w_in, od_gdn_conv_w, od_gdn_dt_bias, od_gdn_a_log, od_gdn_norm, od_rwkv_mu, od_rwkv_w0, od_rwkv_w2, od_rwkv_a0, od_rwkv_a2, od_rwkv_kk, od_rwkv_ka, od_rwkv_rk, od_rwkv_ln_g, od_rwkv_ln_b, od_w_out, od_ln_g, od_ln_b):
    ev = dict(w_in=ev_w_in, gla_w2=ev_gla_w2, gla_b=ev_gla_b, gla_norm=ev_gla_norm,
              ssd_conv_w=ev_ssd_conv_w, ssd_conv_b=ev_ssd_conv_b, ssd_dt_bias=ev_ssd_dt_bias,
              ssd_a_log=ev_ssd_a_log, ssd_d=ev_ssd_d, ssd_norm=ev_ssd_norm,
              w_out=ev_w_out, ln_g=ev_ln_g, ln_b=ev_ln_b)
    od = dict(w_in=od_w_in, gdn_conv_w=od_gdn_conv_w, gdn_dt_bias=od_gdn_dt_bias, gdn_a_log=od_gdn_a_log,
              gdn_norm=od_gdn_norm, rwkv_mu=od_rwkv_mu, rwkv_w0=od_rwkv_w0, rwkv_w2=od_rwkv_w2,
              rwkv_a0=od_rwkv_a0, rwkv_a2=od_rwkv_a2, rwkv_kk=od_rwkv_kk, rwkv_ka=od_rwkv_ka,
              rwkv_rk=od_rwkv_rk, rwkv_ln_g=od_rwkv_ln_g, rwkv_ln_b=od_rwkv_ln_b,
              w_out=od_w_out, ln_g=od_ln_g, ln_b=od_ln_b)
    bp, tp, _ = x_prompt.shape
    bs, ts, _ = x_sample.shape
    mp, ms = bp * tp, bs * SMALL_T
    grp_p = _Group(bp, tp, None, 0, GDN_CHUNK)
    grp_s = _Group(bs, SMALL_T, ts, mp, SMALL_T, nseq=SAMPLE_NSEQ)

    w_kv = jnp.moveaxis(mem_w_kv, 0, 1).reshape(D_MODEL, DEPTH * 2 * MEM_WIDTH).astype(BF16)
    kv = _matmul(mem_prompt.reshape(bp * MEM_LEN, D_MODEL).astype(BF16), w_kv, 512, 1024)
    kv6 = kv.reshape(bp, MEM_LEN, DEPTH, 2, MEM_HEADS, MEM_HEAD_DIM)
    mem_k_p = jnp.moveaxis(kv6[:, :, :, 0], 2, 0)
    mem_v_p = jnp.moveaxis(kv6[:, :, :, 1], 2, 0)
    mk_s, mv_s = _cache_view(cache_mem_k), _cache_view(cache_mem_v)

    x = jnp.concatenate([x_prompt.reshape(mp, D_MODEL),
                         _pad_t(x_sample, SMALL_T).reshape(ms, D_MODEL)], axis=0)
    x_bf = x.astype(BF16)
    zp = lambda shape: jnp.zeros(shape, F32)
    names = ('gla', 'ssd', 'gdn', 'rwkv')
    none = {n: None for n in names}
    in_s = dict(gla=state_gla, ssd=state_ssd, gdn=state_gdn, rwkv=state_rwkv)
    shift_s = state_rwkv_shift.reshape(N_PAIRS, bs, 1, RWKV_SHIFT_DIM)
    out_p, out_s = dict(none), dict(none)
    small_p = {n: [] for n in ('ssd_conv', 'gdn_conv', 'rwkv_shift')}
    small_s = {n: [] for n in small_p}
    tm = 1024
    w_in_ev = _pack_w_in(ev_w_in, EVEN_SIZES, EVEN_ORDER, EVEN_N)
    w_in_od = _pack_w_in(od_w_in, ODD_SIZES, ODD_ORDER, ODD_N)
    w_out_ev = jnp.concatenate([ev_w_out[:, GLA_WIDTH:GLA_WIDTH + SSD_WIDTH].astype(BF16),
                                ev_w_out[:, :GLA_WIDTH].astype(BF16),
                                ev_w_out[:, GLA_WIDTH + SSD_WIDTH:].astype(BF16)], axis=1)
    w_out_od = od_w_out.astype(BF16)
    small = lambda d: {n: w[p] for n, w in d.items() if n not in ('w_in', 'w_out')}
    for layer in range(DEPTH):
        p = layer // 2
        if layer % 2 == 0:
            P = small(ev)
            h = _matmul(x_bf, w_in_ev, tm, PROJ_TN, p=p)
            mix, new, c1 = _even_mix(h, None, grp_p, kv, kv, layer, none, out_p, None, P)
            out_p.update(new)
            mix, new, c2 = _even_mix(h, mix, grp_s, mk_s, mv_s, layer, in_s, out_s, state_ssd_conv, P)
            out_s.update(new)
            small_p['ssd_conv'].append(c1)
            small_s['ssd_conv'].append(c2)
            w_out = w_out_ev
        else:
            P = small(od)
            h = _matmul(x_bf, w_in_od, tm, PROJ_TN, p=p)
            mix, new, c1, h1 = _odd_mix(h, None, grp_p, kv, kv, layer, none, out_p, None, None, P)
            out_p.update(new)
            mix, new, c2, h2 = _odd_mix(h, mix, grp_s, mk_s, mv_s, layer, in_s, out_s, state_gdn_conv, shift_s, P)
            out_s.update(new)
            small_p['gdn_conv'].append(c1)
            small_s['gdn_conv'].append(c2)
            small_p['rwkv_shift'].append(h1)
            small_s['rwkv_shift'].append(h2)
            w_out = w_out_od
        x, x_bf = _out_ln(mix, w_out, p, x, P['ln_g'], P['ln_b'])

    y_prompt = x[:mp].reshape(bp, tp, D_MODEL)
    y_sample = x[mp:].reshape(bs, SMALL_T, D_MODEL)[:, :ts]
    st = lambda d, n: jnp.stack(d[n])
    return (y_prompt, y_sample, mem_k_p, mem_v_p,
            out_p['gla'], out_s['gla'], out_p['ssd'], out_s['ssd'],
            st(small_p, 'ssd_conv'), st(small_s, 'ssd_conv'), out_p['gdn'], out_s['gdn'],
            st(small_p, 'gdn_conv'), st(small_s, 'gdn_conv'), out_p['rwkv'], out_s['rwkv'],
            st(small_p, 'rwkv_shift'), st(small_s, 'rwkv_shift'))
```

```python
import functools

import numpy as np
import jax
import jax.numpy as jnp
from jax import lax
from jax.experimental import pallas as pl
from jax.experimental.pallas import tpu as pltpu

F32 = jnp.float32
BF16 = jnp.bfloat16
HI = lax.Precision.HIGHEST

D_MODEL = 2048
DEPTH = 4
N_PAIRS = DEPTH // 2
CONV_W = 4
MEM_LEN = 256
MEM_HEADS = 4
MEM_HEAD_DIM = 256
MEM_WIDTH = 1024
GLA_HEADS = 4
GLA_DK = 128
GLA_DV = 256
GLA_QK = 512
GLA_WIDTH = 1024
GLA_RANK = 16
GLA_TAU = 16.0
SSD_WIDTH = 2048
SSD_HEAD_DIM = 64
SSD_HEADS = 32
SSD_GROUPS = 4
SSD_REP = 8
SSD_STATE = 128
SSD_CONV_DIM = SSD_WIDTH + 2 * SSD_GROUPS * SSD_STATE
GDN_WIDTH = 2048
GDN_HEAD_DIM = 128
GDN_HEADS = 16
RWKV_WIDTH = 1024
RWKV_HEAD_DIM = 64
RWKV_HEADS = 16
RWKV_W_RANK = 64
RWKV_A_RANK = 64
RWKV_SHIFT_DIM = 3 * RWKV_WIDTH + RWKV_W_RANK + RWKV_A_RANK
RWKV_GN_EPS = 64e-5
EVEN_SIZES = (GLA_QK, GLA_QK, GLA_WIDTH, GLA_RANK, GLA_WIDTH, SSD_WIDTH, SSD_CONV_DIM, SSD_HEADS,
              MEM_WIDTH, MEM_WIDTH)
ODD_SIZES = (3 * GDN_WIDTH, GDN_WIDTH, GDN_HEADS, GDN_HEADS, RWKV_SHIFT_DIM, RWKV_WIDTH, MEM_WIDTH, MEM_WIDTH)
MIX_WIDTH = 4096
DEEPNORM_ALPHA = (2 * DEPTH) ** 0.25

EVEN_ORDER = (5, 2, 4, 8, 9, 6, 0, 1, 3, 7)
ODD_ORDER = (0, 1, 5, 6, 7, 4, 2, 3)
PROJ_TN = 768
VMEM_LIMIT = 60 * 1024 * 1024

GLA_CHUNK = 16
SEQ_CHUNK = 64
SMALL_T = 8
SAMPLE_NSEQ = 4

_NN = ((1,), (0,))
_NT = ((1,), (1,))
_TN = ((0,), (0,))


def _packed_layout(sizes, order):
    offs, o = {}, 0
    for i in order:
        offs[i] = o
        o += sizes[i]
    total = -(-o // PROJ_TN) * PROJ_TN
    return offs, total


EVEN_OFF, EVEN_N = _packed_layout(EVEN_SIZES, EVEN_ORDER)
ODD_OFF, ODD_N = _packed_layout(ODD_SIZES, ODD_ORDER)


def _pack_w_in(w, sizes, order, total):
    ends = np.cumsum(sizes)
    parts = [w[..., ends[i] - sizes[i]:ends[i]].astype(BF16) for i in order]
    used = sum(sizes)
    if total > used:
        parts.append(jnp.zeros(w.shape[:-1] + (total - used,), BF16))
    return jnp.concatenate(parts, axis=-1)


def _cparams(sem):
    return pltpu.CompilerParams(dimension_semantics=sem, vmem_limit_bytes=VMEM_LIMIT)


def _mxu(a, b, dims):
    return lax.dot_general(a.astype(BF16), b.astype(BF16), (dims, ((), ())), preferred_element_type=F32)


def _mxu_f32(a, b, dims):
    return lax.dot_general(a, b, (dims, ((), ())), precision=HI, preferred_element_type=F32)


def _sigmoid(x):
    return 1.0 / (1.0 + jnp.exp(-x))


def _silu(x):
    return x * _sigmoid(x)


def _softplus(x):
    return jnp.maximum(x, 0.0) + jnp.log(1.0 + jnp.exp(-jnp.abs(x)))


SUBLANES = 8


def _shifted(u, prev8, j, row8):
    ru = pltpu.roll(u, j, 0)
    top = jnp.where(row8 < j, pltpu.roll(prev8, j, 0), ru[:SUBLANES])
    return top if u.shape[0] == SUBLANES else jnp.concatenate([top, ru[SUBLANES:]], axis=0)


class _Group:
    def __init__(self, bsz, t_rows, t_real, row0, chunk, nseq=1):
        assert nseq == 1 or t_rows == chunk
        self.bsz, self.t_rows, self.t_real, self.row0, self.chunk, self.nseq = bsz, t_rows, t_real, row0, chunk, nseq
        self.steps = bsz // nseq

    def spec(self, rows, width, off):
        rows = rows * self.nseq
        assert off % width == 0 and self.row0 % rows == 0 and (self.t_rows * self.nseq) % rows == 0
        base, per, cb = self.row0 // rows, self.t_rows * self.nseq // rows, off // width
        return pl.BlockSpec((rows, width), lambda b, i: (base + b * per + i, cb))


def _alias_last(n_inputs, has_prev, out_index=0):
    return {n_inputs - 1: out_index} if has_prev else {}


TRI_BASE = 4


def _unit_lower_inverse(a, L, eye, r_in, c_in):
    idx = range(len(a))
    base = (r_in // TRI_BASE) == (c_in // TRI_BASE)
    n1 = [-jnp.where(base, a[j], 0.0) for j in idx]
    n2 = [_mxu(n1[j], n1[j], _NN) for j in idx]
    tt = [eye + n1[j] for j in idx]
    tt = [tt[j] + _mxu(tt[j], n2[j], _NN) for j in idx]
    b = TRI_BASE
    while b < L:
        rb, cb = r_in // b, c_in // b
        lower_left = (rb // 2 == cb // 2) & (rb % 2 == 1) & (cb % 2 == 0)
        for j in idx:
            x = _mxu(tt[j], jnp.where(lower_left, a[j], 0.0), _NN)
            tt[j] = tt[j] - _mxu(x, tt[j], _NN)
        b *= 2
    return tt


def _state_io(tail, p, s_in, grp):
    zeros = (0,) * len(tail)
    spec = pl.BlockSpec((None, grp.nseq) + tail, lambda *g: (p, g[0]) + zeros)
    ins, specs = ([s_in], [spec]) if s_in is not None else ([], [])
    shape = jax.ShapeDtypeStruct((N_PAIRS, grp.bsz) + tail, F32)
    return ins, specs, spec, shape


def _mm_kernel(x_ref, w_ref, o_ref, *, precision):
    o_ref[...] = jnp.dot(x_ref[...], w_ref[...], preferred_element_type=F32, precision=precision)


def _matmul(x, w, tm, tn, precision=None, p=None):
    m, k = x.shape
    n = w.shape[-1]
    assert m % tm == 0 and n % tn == 0
    w_spec = (pl.BlockSpec((k, tn), lambda j, i: (0, j)) if p is None
              else pl.BlockSpec((None, k, tn), lambda j, i: (p, 0, j)))
    return pl.pallas_call(
        functools.partial(_mm_kernel, precision=precision),
        grid=(n // tn, m // tm),
        in_specs=[pl.BlockSpec((tm, k), lambda j, i: (i, 0)), w_spec],
        out_specs=pl.BlockSpec((tm, tn), lambda j, i: (i, j)),
        out_shape=jax.ShapeDtypeStruct((m, n), F32),
        compiler_params=_cparams(("parallel", "parallel")),
        name="matmul",
    )(x, w)


def _out_ln_kernel(mix_ref, w_ref, x_ref, g_ref, b_ref, y_ref, ybf_ref, acc, *, nk):
    kk = pl.program_id(1)

    @pl.when(kk == 0)
    def _():
        acc[...] = jnp.zeros_like(acc)

    acc[...] += jnp.dot(mix_ref[...].astype(BF16), w_ref[...], preferred_element_type=F32)

    @pl.when(kk == nk - 1)
    def _():
        z = DEEPNORM_ALPHA * x_ref[...] + acc[...]
        zc = z - jnp.mean(z, axis=-1, keepdims=True)
        var = jnp.mean(zc * zc, axis=-1, keepdims=True)
        y = zc * lax.rsqrt(var + 1e-5) * g_ref[...] + b_ref[...]
        y_ref[...] = y
        ybf_ref[...] = y.astype(BF16)


def _out_ln(mix, w, p, x, g, b, tm=512, tk=2048):
    m, k = mix.shape
    d = w.shape[-1]
    nk = k // tk
    return pl.pallas_call(
        functools.partial(_out_ln_kernel, nk=nk),
        grid=(m // tm, nk),
        in_specs=[pl.BlockSpec((tm, tk), lambda i, j: (i, j)),
                  pl.BlockSpec((None, tk, d), lambda i, j: (p, j, 0)),
                  pl.BlockSpec((tm, d), lambda i, j: (i, 0)),
                  pl.BlockSpec((1, d), lambda i, j: (0, 0)),
                  pl.BlockSpec((1, d), lambda i, j: (0, 0))],
        out_specs=[pl.BlockSpec((tm, d), lambda i, j: (i, 0)),
                   pl.BlockSpec((tm, d), lambda i, j: (i, 0))],
        out_shape=[jax.ShapeDtypeStruct((m, d), F32), jax.ShapeDtypeStruct((m, d), BF16)],
        scratch_shapes=[pltpu.VMEM((tm, d), F32)],
        compiler_params=_cparams(("parallel", "arbitrary")),
        name="out_ln",
    )(mix, w, x, g.reshape(1, d), b.reshape(1, d))


def _mem_kernel(q_ref, gate_ref, k_ref, v_ref, *rest):
    o_ref = rest[-1]
    for h in range(MEM_HEADS):
        sl = slice(h * MEM_HEAD_DIM, (h + 1) * MEM_HEAD_DIM)
        k = k_ref[:, sl]
        v = v_ref[:, sl]
        s = _mxu(q_ref[:, sl], k, _NT) * MEM_HEAD_DIM ** -0.5
        p = jnp.exp(s - jnp.max(s, axis=-1, keepdims=True))
        p = p / jnp.sum(p, axis=-1, keepdims=True)
        o_ref[:, sl] = (_mxu(p, v, _NN) * _silu(gate_ref[:, sl])).astype(BF16)


MEM_DT = MEM_HEAD_DIM // 128
MEM_ROWS = MEM_LEN * MEM_DT * MEM_HEADS


def _cache_view(c):
    d, b = c.shape[:2]
    c = c.reshape(d, b, MEM_LEN, MEM_HEADS, MEM_DT, 128)
    return jnp.transpose(c, (0, 1, 2, 4, 3, 5)).reshape(d, b, MEM_ROWS, 128)


def _mem_cache_kernel(q_ref, gate_ref, k_ref, v_ref, *rest):
    o_ref = rest[-1]
    nseq = k_ref.shape[0]
    t = q_ref.shape[0] // nseq
    grp = MEM_DT * MEM_HEADS
    col = lax.broadcasted_iota(jnp.int32, (MEM_HEADS * t, MEM_ROWS), 1) % grp
    head = lax.broadcasted_iota(jnp.int32, (MEM_HEADS * t, MEM_ROWS), 0) // t
    for i in range(nseq):
        rows = slice(i * t, (i + 1) * t)
        k = k_ref[i]
        v = v_ref[i]
        qs = [jnp.concatenate([q_ref[rows, h * MEM_HEAD_DIM + dt * 128:h * MEM_HEAD_DIM + (dt + 1) * 128]
                               for h in range(MEM_HEADS)], axis=0) for dt in range(MEM_DT)]
        s = _mxu(qs[0], k, _NT)
        for dt in range(1, MEM_DT):
            s = s + pltpu.roll(_mxu(qs[dt], k, _NT), MEM_ROWS - dt * MEM_HEADS, 1)
        s = jnp.where(col == head, s * MEM_HEAD_DIM ** -0.5, -jnp.inf)
        p = jnp.exp(s - jnp.max(s, axis=-1, keepdims=True))
        p = p / jnp.sum(p, axis=-1, keepdims=True)
        for dt in range(MEM_DT):
            o = _mxu(p if dt == 0 else pltpu.roll(p, dt * MEM_HEADS, 1), v, _NN)
            for h in range(MEM_HEADS):
                sl = slice(h * MEM_HEAD_DIM + dt * 128, h * MEM_HEAD_DIM + (dt + 1) * 128)
                o_ref[rows, sl] = (o[h * t:(h + 1) * t] * _silu(gate_ref[rows, sl])).astype(BF16)


def _mem_attention(h, mix_prev, grp, q_off, gate_off, mem_k, mem_v, layer, tq):
    cached = mem_k.ndim == 4
    if cached:
        kv_specs = [pl.BlockSpec((None, grp.nseq, MEM_ROWS, 128), lambda b, i: (layer, b, 0, 0))] * 2
    else:
        kv_specs = [pl.BlockSpec((MEM_LEN, MEM_WIDTH), lambda b, i: (b, 2 * layer)),
                    pl.BlockSpec((MEM_LEN, MEM_WIDTH), lambda b, i: (b, 2 * layer + 1))]
    ins = [h, h, mem_k, mem_v] + ([] if mix_prev is None else [mix_prev])
    specs = [grp.spec(tq, MEM_WIDTH, q_off), grp.spec(tq, MEM_WIDTH, gate_off)] + kv_specs
    if mix_prev is not None:
        specs.append(pl.BlockSpec(memory_space=pl.ANY))
    return pl.pallas_call(
        _mem_cache_kernel if cached else _mem_kernel,
        grid=(grp.steps, grp.t_rows // tq),
        in_specs=specs,
        out_specs=grp.spec(tq, MEM_WIDTH, MIX_WIDTH - MEM_WIDTH),
        out_shape=jax.ShapeDtypeStruct((h.shape[0], MIX_WIDTH), BF16),
        input_output_aliases=_alias_last(len(ins), mix_prev is not None),
        compiler_params=_cparams(("parallel", "parallel")),
        name="mem_attention",
    )(*ins)


def _gla_kernel(q_ref, k_ref, v_ref, gate_ref, sm_ref, w2_ref, gb_ref, nw_ref, *rest, L, nblk, t_real, nseq,
                has_state, has_prev):
    n_opt = has_state + has_prev
    mix_ref, s_ref, ST, QK, B, OI = rest[n_opt:]
    tb = pl.program_id(1)
    rows_blk = q_ref.shape[0]

    @pl.when(tb == 0)
    def _():
        for i in range(nseq):
            for h in range(GLA_HEADS):
                ST[i * GLA_HEADS + h] = rest[0][i, h].T if has_state else jnp.zeros((GLA_DV, GLA_DK), F32)

    z = _mxu_f32(sm_ref[...], w2_ref[...], _NN) + gb_ref[...]
    g_all = -_softplus(-z) * (1.0 / GLA_TAU)
    t_i = lax.broadcasted_iota(jnp.int32, (rows_blk, GLA_DK), 0)
    t_c = t_i & (L - 1)
    nw = nw_ref[...]
    for h in range(GLA_HEADS):
        ks = slice(h * GLA_DK, (h + 1) * GLA_DK)
        vs = slice(h * GLA_DV, (h + 1) * GLA_DV)
        q = q_ref[:, ks] * GLA_DK ** -0.5
        k = k_ref[:, ks]
        b = g_all[:, ks]
        v = v_ref[:, vs]
        if t_real < L:
            b = jnp.where(t_c < t_real, b, 0.0)
            k = jnp.where(t_c < t_real, k, 0.0)
        sh = 1
        while sh < L:
            b = b + jnp.where(t_c >= sh, pltpu.roll(b, sh, 0), 0.0)
            sh *= 2
        o = jnp.sum(q * k, axis=-1, keepdims=True) * v
        for j in range(1, L):
            d = jnp.where(t_c >= j, b - pltpu.roll(b, j, 0), -jnp.inf)
            p = jnp.exp(d) * q * pltpu.roll(k, j, 0)
            o = o + jnp.sum(p, axis=-1, keepdims=True) * pltpu.roll(v, j, 0)
        OI[:, vs] = o
        B[:, ks] = b
        QK[:, ks] = q
        QK[:, GLA_QK + h * GLA_DK:GLA_QK + (h + 1) * GLA_DK] = k
    for c in range(rows_blk // L):
        rows = slice(c * L, (c + 1) * L)
        for h in range(GLA_HEADS):
            ks = slice(h * GLA_DK, (h + 1) * GLA_DK)
            vs = slice(h * GLA_DV, (h + 1) * GLA_DV)
            si = (c if nseq > 1 else 0) * GLA_HEADS + h
            b = B[rows, ks]
            b_last = b[L - 1:L, :]
            st = ST[si]
            o = OI[rows, vs] + _mxu(QK[rows, ks] * jnp.exp(b), st, _NT)
            y = o * lax.rsqrt(jnp.mean(o * o, axis=-1, keepdims=True) + 1e-6) * nw
            mix_ref[rows, vs] = (y * _silu(gate_ref[rows, vs])).astype(BF16)
            kd = QK[rows, GLA_QK + h * GLA_DK:GLA_QK + (h + 1) * GLA_DK] * jnp.exp(b_last - b)
            ST[si] = st * jnp.exp(b_last) + _mxu(v_ref[rows, vs], kd, _TN)

    @pl.when(tb == nblk - 1)
    def _():
        for i in range(nseq):
            for h in range(GLA_HEADS):
                s_ref[i, h] = ST[i * GLA_HEADS + h].T


def _gla(h, mix_prev, grp, P, st_in, s_prev, p, L, tb):
    nblk = grp.t_rows // tb
    tail = (GLA_HEADS, GLA_DK, GLA_DV)
    rows_blk = tb * grp.nseq
    s_ins, s_specs, s_out, s_shape = _state_io(tail, p, st_in, grp)
    full = lambda shape: pl.BlockSpec(shape, lambda b, c: (0,) * len(shape))
    w2 = jnp.concatenate([P['gla_w2'], jnp.zeros((128 - GLA_RANK, GLA_QK), F32)], axis=0)
    ins = [h, h, h, h, h, w2, P['gla_b'].reshape(1, GLA_QK), P['gla_norm'].reshape(1, GLA_DV)]
    specs = [grp.spec(tb, GLA_QK, EVEN_OFF[0]), grp.spec(tb, GLA_QK, EVEN_OFF[1]), grp.spec(tb, GLA_WIDTH, EVEN_OFF[2]),
             grp.spec(tb, GLA_WIDTH, EVEN_OFF[4]), grp.spec(tb, 128, EVEN_OFF[3]),
             full((128, GLA_QK)), full((1, GLA_QK)), full((1, GLA_DV))]
    ins += s_ins
    specs += s_specs
    any_spec = pl.BlockSpec(memory_space=pl.ANY)
    alias = {}
    for prev, out_idx in ((mix_prev, 0), (s_prev, 1)):
        if prev is not None:
            alias[len(ins)] = out_idx
            ins.append(prev)
            specs.append(any_spec)
    n_prev = (mix_prev is not None) + (s_prev is not None)
    return pl.pallas_call(
        functools.partial(_gla_kernel, L=L, nblk=nblk, t_real=grp.t_real or L, nseq=grp.nseq,
                          has_state=st_in is not None, has_prev=n_prev),
        grid=(grp.steps, nblk),
        in_specs=specs,
        out_specs=[grp.spec(tb, GLA_WIDTH, SSD_WIDTH), s_out],
        out_shape=[jax.ShapeDtypeStruct((h.shape[0], MIX_WIDTH), BF16), s_shape],
        scratch_shapes=[pltpu.VMEM((grp.nseq * GLA_HEADS, GLA_DV, GLA_DK), F32),
                        pltpu.VMEM((rows_blk, 2 * GLA_QK), F32), pltpu.VMEM((rows_blk, GLA_QK), F32),
                        pltpu.VMEM((rows_blk, GLA_WIDTH), F32)],
        input_output_aliases=alias,
        compiler_params=_cparams(("parallel", "arbitrary")),
        name="gla",
    )(*ins)


SSD_GW = SSD_REP * SSD_HEAD_DIM


def _ssd_lanes(L):
    return max(SSD_REP * L, 128)


def _ssd_kernel(sz_ref, xbc_ref, sm_ref, cw_ref, cbias_ref, dtb_ref, alog_ref, dvec_ref, nw_ref, ep_ref, es_ref, *rest,
                L, nchunk, t_real, nseq, has_state, has_prev):
    n_opt = 2 * has_state + has_prev
    mix_ref, s_ref, conv_out_ref, ST, tail = rest[n_opt:]
    ci = pl.program_id(1)
    gs = _ssd_lanes(L)
    cat = jnp.concatenate

    @pl.when(ci == 0)
    def _():
        tail[...] = jnp.zeros_like(tail)
        for i in range(nseq):
            for g in range(SSD_GROUPS):
                if has_state:
                    ST[i * SSD_GROUPS + g] = rest[1][i, g * SSD_REP:(g + 1) * SSD_REP].reshape(SSD_GW, SSD_STATE).T
                else:
                    ST[i * SSD_GROUPS + g] = jnp.zeros((SSD_STATE, SSD_GW), F32)
            if has_state:
                tail[i, 8 - (CONV_W - 1):8, :] = rest[0][i]

    d = 128
    row8 = lax.broadcasted_iota(jnp.int32, (8, d), 0)
    row = lax.broadcasted_iota(jnp.int32, (L, L), 0)
    col = lax.broadcasted_iota(jnp.int32, (L, L), 1)
    tril = (col <= row).astype(F32)
    ep = ep_ref[...]
    t_i = lax.broadcasted_iota(jnp.int32, (L, SSD_GROUPS * gs), 0)
    s_i = lax.broadcasted_iota(jnp.int32, (L, SSD_GROUPS * gs), 1) & (L - 1)
    blk_r = lax.broadcasted_iota(jnp.int32, (gs, SSD_GW), 0) // L
    blk_c = lax.broadcasted_iota(jnp.int32, (gs, SSD_GW), 1) // SSD_HEAD_DIM
    diag = blk_r == blk_c
    reps = SSD_REP * L
    for i in range(nseq):
        rows = slice(i * L, (i + 1) * L)

        def conv_tile(c0):
            u = xbc_ref[rows, c0:c0 + d]
            p8 = tail[i, :, c0:c0 + d]
            w = cw_ref[:, c0:c0 + d]
            acc = u * w[CONV_W - 1:CONV_W] + cbias_ref[:, c0:c0 + d]
            for j in range(1, CONV_W):
                acc = acc + _shifted(u, p8, j, row8) * w[CONV_W - 1 - j:CONV_W - j]
            return _silu(acc)

        dt = _softplus(sm_ref[rows, GLA_RANK:GLA_RANK + SSD_HEADS] + dtb_ref[...])
        if t_real < L:
            dt = jnp.where(lax.broadcasted_iota(jnp.int32, (L, SSD_HEADS), 0) < t_real, dt, 0.0)
        c = _mxu_f32(tril, dt * -jnp.exp(alog_ref[...]), _NN)
        dt_x = _mxu_f32(dt, ep, _NN)
        c_x = _mxu_f32(c, ep, _NN)
        c_s = _mxu_f32(c, es_ref[...], _NN)
        c_src = jnp.sum(jnp.where(t_i == s_i, c_s, 0.0), axis=0, keepdims=True)
        seg = jnp.exp(jnp.where(s_i <= t_i, c_s - c_src, -jnp.inf))
        for g in range(SSD_GROUPS):
            gl = slice(g * SSD_GW, (g + 1) * SSD_GW)
            sx = cat([conv_tile(g * SSD_GW + n * d) for n in range(SSD_GW // d)], axis=1)
            bm = conv_tile(SSD_WIDTH + g * SSD_STATE)
            cm = conv_tile(SSD_WIDTH + SSD_GROUPS * SSD_STATE + g * SSD_STATE)
            xdt = sx * dt_x[:, gl]
            pad_rows = [] if reps == gs else [jnp.zeros((gs - reps, SSD_STATE), F32)]
            cb = _mxu(cm, cat([bm] * SSD_REP + pad_rows, axis=0), _NT)
            pad_rows = [] if reps == gs else [jnp.zeros((gs - reps, SSD_GW), F32)]
            xbd = jnp.where(diag, cat([xdt] * SSD_REP + pad_rows, axis=0), 0.0)
            st = ST[i * SSD_GROUPS + g]
            y = _mxu(cb * seg[:, g * gs:(g + 1) * gs], xbd, _NN) + _mxu(cm, st, _NN) * jnp.exp(c_x[:, gl])
            y = (y + sx * dvec_ref[:, gl]) * _silu(sz_ref[rows, gl])
            y = y * lax.rsqrt(jnp.mean(y * y, axis=-1, keepdims=True) + 1e-6) * nw_ref[:, gl]
            mix_ref[rows, gl] = y.astype(BF16)
            c_end = c_x[L - 1:L, gl]
            ST[i * SSD_GROUPS + g] = st * jnp.exp(c_end) + _mxu(bm, xdt * jnp.exp(c_end - c_x[:, gl]), _TN)
        if nchunk > 1:
            tail[i] = xbc_ref[i * L + L - 8:(i + 1) * L, :]

    @pl.when(ci == nchunk - 1)
    def _():
        for i in range(nseq):
            conv_out_ref[i] = xbc_ref[i * L + t_real - (CONV_W - 1):i * L + t_real, :]
            for g in range(SSD_GROUPS):
                s_ref[i, g * SSD_REP:(g + 1) * SSD_REP] = ST[i * SSD_GROUPS + g].T.reshape(
                    SSD_REP, SSD_HEAD_DIM, SSD_STATE)


def _ssd(h, mix_prev, grp, P, st_in, conv_in, s_prev, p):
    L = grp.chunk
    nchunk = grp.t_rows // L
    gs = _ssd_lanes(L)
    tail = (SSD_HEADS, SSD_HEAD_DIM, SSD_STATE)
    ns = grp.nseq
    s_ins, s_specs, s_out, s_shape = _state_io(tail, p, st_in, grp)
    full = lambda shape: pl.BlockSpec(shape, lambda b, c: (0,) * len(shape))
    heads = jnp.arange(SSD_HEADS)[:, None]
    lane_p = jnp.arange(SSD_WIDTH)[None, :]
    ep = (lane_p // SSD_HEAD_DIM == heads).astype(F32)
    lane_s = jnp.arange(SSD_GROUPS * gs)[None, :]
    in_grp = lane_s % gs
    es = ((in_grp < SSD_REP * L) & ((lane_s // gs) * SSD_REP + in_grp // L == heads)).astype(F32)
    row = lambda a: a.reshape(1, -1)
    ins = [h, h, h, P['ssd_conv_w'], row(P['ssd_conv_b']), row(P['ssd_dt_bias']), row(P['ssd_a_log']),
           row(jnp.repeat(P['ssd_d'], SSD_HEAD_DIM)), row(P['ssd_norm']), ep, es]
    specs = [grp.spec(L, SSD_WIDTH, EVEN_OFF[5]), grp.spec(L, SSD_CONV_DIM, EVEN_OFF[6]), grp.spec(L, 128, EVEN_OFF[3]),
             full((CONV_W, SSD_CONV_DIM)), full((1, SSD_CONV_DIM)), full((1, SSD_HEADS)), full((1, SSD_HEADS)),
             full((1, SSD_WIDTH)), full((1, SSD_WIDTH)), full(ep.shape), full(es.shape)]
    if st_in is not None:
        ins += [conv_in] + s_ins
        specs += [pl.BlockSpec((None, ns, CONV_W - 1, SSD_CONV_DIM), lambda b, c: (p, b, 0, 0))] + s_specs
    any_spec = pl.BlockSpec(memory_space=pl.ANY)
    alias = {}
    for prev, out_idx in ((mix_prev, 0), (s_prev, 1)):
        if prev is not None:
            alias[len(ins)] = out_idx
            ins.append(prev)
            specs.append(any_spec)
    n_prev = (mix_prev is not None) + (s_prev is not None)
    return pl.pallas_call(
        functools.partial(_ssd_kernel, L=L, nchunk=nchunk, t_real=grp.t_real or L, nseq=ns,
                          has_state=st_in is not None, has_prev=n_prev),
        grid=(grp.steps, nchunk),
        in_specs=specs,
        out_specs=[grp.spec(L, SSD_WIDTH, 0), s_out,
                   pl.BlockSpec((ns, CONV_W - 1, SSD_CONV_DIM), lambda b, c: (b, 0, 0))],
        out_shape=[jax.ShapeDtypeStruct((h.shape[0], MIX_WIDTH), BF16), s_shape,
                   jax.ShapeDtypeStruct((grp.bsz, CONV_W - 1, SSD_CONV_DIM), F32)],
        scratch_shapes=[pltpu.VMEM((ns * SSD_GROUPS, SSD_STATE, SSD_GW), F32),
                        pltpu.VMEM((ns, 8, SSD_CONV_DIM), F32)],
        input_output_aliases=alias,
        compiler_params=_cparams(("parallel", "arbitrary")),
        name="ssd",
    )(*ins)


def _gdn_kernel(qkv_ref, cz_ref, sm_ref, cw_ref, alog_ref, dtb_ref, nw_ref, *rest, L, nchunk, t_real, nseq,
                has_state, has_prev):
    n_opt = 2 * has_state + has_prev
    mix_ref, s_ref, conv_out_ref, S, tail = rest[n_opt:]
    ci = pl.program_id(1)

    @pl.when(ci == 0)
    def _():
        tail[...] = jnp.zeros_like(tail)
        if has_state:
            for i in range(nseq):
                tail[i, 8 - (CONV_W - 1):8, :] = rest[0][i]
            S[...] = rest[1][...]
        else:
            S[...] = jnp.zeros_like(S)

    n2 = 2 * L
    d = GDN_HEAD_DIM
    cat = jnp.concatenate
    row8 = lax.broadcasted_iota(jnp.int32, (8, d), 0)

    def conv_tile(i, c0):
        u = qkv_ref[i * L:(i + 1) * L, c0:c0 + d]
        p8 = tail[i, :, c0:c0 + d]
        w = cw_ref[:, c0:c0 + d]
        acc = u * w[CONV_W - 1:CONV_W]
        for j in range(1, CONV_W):
            acc = acc + _shifted(u, p8, j, row8) * w[CONV_W - 1 - j:CONV_W - j]
        return _silu(acc)

    def l2n(x):
        return x * lax.rsqrt(jnp.sum(x * x, axis=-1, keepdims=True) + 1e-6)

    row = lax.broadcasted_iota(jnp.int32, (L, L), 0)
    col = lax.broadcasted_iota(jnp.int32, (L, L), 1)
    tril = (col <= row).astype(F32)
    r2 = lax.broadcasted_iota(jnp.int32, (n2, n2), 0)
    c2 = lax.broadcasted_iota(jnp.int32, (n2, n2), 1)
    same = (r2 >= L) == (c2 >= L)
    strict = same & (c2 < r2)
    incl = same & (c2 <= r2)
    upper = same & (r2 <= c2)
    eye = (r2 == c2).astype(F32)
    zl = jnp.zeros((L, d), F32)
    units = [(i, j) for i in range(nseq) for j in range(GDN_HEADS // 2)]
    nn, tt, qk, kq, kdec, ec, bcol, elast, vst = [], [], [], [], [], [], [], [], []
    for i in range(nseq):
        sm = sm_ref[i * L:(i + 1) * L, :]
        beta_all = _sigmoid(sm[:, :GDN_HEADS])
        g_all = -jnp.exp(alog_ref[...]) * _softplus(sm[:, GDN_HEADS:2 * GDN_HEADS] + dtb_ref[...])
        if t_real < L:
            valid = lax.broadcasted_iota(jnp.int32, (L, GDN_HEADS), 0) < t_real
            beta_all = jnp.where(valid, beta_all, 0.0)
            g_all = jnp.where(valid, g_all, 0.0)
        c_all = _mxu_f32(tril, g_all, _NN)
        for j in range(GDN_HEADS // 2):
            h0, h1 = 2 * j, 2 * j + 1
            stack_col = lambda a: cat([a[:, h0:h0 + 1], a[:, h1:h1 + 1]], axis=0)
            c_col = stack_col(c_all)
            beta_col = stack_col(beta_all)
            c_row = jnp.sum(jnp.where(upper, stack_col(g_all), 0.0), axis=0, keepdims=True)
            decay = jnp.exp(jnp.where(incl, c_col - c_row, -jnp.inf))
            last = lambda rows: cat([jnp.broadcast_to(c_all[L - 1:L, h0:h0 + 1], (rows, 1)),
                                     jnp.broadcast_to(c_all[L - 1:L, h1:h1 + 1], (rows, 1))], axis=0)
            q0, q1 = (l2n(conv_tile(i, h * d)) * d ** -0.5 for h in (h0, h1))
            k0, k1 = (l2n(conv_tile(i, GDN_WIDTH + h * d)) for h in (h0, h1))
            vst.append(cat([conv_tile(i, 2 * GDN_WIDTH + h0 * d), conv_tile(i, 2 * GDN_WIDTH + h1 * d)], axis=0))
            k_st = cat([cat([k0, zl], axis=1), cat([zl, k1], axis=1)], axis=0)
            q_st = cat([cat([q0, zl], axis=1), cat([zl, q1], axis=1)], axis=0)
            both = cat([k_st, q_st], axis=0)
            full = _mxu(both, k_st, _NT)
            a = jnp.where(strict, full[:n2] * decay * beta_col, 0.0)
            tt.append(a)
            qk.append(full[n2:] * decay)
            kq.append(both)
            kdec.append(k_st * jnp.exp(last(L) - c_col))
            ec.append(jnp.exp(c_col))
            bcol.append(beta_col)
            elast.append(jnp.exp(last(d)))
    tt = _unit_lower_inverse(tt, L, eye, r2 & (L - 1), c2 & (L - 1))
    s_old, ksqs, u = [], [], []
    for n, (i, j) in enumerate(units):
        s_old.append(cat([S[i, 2 * j], S[i, 2 * j + 1]], axis=0))
        ksqs.append(_mxu(kq[n], s_old[n], _NN))
    for n in range(len(units)):
        u.append(_mxu(tt[n], bcol[n] * (vst[n] - ec[n] * ksqs[n][:n2]), _NN))
    nw = nw_ref[...]
    for n, (i, j) in enumerate(units):
        rows = slice(i * L, (i + 1) * L)
        o = ec[n] * ksqs[n][n2:] + _mxu(qk[n], u[n], _NN)
        for hh, oh in ((2 * j, o[:L]), (2 * j + 1, o[L:])):
            cols = slice(hh * d, (hh + 1) * d)
            y = oh * lax.rsqrt(jnp.mean(oh * oh, axis=-1, keepdims=True) + 1e-6) * nw
            mix_ref[rows, cols] = (y * _silu(cz_ref[rows, cols])).astype(BF16)
        new = s_old[n] * elast[n] + _mxu(kdec[n], u[n], _TN)
        S[i, 2 * j] = new[:d]
        S[i, 2 * j + 1] = new[d:]
    if nchunk > 1:
        for i in range(nseq):
            tail[i] = qkv_ref[i * L + L - 8:(i + 1) * L, :]

    @pl.when(ci == nchunk - 1)
    def _():
        s_ref[...] = S[...]
        for i in range(nseq):
            conv_out_ref[i] = qkv_ref[i * L + t_real - (CONV_W - 1):i * L + t_real, :]


def _gdn(h, mix_prev, grp, P, st_in, conv_in, s_prev, p):
    L = grp.chunk
    nchunk = grp.t_rows // L
    tail = (GDN_HEADS, GDN_HEAD_DIM, GDN_HEAD_DIM)
    ns = grp.nseq
    s_ins, s_specs, s_out, s_shape = _state_io(tail, p, st_in, grp)
    cw = 3 * GDN_WIDTH
    full = lambda shape: pl.BlockSpec(shape, lambda b, c: (0,) * len(shape))
    ins = [h, h, h, P['gdn_conv_w'], P['gdn_a_log'].reshape(1, GDN_HEADS), P['gdn_dt_bias'].reshape(1, GDN_HEADS),
           P['gdn_norm'].reshape(1, GDN_HEAD_DIM)]
    specs = [grp.spec(L, cw, ODD_OFF[0]), grp.spec(L, GDN_WIDTH, ODD_OFF[1]), grp.spec(L, 128, ODD_OFF[2]),
             full((CONV_W, cw)), full((1, GDN_HEADS)), full((1, GDN_HEADS)), full((1, GDN_HEAD_DIM))]
    if st_in is not None:
        ins += [conv_in] + s_ins
        specs += [pl.BlockSpec((None, ns, CONV_W - 1, cw), lambda b, c: (p, b, 0, 0))] + s_specs
    any_spec = pl.BlockSpec(memory_space=pl.ANY)
    alias = {}
    for prev, out_idx in ((mix_prev, 0), (s_prev, 1)):
        if prev is not None:
            alias[len(ins)] = out_idx
            ins.append(prev)
            specs.append(any_spec)
    n_prev = (mix_prev is not None) + (s_prev is not None)
    return pl.pallas_call(
        functools.partial(_gdn_kernel, L=L, nchunk=nchunk, t_real=grp.t_real or L,
                          nseq=ns, has_state=st_in is not None, has_prev=n_prev),
        grid=(grp.steps, nchunk),
        in_specs=specs,
        out_specs=[grp.spec(L, GDN_WIDTH, 0), s_out,
                   pl.BlockSpec((ns, CONV_W - 1, cw), lambda b, c: (b, 0, 0))],
        out_shape=[jax.ShapeDtypeStruct((h.shape[0], MIX_WIDTH), BF16), s_shape,
                   jax.ShapeDtypeStruct((grp.bsz, CONV_W - 1, cw), F32)],
        scratch_shapes=[pltpu.VMEM((ns,) + tail, F32), pltpu.VMEM((ns, 8, cw), F32)],
        input_output_aliases=alias,
        compiler_params=_cparams(("parallel", "arbitrary")),
        name="gdn",
    )(*ins)


RWKV_PAIRS = RWKV_HEADS // 2
RWKV_PW = 2 * RWKV_HEAD_DIM


def _rwkv_kernel(r_ref, k_ref, v_ref, xwa_ref, gate_ref, mu_ref, w0_ref, w2_ref, a0_ref, a2_ref, kkp_ref, ka_ref,
                 rk_ref, lng_ref, lnb_ref, *rest, L, nchunk, t_real, nseq, has_state, has_prev):
    n_opt = 2 * has_state + has_prev
    mix_ref, s_ref, shift_out_ref, S, last = rest[n_opt:]
    ci = pl.program_id(1)
    n = RWKV_HEAD_DIM
    w3 = 3 * RWKV_WIDTH
    cat = jnp.concatenate
    pairs = range(RWKV_PAIRS)

    @pl.when(ci == 0)
    def _():
        last[...] = jnp.zeros_like(last)
        if has_state:
            zn = jnp.zeros((n, n), F32)
            for i in range(nseq):
                last[i, 7:8, :] = rest[0][i]
                for j in pairs:
                    S[i * RWKV_PAIRS + j] = cat([cat([rest[1][i, 2 * j], zn], axis=1),
                                                 cat([zn, rest[1][i, 2 * j + 1]], axis=1)], axis=0)
        else:
            S[...] = jnp.zeros_like(S)

    n2 = 2 * L
    row = lax.broadcasted_iota(jnp.int32, (L, L), 0)
    col = lax.broadcasted_iota(jnp.int32, (L, L), 1)
    tril = (col <= row).astype(F32)
    r2 = lax.broadcasted_iota(jnp.int32, (n2, n2), 0)
    c2 = lax.broadcasted_iota(jnp.int32, (n2, n2), 1)
    same = (r2 >= L) == (c2 >= L)
    strict = same & (c2 < r2)
    incl = same & (c2 <= r2)
    eye = (r2 == c2).astype(F32)
    lane = lax.broadcasted_iota(jnp.int32, (L, RWKV_PW), 1)
    lo = lane < n

    def stack(x):
        return cat([jnp.where(lo, x, 0.0), jnp.where(lo, 0.0, x)], axis=0)

    row8 = lax.broadcasted_iota(jnp.int32, (8, RWKV_PW), 0)
    valid = lax.broadcasted_iota(jnp.int32, (L, RWKV_PW), 0) < t_real

    def seg_sum(x):
        s_lo = jnp.sum(jnp.where(lo, x, 0.0), axis=-1, keepdims=True)
        s_hi = jnp.sum(jnp.where(lo, 0.0, x), axis=-1, keepdims=True)
        return jnp.where(lo, s_lo, s_hi)

    def shift_mix(ref, i, c_src, c_all):
        x = ref[i * L:(i + 1) * L, c_src:c_src + RWKV_PW]
        prev = _shifted(x, last[i, :, c_all:c_all + RWKV_PW], 1, row8)
        return x + (prev - x) * mu_ref[:, c_all:c_all + RWKV_PW]

    xwa = cat([shift_mix(xwa_ref, i, 0, w3) for i in range(nseq)], axis=0)
    lr_w_all = _mxu_f32(jnp.tanh(xwa), w2_ref[...], _NN)
    lr_a_all = _mxu_f32(xwa, a2_ref[...], _NN)
    units = [(i, j) for i in range(nseq) for j in pairs]
    a_ak, a_rk, a_rb, nn, tt, sread, kdbd, egl, vs, bonus = [], [], [], [], [], [], [], [], [], []
    for i, j in units:
        sl = slice(j * RWKV_PW, (j + 1) * RWKV_PW)
        r = shift_mix(r_ref, i, j * RWKV_PW, j * RWKV_PW)
        k = shift_mix(k_ref, i, j * RWKV_PW, RWKV_WIDTH + j * RWKV_PW)
        v = shift_mix(v_ref, i, j * RWKV_PW, 2 * RWKV_WIDTH + j * RWKV_PW)
        w_log = -_softplus(-(w0_ref[:, sl] + lr_w_all[i * L:(i + 1) * L, sl])) - 0.5
        lw = -jnp.exp(w_log)
        a7 = _sigmoid(a0_ref[:, sl] + lr_a_all[i * L:(i + 1) * L, sl])
        kx = k * kkp_ref[:, sl]
        kk = kx * lax.rsqrt(seg_sum(kx * kx) + 1e-6)
        k = k * (1.0 + (a7 - 1.0) * ka_ref[:, sl])
        if t_real < L:
            lw, kk, k, v = (jnp.where(valid, a, 0.0) for a in (lw, kk, k, v))
        b = kk * a7
        bonus.append(seg_sum(r * k * rk_ref[:, sl]) * v)
        vs.append(stack(v))
        g = _mxu_f32(tril, lw, _NN)
        gp = g - lw
        gm = g[L // 2 - 1:L // 2, :]
        gl = g[L - 1:L, :]
        e_neg = jnp.exp(gm - g)
        lhs = cat([stack(kk * jnp.exp(gp - gm)), stack(r * jnp.exp(g - gm))], axis=0)
        rhs = cat([stack(b * e_neg), stack(k * e_neg)], axis=0)
        full = _mxu(lhs, rhs, _NT)
        a_ab = jnp.where(strict, full[:n2, :n2], 0.0)
        a_ak.append(jnp.where(strict, full[:n2, n2:], 0.0))
        a_rb.append(jnp.where(incl, full[n2:, :n2], 0.0))
        a_rk.append(jnp.where(incl, full[n2:, n2:], 0.0))
        tt.append(a_ab)
        sread.append(cat([stack(kk * jnp.exp(gp)), stack(r * jnp.exp(g))], axis=0))
        dec = jnp.exp(gl - g)
        kdbd.append(cat([stack(k * dec), stack(-b * dec)], axis=0))
        egl.append(jnp.exp(gl))
    tt = _unit_lower_inverse(tt, L, eye, r2 & (L - 1), c2 & (L - 1))
    s_old, sr, av, u = [], [], [], []
    for m, (i, j) in enumerate(units):
        s_old.append(S[i * RWKV_PAIRS + j])
        sr.append(_mxu(sread[m], s_old[m], _NT))
        av.append(_mxu(cat([a_ak[m], a_rk[m]], axis=0), vs[m], _NN))
    for m in range(len(units)):
        u.append(_mxu(tt[m], sr[m][:n2] + av[m][:n2], _NN))
    for m, (i, j) in enumerate(units):
        sl = slice(j * RWKV_PW, (j + 1) * RWKV_PW)
        rows = slice(i * L, (i + 1) * L)
        o = sr[m][n2:] + av[m][n2:] - _mxu(a_rb[m], u[m], _NN)
        o = o[:L] + o[L:]
        oc = o - seg_sum(o) * (1.0 / n)
        gn = oc * lax.rsqrt(seg_sum(oc * oc) * (1.0 / n) + RWKV_GN_EPS)
        y = gn * lng_ref[:, sl] + lnb_ref[:, sl] + bonus[m]
        mix_ref[rows, sl] = (y * _silu(gate_ref[rows, sl])).astype(BF16)
        S[i * RWKV_PAIRS + j] = s_old[m] * egl[m] + _mxu(cat([vs[m], u[m]], axis=0), kdbd[m], _TN)
    pieces = ((r_ref, 0, RWKV_WIDTH), (k_ref, RWKV_WIDTH, RWKV_WIDTH), (v_ref, 2 * RWKV_WIDTH, RWKV_WIDTH),
              (xwa_ref, w3, RWKV_PW))
    if nchunk > 1:
        for i in range(nseq):
            for ref, c0, wd in pieces:
                last[i, :, c0:c0 + wd] = ref[i * L + L - 8:(i + 1) * L, :]

    @pl.when(ci == nchunk - 1)
    def _():
        for i in range(nseq):
            for ref, c0, wd in pieces:
                shift_out_ref[i, :, c0:c0 + wd] = ref[i * L + t_real - 1:i * L + t_real, :]
            for j in pairs:
                s_ref[i, 2 * j] = S[i * RWKV_PAIRS + j][:n, :n]
                s_ref[i, 2 * j + 1] = S[i * RWKV_PAIRS + j][n:, n:]


def _rwkv(h, mix_prev, grp, P, st_in, shift_in, s_prev, p):
    L = grp.chunk
    nchunk = grp.t_rows // L
    tail = (RWKV_HEADS, RWKV_HEAD_DIM, RWKV_HEAD_DIM)
    ns = grp.nseq
    s_ins, s_specs, s_out, s_shape = _state_io(tail, p, st_in, grp)
    full = lambda shape: pl.BlockSpec(shape, lambda b, c: (0,) * len(shape))
    row = lambda a: a.reshape(1, -1)
    zr = jnp.zeros((RWKV_HEAD_DIM, RWKV_WIDTH), F32)
    w2 = jnp.concatenate([P['rwkv_w2'], zr], axis=0)
    a2 = jnp.concatenate([zr, P['rwkv_a2']], axis=0)
    off = ODD_OFF[4]
    ins = [h, h, h, h, h, row(P['rwkv_mu']), row(P['rwkv_w0']), w2, row(P['rwkv_a0']), a2, row(P['rwkv_kk']),
           row(P['rwkv_ka']), row(P['rwkv_rk']), row(P['rwkv_ln_g']), row(P['rwkv_ln_b'])]
    vec = full((1, RWKV_WIDTH))
    specs = [grp.spec(L, RWKV_WIDTH, off), grp.spec(L, RWKV_WIDTH, off + RWKV_WIDTH),
             grp.spec(L, RWKV_WIDTH, off + 2 * RWKV_WIDTH), grp.spec(L, RWKV_PW, off + 3 * RWKV_WIDTH),
             grp.spec(L, RWKV_WIDTH, ODD_OFF[5]), full((1, RWKV_SHIFT_DIM)), vec, full((RWKV_PW, RWKV_WIDTH)), vec,
             full((RWKV_PW, RWKV_WIDTH)), vec, vec, vec, vec, vec]
    if st_in is not None:
        ins += [shift_in] + s_ins
        specs += [pl.BlockSpec((None, ns, 1, RWKV_SHIFT_DIM), lambda b, c: (p, b, 0, 0))] + s_specs
    any_spec = pl.BlockSpec(memory_space=pl.ANY)
    alias = {}
    for prev, out_idx in ((mix_prev, 0), (s_prev, 1)):
        if prev is not None:
            alias[len(ins)] = out_idx
            ins.append(prev)
            specs.append(any_spec)
    n_prev = (mix_prev is not None) + (s_prev is not None)
    return pl.pallas_call(
        functools.partial(_rwkv_kernel, L=L, nchunk=nchunk, t_real=grp.t_real or L,
                          nseq=ns, has_state=st_in is not None, has_prev=n_prev),
        grid=(grp.steps, nchunk),
        in_specs=specs,
        out_specs=[grp.spec(L, RWKV_WIDTH, GDN_WIDTH), s_out,
                   pl.BlockSpec((ns, 1, RWKV_SHIFT_DIM), lambda b, c: (b, 0, 0))],
        out_shape=[jax.ShapeDtypeStruct((h.shape[0], MIX_WIDTH), BF16), s_shape,
                   jax.ShapeDtypeStruct((grp.bsz, 1, RWKV_SHIFT_DIM), F32)],
        scratch_shapes=[pltpu.VMEM((ns * RWKV_PAIRS, RWKV_PW, RWKV_PW), F32),
                        pltpu.VMEM((ns, 8, RWKV_SHIFT_DIM), F32)],
        input_output_aliases=alias,
        compiler_params=_cparams(("parallel", "arbitrary")),
        name="rwkv7",
    )(*ins)


def _pad_t(a, t_to):
    t = a.shape[1]
    if t == t_to:
        return a
    return jnp.pad(a, [(0, 0), (0, t_to - t)] + [(0, 0)] * (a.ndim - 2))


def _even_mix(h, mix, grp, mem_k, mem_v, layer, st_in, st_prev, conv_in, P):
    p = layer // 2
    mix, s_ssd, s_conv = _ssd(h, mix, grp, P, st_in['ssd'], conv_in, st_prev['ssd'], p)
    gla_l, gla_tb = (GLA_CHUNK, 256) if grp.t_real is None else (grp.chunk, grp.chunk)
    mix, s_gla = _gla(h, mix, grp, P, st_in['gla'], st_prev['gla'], p, gla_l, gla_tb)
    mix = _mem_attention(h, mix, grp, EVEN_OFF[8], EVEN_OFF[9], mem_k, mem_v, layer, min(512, grp.t_rows))
    return mix, dict(gla=s_gla, ssd=s_ssd), s_conv


def _odd_mix(h, mix, grp, mem_k, mem_v, layer, st_in, st_prev, conv_in, shift_in, P):
    p = layer // 2
    mix, s_gdn, s_conv = _gdn(h, mix, grp, P, st_in['gdn'], conv_in, st_prev['gdn'], p)
    mix, s_rwkv, s_shift = _rwkv(h, mix, grp, P, st_in['rwkv'], shift_in, st_prev['rwkv'], p)
    mix = _mem_attention(h, mix, grp, ODD_OFF[6], ODD_OFF[7], mem_k, mem_v, layer, min(512, grp.t_rows))
    return mix, dict(gdn=s_gdn, rwkv=s_rwkv), s_conv, s_shift.reshape(grp.bsz, RWKV_SHIFT_DIM)


def kernel(x_prompt, x_sample, mem_prompt, cache_mem_k, cache_mem_v, state_gla, state_ssd, state_ssd_conv, state_gdn, state_gdn_conv, state_rwkv, state_rwkv_shift, mem_w_kv, ev_w_in, ev_gla_w2, ev_gla_b, ev_gla_norm, ev_ssd_conv_w, ev_ssd_conv_b, ev_ssd_dt_bias, ev_ssd_a_log, ev_ssd_d, ev_ssd_norm, ev_w_out, ev_ln_g, ev_ln_b, od_w_in, od_gdn_conv_w, od_gdn_dt_bias, od_gdn_a_log, od_gdn_norm, od_rwkv_mu, od_rwkv_w0, od_rwkv_w2, od_rwkv_a0, od_rwkv_a2, od_rwkv_kk, od_rwkv_ka, od_rwkv_rk, od_rwkv_ln_g, od_rwkv_ln_b, od_w_out, od_ln_g, od_ln_b):
    ev = dict(w_in=ev_w_in, gla_w2=ev_gla_w2, gla_b=ev_gla_b, gla_norm=ev_gla_norm,
              ssd_conv_w=ev_ssd_conv_w, ssd_conv_b=ev_ssd_conv_b, ssd_dt_bias=ev_ssd_dt_bias,
              ssd_a_log=ev_ssd_a_log, ssd_d=ev_ssd_d, ssd_norm=ev_ssd_norm,
              w_out=ev_w_out, ln_g=ev_ln_g, ln_b=ev_ln_b)
    od = dict(w_in=od_w_in, gdn_conv_w=od_gdn_conv_w, gdn_dt_bias=od_gdn_dt_bias, gdn_a_log=od_gdn_a_log,
              gdn_norm=od_gdn_norm, rwkv_mu=od_rwkv_mu, rwkv_w0=od_rwkv_w0, rwkv_w2=od_rwkv_w2,
              rwkv_a0=od_rwkv_a0, rwkv_a2=od_rwkv_a2, rwkv_kk=od_rwkv_kk, rwkv_ka=od_rwkv_ka,
              rwkv_rk=od_rwkv_rk, rwkv_ln_g=od_rwkv_ln_g, rwkv_ln_b=od_rwkv_ln_b,
              w_out=od_w_out, ln_g=od_ln_g, ln_b=od_ln_b)
    bp, tp, _ = x_prompt.shape
    bs, ts, _ = x_sample.shape
    mp, ms = bp * tp, bs * SMALL_T
    grp_p = _Group(bp, tp, None, 0, SEQ_CHUNK)
    grp_s = _Group(bs, SMALL_T, ts, mp, SMALL_T, nseq=SAMPLE_NSEQ)

    w_kv = jnp.moveaxis(mem_w_kv, 0, 1).reshape(D_MODEL, DEPTH * 2 * MEM_WIDTH).astype(BF16)
    kv = _matmul(mem_prompt.reshape(bp * MEM_LEN, D_MODEL).astype(BF16), w_kv, 512, 1024)
    kv6 = kv.reshape(bp, MEM_LEN, DEPTH, 2, MEM_HEADS, MEM_HEAD_DIM)
    mem_k_p = jnp.moveaxis(kv6[:, :, :, 0], 2, 0)
    mem_v_p = jnp.moveaxis(kv6[:, :, :, 1], 2, 0)
    mk_s, mv_s = _cache_view(cache_mem_k), _cache_view(cache_mem_v)

    x = jnp.concatenate([x_prompt.reshape(mp, D_MODEL),
                         _pad_t(x_sample, SMALL_T).reshape(ms, D_MODEL)], axis=0)
    x_bf = x.astype(BF16)
    none = {n: None for n in ('gla', 'ssd', 'gdn', 'rwkv')}
    in_s = dict(gla=state_gla, ssd=state_ssd, gdn=state_gdn, rwkv=state_rwkv)
    shift_s = state_rwkv_shift.reshape(N_PAIRS, bs, 1, RWKV_SHIFT_DIM)
    out_p, out_s = dict(none), dict(none)
    small_p = {n: [] for n in ('ssd_conv', 'gdn_conv', 'rwkv_shift')}
    small_s = {n: [] for n in small_p}
    tm = 1024
    w_in_ev = _pack_w_in(ev_w_in, EVEN_SIZES, EVEN_ORDER, EVEN_N)
    w_in_od = _pack_w_in(od_w_in, ODD_SIZES, ODD_ORDER, ODD_N)
    w_out_ev = jnp.concatenate([ev_w_out[:, GLA_WIDTH:GLA_WIDTH + SSD_WIDTH].astype(BF16),
                                ev_w_out[:, :GLA_WIDTH].astype(BF16),
                                ev_w_out[:, GLA_WIDTH + SSD_WIDTH:].astype(BF16)], axis=1)
    w_out_od = od_w_out.astype(BF16)
    small = lambda d: {n: w[p] for n, w in d.items() if n not in ('w_in', 'w_out')}
    for layer in range(DEPTH):
        p = layer // 2
        if layer % 2 == 0:
            P = small(ev)
            h = _matmul(x_bf, w_in_ev, tm, PROJ_TN, p=p)
            mix, new, c1 = _even_mix(h, None, grp_p, kv, kv, layer, none, out_p, None, P)
            out_p.update(new)
            mix, new, c2 = _even_mix(h, mix, grp_s, mk_s, mv_s, layer, in_s, out_s, state_ssd_conv, P)
            out_s.update(new)
            small_p['ssd_conv'].append(c1)
            small_s['ssd_conv'].append(c2)
            w_out = w_out_ev
        else:
            P = small(od)
            h = _matmul(x_bf, w_in_od, tm, PROJ_TN, p=p)
            mix, new, c1, h1 = _odd_mix(h, None, grp_p, kv, kv, layer, none, out_p, None, None, P)
            out_p.update(new)
            mix, new, c2, h2 = _odd_mix(h, mix, grp_s, mk_s, mv_s, layer, in_s, out_s, state_gdn_conv, shift_s, P)
            out_s.update(new)
            small_p['gdn_conv'].append(c1)
            small_s['gdn_conv'].append(c2)
            small_p['rwkv_shift'].append(h1)
            small_s['rwkv_shift'].append(h2)
            w_out = w_out_od
        x, x_bf = _out_ln(mix, w_out, p, x, P['ln_g'], P['ln_b'])

    y_prompt = x[:mp].reshape(bp, tp, D_MODEL)
    y_sample = x[mp:].reshape(bs, SMALL_T, D_MODEL)[:, :ts]
    st = lambda d, n: jnp.stack(d[n])
    return (y_prompt, y_sample, mem_k_p, mem_v_p,
            out_p['gla'], out_s['gla'], out_p['ssd'], out_s['ssd'],
            st(small_p, 'ssd_conv'), st(small_s, 'ssd_conv'), out_p['gdn'], out_s['gdn'],
            st(small_p, 'gdn_conv'), st(small_s, 'gdn_conv'), out_p['rwkv'], out_s['rwkv'],
            st(small_p, 'rwkv_shift'), st(small_s, 'rwkv_shift'))
```

```python
import functools

import numpy as np
import jax
import jax.numpy as jnp
from jax import lax
from jax.experimental import pallas as pl
from jax.experimental.pallas import tpu as pltpu

F32 = jnp.float32
BF16 = jnp.bfloat16
HI = lax.Precision.HIGHEST

D_MODEL = 2048
DEPTH = 4
N_PAIRS = DEPTH // 2
CONV_W = 4
MEM_LEN = 256
MEM_HEADS = 4
MEM_HEAD_DIM = 256
MEM_WIDTH = 1024
GLA_HEADS = 4
GLA_DK = 128
GLA_DV = 256
GLA_QK = 512
GLA_WIDTH = 1024
GLA_RANK = 16
GLA_TAU = 16.0
SSD_WIDTH = 2048
SSD_HEAD_DIM = 64
SSD_HEADS = 32
SSD_GROUPS = 4
SSD_REP = 8
SSD_STATE = 128
SSD_CONV_DIM = SSD_WIDTH + 2 * SSD_GROUPS * SSD_STATE
GDN_WIDTH = 2048
GDN_HEAD_DIM = 128
GDN_HEADS = 16
RWKV_WIDTH = 1024
RWKV_HEAD_DIM = 64
RWKV_HEADS = 16
RWKV_W_RANK = 64
RWKV_A_RANK = 64
RWKV_SHIFT_DIM = 3 * RWKV_WIDTH + RWKV_W_RANK + RWKV_A_RANK
RWKV_GN_EPS = 64e-5
EVEN_SIZES = (GLA_QK, GLA_QK, GLA_WIDTH, GLA_RANK, GLA_WIDTH, SSD_WIDTH, SSD_CONV_DIM, SSD_HEADS,
              MEM_WIDTH, MEM_WIDTH)
ODD_SIZES = (3 * GDN_WIDTH, GDN_WIDTH, GDN_HEADS, GDN_HEADS, RWKV_SHIFT_DIM, RWKV_WIDTH, MEM_WIDTH, MEM_WIDTH)
MIX_WIDTH = 4096
DEEPNORM_ALPHA = (2 * DEPTH) ** 0.25

EVEN_ORDER = (5, 2, 4, 8, 9, 6, 0, 1, 3, 7)
ODD_ORDER = (0, 1, 5, 6, 7, 4, 2, 3)
PROJ_TN = 768
VMEM_LIMIT = 60 * 1024 * 1024

GLA_CHUNK = 16
SEQ_CHUNK = 64
SMALL_T = 8
SAMPLE_NSEQ = 4

_NN = ((1,), (0,))
_NT = ((1,), (1,))
_TN = ((0,), (0,))


def _packed_layout(sizes, order):
    offs, o = {}, 0
    for i in order:
        offs[i] = o
        o += sizes[i]
    total = -(-o // PROJ_TN) * PROJ_TN
    return offs, total


EVEN_OFF, EVEN_N = _packed_layout(EVEN_SIZES, EVEN_ORDER)
ODD_OFF, ODD_N = _packed_layout(ODD_SIZES, ODD_ORDER)


def _pack_w_in(w, sizes, order, total):
    ends = np.cumsum(sizes)
    parts = [w[..., ends[i] - sizes[i]:ends[i]].astype(BF16) for i in order]
    used = sum(sizes)
    if total > used:
        parts.append(jnp.zeros(w.shape[:-1] + (total - used,), BF16))
    return jnp.concatenate(parts, axis=-1)


def _cparams(sem):
    return pltpu.CompilerParams(dimension_semantics=sem, vmem_limit_bytes=VMEM_LIMIT)


def _mxu(a, b, dims):
    return lax.dot_general(a.astype(BF16), b.astype(BF16), (dims, ((), ())), preferred_element_type=F32)


def _mxu_f32(a, b, dims):
    return lax.dot_general(a, b, (dims, ((), ())), precision=HI, preferred_element_type=F32)


def _sigmoid(x):
    return 1.0 / (1.0 + jnp.exp(-x))


def _silu(x):
    return x * _sigmoid(x)


def _softplus(x):
    return jnp.maximum(x, 0.0) + jnp.log(1.0 + jnp.exp(-jnp.abs(x)))


SUBLANES = 8


def _shifted(u, prev8, j, row8):
    ru = pltpu.roll(u, j, 0)
    top = jnp.where(row8 < j, pltpu.roll(prev8, j, 0), ru[:SUBLANES])
    return top if u.shape[0] == SUBLANES else jnp.concatenate([top, ru[SUBLANES:]], axis=0)


class _Group:
    def __init__(self, bsz, t_rows, t_real, row0, chunk, nseq=1):
        assert nseq == 1 or t_rows == chunk
        self.bsz, self.t_rows, self.t_real, self.row0, self.chunk, self.nseq = bsz, t_rows, t_real, row0, chunk, nseq
        self.steps = bsz // nseq

    def spec(self, rows, width, off):
        rows = rows * self.nseq
        assert off % width == 0 and self.row0 % rows == 0 and (self.t_rows * self.nseq) % rows == 0
        base, per, cb = self.row0 // rows, self.t_rows * self.nseq // rows, off // width
        return pl.BlockSpec((rows, width), lambda b, i: (base + b * per + i, cb))


def _alias_last(n_inputs, has_prev, out_index=0):
    return {n_inputs - 1: out_index} if has_prev else {}


TRI_BASE = 4


def _unit_lower_inverse(a, L, eye, r_in, c_in):
    idx = range(len(a))
    base = (r_in // TRI_BASE) == (c_in // TRI_BASE)
    n1 = [-jnp.where(base, a[j], 0.0) for j in idx]
    n2 = [_mxu(n1[j], n1[j], _NN) for j in idx]
    tt = [eye + n1[j] for j in idx]
    tt = [tt[j] + _mxu(tt[j], n2[j], _NN) for j in idx]
    b = TRI_BASE
    while b < L:
        rb, cb = r_in // b, c_in // b
        lower_left = (rb // 2 == cb // 2) & (rb % 2 == 1) & (cb % 2 == 0)
        for j in idx:
            x = _mxu(tt[j], jnp.where(lower_left, a[j], 0.0), _NN)
            tt[j] = tt[j] - _mxu(x, tt[j], _NN)
        b *= 2
    return tt


def _state_io(tail, p, s_in, grp):
    zeros = (0,) * len(tail)
    spec = pl.BlockSpec((None, grp.nseq) + tail, lambda *g: (p, g[0]) + zeros)
    ins, specs = ([s_in], [spec]) if s_in is not None else ([], [])
    shape = jax.ShapeDtypeStruct((N_PAIRS, grp.bsz) + tail, F32)
    return ins, specs, spec, shape


def _mm_kernel(x_ref, w_ref, o_ref, *, precision):
    o_ref[...] = jnp.dot(x_ref[...], w_ref[...], preferred_element_type=F32, precision=precision)


def _matmul(x, w, tm, tn, precision=None, p=None):
    m, k = x.shape
    n = w.shape[-1]
    assert m % tm == 0 and n % tn == 0
    w_spec = (pl.BlockSpec((k, tn), lambda j, i: (0, j)) if p is None
              else pl.BlockSpec((None, k, tn), lambda j, i: (p, 0, j)))
    return pl.pallas_call(
        functools.partial(_mm_kernel, precision=precision),
        grid=(n // tn, m // tm),
        in_specs=[pl.BlockSpec((tm, k), lambda j, i: (i, 0)), w_spec],
        out_specs=pl.BlockSpec((tm, tn), lambda j, i: (i, j)),
        out_shape=jax.ShapeDtypeStruct((m, n), F32),
        compiler_params=_cparams(("parallel", "parallel")),
        name="matmul",
    )(x, w)


def _out_ln_kernel(mix_ref, w_ref, x_ref, g_ref, b_ref, y_ref, ybf_ref, acc, *, nk):
    kk = pl.program_id(1)

    @pl.when(kk == 0)
    def _():
        acc[...] = jnp.zeros_like(acc)

    acc[...] += jnp.dot(mix_ref[...].astype(BF16), w_ref[...], preferred_element_type=F32)

    @pl.when(kk == nk - 1)
    def _():
        z = DEEPNORM_ALPHA * x_ref[...] + acc[...]
        zc = z - jnp.mean(z, axis=-1, keepdims=True)
        var = jnp.mean(zc * zc, axis=-1, keepdims=True)
        y = zc * lax.rsqrt(var + 1e-5) * g_ref[...] + b_ref[...]
        y_ref[...] = y
        ybf_ref[...] = y.astype(BF16)


def _out_ln(mix, w, p, x, g, b, tm=512, tk=2048):
    m, k = mix.shape
    d = w.shape[-1]
    nk = k // tk
    return pl.pallas_call(
        functools.partial(_out_ln_kernel, nk=nk),
        grid=(m // tm, nk),
        in_specs=[pl.BlockSpec((tm, tk), lambda i, j: (i, j)),
                  pl.BlockSpec((None, tk, d), lambda i, j: (p, j, 0)),
                  pl.BlockSpec((tm, d), lambda i, j: (i, 0)),
                  pl.BlockSpec((1, d), lambda i, j: (0, 0)),
                  pl.BlockSpec((1, d), lambda i, j: (0, 0))],
        out_specs=[pl.BlockSpec((tm, d), lambda i, j: (i, 0)),
                   pl.BlockSpec((tm, d), lambda i, j: (i, 0))],
        out_shape=[jax.ShapeDtypeStruct((m, d), F32), jax.ShapeDtypeStruct((m, d), BF16)],
        scratch_shapes=[pltpu.VMEM((tm, d), F32)],
        compiler_params=_cparams(("parallel", "arbitrary")),
        name="out_ln",
    )(mix, w, x, g.reshape(1, d), b.reshape(1, d))


def _mem_kernel(q_ref, gate_ref, k_ref, v_ref, *rest):
    o_ref = rest[-1]
    for h in range(MEM_HEADS):
        sl = slice(h * MEM_HEAD_DIM, (h + 1) * MEM_HEAD_DIM)
        k = k_ref[:, sl]
        v = v_ref[:, sl]
        s = _mxu(q_ref[:, sl], k, _NT) * MEM_HEAD_DIM ** -0.5
        p = jnp.exp(s - jnp.max(s, axis=-1, keepdims=True))
        p = p / jnp.sum(p, axis=-1, keepdims=True)
        o_ref[:, sl] = (_mxu(p, v, _NN) * _silu(gate_ref[:, sl])).astype(BF16)


MEM_DT = MEM_HEAD_DIM // 128
MEM_ROWS = MEM_LEN * MEM_DT * MEM_HEADS


def _cache_view(c):
    d, b = c.shape[:2]
    c = c.reshape(d, b, MEM_LEN, MEM_HEADS, MEM_DT, 128)
    return jnp.transpose(c, (0, 1, 2, 4, 3, 5)).reshape(d, b, MEM_ROWS, 128)


def _mem_cache_kernel(q_ref, gate_ref, k_ref, v_ref, *rest):
    o_ref = rest[-1]
    nseq = k_ref.shape[0]
    t = q_ref.shape[0] // nseq
    grp = MEM_DT * MEM_HEADS
    col = lax.broadcasted_iota(jnp.int32, (MEM_HEADS * t, MEM_ROWS), 1) % grp
    head = lax.broadcasted_iota(jnp.int32, (MEM_HEADS * t, MEM_ROWS), 0) // t
    for i in range(nseq):
        rows = slice(i * t, (i + 1) * t)
        k = k_ref[i]
        v = v_ref[i]
        qs = [jnp.concatenate([q_ref[rows, h * MEM_HEAD_DIM + dt * 128:h * MEM_HEAD_DIM + (dt + 1) * 128]
                               for h in range(MEM_HEADS)], axis=0) for dt in range(MEM_DT)]
        s = _mxu(qs[0], k, _NT)
        for dt in range(1, MEM_DT):
            s = s + pltpu.roll(_mxu(qs[dt], k, _NT), MEM_ROWS - dt * MEM_HEADS, 1)
        s = jnp.where(col == head, s * MEM_HEAD_DIM ** -0.5, -jnp.inf)
        p = jnp.exp(s - jnp.max(s, axis=-1, keepdims=True))
        p = p / jnp.sum(p, axis=-1, keepdims=True)
        for dt in range(MEM_DT):
            o = _mxu(p if dt == 0 else pltpu.roll(p, dt * MEM_HEADS, 1), v, _NN)
            for h in range(MEM_HEADS):
                sl = slice(h * MEM_HEAD_DIM + dt * 128, h * MEM_HEAD_DIM + (dt + 1) * 128)
                o_ref[rows, sl] = (o[h * t:(h + 1) * t] * _silu(gate_ref[rows, sl])).astype(BF16)


def _mem_attention(h, mix_prev, grp, q_off, gate_off, mem_k, mem_v, layer, tq):
    cached = mem_k.ndim == 4
    if cached:
        kv_specs = [pl.BlockSpec((None, grp.nseq, MEM_ROWS, 128), lambda b, i: (layer, b, 0, 0))] * 2
    else:
        kv_specs = [pl.BlockSpec((MEM_LEN, MEM_WIDTH), lambda b, i: (b, 2 * layer)),
                    pl.BlockSpec((MEM_LEN, MEM_WIDTH), lambda b, i: (b, 2 * layer + 1))]
    ins = [h, h, mem_k, mem_v] + ([] if mix_prev is None else [mix_prev])
    specs = [grp.spec(tq, MEM_WIDTH, q_off), grp.spec(tq, MEM_WIDTH, gate_off)] + kv_specs
    if mix_prev is not None:
        specs.append(pl.BlockSpec(memory_space=pl.ANY))
    return pl.pallas_call(
        _mem_cache_kernel if cached else _mem_kernel,
        grid=(grp.steps, grp.t_rows // tq),
        in_specs=specs,
        out_specs=grp.spec(tq, MEM_WIDTH, MIX_WIDTH - MEM_WIDTH),
        out_shape=jax.ShapeDtypeStruct((h.shape[0], MIX_WIDTH), BF16),
        input_output_aliases=_alias_last(len(ins), mix_prev is not None),
        compiler_params=_cparams(("parallel", "parallel")),
        name="mem_attention",
    )(*ins)


def _gla_kernel(q_ref, k_ref, v_ref, gate_ref, sm_ref, w2_ref, gb_ref, nw_ref, *rest, L, nblk, t_real, nseq,
                has_state, has_prev):
    n_opt = has_state + has_prev
    mix_ref, s_ref, ST, QK, B, OI = rest[n_opt:]
    tb = pl.program_id(1)
    rows_blk = q_ref.shape[0]

    @pl.when(tb == 0)
    def _():
        for i in range(nseq):
            for h in range(GLA_HEADS):
                ST[i * GLA_HEADS + h] = rest[0][i, h].T if has_state else jnp.zeros((GLA_DV, GLA_DK), F32)

    z = _mxu(sm_ref[...], w2_ref[...], _NN) + gb_ref[...]
    g_all = -_softplus(-z) * (1.0 / GLA_TAU)
    t_i = lax.broadcasted_iota(jnp.int32, (rows_blk, GLA_DK), 0)
    t_c = t_i & (L - 1)
    nw = nw_ref[...]
    for h in range(GLA_HEADS):
        ks = slice(h * GLA_DK, (h + 1) * GLA_DK)
        vs = slice(h * GLA_DV, (h + 1) * GLA_DV)
        q = q_ref[:, ks] * GLA_DK ** -0.5
        k = k_ref[:, ks]
        b = g_all[:, ks]
        v = v_ref[:, vs]
        if t_real < L:
            b = jnp.where(t_c < t_real, b, 0.0)
            k = jnp.where(t_c < t_real, k, 0.0)
        sh = 1
        while sh < L:
            b = b + jnp.where(t_c >= sh, pltpu.roll(b, sh, 0), 0.0)
            sh *= 2
        o = jnp.sum(q * k, axis=-1, keepdims=True) * v
        for j in range(1, L):
            d = jnp.where(t_c >= j, b - pltpu.roll(b, j, 0), -jnp.inf)
            p = jnp.exp(d) * q * pltpu.roll(k, j, 0)
            o = o + jnp.sum(p, axis=-1, keepdims=True) * pltpu.roll(v, j, 0)
        OI[:, vs] = o
        B[:, ks] = b
        QK[:, ks] = q
        QK[:, GLA_QK + h * GLA_DK:GLA_QK + (h + 1) * GLA_DK] = k
    for c in range(rows_blk // L):
        rows = slice(c * L, (c + 1) * L)
        for h in range(GLA_HEADS):
            ks = slice(h * GLA_DK, (h + 1) * GLA_DK)
            vs = slice(h * GLA_DV, (h + 1) * GLA_DV)
            si = (c if nseq > 1 else 0) * GLA_HEADS + h
            b = B[rows, ks]
            b_last = b[L - 1:L, :]
            st = ST[si]
            o = OI[rows, vs] + _mxu(QK[rows, ks] * jnp.exp(b), st, _NT)
            y = o * lax.rsqrt(jnp.mean(o * o, axis=-1, keepdims=True) + 1e-6) * nw
            mix_ref[rows, vs] = (y * _silu(gate_ref[rows, vs])).astype(BF16)
            kd = QK[rows, GLA_QK + h * GLA_DK:GLA_QK + (h + 1) * GLA_DK] * jnp.exp(b_last - b)
            ST[si] = st * jnp.exp(b_last) + _mxu(v_ref[rows, vs], kd, _TN)

    @pl.when(tb == nblk - 1)
    def _():
        for i in range(nseq):
            for h in range(GLA_HEADS):
                s_ref[i, h] = ST[i * GLA_HEADS + h].T


def _gla(h, mix_prev, grp, P, st_in, s_prev, p, L, tb):
    nblk = grp.t_rows // tb
    tail = (GLA_HEADS, GLA_DK, GLA_DV)
    rows_blk = tb * grp.nseq
    s_ins, s_specs, s_out, s_shape = _state_io(tail, p, st_in, grp)
    full = lambda shape: pl.BlockSpec(shape, lambda b, c: (0,) * len(shape))
    w2 = jnp.concatenate([P['gla_w2'], jnp.zeros((128 - GLA_RANK, GLA_QK), F32)], axis=0)
    ins = [h, h, h, h, h, w2, P['gla_b'].reshape(1, GLA_QK), P['gla_norm'].reshape(1, GLA_DV)]
    specs = [grp.spec(tb, GLA_QK, EVEN_OFF[0]), grp.spec(tb, GLA_QK, EVEN_OFF[1]), grp.spec(tb, GLA_WIDTH, EVEN_OFF[2]),
             grp.spec(tb, GLA_WIDTH, EVEN_OFF[4]), grp.spec(tb, 128, EVEN_OFF[3]),
             full((128, GLA_QK)), full((1, GLA_QK)), full((1, GLA_DV))]
    ins += s_ins
    specs += s_specs
    any_spec = pl.BlockSpec(memory_space=pl.ANY)
    alias = {}
    for prev, out_idx in ((mix_prev, 0), (s_prev, 1)):
        if prev is not None:
            alias[len(ins)] = out_idx
            ins.append(prev)
            specs.append(any_spec)
    n_prev = (mix_prev is not None) + (s_prev is not None)
    return pl.pallas_call(
        functools.partial(_gla_kernel, L=L, nblk=nblk, t_real=grp.t_real or L, nseq=grp.nseq,
                          has_state=st_in is not None, has_prev=n_prev),
        grid=(grp.steps, nblk),
        in_specs=specs,
        out_specs=[grp.spec(tb, GLA_WIDTH, SSD_WIDTH), s_out],
        out_shape=[jax.ShapeDtypeStruct((h.shape[0], MIX_WIDTH), BF16), s_shape],
        scratch_shapes=[pltpu.VMEM((grp.nseq * GLA_HEADS, GLA_DV, GLA_DK), F32),
                        pltpu.VMEM((rows_blk, 2 * GLA_QK), F32), pltpu.VMEM((rows_blk, GLA_QK), F32),
                        pltpu.VMEM((rows_blk, GLA_WIDTH), F32)],
        input_output_aliases=alias,
        compiler_params=_cparams(("parallel", "arbitrary")),
        name="gla",
    )(*ins)


SSD_GW = SSD_REP * SSD_HEAD_DIM


def _ssd_lanes(L):
    return max(SSD_REP * L, 128)


def _ssd_kernel(sz_ref, xbc_ref, sm_ref, cw_ref, cbias_ref, dtb_ref, alog_ref, dvec_ref, nw_ref, ep_ref, es_ref, *rest,
                L, nchunk, t_real, nseq, has_state, has_prev):
    n_opt = 2 * has_state + has_prev
    mix_ref, s_ref, conv_out_ref, ST, tail = rest[n_opt:]
    ci = pl.program_id(1)
    gs = _ssd_lanes(L)
    cat = jnp.concatenate

    @pl.when(ci == 0)
    def _():
        tail[...] = jnp.zeros_like(tail)
        for i in range(nseq):
            for g in range(SSD_GROUPS):
                if has_state:
                    ST[i * SSD_GROUPS + g] = rest[1][i, g * SSD_REP:(g + 1) * SSD_REP].reshape(SSD_GW, SSD_STATE).T
                else:
                    ST[i * SSD_GROUPS + g] = jnp.zeros((SSD_STATE, SSD_GW), F32)
            if has_state:
                tail[i, 8 - (CONV_W - 1):8, :] = rest[0][i]

    d = 128
    row8 = lax.broadcasted_iota(jnp.int32, (8, d), 0)
    row = lax.broadcasted_iota(jnp.int32, (L, L), 0)
    col = lax.broadcasted_iota(jnp.int32, (L, L), 1)
    tril = (col <= row).astype(F32)
    ep = ep_ref[...]
    t_i = lax.broadcasted_iota(jnp.int32, (L, SSD_GROUPS * gs), 0)
    s_i = lax.broadcasted_iota(jnp.int32, (L, SSD_GROUPS * gs), 1) & (L - 1)
    blk_r = lax.broadcasted_iota(jnp.int32, (gs, SSD_GW), 0) // L
    blk_c = lax.broadcasted_iota(jnp.int32, (gs, SSD_GW), 1) // SSD_HEAD_DIM
    diag = blk_r == blk_c
    reps = SSD_REP * L
    for i in range(nseq):
        rows = slice(i * L, (i + 1) * L)

        def conv_tile(c0):
            u = xbc_ref[rows, c0:c0 + d]
            p8 = tail[i, :, c0:c0 + d]
            w = cw_ref[:, c0:c0 + d]
            acc = u * w[CONV_W - 1:CONV_W] + cbias_ref[:, c0:c0 + d]
            for j in range(1, CONV_W):
                acc = acc + _shifted(u, p8, j, row8) * w[CONV_W - 1 - j:CONV_W - j]
            return _silu(acc)

        dt = _softplus(sm_ref[rows, GLA_RANK:GLA_RANK + SSD_HEADS] + dtb_ref[...])
        if t_real < L:
            dt = jnp.where(lax.broadcasted_iota(jnp.int32, (L, SSD_HEADS), 0) < t_real, dt, 0.0)
        c = _mxu_f32(tril, dt * -jnp.exp(alog_ref[...]), _NN)
        dt_x = _mxu_f32(dt, ep, _NN)
        c_x = _mxu_f32(c, ep, _NN)
        c_s = _mxu_f32(c, es_ref[...], _NN)
        c_src = jnp.sum(jnp.where(t_i == s_i, c_s, 0.0), axis=0, keepdims=True)
        seg = jnp.exp(jnp.where(s_i <= t_i, c_s - c_src, -jnp.inf))
        for g in range(SSD_GROUPS):
            gl = slice(g * SSD_GW, (g + 1) * SSD_GW)
            sx = cat([conv_tile(g * SSD_GW + n * d) for n in range(SSD_GW // d)], axis=1)
            bm = conv_tile(SSD_WIDTH + g * SSD_STATE)
            cm = conv_tile(SSD_WIDTH + SSD_GROUPS * SSD_STATE + g * SSD_STATE)
            xdt = sx * dt_x[:, gl]
            pad_rows = [] if reps == gs else [jnp.zeros((gs - reps, SSD_STATE), F32)]
            cb = _mxu(cm, cat([bm] * SSD_REP + pad_rows, axis=0), _NT)
            pad_rows = [] if reps == gs else [jnp.zeros((gs - reps, SSD_GW), F32)]
            xbd = jnp.where(diag, cat([xdt] * SSD_REP + pad_rows, axis=0), 0.0)
            st = ST[i * SSD_GROUPS + g]
            y = _mxu(cb * seg[:, g * gs:(g + 1) * gs], xbd, _NN) + _mxu(cm, st, _NN) * jnp.exp(c_x[:, gl])
            y = (y + sx * dvec_ref[:, gl]) * _silu(sz_ref[rows, gl])
            y = y * lax.rsqrt(jnp.mean(y * y, axis=-1, keepdims=True) + 1e-6) * nw_ref[:, gl]
            mix_ref[rows, gl] = y.astype(BF16)
            c_end = c_x[L - 1:L, gl]
            ST[i * SSD_GROUPS + g] = st * jnp.exp(c_end) + _mxu(bm, xdt * jnp.exp(c_end - c_x[:, gl]), _TN)
        if nchunk > 1:
            tail[i] = xbc_ref[i * L + L - 8:(i + 1) * L, :]

    @pl.when(ci == nchunk - 1)
    def _():
        for i in range(nseq):
            conv_out_ref[i] = xbc_ref[i * L + t_real - (CONV_W - 1):i * L + t_real, :]
            for g in range(SSD_GROUPS):
                s_ref[i, g * SSD_REP:(g + 1) * SSD_REP] = ST[i * SSD_GROUPS + g].T.reshape(
                    SSD_REP, SSD_HEAD_DIM, SSD_STATE)


def _ssd(h, mix_prev, grp, P, st_in, conv_in, s_prev, p):
    L = grp.chunk
    nchunk = grp.t_rows // L
    gs = _ssd_lanes(L)
    tail = (SSD_HEADS, SSD_HEAD_DIM, SSD_STATE)
    ns = grp.nseq
    s_ins, s_specs, s_out, s_shape = _state_io(tail, p, st_in, grp)
    full = lambda shape: pl.BlockSpec(shape, lambda b, c: (0,) * len(shape))
    heads = jnp.arange(SSD_HEADS)[:, None]
    lane_p = jnp.arange(SSD_WIDTH)[None, :]
    ep = (lane_p // SSD_HEAD_DIM == heads).astype(F32)
    lane_s = jnp.arange(SSD_GROUPS * gs)[None, :]
    in_grp = lane_s % gs
    es = ((in_grp < SSD_REP * L) & ((lane_s // gs) * SSD_REP + in_grp // L == heads)).astype(F32)
    row = lambda a: a.reshape(1, -1)
    ins = [h, h, h, P['ssd_conv_w'], row(P['ssd_conv_b']), row(P['ssd_dt_bias']), row(P['ssd_a_log']),
           row(jnp.repeat(P['ssd_d'], SSD_HEAD_DIM)), row(P['ssd_norm']), ep, es]
    specs = [grp.spec(L, SSD_WIDTH, EVEN_OFF[5]), grp.spec(L, SSD_CONV_DIM, EVEN_OFF[6]), grp.spec(L, 128, EVEN_OFF[3]),
             full((CONV_W, SSD_CONV_DIM)), full((1, SSD_CONV_DIM)), full((1, SSD_HEADS)), full((1, SSD_HEADS)),
             full((1, SSD_WIDTH)), full((1, SSD_WIDTH)), full(ep.shape), full(es.shape)]
    if st_in is not None:
        ins += [conv_in] + s_ins
        specs += [pl.BlockSpec((None, ns, CONV_W - 1, SSD_CONV_DIM), lambda b, c: (p, b, 0, 0))] + s_specs
    any_spec = pl.BlockSpec(memory_space=pl.ANY)
    alias = {}
    for prev, out_idx in ((mix_prev, 0), (s_prev, 1)):
        if prev is not None:
            alias[len(ins)] = out_idx
            ins.append(prev)
            specs.append(any_spec)
    n_prev = (mix_prev is not None) + (s_prev is not None)
    return pl.pallas_call(
        functools.partial(_ssd_kernel, L=L, nchunk=nchunk, t_real=grp.t_real or L, nseq=ns,
                          has_state=st_in is not None, has_prev=n_prev),
        grid=(grp.steps, nchunk),
        in_specs=specs,
        out_specs=[grp.spec(L, SSD_WIDTH, 0), s_out,
                   pl.BlockSpec((ns, CONV_W - 1, SSD_CONV_DIM), lambda b, c: (b, 0, 0))],
        out_shape=[jax.ShapeDtypeStruct((h.shape[0], MIX_WIDTH), BF16), s_shape,
                   jax.ShapeDtypeStruct((grp.bsz, CONV_W - 1, SSD_CONV_DIM), F32)],
        scratch_shapes=[pltpu.VMEM((ns * SSD_GROUPS, SSD_STATE, SSD_GW), F32),
                        pltpu.VMEM((ns, 8, SSD_CONV_DIM), F32)],
        input_output_aliases=alias,
        compiler_params=_cparams(("parallel", "arbitrary")),
        name="ssd",
    )(*ins)


def _gdn_kernel(qkv_ref, cz_ref, sm_ref, cw_ref, alog_ref, dtb_ref, nw_ref, *rest, L, nchunk, t_real, nseq,
                has_state, has_prev):
    n_opt = 2 * has_state + has_prev
    mix_ref, s_ref, conv_out_ref, S, tail = rest[n_opt:]
    ci = pl.program_id(1)

    @pl.when(ci == 0)
    def _():
        tail[...] = jnp.zeros_like(tail)
        if has_state:
            for i in range(nseq):
                tail[i, 8 - (CONV_W - 1):8, :] = rest[0][i]
            S[...] = rest[1][...]
        else:
            S[...] = jnp.zeros_like(S)

    n2 = 2 * L
    d = GDN_HEAD_DIM
    cat = jnp.concatenate
    row8 = lax.broadcasted_iota(jnp.int32, (8, d), 0)

    def conv_tile(i, c0):
        u = qkv_ref[i * L:(i + 1) * L, c0:c0 + d]
        p8 = tail[i, :, c0:c0 + d]
        w = cw_ref[:, c0:c0 + d]
        acc = u * w[CONV_W - 1:CONV_W]
        for j in range(1, CONV_W):
            acc = acc + _shifted(u, p8, j, row8) * w[CONV_W - 1 - j:CONV_W - j]
        return _silu(acc)

    def l2n(x):
        return x * lax.rsqrt(jnp.sum(x * x, axis=-1, keepdims=True) + 1e-6)

    row = lax.broadcasted_iota(jnp.int32, (L, L), 0)
    col = lax.broadcasted_iota(jnp.int32, (L, L), 1)
    tril = (col <= row).astype(F32)
    r2 = lax.broadcasted_iota(jnp.int32, (n2, n2), 0)
    c2 = lax.broadcasted_iota(jnp.int32, (n2, n2), 1)
    same = (r2 >= L) == (c2 >= L)
    strict = same & (c2 < r2)
    incl = same & (c2 <= r2)
    upper = same & (r2 <= c2)
    eye = (r2 == c2).astype(F32)
    zl = jnp.zeros((L, d), F32)
    units = [(i, j) for i in range(nseq) for j in range(GDN_HEADS // 2)]
    nn, tt, qk, kq, kdec, ec, bcol, elast, vst = [], [], [], [], [], [], [], [], []
    for i in range(nseq):
        sm = sm_ref[i * L:(i + 1) * L, :]
        beta_all = _sigmoid(sm[:, :GDN_HEADS])
        g_all = -jnp.exp(alog_ref[...]) * _softplus(sm[:, GDN_HEADS:2 * GDN_HEADS] + dtb_ref[...])
        if t_real < L:
            valid = lax.broadcasted_iota(jnp.int32, (L, GDN_HEADS), 0) < t_real
            beta_all = jnp.where(valid, beta_all, 0.0)
            g_all = jnp.where(valid, g_all, 0.0)
        c_all = _mxu_f32(tril, g_all, _NN)
        for j in range(GDN_HEADS // 2):
            h0, h1 = 2 * j, 2 * j + 1
            stack_col = lambda a: cat([a[:, h0:h0 + 1], a[:, h1:h1 + 1]], axis=0)
            c_col = stack_col(c_all)
            beta_col = stack_col(beta_all)
            c_row = jnp.sum(jnp.where(upper, stack_col(g_all), 0.0), axis=0, keepdims=True)
            decay = jnp.exp(jnp.where(incl, c_col - c_row, -jnp.inf))
            last = lambda rows: cat([jnp.broadcast_to(c_all[L - 1:L, h0:h0 + 1], (rows, 1)),
                                     jnp.broadcast_to(c_all[L - 1:L, h1:h1 + 1], (rows, 1))], axis=0)
            q0, q1 = (l2n(conv_tile(i, h * d)) * d ** -0.5 for h in (h0, h1))
            k0, k1 = (l2n(conv_tile(i, GDN_WIDTH + h * d)) for h in (h0, h1))
            vst.append(cat([conv_tile(i, 2 * GDN_WIDTH + h0 * d), conv_tile(i, 2 * GDN_WIDTH + h1 * d)], axis=0))
            k_st = cat([cat([k0, zl], axis=1), cat([zl, k1], axis=1)], axis=0)
            q_st = cat([cat([q0, zl], axis=1), cat([zl, q1], axis=1)], axis=0)
            both = cat([k_st, q_st], axis=0)
            full = _mxu(both, k_st, _NT)
            a = jnp.where(strict, full[:n2] * decay * beta_col, 0.0)
            tt.append(a)
            qk.append(full[n2:] * decay)
            kq.append(both)
            kdec.append(k_st * jnp.exp(last(L) - c_col))
            ec.append(jnp.exp(c_col))
            bcol.append(beta_col)
            elast.append(jnp.exp(last(d)))
    tt = _unit_lower_inverse(tt, L, eye, r2 & (L - 1), c2 & (L - 1))
    s_old, ksqs, u = [], [], []
    for n, (i, j) in enumerate(units):
        s_old.append(cat([S[i, 2 * j], S[i, 2 * j + 1]], axis=0))
        ksqs.append(_mxu(kq[n], s_old[n], _NN))
    for n in range(len(units)):
        u.append(_mxu(tt[n], bcol[n] * (vst[n] - ec[n] * ksqs[n][:n2]), _NN))
    nw = nw_ref[...]
    for n, (i, j) in enumerate(units):
        rows = slice(i * L, (i + 1) * L)
        o = ec[n] * ksqs[n][n2:] + _mxu(qk[n], u[n], _NN)
        for hh, oh in ((2 * j, o[:L]), (2 * j + 1, o[L:])):
            cols = slice(hh * d, (hh + 1) * d)
            y = oh * lax.rsqrt(jnp.mean(oh * oh, axis=-1, keepdims=True) + 1e-6) * nw
            mix_ref[rows, cols] = (y * _silu(cz_ref[rows, cols])).astype(BF16)
        new = s_old[n] * elast[n] + _mxu(kdec[n], u[n], _TN)
        S[i, 2 * j] = new[:d]
        S[i, 2 * j + 1] = new[d:]
    if nchunk > 1:
        for i in range(nseq):
            tail[i] = qkv_ref[i * L + L - 8:(i + 1) * L, :]

    @pl.when(ci == nchunk - 1)
    def _():
        s_ref[...] = S[...]
        for i in range(nseq):
            conv_out_ref[i] = qkv_ref[i * L + t_real - (CONV_W - 1):i * L + t_real, :]


def _gdn(h, mix_prev, grp, P, st_in, conv_in, s_prev, p):
    L = grp.chunk
    nchunk = grp.t_rows // L
    tail = (GDN_HEADS, GDN_HEAD_DIM, GDN_HEAD_DIM)
    ns = grp.nseq
    s_ins, s_specs, s_out, s_shape = _state_io(tail, p, st_in, grp)
    cw = 3 * GDN_WIDTH
    full = lambda shape: pl.BlockSpec(shape, lambda b, c: (0,) * len(shape))
    ins = [h, h, h, P['gdn_conv_w'], P['gdn_a_log'].reshape(1, GDN_HEADS), P['gdn_dt_bias'].reshape(1, GDN_HEADS),
           P['gdn_norm'].reshape(1, GDN_HEAD_DIM)]
    specs = [grp.spec(L, cw, ODD_OFF[0]), grp.spec(L, GDN_WIDTH, ODD_OFF[1]), grp.spec(L, 128, ODD_OFF[2]),
             full((CONV_W, cw)), full((1, GDN_HEADS)), full((1, GDN_HEADS)), full((1, GDN_HEAD_DIM))]
    if st_in is not None:
        ins += [conv_in] + s_ins
        specs += [pl.BlockSpec((None, ns, CONV_W - 1, cw), lambda b, c: (p, b, 0, 0))] + s_specs
    any_spec = pl.BlockSpec(memory_space=pl.ANY)
    alias = {}
    for prev, out_idx in ((mix_prev, 0), (s_prev, 1)):
        if prev is not None:
            alias[len(ins)] = out_idx
            ins.append(prev)
            specs.append(any_spec)
    n_prev = (mix_prev is not None) + (s_prev is not None)
    return pl.pallas_call(
        functools.partial(_gdn_kernel, L=L, nchunk=nchunk, t_real=grp.t_real or L,
                          nseq=ns, has_state=st_in is not None, has_prev=n_prev),
        grid=(grp.steps, nchunk),
        in_specs=specs,
        out_specs=[grp.spec(L, GDN_WIDTH, 0), s_out,
                   pl.BlockSpec((ns, CONV_W - 1, cw), lambda b, c: (b, 0, 0))],
        out_shape=[jax.ShapeDtypeStruct((h.shape[0], MIX_WIDTH), BF16), s_shape,
                   jax.ShapeDtypeStruct((grp.bsz, CONV_W - 1, cw), F32)],
        scratch_shapes=[pltpu.VMEM((ns,) + tail, F32), pltpu.VMEM((ns, 8, cw), F32)],
        input_output_aliases=alias,
        compiler_params=_cparams(("parallel", "arbitrary")),
        name="gdn",
    )(*ins)


RWKV_PAIRS = RWKV_HEADS // 2
RWKV_PW = 2 * RWKV_HEAD_DIM


def _rwkv_kernel(r_ref, k_ref, v_ref, xwa_ref, gate_ref, mu_ref, w0_ref, w2_ref, a0_ref, a2_ref, kkp_ref, ka_ref,
                 rk_ref, lng_ref, lnb_ref, *rest, L, nchunk, t_real, nseq, has_state, has_prev):
    n_opt = 2 * has_state + has_prev
    mix_ref, s_ref, shift_out_ref, S, last = rest[n_opt:]
    ci = pl.program_id(1)
    n = RWKV_HEAD_DIM
    w3 = 3 * RWKV_WIDTH
    cat = jnp.concatenate
    pairs = range(RWKV_PAIRS)

    @pl.when(ci == 0)
    def _():
        last[...] = jnp.zeros_like(last)
        if has_state:
            zn = jnp.zeros((n, n), F32)
            for i in range(nseq):
                last[i, 7:8, :] = rest[0][i]
                for j in pairs:
                    S[i * RWKV_PAIRS + j] = cat([cat([rest[1][i, 2 * j], zn], axis=1),
                                                 cat([zn, rest[1][i, 2 * j + 1]], axis=1)], axis=0)
        else:
            S[...] = jnp.zeros_like(S)

    n2 = 2 * L
    row = lax.broadcasted_iota(jnp.int32, (L, L), 0)
    col = lax.broadcasted_iota(jnp.int32, (L, L), 1)
    tril = (col <= row).astype(F32)
    r2 = lax.broadcasted_iota(jnp.int32, (n2, n2), 0)
    c2 = lax.broadcasted_iota(jnp.int32, (n2, n2), 1)
    same = (r2 >= L) == (c2 >= L)
    strict = same & (c2 < r2)
    incl = same & (c2 <= r2)
    eye = (r2 == c2).astype(F32)
    lane = lax.broadcasted_iota(jnp.int32, (L, RWKV_PW), 1)
    lo = lane < n

    def stack(x):
        return cat([jnp.where(lo, x, 0.0), jnp.where(lo, 0.0, x)], axis=0)

    row8 = lax.broadcasted_iota(jnp.int32, (8, RWKV_PW), 0)
    valid = lax.broadcasted_iota(jnp.int32, (L, RWKV_PW), 0) < t_real

    def seg_sum(x):
        s_lo = jnp.sum(jnp.where(lo, x, 0.0), axis=-1, keepdims=True)
        s_hi = jnp.sum(jnp.where(lo, 0.0, x), axis=-1, keepdims=True)
        return jnp.where(lo, s_lo, s_hi)

    def shift_mix(ref, i, c_src, c_all):
        x = ref[i * L:(i + 1) * L, c_src:c_src + RWKV_PW]
        prev = _shifted(x, last[i, :, c_all:c_all + RWKV_PW], 1, row8)
        return x + (prev - x) * mu_ref[:, c_all:c_all + RWKV_PW]

    xwa = cat([shift_mix(xwa_ref, i, 0, w3) for i in range(nseq)], axis=0)
    lr_w_all = _mxu(jnp.tanh(xwa), w2_ref[...], _NN)
    lr_a_all = _mxu(xwa, a2_ref[...], _NN)
    units = [(i, j) for i in range(nseq) for j in pairs]
    a_ak, a_rk, a_rb, nn, tt, sread, kdbd, egl, vs, bonus = [], [], [], [], [], [], [], [], [], []
    for i, j in units:
        sl = slice(j * RWKV_PW, (j + 1) * RWKV_PW)
        r = shift_mix(r_ref, i, j * RWKV_PW, j * RWKV_PW)
        k = shift_mix(k_ref, i, j * RWKV_PW, RWKV_WIDTH + j * RWKV_PW)
        v = shift_mix(v_ref, i, j * RWKV_PW, 2 * RWKV_WIDTH + j * RWKV_PW)
        w_log = -_softplus(-(w0_ref[:, sl] + lr_w_all[i * L:(i + 1) * L, sl])) - 0.5
        lw = -jnp.exp(w_log)
        a7 = _sigmoid(a0_ref[:, sl] + lr_a_all[i * L:(i + 1) * L, sl])
        kx = k * kkp_ref[:, sl]
        kk = kx * lax.rsqrt(seg_sum(kx * kx) + 1e-6)
        k = k * (1.0 + (a7 - 1.0) * ka_ref[:, sl])
        if t_real < L:
            lw, kk, k, v = (jnp.where(valid, a, 0.0) for a in (lw, kk, k, v))
        b = kk * a7
        bonus.append(seg_sum(r * k * rk_ref[:, sl]) * v)
        vs.append(stack(v))
        g = _mxu_f32(tril, lw, _NN)
        gp = g - lw
        gm = g[L // 2 - 1:L // 2, :]
        gl = g[L - 1:L, :]
        e_neg = jnp.exp(gm - g)
        lhs = cat([stack(kk * jnp.exp(gp - gm)), stack(r * jnp.exp(g - gm))], axis=0)
        rhs = cat([stack(b * e_neg), stack(k * e_neg)], axis=0)
        full = _mxu(lhs, rhs, _NT)
        a_ab = jnp.where(strict, full[:n2, :n2], 0.0)
        a_ak.append(jnp.where(strict, full[:n2, n2:], 0.0))
        a_rb.append(jnp.where(incl, full[n2:, :n2], 0.0))
        a_rk.append(jnp.where(incl, full[n2:, n2:], 0.0))
        tt.append(a_ab)
        sread.append(cat([stack(kk * jnp.exp(gp)), stack(r * jnp.exp(g))], axis=0))
        dec = jnp.exp(gl - g)
        kdbd.append(cat([stack(k * dec), stack(-b * dec)], axis=0))
        egl.append(jnp.exp(gl))
    tt = _unit_lower_inverse(tt, L, eye, r2 & (L - 1), c2 & (L - 1))
    s_old, sr, av, u = [], [], [], []
    for m, (i, j) in enumerate(units):
        s_old.append(S[i * RWKV_PAIRS + j])
        sr.append(_mxu(sread[m], s_old[m], _NT))
        av.append(_mxu(cat([a_ak[m], a_rk[m]], axis=0), vs[m], _NN))
    for m in range(len(units)):
        u.append(_mxu(tt[m], sr[m][:n2] + av[m][:n2], _NN))
    for m, (i, j) in enumerate(units):
        sl = slice(j * RWKV_PW, (j + 1) * RWKV_PW)
        rows = slice(i * L, (i + 1) * L)
        o = sr[m][n2:] + av[m][n2:] - _mxu(a_rb[m], u[m], _NN)
        o = o[:L] + o[L:]
        oc = o - seg_sum(o) * (1.0 / n)
        gn = oc * lax.rsqrt(seg_sum(oc * oc) * (1.0 / n) + RWKV_GN_EPS)
        y = gn * lng_ref[:, sl] + lnb_ref[:, sl] + bonus[m]
        mix_ref[rows, sl] = (y * _silu(gate_ref[rows, sl])).astype(BF16)
        S[i * RWKV_PAIRS + j] = s_old[m] * egl[m] + _mxu(cat([vs[m], u[m]], axis=0), kdbd[m], _TN)
    pieces = ((r_ref, 0, RWKV_WIDTH), (k_ref, RWKV_WIDTH, RWKV_WIDTH), (v_ref, 2 * RWKV_WIDTH, RWKV_WIDTH),
              (xwa_ref, w3, RWKV_PW))
    if nchunk > 1:
        for i in range(nseq):
            for ref, c0, wd in pieces:
                last[i, :, c0:c0 + wd] = ref[i * L + L - 8:(i + 1) * L, :]

    @pl.when(ci == nchunk - 1)
    def _():
        for i in range(nseq):
            for ref, c0, wd in pieces:
                shift_out_ref[i, :, c0:c0 + wd] = ref[i * L + t_real - 1:i * L + t_real, :]
            for j in pairs:
                s_ref[i, 2 * j] = S[i * RWKV_PAIRS + j][:n, :n]
                s_ref[i, 2 * j + 1] = S[i * RWKV_PAIRS + j][n:, n:]


def _rwkv(h, mix_prev, grp, P, st_in, shift_in, s_prev, p):
    L = grp.chunk
    nchunk = grp.t_rows // L
    tail = (RWKV_HEADS, RWKV_HEAD_DIM, RWKV_HEAD_DIM)
    ns = grp.nseq
    s_ins, s_specs, s_out, s_shape = _state_io(tail, p, st_in, grp)
    full = lambda shape: pl.BlockSpec(shape, lambda b, c: (0,) * len(shape))
    row = lambda a: a.reshape(1, -1)
    zr = jnp.zeros((RWKV_HEAD_DIM, RWKV_WIDTH), F32)
    w2 = jnp.concatenate([P['rwkv_w2'], zr], axis=0)
    a2 = jnp.concatenate([zr, P['rwkv_a2']], axis=0)
    off = ODD_OFF[4]
    ins = [h, h, h, h, h, row(P['rwkv_mu']), row(P['rwkv_w0']), w2, row(P['rwkv_a0']), a2, row(P['rwkv_kk']),
           row(P['rwkv_ka']), row(P['rwkv_rk']), row(P['rwkv_ln_g']), row(P['rwkv_ln_b'])]
    vec = full((1, RWKV_WIDTH))
    specs = [grp.spec(L, RWKV_WIDTH, off), grp.spec(L, RWKV_WIDTH, off + RWKV_WIDTH),
             grp.spec(L, RWKV_WIDTH, off + 2 * RWKV_WIDTH), grp.spec(L, RWKV_PW, off + 3 * RWKV_WIDTH),
             grp.spec(L, RWKV_WIDTH, ODD_OFF[5]), full((1, RWKV_SHIFT_DIM)), vec, full((RWKV_PW, RWKV_WIDTH)), vec,
             full((RWKV_PW, RWKV_WIDTH)), vec, vec, vec, vec, vec]
    if st_in is not None:
        ins += [shift_in] + s_ins
        specs += [pl.BlockSpec((None, ns, 1, RWKV_SHIFT_DIM), lambda b, c: (p, b, 0, 0))] + s_specs
    any_spec = pl.BlockSpec(memory_space=pl.ANY)
    alias = {}
    for prev, out_idx in ((mix_prev, 0), (s_prev, 1)):
        if prev is not None:
            alias[len(ins)] = out_idx
            ins.append(prev)
            specs.append(any_spec)
    n_prev = (mix_prev is not None) + (s_prev is not None)
    return pl.pallas_call(
        functools.partial(_rwkv_kernel, L=L, nchunk=nchunk, t_real=grp.t_real or L,
                          nseq=ns, has_state=st_in is not None, has_prev=n_prev),
        grid=(grp.steps, nchunk),
        in_specs=specs,
        out_specs=[grp.spec(L, RWKV_WIDTH, GDN_WIDTH), s_out,
                   pl.BlockSpec((ns, 1, RWKV_SHIFT_DIM), lambda b, c: (b, 0, 0))],
        out_shape=[jax.ShapeDtypeStruct((h.shape[0], MIX_WIDTH), BF16), s_shape,
                   jax.ShapeDtypeStruct((grp.bsz, 1, RWKV_SHIFT_DIM), F32)],
        scratch_shapes=[pltpu.VMEM((ns * RWKV_PAIRS, RWKV_PW, RWKV_PW), F32),
                        pltpu.VMEM((ns, 8, RWKV_SHIFT_DIM), F32)],
        input_output_aliases=alias,
        compiler_params=_cparams(("parallel", "arbitrary")),
        name="rwkv7",
    )(*ins)


def _pad_t(a, t_to):
    t = a.shape[1]
    if t == t_to:
        return a
    return jnp.pad(a, [(0, 0), (0, t_to - t)] + [(0, 0)] * (a.ndim - 2))


def _even_mix(h, mix, grp, mem_k, mem_v, layer, st_in, st_prev, conv_in, P):
    p = layer // 2
    mix, s_ssd, s_conv = _ssd(h, mix, grp, P, st_in['ssd'], conv_in, st_prev['ssd'], p)
    gla_l, gla_tb = (GLA_CHUNK, 256) if grp.t_real is None else (grp.chunk, grp.chunk)
    mix, s_gla = _gla(h, mix, grp, P, st_in['gla'], st_prev['gla'], p, gla_l, gla_tb)
    mix = _mem_attention(h, mix, grp, EVEN_OFF[8], EVEN_OFF[9], mem_k, mem_v, layer, min(512, grp.t_rows))
    return mix, dict(gla=s_gla, ssd=s_ssd), s_conv


def _odd_mix(h, mix, grp, mem_k, mem_v, layer, st_in, st_prev, conv_in, shift_in, P):
    p = layer // 2
    mix, s_gdn, s_conv = _gdn(h, mix, grp, P, st_in['gdn'], conv_in, st_prev['gdn'], p)
    mix, s_rwkv, s_shift = _rwkv(h, mix, grp, P, st_in['rwkv'], shift_in, st_prev['rwkv'], p)
    mix = _mem_attention(h, mix, grp, ODD_OFF[6], ODD_OFF[7], mem_k, mem_v, layer, min(512, grp.t_rows))
    return mix, dict(gdn=s_gdn, rwkv=s_rwkv), s_conv, s_shift.reshape(grp.bsz, RWKV_SHIFT_DIM)


def kernel(x_prompt, x_sample, mem_prompt, cache_mem_k, cache_mem_v, state_gla, state_ssd, state_ssd_conv, state_gdn, state_gdn_conv, state_rwkv, state_rwkv_shift, mem_w_kv, ev_w_in, ev_gla_w2, ev_gla_b, ev_gla_norm, ev_ssd_conv_w, ev_ssd_conv_b, ev_ssd_dt_bias, ev_ssd_a_log, ev_ssd_d, ev_ssd_norm, ev_w_out, ev_ln_g, ev_ln_b, od_w_in, od_gdn_conv_w, od_gdn_dt_bias, od_gdn_a_log, od_gdn_norm, od_rwkv_mu, od_rwkv_w0, od_rwkv_w2, od_rwkv_a0, od_rwkv_a2, od_rwkv_kk, od_rwkv_ka, od_rwkv_rk, od_rwkv_ln_g, od_rwkv_ln_b, od_w_out, od_ln_g, od_ln_b):
    ev = dict(w_in=ev_w_in, gla_w2=ev_gla_w2, gla_b=ev_gla_b, gla_norm=ev_gla_norm,
              ssd_conv_w=ev_ssd_conv_w, ssd_conv_b=ev_ssd_conv_b, ssd_dt_bias=ev_ssd_dt_bias,
              ssd_a_log=ev_ssd_a_log, ssd_d=ev_ssd_d, ssd_norm=ev_ssd_norm,
              w_out=ev_w_out, ln_g=ev_ln_g, ln_b=ev_ln_b)
    od = dict(w_in=od_w_in, gdn_conv_w=od_gdn_conv_w, gdn_dt_bias=od_gdn_dt_bias, gdn_a_log=od_gdn_a_log,
              gdn_norm=od_gdn_norm, rwkv_mu=od_rwkv_mu, rwkv_w0=od_rwkv_w0, rwkv_w2=od_rwkv_w2,
              rwkv_a0=od_rwkv_a0, rwkv_a2=od_rwkv_a2, rwkv_kk=od_rwkv_kk, rwkv_ka=od_rwkv_ka,
              rwkv_rk=od_rwkv_rk, rwkv_ln_g=od_rwkv_ln_g, rwkv_ln_b=od_rwkv_ln_b,
              w_out=od_w_out, ln_g=od_ln_g, ln_b=od_ln_b)
    bp, tp, _ = x_prompt.shape
    bs, ts, _ = x_sample.shape
    mp, ms = bp * tp, bs * SMALL_T
    grp_p = _Group(bp, tp, None, 0, SEQ_CHUNK)
    grp_s = _Group(bs, SMALL_T, ts, mp, SMALL_T, nseq=SAMPLE_NSEQ)

    w_kv = jnp.moveaxis(mem_w_kv, 0, 1).reshape(D_MODEL, DEPTH * 2 * MEM_WIDTH).astype(BF16)
    kv = _matmul(mem_prompt.reshape(bp * MEM_LEN, D_MODEL).astype(BF16), w_kv, 512, 1024)
    kv6 = kv.reshape(bp, MEM_LEN, DEPTH, 2, MEM_HEADS, MEM_HEAD_DIM)
    mem_k_p = jnp.moveaxis(kv6[:, :, :, 0], 2, 0)
    mem_v_p = jnp.moveaxis(kv6[:, :, :, 1], 2, 0)
    mk_s, mv_s = _cache_view(cache_mem_k), _cache_view(cache_mem_v)

    x = jnp.concatenate([x_prompt.reshape(mp, D_MODEL),
                         _pad_t(x_sample, SMALL_T).reshape(ms, D_MODEL)], axis=0)
    x_bf = x.astype(BF16)
    none = {n: None for n in ('gla', 'ssd', 'gdn', 'rwkv')}
    in_s = dict(gla=state_gla, ssd=state_ssd, gdn=state_gdn, rwkv=state_rwkv)
    shift_s = state_rwkv_shift.reshape(N_PAIRS, bs, 1, RWKV_SHIFT_DIM)
    out_p, out_s = dict(none), dict(none)
    small_p = {n: [] for n in ('ssd_conv', 'gdn_conv', 'rwkv_shift')}
    small_s = {n: [] for n in small_p}
    tm = 1024
    w_in_ev = _pack_w_in(ev_w_in, EVEN_SIZES, EVEN_ORDER, EVEN_N)
    w_in_od = _pack_w_in(od_w_in, ODD_SIZES, ODD_ORDER, ODD_N)
    w_out_ev = jnp.concatenate([ev_w_out[:, GLA_WIDTH:GLA_WIDTH + SSD_WIDTH].astype(BF16),
                                ev_w_out[:, :GLA_WIDTH].astype(BF16),
                                ev_w_out[:, GLA_WIDTH + SSD_WIDTH:].astype(BF16)], axis=1)
    w_out_od = od_w_out.astype(BF16)
    small = lambda d: {n: w[p] for n, w in d.items() if n not in ('w_in', 'w_out')}
    for layer in range(DEPTH):
        p = layer // 2
        if layer % 2 == 0:
            P = small(ev)
            h = _matmul(x_bf, w_in_ev, tm, PROJ_TN, p=p)
            mix, new, c1 = _even_mix(h, None, grp_p, kv, kv, layer, none, out_p, None, P)
            out_p.update(new)
            mix, new, c2 = _even_mix(h, mix, grp_s, mk_s, mv_s, layer, in_s, out_s, state_ssd_conv, P)
            out_s.update(new)
            small_p['ssd_conv'].append(c1)
            small_s['ssd_conv'].append(c2)
            w_out = w_out_ev
        else:
            P = small(od)
            h = _matmul(x_bf, w_in_od, tm, PROJ_TN, p=p)
            mix, new, c1, h1 = _odd_mix(h, None, grp_p, kv, kv, layer, none, out_p, None, None, P)
            out_p.update(new)
            mix, new, c2, h2 = _odd_mix(h, mix, grp_s, mk_s, mv_s, layer, in_s, out_s, state_gdn_conv, shift_s, P)
            out_s.update(new)
            small_p['gdn_conv'].append(c1)
            small_s['gdn_conv'].append(c2)
            small_p['rwkv_shift'].append(h1)
            small_s['rwkv_shift'].append(h2)
            w_out = w_out_od
        x, x_bf = _out_ln(mix, w_out, p, x, P['ln_g'], P['ln_b'])

    y_prompt = x[:mp].reshape(bp, tp, D_MODEL)
    y_sample = x[mp:].reshape(bs, SMALL_T, D_MODEL)[:, :ts]
    st = lambda d, n: jnp.stack(d[n])
    return (y_prompt, y_sample, mem_k_p, mem_v_p,
            out_p['gla'], out_s['gla'], out_p['ssd'], out_s['ssd'],
            st(small_p, 'ssd_conv'), st(small_s, 'ssd_conv'), out_p['gdn'], out_s['gdn'],
            st(small_p, 'gdn_conv'), st(small_s, 'gdn_conv'), out_p['rwkv'], out_s['rwkv'],
            st(small_p, 'rwkv_shift'), st(small_s, 'rwkv_shift'))
```

```python
import functools

import numpy as np
import jax
import jax.numpy as jnp
from jax import lax
from jax.experimental import pallas as pl
from jax.experimental.pallas import tpu as pltpu

F32 = jnp.float32
BF16 = jnp.bfloat16
HI = lax.Precision.HIGHEST

D_MODEL = 2048
DEPTH = 4
N_PAIRS = DEPTH // 2
CONV_W = 4
MEM_LEN = 256
MEM_HEADS = 4
MEM_HEAD_DIM = 256
MEM_WIDTH = 1024
GLA_HEADS = 4
GLA_DK = 128
GLA_DV = 256
GLA_QK = 512
GLA_WIDTH = 1024
GLA_RANK = 16
GLA_TAU = 16.0
SSD_WIDTH = 2048
SSD_HEAD_DIM = 64
SSD_HEADS = 32
SSD_GROUPS = 4
SSD_REP = 8
SSD_STATE = 128
SSD_CONV_DIM = SSD_WIDTH + 2 * SSD_GROUPS * SSD_STATE
GDN_WIDTH = 2048
GDN_HEAD_DIM = 128
GDN_HEADS = 16
RWKV_WIDTH = 1024
RWKV_HEAD_DIM = 64
RWKV_HEADS = 16
RWKV_W_RANK = 64
RWKV_A_RANK = 64
RWKV_SHIFT_DIM = 3 * RWKV_WIDTH + RWKV_W_RANK + RWKV_A_RANK
RWKV_GN_EPS = 64e-5
EVEN_SIZES = (GLA_QK, GLA_QK, GLA_WIDTH, GLA_RANK, GLA_WIDTH, SSD_WIDTH, SSD_CONV_DIM, SSD_HEADS,
              MEM_WIDTH, MEM_WIDTH)
ODD_SIZES = (3 * GDN_WIDTH, GDN_WIDTH, GDN_HEADS, GDN_HEADS, RWKV_SHIFT_DIM, RWKV_WIDTH, MEM_WIDTH, MEM_WIDTH)
MIX_WIDTH = 4096
DEEPNORM_ALPHA = (2 * DEPTH) ** 0.25

EVEN_ORDER = (5, 2, 4, 8, 9, 6, 0, 1, 3, 7)
ODD_ORDER = (0, 1, 5, 6, 7, 4, 2, 3)
PROJ_TN = 768
VMEM_LIMIT = 60 * 1024 * 1024

GLA_CHUNK = 16
SEQ_CHUNK = 64
SMALL_T = 8
SAMPLE_NSEQ = 4

_NN = ((1,), (0,))
_NT = ((1,), (1,))
_TN = ((0,), (0,))


def _packed_layout(sizes, order):
    offs, o = {}, 0
    for i in order:
        offs[i] = o
        o += sizes[i]
    total = -(-o // PROJ_TN) * PROJ_TN
    return offs, total


EVEN_OFF, EVEN_N = _packed_layout(EVEN_SIZES, EVEN_ORDER)
ODD_OFF, ODD_N = _packed_layout(ODD_SIZES, ODD_ORDER)


def _pack_w_in(w, sizes, order, total):
    ends = np.cumsum(sizes)
    parts = [w[..., ends[i] - sizes[i]:ends[i]].astype(BF16) for i in order]
    used = sum(sizes)
    if total > used:
        parts.append(jnp.zeros(w.shape[:-1] + (total - used,), BF16))
    return jnp.concatenate(parts, axis=-1)


def _cparams(sem):
    return pltpu.CompilerParams(dimension_semantics=sem, vmem_limit_bytes=VMEM_LIMIT)


def _mxu(a, b, dims):
    return lax.dot_general(a.astype(BF16), b.astype(BF16), (dims, ((), ())), preferred_element_type=F32)


def _mxu_f32(a, b, dims):
    return lax.dot_general(a, b, (dims, ((), ())), precision=HI, preferred_element_type=F32)


def _sigmoid(x):
    return 1.0 / (1.0 + jnp.exp(-x))


def _silu(x):
    return x * _sigmoid(x)


def _softplus(x):
    return jnp.maximum(x, 0.0) + jnp.log(1.0 + jnp.exp(-jnp.abs(x)))


SUBLANES = 8


def _shifted(u, prev8, j, row8):
    ru = pltpu.roll(u, j, 0)
    top = jnp.where(row8 < j, pltpu.roll(prev8, j, 0), ru[:SUBLANES])
    return top if u.shape[0] == SUBLANES else jnp.concatenate([top, ru[SUBLANES:]], axis=0)


class _Group:
    def __init__(self, bsz, t_rows, t_real, row0, chunk, nseq=1):
        assert nseq == 1 or t_rows == chunk
        self.bsz, self.t_rows, self.t_real, self.row0, self.chunk, self.nseq = bsz, t_rows, t_real, row0, chunk, nseq
        self.steps = bsz // nseq

    def spec(self, rows, width, off):
        rows = rows * self.nseq
        assert off % width == 0 and self.row0 % rows == 0 and (self.t_rows * self.nseq) % rows == 0
        base, per, cb = self.row0 // rows, self.t_rows * self.nseq // rows, off // width
        return pl.BlockSpec((rows, width), lambda b, i: (base + b * per + i, cb))


def _alias_last(n_inputs, has_prev, out_index=0):
    return {n_inputs - 1: out_index} if has_prev else {}


TRI_BASE = 4


def _unit_lower_inverse(a, L, eye, r_in, c_in):
    idx = range(len(a))
    base = (r_in // TRI_BASE) == (c_in // TRI_BASE)
    n1 = [-jnp.where(base, a[j], 0.0) for j in idx]
    n2 = [_mxu(n1[j], n1[j], _NN) for j in idx]
    tt = [eye + n1[j] for j in idx]
    tt = [tt[j] + _mxu(tt[j], n2[j], _NN) for j in idx]
    b = TRI_BASE
    while b < L:
        rb, cb = r_in // b, c_in // b
        lower_left = (rb // 2 == cb // 2) & (rb % 2 == 1) & (cb % 2 == 0)
        for j in idx:
            x = _mxu(tt[j], jnp.where(lower_left, a[j], 0.0), _NN)
            tt[j] = tt[j] - _mxu(x, tt[j], _NN)
        b *= 2
    return tt


def _state_io(tail, p, s_in, grp):
    zeros = (0,) * len(tail)
    spec = pl.BlockSpec((None, grp.nseq) + tail, lambda *g: (p, g[0]) + zeros)
    ins, specs = ([s_in], [spec]) if s_in is not None else ([], [])
    shape = jax.ShapeDtypeStruct((N_PAIRS, grp.bsz) + tail, F32)
    return ins, specs, spec, shape


def _mm_kernel(x_ref, w_ref, o_ref, *, precision):
    o_ref[...] = jnp.dot(x_ref[...], w_ref[...], preferred_element_type=F32, precision=precision)


def _matmul(x, w, tm, tn, precision=None, p=None):
    m, k = x.shape
    n = w.shape[-1]
    assert m % tm == 0 and n % tn == 0
    w_spec = (pl.BlockSpec((k, tn), lambda j, i: (0, j)) if p is None
              else pl.BlockSpec((None, k, tn), lambda j, i: (p, 0, j)))
    return pl.pallas_call(
        functools.partial(_mm_kernel, precision=precision),
        grid=(n // tn, m // tm),
        in_specs=[pl.BlockSpec((tm, k), lambda j, i: (i, 0)), w_spec],
        out_specs=pl.BlockSpec((tm, tn), lambda j, i: (i, j)),
        out_shape=jax.ShapeDtypeStruct((m, n), F32),
        compiler_params=_cparams(("parallel", "parallel")),
        name="matmul",
    )(x, w)


def _out_ln_kernel(mix_ref, w_ref, x_ref, g_ref, b_ref, y_ref, ybf_ref, acc, *, nk):
    kk = pl.program_id(1)

    @pl.when(kk == 0)
    def _():
        acc[...] = jnp.zeros_like(acc)

    acc[...] += jnp.dot(mix_ref[...].astype(BF16), w_ref[...], preferred_element_type=F32)

    @pl.when(kk == nk - 1)
    def _():
        z = DEEPNORM_ALPHA * x_ref[...] + acc[...]
        zc = z - jnp.mean(z, axis=-1, keepdims=True)
        var = jnp.mean(zc * zc, axis=-1, keepdims=True)
        y = zc * lax.rsqrt(var + 1e-5) * g_ref[...] + b_ref[...]
        y_ref[...] = y
        ybf_ref[...] = y.astype(BF16)


def _out_ln(mix, w, p, x, g, b, tm=256, tk=4096):
    m, k = mix.shape
    d = w.shape[-1]
    nk = k // tk
    return pl.pallas_call(
        functools.partial(_out_ln_kernel, nk=nk),
        grid=(m // tm, nk),
        in_specs=[pl.BlockSpec((tm, tk), lambda i, j: (i, j)),
                  pl.BlockSpec((None, tk, d), lambda i, j: (p, j, 0)),
                  pl.BlockSpec((tm, d), lambda i, j: (i, 0)),
                  pl.BlockSpec((1, d), lambda i, j: (0, 0)),
                  pl.BlockSpec((1, d), lambda i, j: (0, 0))],
        out_specs=[pl.BlockSpec((tm, d), lambda i, j: (i, 0)),
                   pl.BlockSpec((tm, d), lambda i, j: (i, 0))],
        out_shape=[jax.ShapeDtypeStruct((m, d), F32), jax.ShapeDtypeStruct((m, d), BF16)],
        scratch_shapes=[pltpu.VMEM((tm, d), F32)],
        compiler_params=_cparams(("parallel", "arbitrary")),
        name="out_ln",
    )(mix, w, x, g.reshape(1, d), b.reshape(1, d))


def _mem_kernel(q_ref, gate_ref, k_ref, v_ref, *rest):
    o_ref = rest[-1]
    for h in range(MEM_HEADS):
        sl = slice(h * MEM_HEAD_DIM, (h + 1) * MEM_HEAD_DIM)
        k = k_ref[:, sl]
        v = v_ref[:, sl]
        s = _mxu(q_ref[:, sl], k, _NT) * MEM_HEAD_DIM ** -0.5
        p = jnp.exp(s - jnp.max(s, axis=-1, keepdims=True))
        p = p / jnp.sum(p, axis=-1, keepdims=True)
        o_ref[:, sl] = (_mxu(p, v, _NN) * _silu(gate_ref[:, sl])).astype(BF16)


MEM_DT = MEM_HEAD_DIM // 128
MEM_ROWS = MEM_LEN * MEM_DT * MEM_HEADS


def _cache_view(c):
    d, b = c.shape[:2]
    c = c.reshape(d, b, MEM_LEN, MEM_HEADS, MEM_DT, 128)
    return jnp.transpose(c, (0, 1, 2, 4, 3, 5)).reshape(d, b, MEM_ROWS, 128)


def _mem_cache_kernel(q_ref, gate_ref, k_ref, v_ref, *rest):
    o_ref = rest[-1]
    nseq = k_ref.shape[0]
    t = q_ref.shape[0] // nseq
    grp = MEM_DT * MEM_HEADS
    col = lax.broadcasted_iota(jnp.int32, (MEM_HEADS * t, MEM_ROWS), 1) % grp
    head = lax.broadcasted_iota(jnp.int32, (MEM_HEADS * t, MEM_ROWS), 0) // t
    for i in range(nseq):
        rows = slice(i * t, (i + 1) * t)
        k = k_ref[i]
        v = v_ref[i]
        qs = [jnp.concatenate([q_ref[rows, h * MEM_HEAD_DIM + dt * 128:h * MEM_HEAD_DIM + (dt + 1) * 128]
                               for h in range(MEM_HEADS)], axis=0) for dt in range(MEM_DT)]
        s = _mxu(qs[0], k, _NT)
        for dt in range(1, MEM_DT):
            s = s + pltpu.roll(_mxu(qs[dt], k, _NT), MEM_ROWS - dt * MEM_HEADS, 1)
        s = jnp.where(col == head, s * MEM_HEAD_DIM ** -0.5, -jnp.inf)
        p = jnp.exp(s - jnp.max(s, axis=-1, keepdims=True))
        p = p / jnp.sum(p, axis=-1, keepdims=True)
        for dt in range(MEM_DT):
            o = _mxu(p if dt == 0 else pltpu.roll(p, dt * MEM_HEADS, 1), v, _NN)
            for h in range(MEM_HEADS):
                sl = slice(h * MEM_HEAD_DIM + dt * 128, h * MEM_HEAD_DIM + (dt + 1) * 128)
                o_ref[rows, sl] = (o[h * t:(h + 1) * t] * _silu(gate_ref[rows, sl])).astype(BF16)


def _mem_attention(h, mix_prev, grp, q_off, gate_off, mem_k, mem_v, layer, tq):
    cached = mem_k.ndim == 4
    if cached:
        kv_specs = [pl.BlockSpec((None, grp.nseq, MEM_ROWS, 128), lambda b, i: (layer, b, 0, 0))] * 2
    else:
        kv_specs = [pl.BlockSpec((MEM_LEN, MEM_WIDTH), lambda b, i: (b, 2 * layer)),
                    pl.BlockSpec((MEM_LEN, MEM_WIDTH), lambda b, i: (b, 2 * layer + 1))]
    ins = [h, h, mem_k, mem_v] + ([] if mix_prev is None else [mix_prev])
    specs = [grp.spec(tq, MEM_WIDTH, q_off), grp.spec(tq, MEM_WIDTH, gate_off)] + kv_specs
    if mix_prev is not None:
        specs.append(pl.BlockSpec(memory_space=pl.ANY))
    return pl.pallas_call(
        _mem_cache_kernel if cached else _mem_kernel,
        grid=(grp.steps, grp.t_rows // tq),
        in_specs=specs,
        out_specs=grp.spec(tq, MEM_WIDTH, MIX_WIDTH - MEM_WIDTH),
        out_shape=jax.ShapeDtypeStruct((h.shape[0], MIX_WIDTH), BF16),
        input_output_aliases=_alias_last(len(ins), mix_prev is not None),
        compiler_params=_cparams(("parallel", "parallel")),
        name="mem_attention",
    )(*ins)


def _gla_kernel(q_ref, k_ref, v_ref, gate_ref, sm_ref, w2_ref, gb_ref, nw_ref, *rest, L, nblk, t_real, nseq,
                has_state, has_prev):
    n_opt = has_state + has_prev
    mix_ref, s_ref, ST, QK, B, OI = rest[n_opt:]
    tb = pl.program_id(1)
    rows_blk = q_ref.shape[0]

    @pl.when(tb == 0)
    def _():
        for i in range(nseq):
            for h in range(GLA_HEADS):
                ST[i * GLA_HEADS + h] = rest[0][i, h].T if has_state else jnp.zeros((GLA_DV, GLA_DK), F32)

    z = _mxu(sm_ref[...], w2_ref[...], _NN) + gb_ref[...]
    g_all = -_softplus(-z) * (1.0 / GLA_TAU)
    t_i = lax.broadcasted_iota(jnp.int32, (rows_blk, GLA_DK), 0)
    t_c = t_i & (L - 1)
    nw = nw_ref[...]
    for h in range(GLA_HEADS):
        ks = slice(h * GLA_DK, (h + 1) * GLA_DK)
        vs = slice(h * GLA_DV, (h + 1) * GLA_DV)
        q = q_ref[:, ks] * GLA_DK ** -0.5
        k = k_ref[:, ks]
        b = g_all[:, ks]
        v = v_ref[:, vs]
        if t_real < L:
            b = jnp.where(t_c < t_real, b, 0.0)
            k = jnp.where(t_c < t_real, k, 0.0)
        sh = 1
        while sh < L:
            b = b + jnp.where(t_c >= sh, pltpu.roll(b, sh, 0), 0.0)
            sh *= 2
        o = jnp.sum(q * k, axis=-1, keepdims=True) * v
        for j in range(1, L):
            d = jnp.where(t_c >= j, b - pltpu.roll(b, j, 0), -jnp.inf)
            p = jnp.exp(d) * q * pltpu.roll(k, j, 0)
            o = o + jnp.sum(p, axis=-1, keepdims=True) * pltpu.roll(v, j, 0)
        OI[:, vs] = o
        B[:, ks] = b
        QK[:, ks] = q
        QK[:, GLA_QK + h * GLA_DK:GLA_QK + (h + 1) * GLA_DK] = k
    for c in range(rows_blk // L):
        rows = slice(c * L, (c + 1) * L)
        for h in range(GLA_HEADS):
            ks = slice(h * GLA_DK, (h + 1) * GLA_DK)
            vs = slice(h * GLA_DV, (h + 1) * GLA_DV)
            si = (c if nseq > 1 else 0) * GLA_HEADS + h
            b = B[rows, ks]
            b_last = b[L - 1:L, :]
            st = ST[si]
            o = OI[rows, vs] + _mxu(QK[rows, ks] * jnp.exp(b), st, _NT)
            y = o * lax.rsqrt(jnp.mean(o * o, axis=-1, keepdims=True) + 1e-6) * nw
            mix_ref[rows, vs] = (y * _silu(gate_ref[rows, vs])).astype(BF16)
            kd = QK[rows, GLA_QK + h * GLA_DK:GLA_QK + (h + 1) * GLA_DK] * jnp.exp(b_last - b)
            ST[si] = st * jnp.exp(b_last) + _mxu(v_ref[rows, vs], kd, _TN)

    @pl.when(tb == nblk - 1)
    def _():
        for i in range(nseq):
            for h in range(GLA_HEADS):
                s_ref[i, h] = ST[i * GLA_HEADS + h].T


def _gla(h, mix_prev, grp, P, st_in, s_prev, p, L, tb):
    nblk = grp.t_rows // tb
    tail = (GLA_HEADS, GLA_DK, GLA_DV)
    rows_blk = tb * grp.nseq
    s_ins, s_specs, s_out, s_shape = _state_io(tail, p, st_in, grp)
    full = lambda shape: pl.BlockSpec(shape, lambda b, c: (0,) * len(shape))
    w2 = jnp.concatenate([P['gla_w2'], jnp.zeros((128 - GLA_RANK, GLA_QK), F32)], axis=0)
    ins = [h, h, h, h, h, w2, P['gla_b'].reshape(1, GLA_QK), P['gla_norm'].reshape(1, GLA_DV)]
    specs = [grp.spec(tb, GLA_QK, EVEN_OFF[0]), grp.spec(tb, GLA_QK, EVEN_OFF[1]), grp.spec(tb, GLA_WIDTH, EVEN_OFF[2]),
             grp.spec(tb, GLA_WIDTH, EVEN_OFF[4]), grp.spec(tb, 128, EVEN_OFF[3]),
             full((128, GLA_QK)), full((1, GLA_QK)), full((1, GLA_DV))]
    ins += s_ins
    specs += s_specs
    any_spec = pl.BlockSpec(memory_space=pl.ANY)
    alias = {}
    for prev, out_idx in ((mix_prev, 0), (s_prev, 1)):
        if prev is not None:
            alias[len(ins)] = out_idx
            ins.append(prev)
            specs.append(any_spec)
    n_prev = (mix_prev is not None) + (s_prev is not None)
    return pl.pallas_call(
        functools.partial(_gla_kernel, L=L, nblk=nblk, t_real=grp.t_real or L, nseq=grp.nseq,
                          has_state=st_in is not None, has_prev=n_prev),
        grid=(grp.steps, nblk),
        in_specs=specs,
        out_specs=[grp.spec(tb, GLA_WIDTH, SSD_WIDTH), s_out],
        out_shape=[jax.ShapeDtypeStruct((h.shape[0], MIX_WIDTH), BF16), s_shape],
        scratch_shapes=[pltpu.VMEM((grp.nseq * GLA_HEADS, GLA_DV, GLA_DK), F32),
                        pltpu.VMEM((rows_blk, 2 * GLA_QK), F32), pltpu.VMEM((rows_blk, GLA_QK), F32),
                        pltpu.VMEM((rows_blk, GLA_WIDTH), F32)],
        input_output_aliases=alias,
        compiler_params=_cparams(("parallel", "arbitrary")),
        name="gla",
    )(*ins)


SSD_GW = SSD_REP * SSD_HEAD_DIM


def _ssd_lanes(L):
    return max(SSD_REP * L, 128)


def _ssd_kernel(sz_ref, xbc_ref, sm_ref, cw_ref, cbias_ref, dtb_ref, alog_ref, dvec_ref, nw_ref, ep_ref, es_ref, *rest,
                L, nchunk, t_real, nseq, has_state, has_prev):
    n_opt = 2 * has_state + has_prev
    mix_ref, s_ref, conv_out_ref, ST, tail = rest[n_opt:]
    ci = pl.program_id(1)
    gs = _ssd_lanes(L)
    cat = jnp.concatenate

    @pl.when(ci == 0)
    def _():
        tail[...] = jnp.zeros_like(tail)
        for i in range(nseq):
            for g in range(SSD_GROUPS):
                if has_state:
                    ST[i * SSD_GROUPS + g] = rest[1][i, g * SSD_REP:(g + 1) * SSD_REP].reshape(SSD_GW, SSD_STATE).T
                else:
                    ST[i * SSD_GROUPS + g] = jnp.zeros((SSD_STATE, SSD_GW), F32)
            if has_state:
                tail[i, 8 - (CONV_W - 1):8, :] = rest[0][i]

    d = 128
    row8 = lax.broadcasted_iota(jnp.int32, (8, d), 0)
    row = lax.broadcasted_iota(jnp.int32, (L, L), 0)
    col = lax.broadcasted_iota(jnp.int32, (L, L), 1)
    tril = (col <= row).astype(F32)
    ep = ep_ref[...]
    t_i = lax.broadcasted_iota(jnp.int32, (L, SSD_GROUPS * gs), 0)
    s_i = lax.broadcasted_iota(jnp.int32, (L, SSD_GROUPS * gs), 1) & (L - 1)
    blk_r = lax.broadcasted_iota(jnp.int32, (gs, SSD_GW), 0) // L
    blk_c = lax.broadcasted_iota(jnp.int32, (gs, SSD_GW), 1) // SSD_HEAD_DIM
    diag = blk_r == blk_c
    reps = SSD_REP * L
    for i in range(nseq):
        rows = slice(i * L, (i + 1) * L)

        def conv_tile(c0):
            u = xbc_ref[rows, c0:c0 + d]
            p8 = tail[i, :, c0:c0 + d]
            w = cw_ref[:, c0:c0 + d]
            acc = u * w[CONV_W - 1:CONV_W] + cbias_ref[:, c0:c0 + d]
            for j in range(1, CONV_W):
                acc = acc + _shifted(u, p8, j, row8) * w[CONV_W - 1 - j:CONV_W - j]
            return _silu(acc)

        dt = _softplus(sm_ref[rows, GLA_RANK:GLA_RANK + SSD_HEADS] + dtb_ref[...])
        if t_real < L:
            dt = jnp.where(lax.broadcasted_iota(jnp.int32, (L, SSD_HEADS), 0) < t_real, dt, 0.0)
        c = _mxu_f32(tril, dt * -jnp.exp(alog_ref[...]), _NN)
        dt_x = _mxu_f32(dt, ep, _NN)
        c_x = _mxu_f32(c, ep, _NN)
        c_s = _mxu_f32(c, es_ref[...], _NN)
        c_src = jnp.sum(jnp.where(t_i == s_i, c_s, 0.0), axis=0, keepdims=True)
        seg = jnp.exp(jnp.where(s_i <= t_i, c_s - c_src, -jnp.inf))
        for g in range(SSD_GROUPS):
            gl = slice(g * SSD_GW, (g + 1) * SSD_GW)
            sx = cat([conv_tile(g * SSD_GW + n * d) for n in range(SSD_GW // d)], axis=1)
            bm = conv_tile(SSD_WIDTH + g * SSD_STATE)
            cm = conv_tile(SSD_WIDTH + SSD_GROUPS * SSD_STATE + g * SSD_STATE)
            xdt = sx * dt_x[:, gl]
            pad_rows = [] if reps == gs else [jnp.zeros((gs - reps, SSD_STATE), F32)]
            cb = _mxu(cm, cat([bm] * SSD_REP + pad_rows, axis=0), _NT)
            pad_rows = [] if reps == gs else [jnp.zeros((gs - reps, SSD_GW), F32)]
            xbd = jnp.where(diag, cat([xdt] * SSD_REP + pad_rows, axis=0), 0.0)
            st = ST[i * SSD_GROUPS + g]
            y = _mxu(cb * seg[:, g * gs:(g + 1) * gs], xbd, _NN) + _mxu(cm, st, _NN) * jnp.exp(c_x[:, gl])
            y = (y + sx * dvec_ref[:, gl]) * _silu(sz_ref[rows, gl])
            y = y * lax.rsqrt(jnp.mean(y * y, axis=-1, keepdims=True) + 1e-6) * nw_ref[:, gl]
            mix_ref[rows, gl] = y.astype(BF16)
            c_end = c_x[L - 1:L, gl]
            ST[i * SSD_GROUPS + g] = st * jnp.exp(c_end) + _mxu(bm, xdt * jnp.exp(c_end - c_x[:, gl]), _TN)
        if nchunk > 1:
            tail[i] = xbc_ref[i * L + L - 8:(i + 1) * L, :]

    @pl.when(ci == nchunk - 1)
    def _():
        for i in range(nseq):
            conv_out_ref[i] = xbc_ref[i * L + t_real - (CONV_W - 1):i * L + t_real, :]
            for g in range(SSD_GROUPS):
                s_ref[i, g * SSD_REP:(g + 1) * SSD_REP] = ST[i * SSD_GROUPS + g].T.reshape(
                    SSD_REP, SSD_HEAD_DIM, SSD_STATE)


def _ssd(h, mix_prev, grp, P, st_in, conv_in, s_prev, p):
    L = grp.chunk
    nchunk = grp.t_rows // L
    gs = _ssd_lanes(L)
    tail = (SSD_HEADS, SSD_HEAD_DIM, SSD_STATE)
    ns = grp.nseq
    s_ins, s_specs, s_out, s_shape = _state_io(tail, p, st_in, grp)
    full = lambda shape: pl.BlockSpec(shape, lambda b, c: (0,) * len(shape))
    heads = jnp.arange(SSD_HEADS)[:, None]
    lane_p = jnp.arange(SSD_WIDTH)[None, :]
    ep = (lane_p // SSD_HEAD_DIM == heads).astype(F32)
    lane_s = jnp.arange(SSD_GROUPS * gs)[None, :]
    in_grp = lane_s % gs
    es = ((in_grp < SSD_REP * L) & ((lane_s // gs) * SSD_REP + in_grp // L == heads)).astype(F32)
    row = lambda a: a.reshape(1, -1)
    ins = [h, h, h, P['ssd_conv_w'], row(P['ssd_conv_b']), row(P['ssd_dt_bias']), row(P['ssd_a_log']),
           row(jnp.repeat(P['ssd_d'], SSD_HEAD_DIM)), row(P['ssd_norm']), ep, es]
    specs = [grp.spec(L, SSD_WIDTH, EVEN_OFF[5]), grp.spec(L, SSD_CONV_DIM, EVEN_OFF[6]), grp.spec(L, 128, EVEN_OFF[3]),
             full((CONV_W, SSD_CONV_DIM)), full((1, SSD_CONV_DIM)), full((1, SSD_HEADS)), full((1, SSD_HEADS)),
             full((1, SSD_WIDTH)), full((1, SSD_WIDTH)), full(ep.shape), full(es.shape)]
    if st_in is not None:
        ins += [conv_in] + s_ins
        specs += [pl.BlockSpec((None, ns, CONV_W - 1, SSD_CONV_DIM), lambda b, c: (p, b, 0, 0))] + s_specs
    any_spec = pl.BlockSpec(memory_space=pl.ANY)
    alias = {}
    for prev, out_idx in ((mix_prev, 0), (s_prev, 1)):
        if prev is not None:
            alias[len(ins)] = out_idx
            ins.append(prev)
            specs.append(any_spec)
    n_prev = (mix_prev is not None) + (s_prev is not None)
    return pl.pallas_call(
        functools.partial(_ssd_kernel, L=L, nchunk=nchunk, t_real=grp.t_real or L, nseq=ns,
                          has_state=st_in is not None, has_prev=n_prev),
        grid=(grp.steps, nchunk),
        in_specs=specs,
        out_specs=[grp.spec(L, SSD_WIDTH, 0), s_out,
                   pl.BlockSpec((ns, CONV_W - 1, SSD_CONV_DIM), lambda b, c: (b, 0, 0))],
        out_shape=[jax.ShapeDtypeStruct((h.shape[0], MIX_WIDTH), BF16), s_shape,
                   jax.ShapeDtypeStruct((grp.bsz, CONV_W - 1, SSD_CONV_DIM), F32)],
        scratch_shapes=[pltpu.VMEM((ns * SSD_GROUPS, SSD_STATE, SSD_GW), F32),
                        pltpu.VMEM((ns, 8, SSD_CONV_DIM), F32)],
        input_output_aliases=alias,
        compiler_params=_cparams(("parallel", "arbitrary")),
        name="ssd",
    )(*ins)


def _gdn_kernel(qkv_ref, cz_ref, sm_ref, cw_ref, alog_ref, dtb_ref, nw_ref, *rest, L, nchunk, t_real, nseq,
                has_state, has_prev):
    n_opt = 2 * has_state + has_prev
    mix_ref, s_ref, conv_out_ref, S, tail = rest[n_opt:]
    ci = pl.program_id(1)

    @pl.when(ci == 0)
    def _():
        tail[...] = jnp.zeros_like(tail)
        if has_state:
            for i in range(nseq):
                tail[i, 8 - (CONV_W - 1):8, :] = rest[0][i]
            S[...] = rest[1][...]
        else:
            S[...] = jnp.zeros_like(S)

    n2 = 2 * L
    d = GDN_HEAD_DIM
    cat = jnp.concatenate
    row8 = lax.broadcasted_iota(jnp.int32, (8, d), 0)

    def conv_tile(i, c0):
        u = qkv_ref[i * L:(i + 1) * L, c0:c0 + d]
        p8 = tail[i, :, c0:c0 + d]
        w = cw_ref[:, c0:c0 + d]
        acc = u * w[CONV_W - 1:CONV_W]
        for j in range(1, CONV_W):
            acc = acc + _shifted(u, p8, j, row8) * w[CONV_W - 1 - j:CONV_W - j]
        return _silu(acc)

    def l2n(x):
        return x * lax.rsqrt(jnp.sum(x * x, axis=-1, keepdims=True) + 1e-6)

    row = lax.broadcasted_iota(jnp.int32, (L, L), 0)
    col = lax.broadcasted_iota(jnp.int32, (L, L), 1)
    tril = (col <= row).astype(F32)
    r2 = lax.broadcasted_iota(jnp.int32, (n2, n2), 0)
    c2 = lax.broadcasted_iota(jnp.int32, (n2, n2), 1)
    same = (r2 >= L) == (c2 >= L)
    strict = same & (c2 < r2)
    incl = same & (c2 <= r2)
    upper = same & (r2 <= c2)
    eye = (r2 == c2).astype(F32)
    zl = jnp.zeros((L, d), F32)
    units = [(i, j) for i in range(nseq) for j in range(GDN_HEADS // 2)]
    nn, tt, qk, kq, kdec, ec, bcol, elast, vst = [], [], [], [], [], [], [], [], []
    for i in range(nseq):
        sm = sm_ref[i * L:(i + 1) * L, :]
        beta_all = _sigmoid(sm[:, :GDN_HEADS])
        g_all = -jnp.exp(alog_ref[...]) * _softplus(sm[:, GDN_HEADS:2 * GDN_HEADS] + dtb_ref[...])
        if t_real < L:
            valid = lax.broadcasted_iota(jnp.int32, (L, GDN_HEADS), 0) < t_real
            beta_all = jnp.where(valid, beta_all, 0.0)
            g_all = jnp.where(valid, g_all, 0.0)
        c_all = _mxu_f32(tril, g_all, _NN)
        for j in range(GDN_HEADS // 2):
            h0, h1 = 2 * j, 2 * j + 1
            stack_col = lambda a: cat([a[:, h0:h0 + 1], a[:, h1:h1 + 1]], axis=0)
            c_col = stack_col(c_all)
            beta_col = stack_col(beta_all)
            c_row = jnp.sum(jnp.where(upper, stack_col(g_all), 0.0), axis=0, keepdims=True)
            decay = jnp.exp(jnp.where(incl, c_col - c_row, -jnp.inf))
            last = lambda rows: cat([jnp.broadcast_to(c_all[L - 1:L, h0:h0 + 1], (rows, 1)),
                                     jnp.broadcast_to(c_all[L - 1:L, h1:h1 + 1], (rows, 1))], axis=0)
            q0, q1 = (l2n(conv_tile(i, h * d)) * d ** -0.5 for h in (h0, h1))
            k0, k1 = (l2n(conv_tile(i, GDN_WIDTH + h * d)) for h in (h0, h1))
            vst.append(cat([conv_tile(i, 2 * GDN_WIDTH + h0 * d), conv_tile(i, 2 * GDN_WIDTH + h1 * d)], axis=0))
            k_st = cat([cat([k0, zl], axis=1), cat([zl, k1], axis=1)], axis=0)
            q_st = cat([cat([q0, zl], axis=1), cat([zl, q1], axis=1)], axis=0)
            both = cat([k_st, q_st], axis=0)
            full = _mxu(both, k_st, _NT)
            a = jnp.where(strict, full[:n2] * decay * beta_col, 0.0)
            tt.append(a)
            qk.append(full[n2:] * decay)
            kq.append(both)
            kdec.append(k_st * jnp.exp(last(L) - c_col))
            ec.append(jnp.exp(c_col))
            bcol.append(beta_col)
            elast.append(jnp.exp(last(d)))
    tt = _unit_lower_inverse(tt, L, eye, r2 & (L - 1), c2 & (L - 1))
    s_old, ksqs, u = [], [], []
    for n, (i, j) in enumerate(units):
        s_old.append(cat([S[i, 2 * j], S[i, 2 * j + 1]], axis=0))
        ksqs.append(_mxu(kq[n], s_old[n], _NN))
    for n in range(len(units)):
        u.append(_mxu(tt[n], bcol[n] * (vst[n] - ec[n] * ksqs[n][:n2]), _NN))
    nw = nw_ref[...]
    for n, (i, j) in enumerate(units):
        rows = slice(i * L, (i + 1) * L)
        o = ec[n] * ksqs[n][n2:] + _mxu(qk[n], u[n], _NN)
        for hh, oh in ((2 * j, o[:L]), (2 * j + 1, o[L:])):
            cols = slice(hh * d, (hh + 1) * d)
            y = oh * lax.rsqrt(jnp.mean(oh * oh, axis=-1, keepdims=True) + 1e-6) * nw
            mix_ref[rows, cols] = (y * _silu(cz_ref[rows, cols])).astype(BF16)
        new = s_old[n] * elast[n] + _mxu(kdec[n], u[n], _TN)
        S[i, 2 * j] = new[:d]
        S[i, 2 * j + 1] = new[d:]
    if nchunk > 1:
        for i in range(nseq):
            tail[i] = qkv_ref[i * L + L - 8:(i + 1) * L, :]

    @pl.when(ci == nchunk - 1)
    def _():
        s_ref[...] = S[...]
        for i in range(nseq):
            conv_out_ref[i] = qkv_ref[i * L + t_real - (CONV_W - 1):i * L + t_real, :]


def _gdn(h, mix_prev, grp, P, st_in, conv_in, s_prev, p):
    L = grp.chunk
    nchunk = grp.t_rows // L
    tail = (GDN_HEADS, GDN_HEAD_DIM, GDN_HEAD_DIM)
    ns = grp.nseq
    s_ins, s_specs, s_out, s_shape = _state_io(tail, p, st_in, grp)
    cw = 3 * GDN_WIDTH
    full = lambda shape: pl.BlockSpec(shape, lambda b, c: (0,) * len(shape))
    ins = [h, h, h, P['gdn_conv_w'], P['gdn_a_log'].reshape(1, GDN_HEADS), P['gdn_dt_bias'].reshape(1, GDN_HEADS),
           P['gdn_norm'].reshape(1, GDN_HEAD_DIM)]
    specs = [grp.spec(L, cw, ODD_OFF[0]), grp.spec(L, GDN_WIDTH, ODD_OFF[1]), grp.spec(L, 128, ODD_OFF[2]),
             full((CONV_W, cw)), full((1, GDN_HEADS)), full((1, GDN_HEADS)), full((1, GDN_HEAD_DIM))]
    if st_in is not None:
        ins += [conv_in] + s_ins
        specs += [pl.BlockSpec((None, ns, CONV_W - 1, cw), lambda b, c: (p, b, 0, 0))] + s_specs
    any_spec = pl.BlockSpec(memory_space=pl.ANY)
    alias = {}
    for prev, out_idx in ((mix_prev, 0), (s_prev, 1)):
        if prev is not None:
            alias[len(ins)] = out_idx
            ins.append(prev)
            specs.append(any_spec)
    n_prev = (mix_prev is not None) + (s_prev is not None)
    return pl.pallas_call(
        functools.partial(_gdn_kernel, L=L, nchunk=nchunk, t_real=grp.t_real or L,
                          nseq=ns, has_state=st_in is not None, has_prev=n_prev),
        grid=(grp.steps, nchunk),
        in_specs=specs,
        out_specs=[grp.spec(L, GDN_WIDTH, 0), s_out,
                   pl.BlockSpec((ns, CONV_W - 1, cw), lambda b, c: (b, 0, 0))],
        out_shape=[jax.ShapeDtypeStruct((h.shape[0], MIX_WIDTH), BF16), s_shape,
                   jax.ShapeDtypeStruct((grp.bsz, CONV_W - 1, cw), F32)],
        scratch_shapes=[pltpu.VMEM((ns,) + tail, F32), pltpu.VMEM((ns, 8, cw), F32)],
        input_output_aliases=alias,
        compiler_params=_cparams(("parallel", "arbitrary")),
        name="gdn",
    )(*ins)


RWKV_PAIRS = RWKV_HEADS // 2
RWKV_PW = 2 * RWKV_HEAD_DIM


def _rwkv_kernel(r_ref, k_ref, v_ref, xwa_ref, gate_ref, mu_ref, w0_ref, w2_ref, a0_ref, a2_ref, kkp_ref, ka_ref,
                 rk_ref, lng_ref, lnb_ref, *rest, L, nchunk, t_real, nseq, has_state, has_prev):
    n_opt = 2 * has_state + has_prev
    mix_ref, s_ref, shift_out_ref, S, last = rest[n_opt:]
    ci = pl.program_id(1)
    n = RWKV_HEAD_DIM
    w3 = 3 * RWKV_WIDTH
    cat = jnp.concatenate
    pairs = range(RWKV_PAIRS)

    @pl.when(ci == 0)
    def _():
        last[...] = jnp.zeros_like(last)
        if has_state:
            zn = jnp.zeros((n, n), F32)
            for i in range(nseq):
                last[i, 7:8, :] = rest[0][i]
                for j in pairs:
                    S[i * RWKV_PAIRS + j] = cat([cat([rest[1][i, 2 * j], zn], axis=1),
                                                 cat([zn, rest[1][i, 2 * j + 1]], axis=1)], axis=0)
        else:
            S[...] = jnp.zeros_like(S)

    n2 = 2 * L
    row = lax.broadcasted_iota(jnp.int32, (L, L), 0)
    col = lax.broadcasted_iota(jnp.int32, (L, L), 1)
    tril = (col <= row).astype(F32)
    r2 = lax.broadcasted_iota(jnp.int32, (n2, n2), 0)
    c2 = lax.broadcasted_iota(jnp.int32, (n2, n2), 1)
    same = (r2 >= L) == (c2 >= L)
    strict = same & (c2 < r2)
    incl = same & (c2 <= r2)
    eye = (r2 == c2).astype(F32)
    lane = lax.broadcasted_iota(jnp.int32, (L, RWKV_PW), 1)
    lo = lane < n

    def stack(x):
        return cat([jnp.where(lo, x, 0.0), jnp.where(lo, 0.0, x)], axis=0)

    row8 = lax.broadcasted_iota(jnp.int32, (8, RWKV_PW), 0)
    valid = lax.broadcasted_iota(jnp.int32, (L, RWKV_PW), 0) < t_real

    def seg_sum(x):
        s_lo = jnp.sum(jnp.where(lo, x, 0.0), axis=-1, keepdims=True)
        s_hi = jnp.sum(jnp.where(lo, 0.0, x), axis=-1, keepdims=True)
        return jnp.where(lo, s_lo, s_hi)

    def shift_mix(ref, i, c_src, c_all):
        x = ref[i * L:(i + 1) * L, c_src:c_src + RWKV_PW]
        prev = _shifted(x, last[i, :, c_all:c_all + RWKV_PW], 1, row8)
        return x + (prev - x) * mu_ref[:, c_all:c_all + RWKV_PW]

    xwa = cat([shift_mix(xwa_ref, i, 0, w3) for i in range(nseq)], axis=0)
    lr_w_all = _mxu(jnp.tanh(xwa), w2_ref[...], _NN)
    lr_a_all = _mxu(xwa, a2_ref[...], _NN)
    units = [(i, j) for i in range(nseq) for j in pairs]
    a_ak, a_rk, a_rb, nn, tt, sread, kdbd, egl, vs, bonus = [], [], [], [], [], [], [], [], [], []
    for i, j in units:
        sl = slice(j * RWKV_PW, (j + 1) * RWKV_PW)
        r = shift_mix(r_ref, i, j * RWKV_PW, j * RWKV_PW)
        k = shift_mix(k_ref, i, j * RWKV_PW, RWKV_WIDTH + j * RWKV_PW)
        v = shift_mix(v_ref, i, j * RWKV_PW, 2 * RWKV_WIDTH + j * RWKV_PW)
        w_log = -_softplus(-(w0_ref[:, sl] + lr_w_all[i * L:(i + 1) * L, sl])) - 0.5
        lw = -jnp.exp(w_log)
        a7 = _sigmoid(a0_ref[:, sl] + lr_a_all[i * L:(i + 1) * L, sl])
        kx = k * kkp_ref[:, sl]
        kk = kx * lax.rsqrt(seg_sum(kx * kx) + 1e-6)
        k = k * (1.0 + (a7 - 1.0) * ka_ref[:, sl])
        if t_real < L:
            lw, kk, k, v = (jnp.where(valid, a, 0.0) for a in (lw, kk, k, v))
        b = kk * a7
        bonus.append(seg_sum(r * k * rk_ref[:, sl]) * v)
        vs.append(stack(v))
        g = _mxu_f32(tril, lw, _NN)
        gp = g - lw
        gm = g[L // 2 - 1:L // 2, :]
        gl = g[L - 1:L, :]
        e_neg = jnp.exp(gm - g)
        lhs = cat([stack(kk * jnp.exp(gp - gm)), stack(r * jnp.exp(g - gm))], axis=0)
        rhs = cat([stack(b * e_neg), stack(k * e_neg)], axis=0)
        full = _mxu(lhs, rhs, _NT)
        a_ab = jnp.where(strict, full[:n2, :n2], 0.0)
        a_ak.append(jnp.where(strict, full[:n2, n2:], 0.0))
        a_rb.append(jnp.where(incl, full[n2:, :n2], 0.0))
        a_rk.append(jnp.where(incl, full[n2:, n2:], 0.0))
        tt.append(a_ab)
        sread.append(cat([stack(kk * jnp.exp(gp)), stack(r * jnp.exp(g))], axis=0))
        dec = jnp.exp(gl - g)
        kdbd.append(cat([stack(k * dec), stack(-b * dec)], axis=0))
        egl.append(jnp.exp(gl))
    tt = _unit_lower_inverse(tt, L, eye, r2 & (L - 1), c2 & (L - 1))
    s_old, sr, av, u = [], [], [], []
    for m, (i, j) in enumerate(units):
        s_old.append(S[i * RWKV_PAIRS + j])
        sr.append(_mxu(sread[m], s_old[m], _NT))
        av.append(_mxu(cat([a_ak[m], a_rk[m]], axis=0), vs[m], _NN))
    for m in range(len(units)):
        u.append(_mxu(tt[m], sr[m][:n2] + av[m][:n2], _NN))
    for m, (i, j) in enumerate(units):
        sl = slice(j * RWKV_PW, (j + 1) * RWKV_PW)
        rows = slice(i * L, (i + 1) * L)
        o = sr[m][n2:] + av[m][n2:] - _mxu(a_rb[m], u[m], _NN)
        o = o[:L] + o[L:]
        oc = o - seg_sum(o) * (1.0 / n)
        gn = oc * lax.rsqrt(seg_sum(oc * oc) * (1.0 / n) + RWKV_GN_EPS)
        y = gn * lng_ref[:, sl] + lnb_ref[:, sl] + bonus[m]
        mix_ref[rows, sl] = (y * _silu(gate_ref[rows, sl])).astype(BF16)
        S[i * RWKV_PAIRS + j] = s_old[m] * egl[m] + _mxu(cat([vs[m], u[m]], axis=0), kdbd[m], _TN)
    pieces = ((r_ref, 0, RWKV_WIDTH), (k_ref, RWKV_WIDTH, RWKV_WIDTH), (v_ref, 2 * RWKV_WIDTH, RWKV_WIDTH),
              (xwa_ref, w3, RWKV_PW))
    if nchunk > 1:
        for i in range(nseq):
            for ref, c0, wd in pieces:
                last[i, :, c0:c0 + wd] = ref[i * L + L - 8:(i + 1) * L, :]

    @pl.when(ci == nchunk - 1)
    def _():
        for i in range(nseq):
            for ref, c0, wd in pieces:
                shift_out_ref[i, :, c0:c0 + wd] = ref[i * L + t_real - 1:i * L + t_real, :]
            for j in pairs:
                s_ref[i, 2 * j] = S[i * RWKV_PAIRS + j][:n, :n]
                s_ref[i, 2 * j + 1] = S[i * RWKV_PAIRS + j][n:, n:]


def _rwkv(h, mix_prev, grp, P, st_in, shift_in, s_prev, p):
    L = grp.chunk
    nchunk = grp.t_rows // L
    tail = (RWKV_HEADS, RWKV_HEAD_DIM, RWKV_HEAD_DIM)
    ns = grp.nseq
    s_ins, s_specs, s_out, s_shape = _state_io(tail, p, st_in, grp)
    full = lambda shape: pl.BlockSpec(shape, lambda b, c: (0,) * len(shape))
    row = lambda a: a.reshape(1, -1)
    zr = jnp.zeros((RWKV_HEAD_DIM, RWKV_WIDTH), F32)
    w2 = jnp.concatenate([P['rwkv_w2'], zr], axis=0)
    a2 = jnp.concatenate([zr, P['rwkv_a2']], axis=0)
    off = ODD_OFF[4]
    ins = [h, h, h, h, h, row(P['rwkv_mu']), row(P['rwkv_w0']), w2, row(P['rwkv_a0']), a2, row(P['rwkv_kk']),
           row(P['rwkv_ka']), row(P['rwkv_rk']), row(P['rwkv_ln_g']), row(P['rwkv_ln_b'])]
    vec = full((1, RWKV_WIDTH))
    specs = [grp.spec(L, RWKV_WIDTH, off), grp.spec(L, RWKV_WIDTH, off + RWKV_WIDTH),
             grp.spec(L, RWKV_WIDTH, off + 2 * RWKV_WIDTH), grp.spec(L, RWKV_PW, off + 3 * RWKV_WIDTH),
             grp.spec(L, RWKV_WIDTH, ODD_OFF[5]), full((1, RWKV_SHIFT_DIM)), vec, full((RWKV_PW, RWKV_WIDTH)), vec,
             full((RWKV_PW, RWKV_WIDTH)), vec, vec, vec, vec, vec]
    if st_in is not None:
        ins += [shift_in] + s_ins
        specs += [pl.BlockSpec((None, ns, 1, RWKV_SHIFT_DIM), lambda b, c: (p, b, 0, 0))] + s_specs
    any_spec = pl.BlockSpec(memory_space=pl.ANY)
    alias = {}
    for prev, out_idx in ((mix_prev, 0), (s_prev, 1)):
        if prev is not None:
            alias[len(ins)] = out_idx
            ins.append(prev)
            specs.append(any_spec)
    n_prev = (mix_prev is not None) + (s_prev is not None)
    return pl.pallas_call(
        functools.partial(_rwkv_kernel, L=L, nchunk=nchunk, t_real=grp.t_real or L,
                          nseq=ns, has_state=st_in is not None, has_prev=n_prev),
        grid=(grp.steps, nchunk),
        in_specs=specs,
        out_specs=[grp.spec(L, RWKV_WIDTH, GDN_WIDTH), s_out,
                   pl.BlockSpec((ns, 1, RWKV_SHIFT_DIM), lambda b, c: (b, 0, 0))],
        out_shape=[jax.ShapeDtypeStruct((h.shape[0], MIX_WIDTH), BF16), s_shape,
                   jax.ShapeDtypeStruct((grp.bsz, 1, RWKV_SHIFT_DIM), F32)],
        scratch_shapes=[pltpu.VMEM((ns * RWKV_PAIRS, RWKV_PW, RWKV_PW), F32),
                        pltpu.VMEM((ns, 8, RWKV_SHIFT_DIM), F32)],
        input_output_aliases=alias,
        compiler_params=_cparams(("parallel", "arbitrary")),
        name="rwkv7",
    )(*ins)


def _pad_t(a, t_to):
    t = a.shape[1]
    if t == t_to:
        return a
    return jnp.pad(a, [(0, 0), (0, t_to - t)] + [(0, 0)] * (a.ndim - 2))


def _even_mix(h, mix, grp, mem_k, mem_v, layer, st_in, st_prev, conv_in, P):
    p = layer // 2
    mix, s_ssd, s_conv = _ssd(h, mix, grp, P, st_in['ssd'], conv_in, st_prev['ssd'], p)
    gla_l, gla_tb = (GLA_CHUNK, 256) if grp.t_real is None else (grp.chunk, grp.chunk)
    mix, s_gla = _gla(h, mix, grp, P, st_in['gla'], st_prev['gla'], p, gla_l, gla_tb)
    mix = _mem_attention(h, mix, grp, EVEN_OFF[8], EVEN_OFF[9], mem_k, mem_v, layer, min(512, grp.t_rows))
    return mix, dict(gla=s_gla, ssd=s_ssd), s_conv


def _odd_mix(h, mix, grp, mem_k, mem_v, layer, st_in, st_prev, conv_in, shift_in, P):
    p = layer // 2
    mix, s_gdn, s_conv = _gdn(h, mix, grp, P, st_in['gdn'], conv_in, st_prev['gdn'], p)
    mix, s_rwkv, s_shift = _rwkv(h, mix, grp, P, st_in['rwkv'], shift_in, st_prev['rwkv'], p)
    mix = _mem_attention(h, mix, grp, ODD_OFF[6], ODD_OFF[7], mem_k, mem_v, layer, min(512, grp.t_rows))
    return mix, dict(gdn=s_gdn, rwkv=s_rwkv), s_conv, s_shift.reshape(grp.bsz, RWKV_SHIFT_DIM)


def kernel(x_prompt, x_sample, mem_prompt, cache_mem_k, cache_mem_v, state_gla, state_ssd, state_ssd_conv, state_gdn, state_gdn_conv, state_rwkv, state_rwkv_shift, mem_w_kv, ev_w_in, ev_gla_w2, ev_gla_b, ev_gla_norm, ev_ssd_conv_w, ev_ssd_conv_b, ev_ssd_dt_bias, ev_ssd_a_log, ev_ssd_d, ev_ssd_norm, ev_w_out, ev_ln_g, ev_ln_b, od_w_in, od_gdn_conv_w, od_gdn_dt_bias, od_gdn_a_log, od_gdn_norm, od_rwkv_mu, od_rwkv_w0, od_rwkv_w2, od_rwkv_a0, od_rwkv_a2, od_rwkv_kk, od_rwkv_ka, od_rwkv_rk, od_rwkv_ln_g, od_rwkv_ln_b, od_w_out, od_ln_g, od_ln_b):
    ev = dict(w_in=ev_w_in, gla_w2=ev_gla_w2, gla_b=ev_gla_b, gla_norm=ev_gla_norm,
              ssd_conv_w=ev_ssd_conv_w, ssd_conv_b=ev_ssd_conv_b, ssd_dt_bias=ev_ssd_dt_bias,
              ssd_a_log=ev_ssd_a_log, ssd_d=ev_ssd_d, ssd_norm=ev_ssd_norm,
              w_out=ev_w_out, ln_g=ev_ln_g, ln_b=ev_ln_b)
    od = dict(w_in=od_w_in, gdn_conv_w=od_gdn_conv_w, gdn_dt_bias=od_gdn_dt_bias, gdn_a_log=od_gdn_a_log,
              gdn_norm=od_gdn_norm, rwkv_mu=od_rwkv_mu, rwkv_w0=od_rwkv_w0, rwkv_w2=od_rwkv_w2,
              rwkv_a0=od_rwkv_a0, rwkv_a2=od_rwkv_a2, rwkv_kk=od_rwkv_kk, rwkv_ka=od_rwkv_ka,
              rwkv_rk=od_rwkv_rk, rwkv_ln_g=od_rwkv_ln_g, rwkv_ln_b=od_rwkv_ln_b,
              w_out=od_w_out, ln_g=od_ln_g, ln_b=od_ln_b)
    bp, tp, _ = x_prompt.shape
    bs, ts, _ = x_sample.shape
    mp, ms = bp * tp, bs * SMALL_T
    grp_p = _Group(bp, tp, None, 0, SEQ_CHUNK)
    grp_s = _Group(bs, SMALL_T, ts, mp, SMALL_T, nseq=SAMPLE_NSEQ)

    w_kv = jnp.moveaxis(mem_w_kv, 0, 1).reshape(D_MODEL, DEPTH * 2 * MEM_WIDTH).astype(BF16)
    kv = _matmul(mem_prompt.reshape(bp * MEM_LEN, D_MODEL).astype(BF16), w_kv, 512, 1024)
    kv6 = kv.reshape(bp, MEM_LEN, DEPTH, 2, MEM_HEADS, MEM_HEAD_DIM)
    mem_k_p = jnp.moveaxis(kv6[:, :, :, 0], 2, 0)
    mem_v_p = jnp.moveaxis(kv6[:, :, :, 1], 2, 0)
    mk_s, mv_s = _cache_view(cache_mem_k), _cache_view(cache_mem_v)

    x = jnp.concatenate([x_prompt.reshape(mp, D_MODEL),
                         _pad_t(x_sample, SMALL_T).reshape(ms, D_MODEL)], axis=0)
    x_bf = x.astype(BF16)
    none = {n: None for n in ('gla', 'ssd', 'gdn', 'rwkv')}
    in_s = dict(gla=state_gla, ssd=state_ssd, gdn=state_gdn, rwkv=state_rwkv)
    shift_s = state_rwkv_shift.reshape(N_PAIRS, bs, 1, RWKV_SHIFT_DIM)
    out_p, out_s = dict(none), dict(none)
    small_p = {n: [] for n in ('ssd_conv', 'gdn_conv', 'rwkv_shift')}
    small_s = {n: [] for n in small_p}
    tm = 1024
    w_in_ev = _pack_w_in(ev_w_in, EVEN_SIZES, EVEN_ORDER, EVEN_N)
    w_in_od = _pack_w_in(od_w_in, ODD_SIZES, ODD_ORDER, ODD_N)
    w_out_ev = jnp.concatenate([ev_w_out[:, GLA_WIDTH:GLA_WIDTH + SSD_WIDTH].astype(BF16),
                                ev_w_out[:, :GLA_WIDTH].astype(BF16),
                                ev_w_out[:, GLA_WIDTH + SSD_WIDTH:].astype(BF16)], axis=1)
    w_out_od = od_w_out.astype(BF16)
    small = lambda d: {n: w[p] for n, w in d.items() if n not in ('w_in', 'w_out')}
    for layer in range(DEPTH):
        p = layer // 2
        if layer % 2 == 0:
            P = small(ev)
            h = _matmul(x_bf, w_in_ev, tm, PROJ_TN, p=p)
            mix, new, c1 = _even_mix(h, None, grp_p, kv, kv, layer, none, out_p, None, P)
            out_p.update(new)
            mix, new, c2 = _even_mix(h, mix, grp_s, mk_s, mv_s, layer, in_s, out_s, state_ssd_conv, P)
            out_s.update(new)
            small_p['ssd_conv'].append(c1)
            small_s['ssd_conv'].append(c2)
            w_out = w_out_ev
        else:
            P = small(od)
            h = _matmul(x_bf, w_in_od, tm, PROJ_TN, p=p)
            mix, new, c1, h1 = _odd_mix(h, None, grp_p, kv, kv, layer, none, out_p, None, None, P)
            out_p.update(new)
            mix, new, c2, h2 = _odd_mix(h, mix, grp_s, mk_s, mv_s, layer, in_s, out_s, state_gdn_conv, shift_s, P)
            out_s.update(new)
            small_p['gdn_conv'].append(c1)
            small_s['gdn_conv'].append(c2)
            small_p['rwkv_shift'].append(h1)
            small_s['rwkv_shift'].append(h2)
            w_out = w_out_od
        x, x_bf = _out_ln(mix, w_out, p, x, P['ln_g'], P['ln_b'])

    y_prompt = x[:mp].reshape(bp, tp, D_MODEL)
    y_sample = x[mp:].reshape(bs, SMALL_T, D_MODEL)[:, :ts]
    st = lambda d, n: jnp.stack(d[n])
    return (y_prompt, y_sample, mem_k_p, mem_v_p,
            out_p['gla'], out_s['gla'], out_p['ssd'], out_s['ssd'],
            st(small_p, 'ssd_conv'), st(small_s, 'ssd_conv'), out_p['gdn'], out_s['gdn'],
            st(small_p, 'gdn_conv'), st(small_s, 'gdn_conv'), out_p['rwkv'], out_s['rwkv'],
            st(small_p, 'rwkv_shift'), st(small_s, 'rwkv_shift'))
```
